```python
import math
import jax, jax.numpy as jnp
from jax import lax
import numpy as np

D_MODEL = 1024
BATCH = 8
SEQ = 2048
DEPTH = 1
DEC_BATCH = 128
DEC_SEQ = 1
PAST_LEN = 2048
PAGE_SIZE = 128

H_RET = 4
DK_RET = D_MODEL // 8
DV_RET = D_MODEL // 8
RET_WIDTH = H_RET * DV_RET
H_DIFF = 4
DH_DIFF = D_MODEL // 16
DV_DIFF = 2 * DH_DIFF
DIFF_WIDTH = H_DIFF * DV_DIFF
MIX_WIDTH = RET_WIDTH + DIFF_WIDTH
IN_SPLITS = (H_RET * DK_RET, H_RET * DK_RET, H_RET * DV_RET, RET_WIDTH,
             H_DIFF * 2 * DH_DIFF, H_DIFF * 2 * DH_DIFF, H_DIFF * DV_DIFF)
IN_WIDTH = sum(IN_SPLITS)
D_FF = ((8 * D_MODEL // 3 + 127) // 128) * 128
CONV_W = 3
RET_CHUNK = 128
Q_BLOCK = 128
EPS = 1e-6
NEG_INF = -1e30

kernel_name = "hybrid_retention_diffattn_convffn_step"


def rms_norm(x, g):
    xf = x.astype(jnp.float32)
    y = xf * lax.rsqrt(jnp.mean(xf * xf, axis=-1, keepdims=True) + EPS)
    return (y * g.astype(jnp.float32)).astype(x.dtype)


def ret_log_decay():
    return jnp.log(1.0 - 2.0 ** (-5.0 - jnp.arange(H_RET, dtype=jnp.float32)))


def alibi_slopes(n):
    return 2.0 ** (-8.0 / n * jnp.arange(1, n + 1, dtype=jnp.float32))


def alibi_bias(qpos, kpos, slopes):
    dist = (qpos[:, None] - kpos[None, :]).astype(jnp.float32)
    return jnp.where(dist[None] >= 0, -slopes[:, None, None] * dist[None], NEG_INF)


def lambda_init(layer):
    return 0.8 - 0.6 * math.exp(-0.3 * layer)


def in_project(h, w_in):
    B, T = h.shape[0], h.shape[1]
    z = jnp.einsum('btd,de->bte', h, w_in)
    idx = [int(i) for i in np.cumsum(IN_SPLITS)[:-1]]
    rq, rk, rv, rg, dq, dk, dv = jnp.split(z, idx, axis=-1)
    f32 = jnp.float32
    rq = rq.reshape(B, T, H_RET, DK_RET).transpose(0, 2, 1, 3).astype(f32)
    rk = rk.reshape(B, T, H_RET, DK_RET).transpose(0, 2, 1, 3).astype(f32) * (DK_RET ** -0.5)
    rv = rv.reshape(B, T, H_RET, DV_RET).transpose(0, 2, 1, 3).astype(f32)
    dq = dq.reshape(B, T, H_DIFF, 2, DH_DIFF)
    dk = dk.reshape(B, T, H_DIFF, 2, DH_DIFF)
    dv = dv.reshape(B, T, H_DIFF, DV_DIFF)
    return rq, rk, rv, rg, dq, dk, dv


def retention_chunk(q, k, v, s, log_g):
    L = q.shape[2]
    i = jnp.arange(L, dtype=jnp.float32)
    diff = i[:, None] - i[None, :]
    causal = diff >= 0
    decay = jnp.where(causal[None], jnp.exp(jnp.where(causal, diff, 0.0)[None] * log_g[:, None, None]), 0.0)
    att = jnp.einsum('bhid,bhjd->bhij', q, k) * decay[None]
    q_dec = q * jnp.exp((i + 1.0)[None, :] * log_g[:, None])[None, :, :, None]
    o = jnp.einsum('bhij,bhje->bhie', att, v) + jnp.einsum('bhid,bhde->bhie', q_dec, s)
    k_dec = k * jnp.exp((L - 1.0 - i)[None, :] * log_g[:, None])[None, :, :, None]
    s_new = jnp.exp(L * log_g)[None, :, None, None] * s + jnp.einsum('bhjd,bhje->bhde', k_dec, v)
    return o, s_new


def retention_prompt(q, k, v, log_g):
    B, H, T, _ = q.shape
    nc = T // RET_CHUNK

    def to_chunks(a):
        return a.reshape(B, H, nc, RET_CHUNK, a.shape[-1]).transpose(2, 0, 1, 3, 4)

    def step(s, qkv):
        qc, kc, vc = qkv
        o, s = retention_chunk(qc, kc, vc, s, log_g)
        return s, o

    s0 = jnp.zeros((B, H, DK_RET, DV_RET), jnp.float32)
    s_fin, o = lax.scan(step, s0, (to_chunks(q), to_chunks(k), to_chunks(v)))
    o = o.transpose(1, 2, 0, 3, 4).reshape(B, H, T, DV_RET)
    return o, s_fin


def retention_out(o, g):
    B, T = g.shape[0], g.shape[1]
    o = o * lax.rsqrt(jnp.mean(o * o, axis=-1, keepdims=True) + EPS)
    o = o.transpose(0, 2, 1, 3).reshape(B, T, RET_WIDTH)
    return o.astype(g.dtype) * jax.nn.silu(g)


def diff_combine(p, lam):
    return p[:, :, 0] - lam * p[:, :, 1]


def diff_attn_prompt(dq, dk, dv, lam):
    B, T = dq.shape[0], dq.shape[1]
    nb = T // Q_BLOCK
    slopes = alibi_slopes(H_DIFF)
    kpos = jnp.arange(T)
    qb = dq.reshape(B, nb, Q_BLOCK, H_DIFF, 2, DH_DIFF).transpose(1, 0, 2, 3, 4, 5)

    def block(args):
        q, start = args
        qpos = start + jnp.arange(Q_BLOCK)
        s = jnp.einsum('bqhmd,bkhmd->bhmqk', q, dk).astype(jnp.float32) * (DH_DIFF ** -0.5)
        s = s + alibi_bias(qpos, kpos, slopes)[None, :, None]
        a = diff_combine(jax.nn.softmax(s, axis=-1), lam)
        return jnp.einsum('bhqk,bkhe->bqhe', a.astype(dv.dtype), dv)

    out = lax.map(block, (qb, jnp.arange(nb) * Q_BLOCK))
    return out.transpose(1, 0, 2, 3, 4).reshape(B, T, H_DIFF, DV_DIFF)


def diff_attn_sample(dq, dk, dv, k_past, v_past, lam):
    T = dq.shape[1]
    P = k_past.shape[1]
    slopes = alibi_slopes(H_DIFF)
    qpos = P + jnp.arange(T)
    scale = DH_DIFF ** -0.5
    s_past = jnp.einsum('bqhmd,bkhmd->bhmqk', dq, k_past).astype(jnp.float32) * scale
    s_new = jnp.einsum('bqhmd,bkhmd->bhmqk', dq, dk).astype(jnp.float32) * scale
    s_past = s_past + alibi_bias(qpos, jnp.arange(P), slopes)[None, :, None]
    s_new = s_new + alibi_bias(qpos, qpos, slopes)[None, :, None]
    p = jax.nn.softmax(jnp.concatenate([s_past, s_new], axis=-1), axis=-1)
    a = diff_combine(p, lam).astype(dv.dtype)
    return (jnp.einsum('bhqk,bkhe->bqhe', a[..., :P], v_past)
            + jnp.einsum('bhqk,bkhe->bqhe', a[..., P:], dv))


def mix_out(ret_o, rg, diff_o, subln_g, lam_i, w_o):
    B, T = rg.shape[0], rg.shape[1]
    a = retention_out(ret_o, rg)
    b = (rms_norm(diff_o, subln_g) * (1.0 - lam_i)).reshape(B, T, DIFF_WIDTH)
    m = jnp.concatenate([a, b.astype(a.dtype)], axis=-1)
    return jnp.einsum('bte,ed->btd', m, w_o)


def conv_ffn(h, buf, w_in, conv_w, conv_b, w_out):
    T = h.shape[1]
    z = jnp.einsum('btd,df->btf', h, w_in)
    g, u = jnp.split(z, [D_FF], axis=-1)
    xp = jnp.concatenate([buf.astype(g.dtype), g], axis=1)
    c = sum(conv_w[j] * xp[:, j:j + T] for j in range(CONV_W)) + conv_b
    y = jnp.einsum('btf,fd->btd', jax.nn.gelu(c) * u, w_out)
    return y, xp[:, -(CONV_W - 1):]


def setup_inputs(seed: int = 0) -> dict:
    key = jax.random.key(seed)
    ks = jax.random.split(key, 24)
    f32 = jnp.float32
    n_pages = PAST_LEN // PAGE_SIZE
    n_pool = (5 * DEC_BATCH * n_pages + 3) // 4
    nrm = lambda k, shape, s: jax.random.normal(k, shape, f32) * s
    perm = jax.random.permutation(ks[0], n_pool)[:DEC_BATCH * n_pages]
    return {
        'x_prompt': nrm(ks[1], (BATCH, SEQ, D_MODEL), 1.0),
        'x_sample': nrm(ks[2], (DEC_BATCH, DEC_SEQ, D_MODEL), 1.0),
        'state_ret': nrm(ks[3], (DEPTH, DEC_BATCH, H_RET, DK_RET, DV_RET), 0.5),
        'cache_k': nrm(ks[4], (DEPTH, n_pool, PAGE_SIZE, H_DIFF, 2, DH_DIFF), 1.0),
        'cache_v': nrm(ks[5], (DEPTH, n_pool, PAGE_SIZE, H_DIFF, DV_DIFF), 1.0),
        'state_conv': nrm(ks[6], (DEPTH, DEC_BATCH, CONV_W - 1, D_FF), 1.0),
        'page_table': perm.reshape(DEC_BATCH, n_pages).astype(jnp.int32),
        'norm_mix_pre': 1.0 + nrm(ks[7], (DEPTH, D_MODEL), 0.02),
        'norm_mix_post': 1.0 + nrm(ks[8], (DEPTH, D_MODEL), 0.02),
        'w_in': nrm(ks[9], (DEPTH, D_MODEL, IN_WIDTH), D_MODEL ** -0.5),
        'w_o': nrm(ks[10], (DEPTH, MIX_WIDTH, D_MODEL), MIX_WIDTH ** -0.5),
        'lambda_q1': nrm(ks[11], (DEPTH, DH_DIFF), 0.1),
        'lambda_k1': nrm(ks[12], (DEPTH, DH_DIFF), 0.1),
        'lambda_q2': nrm(ks[13], (DEPTH, DH_DIFF), 0.1),
        'lambda_k2': nrm(ks[14], (DEPTH, DH_DIFF), 0.1),
        'subln_g': 1.0 + nrm(ks[15], (DEPTH, DV_DIFF), 0.02),
        'norm_ffn_pre': 1.0 + nrm(ks[16], (DEPTH, D_MODEL), 0.02),
        'norm_ffn_post': 1.0 + nrm(ks[17], (DEPTH, D_MODEL), 0.02),
        'w_ffn_in': nrm(ks[18], (DEPTH, D_MODEL, 2 * D_FF), D_MODEL ** -0.5),
        'conv_w': nrm(ks[19], (DEPTH, CONV_W, D_FF), CONV_W ** -0.5),
        'conv_b': nrm(ks[20], (DEPTH, D_FF), 0.01),
        'w_ffn_out': nrm(ks[21], (DEPTH, D_FF, D_MODEL), D_FF ** -0.5),
    }


def reference(x_prompt, x_sample, state_ret, cache_k, cache_v, state_conv, page_table,
              norm_mix_pre, norm_mix_post, w_in, w_o, lambda_q1, lambda_k1, lambda_q2, lambda_k2,
              subln_g, norm_ffn_pre, norm_ffn_post, w_ffn_in, conv_w, conv_b, w_ffn_out):
    f32 = jnp.float32
    log_g = ret_log_decay()
    B = x_prompt.shape[0]
    DB = x_sample.shape[0]
    xp, xs = x_prompt, x_sample
    rsp, rss, kp, vp, kss, vss, cp, cs = [], [], [], [], [], [], [], []
    for l in range(DEPTH):
        lam_i = lambda_init(l)
        lam = (jnp.exp(jnp.sum(lambda_q1[l].astype(f32) * lambda_k1[l].astype(f32)))
               - jnp.exp(jnp.sum(lambda_q2[l].astype(f32) * lambda_k2[l].astype(f32))) + lam_i)

        h = rms_norm(xp, norm_mix_pre[l])
        rq, rk, rv, rg, dq, dk, dv = in_project(h, w_in[l])
        ro, s_fin = retention_prompt(rq, rk, rv, log_g)
        do = diff_attn_prompt(dq, dk, dv, lam)
        xp = xp + rms_norm(mix_out(ro, rg, do, subln_g[l], lam_i, w_o[l]), norm_mix_post[l])
        h = rms_norm(xp, norm_ffn_pre[l])
        f, cbuf = conv_ffn(h, jnp.zeros((B, CONV_W - 1, D_FF), h.dtype),
                           w_ffn_in[l], conv_w[l], conv_b[l], w_ffn_out[l])
        xp = xp + rms_norm(f, norm_ffn_post[l])
        rsp.append(s_fin.astype(x_prompt.dtype))
        kp.append(dk)
        vp.append(dv)
        cp.append(cbuf)

        h = rms_norm(xs, norm_mix_pre[l])
        rq, rk, rv, rg, dq, dk, dv = in_project(h, w_in[l])
        ro, s_new = retention_chunk(rq, rk, rv, state_ret[l].astype(f32), log_g)
        k_past = cache_k[l][page_table].reshape(DB, -1, H_DIFF, 2, DH_DIFF)
        v_past = cache_v[l][page_table].reshape(DB, -1, H_DIFF, DV_DIFF)
        do = diff_attn_sample(dq, dk, dv, k_past.astype(dq.dtype), v_past.astype(dv.dtype), lam)
        xs = xs + rms_norm(mix_out(ro, rg, do, subln_g[l], lam_i, w_o[l]), norm_mix_post[l])
        h = rms_norm(xs, norm_ffn_pre[l])
        f, cbuf = conv_ffn(h, state_conv[l], w_ffn_in[l], conv_w[l], conv_b[l], w_ffn_out[l])
        xs = xs + rms_norm(f, norm_ffn_post[l])
        rss.append(s_new.astype(state_ret.dtype))
        kss.append(dk)
        vss.append(dv)
        cs.append(cbuf)

    return (xp, xs, jnp.stack(rsp), jnp.stack(rss), jnp.stack(kp), jnp.stack(vp),
            jnp.stack(kss), jnp.stack(vss), jnp.stack(cp), jnp.stack(cs))
```

```python
import functools
import math

import jax
import jax.numpy as jnp
from jax import lax
from jax.experimental import pallas as pl
from jax.experimental.pallas import tpu as pltpu

F32 = jnp.float32
BF16 = jnp.bfloat16
EPS = 1e-6
NEG_INF = -1e30
FINITE_MIN = -3e38

V7X_VMEM_BYTES = 64 * 1024 * 1024
VMEM_LIMIT_BYTES = V7X_VMEM_BYTES * 7 // 8

RET_CHUNK = 128
HEAD_W = 128
IN_TILE = 512
MIX_TILE = 256
FFN_TILE = 256
FFN_CHUNK = 256
DEC_GROUP = 16
NT_DIMS = (((1,), (1,)), ((), ()))
TN_DIMS = (((0,), (0,)), ((), ()))


def _rms(x, g):
    return x * lax.rsqrt(jnp.mean(x * x, axis=-1, keepdims=True) + EPS) * g


def _params(*sem):
    return pltpu.CompilerParams(dimension_semantics=sem, vmem_limit_bytes=VMEM_LIMIT_BYTES)


def _const_spec(shape):
    n = len(shape)
    return pl.BlockSpec(shape, lambda *_: (0,) * n, pipeline_mode=pl.Buffered(1))


def _ret_log_decay(h):
    return math.log(1.0 - 2.0 ** (-5.0 - h))


def _lambda(lam_ref, lam_i):
    a = jnp.sum(lam_ref[0:1, :] * lam_ref[1:2, :], axis=-1, keepdims=True)
    b = jnp.sum(lam_ref[2:3, :] * lam_ref[3:4, :], axis=-1, keepdims=True)
    return jnp.exp(a) - jnp.exp(b) + lam_i


def _inproj_body(x_ref, g_ref, w_ref, rq_ref, rk_ref, rv_ref, rg_ref, dq_ref, dk_ref, dv_ref,
                 dkb_ref, dvb_ref, *, width, k_scale, q_scale):
    h = _rms(x_ref[...], g_ref[...]).astype(BF16)

    def col(j):
        return jnp.dot(h, w_ref[:, j * width:(j + 1) * width], preferred_element_type=F32)

    rq_ref[...] = col(0)
    rk_ref[...] = col(1) * k_scale
    rv_ref[...] = col(2).astype(BF16)
    rg_ref[...] = col(3)
    dq_ref[...] = (col(4) * q_scale).astype(BF16)
    dk = col(5)
    dk_ref[...] = dk
    dkb_ref[...] = dk.astype(BF16)
    dv = col(6)
    dv_ref[...] = dv
    dvb_ref[...] = dv.astype(BF16)


def _inproj(x2, g, w_bf, tile, n_heads, dk_ret, dh_diff):
    m, d = x2.shape
    width = w_bf.shape[1] // 7
    tile = min(tile, m)
    row = lambda i: (i, 0)
    f32_out = jax.ShapeDtypeStruct((m, width), F32)
    bf_out = jax.ShapeDtypeStruct((m, width), BF16)
    spec = pl.BlockSpec((tile, width), row)
    body = functools.partial(_inproj_body, width=width, k_scale=dk_ret ** -0.5, q_scale=dh_diff ** -0.5)
    return pl.pallas_call(
        body,
        grid=(m // tile,),
        in_specs=[pl.BlockSpec((tile, d), row), _const_spec((1, d)), _const_spec(w_bf.shape)],
        out_specs=[spec] * 9,
        out_shape=[f32_out, f32_out, bf_out, f32_out, bf_out, f32_out, f32_out, bf_out, bf_out],
        compiler_params=_params("arbitrary"),
        name="inproj",
    )(x2, g, w_bf)


def _retention_consts(length):
    i = lax.broadcasted_iota(jnp.int32, (length, length), 0)
    j = lax.broadcasted_iota(jnp.int32, (length, length), 1)
    diff = (i - j).astype(F32)
    causal = diff >= 0
    ri = lax.broadcasted_iota(jnp.int32, (length, 1), 0).astype(F32)
    out = []
    for h in range(4):
        lg = _ret_log_decay(h)
        decay = jnp.where(causal, jnp.exp(jnp.where(causal, diff, 0.0) * lg), 0.0)
        qdec = jnp.exp((ri + 1.0) * lg)
        kdec = jnp.exp((length - 1.0 - ri) * lg)
        out.append((decay, qdec, kdec, math.exp(length * lg)))
    return out


def _gated_group_norm(o, g):
    on = o * lax.rsqrt(jnp.mean(o * o, axis=-1, keepdims=True) + EPS)
    return on * (g * jax.nn.sigmoid(g))


def _mixer_body(lam_ref, rq_ref, rk_ref, rv_ref, rg_ref, dq_ref, kb_ref, vb_ref, x_ref, wo_ref,
                gpost_ref, subg_ref, y_ref, s_ref, mix_scr, sc_scr, mx_scr, l_scr, acc_scr,
                *, tq, n_heads, lam_i):
    t = pl.program_id(1)
    ret_w = n_heads * HEAD_W

    @pl.when(t == 0)
    def _():
        s_ref[...] = jnp.zeros_like(s_ref)

    consts = _retention_consts(RET_CHUNK)
    for c in range(tq // RET_CHUNK):
        rows = slice(c * RET_CHUNK, (c + 1) * RET_CHUNK)
        for h in range(n_heads):
            cols = slice(h * HEAD_W, (h + 1) * HEAD_W)
            decay, qdec, kdec, gl = consts[h]
            q = rq_ref[rows, cols]
            k = rk_ref[rows, cols]
            v = rv_ref[rows, cols]
            s = s_ref[0, h]
            att = lax.dot_general(q.astype(BF16), k.astype(BF16), NT_DIMS,
                                  preferred_element_type=F32) * decay
            o = (jnp.dot(att.astype(BF16), v, preferred_element_type=F32)
                 + jnp.dot((q * qdec).astype(BF16), s.astype(BF16), preferred_element_type=F32))
            s_ref[0, h] = gl * s + lax.dot_general((k * kdec).astype(BF16), v, TN_DIMS,
                                                   preferred_element_type=F32)
            mix_scr[rows, cols] = _gated_group_norm(o, rg_ref[rows, cols]).astype(BF16)

    lam = _lambda(lam_ref, lam_i)
    ii = lax.broadcasted_iota(jnp.int32, (tq, tq), 0)
    jj = lax.broadcasted_iota(jnp.int32, (tq, tq), 1)
    rel = (ii - jj).astype(F32)
    lane = lax.broadcasted_iota(jnp.int32, (tq, HEAD_W), 1)
    subg = subg_ref[...]
    for h in range(n_heads):
        cols = slice(h * HEAD_W, (h + 1) * HEAD_W)
        slope = 2.0 ** (-8.0 / n_heads * (h + 1))
        qh = dq_ref[:, cols]
        zero = jnp.zeros_like(qh)
        qq = jnp.concatenate([jnp.where(lane < HEAD_W // 2, qh, zero),
                              jnp.where(lane >= HEAD_W // 2, qh, zero)], axis=0)

        mx_scr[...] = jnp.full(mx_scr.shape, FINITE_MIN, F32)
        l_scr[...] = jnp.zeros_like(l_scr)
        acc_scr[...] = jnp.zeros_like(acc_scr)

        def scores(kb, carry):
            kr = kb_ref[pl.ds(pl.multiple_of(kb * tq, tq), tq), cols]
            s = lax.dot_general(qq, kr, NT_DIMS, preferred_element_type=F32)
            dist = rel + ((t - kb) * tq).astype(F32)
            bias = jnp.where(dist >= 0, -slope * dist, NEG_INF)
            s1 = s[:tq] + bias
            s2 = s[tq:] + bias
            sc_scr[kb, 0] = s1
            sc_scr[kb, 1] = s2
            mx_scr[0] = jnp.maximum(mx_scr[0], s1)
            mx_scr[1] = jnp.maximum(mx_scr[1], s2)
            return carry

        lax.fori_loop(0, t + 1, scores, 0)
        m1 = jnp.broadcast_to(jnp.max(mx_scr[0], axis=-1, keepdims=True), (tq, tq))
        m2 = jnp.broadcast_to(jnp.max(mx_scr[1], axis=-1, keepdims=True), (tq, tq))

        def weighted(kb, carry):
            e1 = jnp.exp(sc_scr[kb, 0] - m1)
            e2 = jnp.exp(sc_scr[kb, 1] - m2)
            l_scr[0] += e1
            l_scr[1] += e2
            vr = vb_ref[pl.ds(pl.multiple_of(kb * tq, tq), tq), cols]
            e = jnp.concatenate([e1.astype(BF16), e2.astype(BF16)], axis=0)
            acc_scr[...] += jnp.dot(e, vr, preferred_element_type=F32)
            return carry

        lax.fori_loop(0, t + 1, weighted, 0)
        l1 = jnp.sum(l_scr[0], axis=-1, keepdims=True)
        l2 = jnp.sum(l_scr[1], axis=-1, keepdims=True)
        o = acc_scr[:tq] / l1 - lam * (acc_scr[tq:] / l2)
        o = _rms(o, subg) * (1.0 - lam_i)
        mix_scr[:, ret_w + h * HEAD_W:ret_w + (h + 1) * HEAD_W] = o.astype(BF16)

    mo = jnp.dot(mix_scr[...], wo_ref[...], preferred_element_type=F32)
    y_ref[...] = x_ref[...] + _rms(mo, gpost_ref[...])


def _mixer(lamv, rq, rk, rv, rg, dq, dkb, dvb, x2, wo_bf, gpost, subg, batch, seq, n_heads, lam_i):
    m, d = x2.shape
    w = rq.shape[1]
    tq = MIX_TILE
    nt = seq // tq
    tile = lambda b, t: (b * nt + t, 0)
    whole = lambda b, t: (b, 0)
    body = functools.partial(_mixer_body, tq=tq, n_heads=n_heads, lam_i=lam_i)
    return pl.pallas_call(
        body,
        grid=(batch, nt),
        in_specs=[
            _const_spec(lamv.shape),
            pl.BlockSpec((tq, w), tile), pl.BlockSpec((tq, w), tile), pl.BlockSpec((tq, w), tile),
            pl.BlockSpec((tq, w), tile), pl.BlockSpec((tq, w), tile),
            pl.BlockSpec((seq, w), whole), pl.BlockSpec((seq, w), whole),
            pl.BlockSpec((tq, d), tile),
            _const_spec(wo_bf.shape), _const_spec((1, d)), _const_spec((1, HEAD_W)),
        ],
        out_specs=[pl.BlockSpec((tq, d), tile),
                   pl.BlockSpec((1, n_heads, HEAD_W, HEAD_W), lambda b, t: (b, 0, 0, 0))],
        out_shape=[jax.ShapeDtypeStruct((m, d), F32),
                   jax.ShapeDtypeStruct((batch, n_heads, HEAD_W, HEAD_W), F32)],
        scratch_shapes=[
            pltpu.VMEM((tq, 2 * w), BF16),
            pltpu.VMEM((nt, 2, tq, tq), F32),
            pltpu.VMEM((2, tq, tq), F32),
            pltpu.VMEM((2, tq, tq), F32),
            pltpu.VMEM((2 * tq, HEAD_W), F32),
        ],
        compiler_params=_params("arbitrary", "arbitrary"),
        name="mixer",
    )(lamv, rq, rk, rv, rg, dq, dkb, dvb, x2, wo_bf, gpost, subg)


def _conv_gate(g, u, g1, g2, cw_ref, cb_ref, cols):
    c = cw_ref[0:1, cols] * g2 + cw_ref[1:2, cols] * g1 + cw_ref[2:3, cols] * g + cb_ref[:, cols]
    return (jax.nn.gelu(c) * u).astype(BF16)


def _ffn_body(x_ref, gpre_ref, win_ref, cw_ref, cb_ref, wout_ref, gpost_ref, y_ref, cs_ref,
              gbuf, carry, act, *, tf, d_ff, ck):
    t = pl.program_id(1)

    @pl.when(t == 0)
    def _():
        carry[...] = jnp.zeros_like(carry)

    x = x_ref[...]
    h = _rms(x, gpre_ref[...]).astype(BF16)
    for j in range(d_ff // ck):
        cols = slice(j * ck, (j + 1) * ck)
        g = jnp.dot(h, win_ref[:, cols], preferred_element_type=F32)
        u = jnp.dot(h, win_ref[:, d_ff + j * ck:d_ff + (j + 1) * ck], preferred_element_type=F32)
        gbuf[0:8, :] = carry[:, cols]
        gbuf[8:8 + tf, :] = g
        act[:, cols] = _conv_gate(g, u, gbuf[7:7 + tf, :], gbuf[6:6 + tf, :], cw_ref, cb_ref, cols)
        carry[:, cols] = gbuf[tf:tf + 8, :]
    f = jnp.dot(act[...], wout_ref[...], preferred_element_type=F32)
    y_ref[...] = x + _rms(f, gpost_ref[...])
    cs_ref[0] = carry[...]


def _ffn(x2, gpre, win_bf, cw, cb, wout_bf, gpost, batch, seq):
    m, d = x2.shape
    d_ff = wout_bf.shape[0]
    tf = FFN_TILE
    nt = seq // tf
    tile = lambda b, t: (b * nt + t, 0)
    body = functools.partial(_ffn_body, tf=tf, d_ff=d_ff, ck=FFN_CHUNK)
    return pl.pallas_call(
        body,
        grid=(batch, nt),
        in_specs=[pl.BlockSpec((tf, d), tile), _const_spec((1, d)), _const_spec(win_bf.shape),
                  _const_spec(cw.shape), _const_spec(cb.shape), _const_spec(wout_bf.shape),
                  _const_spec((1, d))],
        out_specs=[pl.BlockSpec((tf, d), tile), pl.BlockSpec((1, 8, d_ff), lambda b, t: (b, 0, 0))],
        out_shape=[jax.ShapeDtypeStruct((m, d), F32), jax.ShapeDtypeStruct((batch, 8, d_ff), F32)],
        scratch_shapes=[pltpu.VMEM((tf + 8, FFN_CHUNK), F32), pltpu.VMEM((8, d_ff), F32),
                        pltpu.VMEM((tf, d_ff), BF16)],
        compiler_params=_params("arbitrary", "arbitrary"),
        name="conv_ffn",
    )(x2, gpre, win_bf, cw, cb, wout_bf, gpost)


def _ret_dec_body(rq_ref, rk_ref, rv_ref, rg_ref, s_ref, a_ref, sn_ref, *, grp, n_heads):
    row = lax.broadcasted_iota(jnp.int32, (grp, grp * HEAD_W), 0)
    lane = lax.broadcasted_iota(jnp.int32, (grp, grp * HEAD_W), 1)
    own = (lane // HEAD_W) == row

    def block_diag(x):
        return jnp.where(own, jnp.tile(x, (1, grp)), 0.0).astype(BF16)

    for h in range(n_heads):
        cols = slice(h * HEAD_W, (h + 1) * HEAD_W)
        gamma = math.exp(_ret_log_decay(h))
        q = rq_ref[:, cols]
        k = rk_ref[:, cols]
        v = rv_ref[:, cols]
        qb = q.astype(BF16).astype(F32)
        kb = k.astype(BF16).astype(F32)
        att = jnp.sum(qb * kb, axis=-1, keepdims=True).astype(BF16).astype(F32)
        s = s_ref[:, h].reshape(grp * HEAD_W, HEAD_W)
        o = att * v.astype(F32) + jnp.dot(block_diag(q * gamma), s.astype(BF16),
                                          preferred_element_type=F32)
        upd = lax.dot_general(block_diag(k), v, TN_DIMS, preferred_element_type=F32)
        sn_ref[:, h] = (gamma * s + upd).reshape(grp, HEAD_W, HEAD_W)
        a_ref[:, cols] = _gated_group_norm(o, rg_ref[:, cols]).astype(BF16)


def _ret_decode(rq, rk, rv, rg, state, n_heads):
    n, w = rq.shape
    grp = DEC_GROUP
    row = lambda i: (i, 0)
    st = lambda i: (i, 0, 0, 0)
    spec = pl.BlockSpec((grp, w), row)
    st_spec = pl.BlockSpec((grp, n_heads, HEAD_W, HEAD_W), st)
    return pl.pallas_call(
        functools.partial(_ret_dec_body, grp=grp, n_heads=n_heads),
        grid=(n // grp,),
        in_specs=[spec, spec, spec, spec, st_spec],
        out_specs=[spec, st_spec],
        out_shape=[jax.ShapeDtypeStruct((n, w), BF16), jax.ShapeDtypeStruct(state.shape, F32)],
        compiler_params=_params("arbitrary"),
        name="ret_decode",
    )(rq, rk, rv, rg, state)


def _page_copies(pt_ref, ck_hbm, cv_hbm, kbuf, vbuf, sem, b, slot, n_pages):
    out = []
    for p in range(n_pages):
        page = pt_ref[b * n_pages + p]
        out.append(pltpu.make_async_copy(ck_hbm.at[page], kbuf.at[slot, p], sem.at[0, slot]))
        out.append(pltpu.make_async_copy(cv_hbm.at[page], vbuf.at[slot, p], sem.at[1, slot]))
    return out


def _paged_attn_body(pt_ref, lam_ref, q_ref, kn_ref, vn_ref, ck_hbm, cv_hbm, o_ref,
                     kbuf, vbuf, sem, sc_scr, *, n_pages, page, n_heads, lam_i):
    b = pl.program_id(0)
    nb = pl.num_programs(0)
    slot = b % 2
    w = n_heads * HEAD_W
    past = n_pages * page
    copies = functools.partial(_page_copies, pt_ref, ck_hbm, cv_hbm, kbuf, vbuf, sem, n_pages=n_pages)

    @pl.when(b == 0)
    def _():
        for c in copies(b=0, slot=0):
            c.start()

    @pl.when(b + 1 < nb)
    def _():
        for c in copies(b=b + 1, slot=1 - slot):
            c.start()

    for c in copies(b=b, slot=slot):
        c.wait()

    ci = lax.broadcasted_iota(jnp.int32, (HEAD_W, w), 0)
    ri = lax.broadcasted_iota(jnp.int32, (HEAD_W, w), 1)
    col_of = ri // HEAD_W + n_heads * ((ri % HEAD_W) // (HEAD_W // 2))
    q_sel = jnp.where(ci == col_of, jnp.broadcast_to(q_ref[0], (HEAD_W, w)), 0.0).astype(BF16)
    expand = jnp.where((ci < 2 * n_heads) & (ci % n_heads == ri // HEAD_W), 1.0, 0.0).astype(BF16)
    lane = lax.broadcasted_iota(jnp.int32, (1, HEAD_W), 1)
    slope = jnp.zeros((1, HEAD_W), F32)
    for h in range(n_heads):
        slope = jnp.where(lane % n_heads == h, 2.0 ** (-8.0 / n_heads * (h + 1)), slope)
    key_i = lax.broadcasted_iota(jnp.int32, (page, HEAD_W), 0)

    def scores(p, mx):
        kp = kbuf[slot, p].astype(BF16)
        dist = (past - (p * page + key_i)).astype(F32)
        s = lax.dot_general(kp, q_sel, NT_DIMS, preferred_element_type=F32) - slope * dist
        sc_scr[p] = s
        return jnp.maximum(mx, s)

    mx = lax.fori_loop(0, n_pages, scores, jnp.full((page, HEAD_W), FINITE_MIN, F32))
    kn = jnp.broadcast_to(kn_ref[0], (8, w)).astype(BF16)
    s_self = lax.dot_general(kn, q_sel, NT_DIMS, preferred_element_type=F32)[0:1]
    m = jnp.maximum(jnp.max(mx, axis=0, keepdims=True), s_self)

    def expsum(p, l):
        e = jnp.exp(sc_scr[p] - m)
        sc_scr[p] = e
        return l + e

    lp = lax.fori_loop(0, n_pages, expsum, jnp.zeros((page, HEAD_W), F32))
    e_self = jnp.exp(s_self - m)
    l = jnp.sum(lp, axis=0, keepdims=True) + e_self
    lam = _lambda(lam_ref, lam_i)
    r = jnp.where(lane < n_heads, 1.0, -lam) / l

    def weighted(p, acc):
        a = jnp.dot((sc_scr[p] * r).astype(BF16), expand, preferred_element_type=F32)
        prod = a * vbuf[slot, p]
        return acc + jnp.sum(prod.reshape(page // 8, 8, w), axis=0)

    acc = lax.fori_loop(0, n_pages, weighted, jnp.zeros((8, w), F32))
    a_self = jnp.dot(jnp.broadcast_to(e_self * r, (8, HEAD_W)).astype(BF16), expand,
                     preferred_element_type=F32)[0:1]
    o_ref[0] = jnp.sum(acc, axis=0, keepdims=True) + a_self * vn_ref[0]


def _paged_attn(page_table, lamv, dq, dk, dv, cache_k, cache_v, n_heads, lam_i):
    n, w = dq.shape
    n_pages = page_table.shape[1]
    n_pool, page = cache_k.shape[0], cache_k.shape[1]
    ck = cache_k.reshape(n_pool, page, w)
    cv = cache_v.reshape(n_pool, page, w)
    row = lambda b, pt: (b, 0, 0)
    rspec = pl.BlockSpec((1, 1, w), row)
    body = functools.partial(_paged_attn_body, n_pages=n_pages, page=page, n_heads=n_heads, lam_i=lam_i)
    grid_spec = pltpu.PrefetchScalarGridSpec(
        num_scalar_prefetch=1,
        grid=(n,),
        in_specs=[pl.BlockSpec(lamv.shape, lambda b, pt: (0, 0)), rspec, rspec, rspec,
                  pl.BlockSpec(memory_space=pl.ANY), pl.BlockSpec(memory_space=pl.ANY)],
        out_specs=rspec,
        scratch_shapes=[pltpu.VMEM((2, n_pages, page, w), F32), pltpu.VMEM((2, n_pages, page, w), F32),
                        pltpu.SemaphoreType.DMA((2, 2)), pltpu.VMEM((n_pages, page, HEAD_W), F32)],
    )
    out = pl.pallas_call(
        body,
        grid_spec=grid_spec,
        out_shape=jax.ShapeDtypeStruct((n, 1, w), F32),
        compiler_params=_params("arbitrary"),
        name="paged_attn",
    )(page_table.reshape(-1), lamv, dq.astype(F32).reshape(n, 1, w), dk.reshape(n, 1, w),
      dv.reshape(n, 1, w), ck, cv)
    return out.reshape(n, w)


def _dec_ffn_body(a_ref, do_ref, x_ref, wo_ref, gmix_ref, subg_ref, gpre_ref, wg_ref, wu_ref, cw_ref,
                  cb_ref, sc_ref, wout_ref, gpost_ref, y_ref, g_ref, xmid, hbuf, acc, *, lam_i):
    j = pl.program_id(0)
    ret_w = a_ref.shape[1]

    @pl.when(j == 0)
    def _():
        mo = jnp.dot(a_ref[...], wo_ref[0:ret_w, :], preferred_element_type=F32)
        for h in range(do_ref.shape[1] // HEAD_W):
            cols = slice(h * HEAD_W, (h + 1) * HEAD_W)
            bh = (_rms(do_ref[:, cols], subg_ref[...]) * (1.0 - lam_i)).astype(BF16)
            mo += jnp.dot(bh, wo_ref[ret_w + h * HEAD_W:ret_w + (h + 1) * HEAD_W, :],
                          preferred_element_type=F32)
        xm = x_ref[...] + _rms(mo, gmix_ref[...])
        xmid[...] = xm
        hbuf[...] = _rms(xm, gpre_ref[...]).astype(BF16)
        acc[...] = jnp.zeros_like(acc)

    h = hbuf[...]
    g = jnp.dot(h, wg_ref[...], preferred_element_type=F32)
    u = jnp.dot(h, wu_ref[...], preferred_element_type=F32)
    g_ref[...] = g
    c = cw_ref[0:1, :] * sc_ref[0] + cw_ref[1:2, :] * sc_ref[1] + cw_ref[2:3, :] * g + cb_ref[...]
    acc[...] += jnp.dot((jax.nn.gelu(c) * u).astype(BF16), wout_ref[...], preferred_element_type=F32)

    @pl.when(j == pl.num_programs(0) - 1)
    def _():
        y_ref[...] = xmid[...] + _rms(acc[...], gpost_ref[...])


def _dec_ffn(a, do, x2, wo_bf, gmix, subg, gpre, win_bf, cw, cb, conv_state_t, wout_bf, gpost, lam_i):
    n, d = x2.shape
    d_ff = wout_bf.shape[0]
    ck = FFN_CHUNK
    nj = d_ff // ck
    full = lambda shape: pl.BlockSpec(shape, lambda j: (0,) * len(shape))
    return pl.pallas_call(
        functools.partial(_dec_ffn_body, lam_i=lam_i),
        grid=(nj,),
        in_specs=[full(a.shape), full(do.shape), full(x2.shape), full(wo_bf.shape), full((1, d)),
                  full((1, HEAD_W)), full((1, d)),
                  pl.BlockSpec((d, ck), lambda j: (0, j)), pl.BlockSpec((d, ck), lambda j: (0, nj + j)),
                  pl.BlockSpec((3, ck), lambda j: (0, j)), pl.BlockSpec((1, ck), lambda j: (0, j)),
                  pl.BlockSpec((2, n, ck), lambda j: (0, 0, j)),
                  pl.BlockSpec((ck, d), lambda j: (j, 0)), full((1, d))],
        out_specs=[full((n, d)), pl.BlockSpec((n, ck), lambda j: (0, j))],
        out_shape=[jax.ShapeDtypeStruct((n, d), F32), jax.ShapeDtypeStruct((n, d_ff), F32)],
        scratch_shapes=[pltpu.VMEM((n, d), F32), pltpu.VMEM((n, d), BF16), pltpu.VMEM((n, d), F32)],
        compiler_params=_params("arbitrary"),
        name="dec_ffn",
    )(a, do, x2, wo_bf, gmix, subg, gpre, win_bf, win_bf, cw, cb, conv_state_t, wout_bf, gpost)


def kernel(x_prompt, x_sample, state_ret, cache_k, cache_v, state_conv, page_table,
           norm_mix_pre, norm_mix_post, w_in, w_o, lambda_q1, lambda_k1, lambda_q2, lambda_k2,
           subln_g, norm_ffn_pre, norm_ffn_post, w_ffn_in, conv_w, conv_b, w_ffn_out):
    batch, seq, d = x_prompt.shape
    n_dec = x_sample.shape[0]
    assert x_sample.shape[1] == 1, "the sample group is one token per row"
    depth = w_in.shape[0]
    n_heads = state_ret.shape[2]
    dk_ret = state_ret.shape[3]
    dh_diff = cache_k.shape[-1]
    d_ff = w_ffn_out.shape[1]
    w = n_heads * HEAD_W
    assert w_in.shape[2] == 7 * w and cache_v.shape[-1] == HEAD_W and 2 * dh_diff == HEAD_W
    assert seq % MIX_TILE == 0 and seq % FFN_TILE == 0 and d_ff % FFN_CHUNK == 0
    assert n_dec % DEC_GROUP == 0 and cache_k.shape[2] % 8 == 0

    xp = x_prompt.reshape(batch * seq, d)
    xs = x_sample.reshape(n_dec, d)
    outs = [[] for _ in range(8)]
    for l in range(depth):
        lam_i = 0.8 - 0.6 * math.exp(-0.3 * l)
        lamv = jnp.stack([lambda_q1[l], lambda_k1[l], lambda_q2[l], lambda_k2[l]]).astype(F32)
        row = lambda v: v.reshape(1, -1)
        w_in_bf = w_in[l].astype(BF16)
        w_o_bf = w_o[l].astype(BF16)
        w_ffn_in_bf = w_ffn_in[l].astype(BF16)
        w_ffn_out_bf = w_ffn_out[l].astype(BF16)
        g_pre, g_post = row(norm_mix_pre[l]), row(norm_mix_post[l])
        f_pre, f_post = row(norm_ffn_pre[l]), row(norm_ffn_post[l])
        subg = row(subln_g[l])
        cw, cb = conv_w[l], row(conv_b[l])

        rq, rk, rv, rg, dq, dk, dv, dkb, dvb = _inproj(xp, g_pre, w_in_bf, IN_TILE, n_heads, dk_ret, dh_diff)
        xp, s_fin = _mixer(lamv, rq, rk, rv, rg, dq, dkb, dvb, xp, w_o_bf, g_post, subg,
                           batch, seq, n_heads, lam_i)
        xp, cs = _ffn(xp, f_pre, w_ffn_in_bf, cw, cb, w_ffn_out_bf, f_post, batch, seq)
        outs[0].append(s_fin)
        outs[2].append(dk.reshape(batch, seq, n_heads, 2, dh_diff))
        outs[3].append(dv.reshape(batch, seq, n_heads, HEAD_W))
        outs[6].append(cs[:, 6:8, :])

        rq, rk, rv, rg, dq, dk, dv, _, _ = _inproj(xs, g_pre, w_in_bf, IN_TILE, n_heads, dk_ret, dh_diff)
        a, s_new = _ret_decode(rq, rk, rv, rg, state_ret[l], n_heads)
        do = _paged_attn(page_table, lamv, dq, dk, dv, cache_k[l], cache_v[l], n_heads, lam_i)
        conv_t = jnp.swapaxes(state_conv[l], 0, 1)
        xs, g_new = _dec_ffn(a, do, xs, w_o_bf, g_post, subg, f_pre, w_ffn_in_bf, cw, cb, conv_t,
                             w_ffn_out_bf, f_post, lam_i)
        outs[1].append(s_new)
        outs[4].append(dk.reshape(n_dec, 1, n_heads, 2, dh_diff))
        outs[5].append(dv.reshape(n_dec, 1, n_heads, HEAD_W))
        outs[7].append(jnp.stack([state_conv[l][:, 1, :], g_new], axis=1))

    st = [jnp.stack(o) for o in outs]
    return (xp.reshape(batch, seq, d), xs.reshape(n_dec, 1, d), st[0], st[1], st[2], st[3], st[4], st[5],
            st[6], st[7])
```

```python
import functools
import math

import jax
import jax.numpy as jnp
from jax import lax
from jax.experimental import pallas as pl
from jax.experimental.pallas import tpu as pltpu

F32 = jnp.float32
BF16 = jnp.bfloat16
EPS = 1e-6
NEG_INF = -1e30
FINITE_MIN = -3e38

V7X_VMEM_BYTES = 64 * 1024 * 1024
VMEM_LIMIT_BYTES = V7X_VMEM_BYTES * 7 // 8

RET_CHUNK = 128
HEAD_W = 128
IN_TILE = 512
MIX_TILE = 256
FFN_TILE = 256
FFN_CHUNK = 256
DEC_GROUP = 16
NT_DIMS = (((1,), (1,)), ((), ()))
TN_DIMS = (((0,), (0,)), ((), ()))


def _rms(x, g):
    return x * lax.rsqrt(jnp.mean(x * x, axis=-1, keepdims=True) + EPS) * g


def _params(*sem):
    return pltpu.CompilerParams(dimension_semantics=sem, vmem_limit_bytes=VMEM_LIMIT_BYTES)


def _const_spec(shape):
    n = len(shape)
    return pl.BlockSpec(shape, lambda *_: (0,) * n, pipeline_mode=pl.Buffered(1))


def _ret_log_decay(h):
    return math.log(1.0 - 2.0 ** (-5.0 - h))


def _lambda(lam_ref, lam_i):
    a = jnp.sum(lam_ref[0:1, :] * lam_ref[1:2, :], axis=-1, keepdims=True)
    b = jnp.sum(lam_ref[2:3, :] * lam_ref[3:4, :], axis=-1, keepdims=True)
    return jnp.exp(a) - jnp.exp(b) + lam_i


def _inproj_body(x_ref, g_ref, w_ref, wkt_ref, rq_ref, rk_ref, rv_ref, rg_ref, dq_ref, kt_ref, v4_ref,
                 kx_ref, vx_ref, *, width, n_heads, k_scale, q_scale, key_block):
    h = _rms(x_ref[...], g_ref[...]).astype(BF16)

    def col(j):
        return jnp.dot(h, w_ref[:, j * width:(j + 1) * width], preferred_element_type=F32)

    rq_ref[...] = col(0)
    rk_ref[...] = col(1) * k_scale
    rv_ref[...] = col(2).astype(BF16)
    rg_ref[...] = col(3)
    dq_ref[...] = (col(4) * q_scale).astype(BF16)
    kt = lax.dot_general(wkt_ref[...], h, NT_DIMS, preferred_element_type=F32)
    kt_ref[0] = kt
    dv = col(6)
    for hh in range(n_heads):
        v4_ref[:, hh, :] = dv[:, hh * HEAD_W:(hh + 1) * HEAD_W]
    if key_block is None:
        kx_ref[...] = col(5)
        vx_ref[...] = dv
    else:
        for c in range(kx_ref.shape[0]):
            kx_ref[c] = kt[:, c * key_block:(c + 1) * key_block].astype(BF16)
        vx_ref[...] = dv.astype(BF16)


def _inproj(x2, g, w_bf, wkt_bf, batch, seq, tile, n_heads, dk_ret, dh_diff, key_block):
    m, d = x2.shape
    width = wkt_bf.shape[0]
    nt = seq // tile
    row = lambda i: (i, 0)
    f32_out = jax.ShapeDtypeStruct((m, width), F32)
    bf_out = jax.ShapeDtypeStruct((m, width), BF16)
    spec = pl.BlockSpec((tile, width), row)
    if key_block is None:
        kx_shape, kx_spec, vx_shape = f32_out, spec, f32_out
    else:
        kx_shape = jax.ShapeDtypeStruct((m // key_block, width, key_block), BF16)
        kx_spec = pl.BlockSpec((tile // key_block, width, key_block), lambda i: (i, 0, 0))
        vx_shape = bf_out
    body = functools.partial(_inproj_body, width=width, n_heads=n_heads, k_scale=dk_ret ** -0.5,
                             q_scale=dh_diff ** -0.5, key_block=key_block)
    return pl.pallas_call(
        body,
        grid=(m // tile,),
        in_specs=[pl.BlockSpec((tile, d), row), _const_spec((1, d)), _const_spec(w_bf.shape),
                  _const_spec(wkt_bf.shape)],
        out_specs=[spec] * 5 + [pl.BlockSpec((1, width, tile), lambda i: (i // nt, 0, i % nt)),
                                pl.BlockSpec((tile, n_heads, HEAD_W), lambda i: (i, 0, 0)), kx_spec, spec],
        out_shape=[f32_out, f32_out, bf_out, f32_out, bf_out,
                   jax.ShapeDtypeStruct((batch, width, seq), F32),
                   jax.ShapeDtypeStruct((m, n_heads, HEAD_W), F32), kx_shape, vx_shape],
        compiler_params=_params("arbitrary"),
        name="inproj",
    )(x2, g, w_bf, wkt_bf)


def _retention_consts(length, n_heads):
    i = lax.broadcasted_iota(jnp.int32, (length, length), 0)
    j = lax.broadcasted_iota(jnp.int32, (length, length), 1)
    diff = (i - j).astype(F32)
    causal = diff >= 0
    ri = lax.broadcasted_iota(jnp.int32, (length, 1), 0).astype(F32)
    out = []
    for h in range(n_heads):
        lg = _ret_log_decay(h)
        decay = jnp.where(causal, jnp.exp(jnp.where(causal, diff, 0.0) * lg), 0.0)
        qdec = jnp.exp((ri + 1.0) * lg)
        kdec = jnp.exp((length - 1.0 - ri) * lg)
        out.append((decay, qdec, kdec, math.exp(length * lg)))
    return out


def _gated_group_norm(o, g):
    on = o * lax.rsqrt(jnp.mean(o * o, axis=-1, keepdims=True) + EPS)
    return on * (g * jax.nn.sigmoid(g))


def _mixer_body(lam_ref, rq_ref, rk_ref, rv_ref, rg_ref, dq_ref, kt_ref, vb_ref, x_ref, wo_ref,
                gpost_ref, subg_ref, y_ref, s_ref, mix_scr, sc_scr, mx_scr, l_scr, acc_scr,
                *, tq, n_heads, lam_i):
    t = pl.program_id(1)
    ret_w = n_heads * HEAD_W

    @pl.when(t == 0)
    def _():
        s_ref[...] = jnp.zeros_like(s_ref)

    consts = _retention_consts(RET_CHUNK, n_heads)
    for c in range(tq // RET_CHUNK):
        rows = slice(c * RET_CHUNK, (c + 1) * RET_CHUNK)
        for h in range(n_heads):
            cols = slice(h * HEAD_W, (h + 1) * HEAD_W)
            decay, qdec, kdec, gl = consts[h]
            q = rq_ref[rows, cols]
            k = rk_ref[rows, cols]
            v = rv_ref[rows, cols]
            s = s_ref[0, h]
            att = lax.dot_general(q.astype(BF16), k.astype(BF16), NT_DIMS,
                                  preferred_element_type=F32) * decay
            o = (jnp.dot(att.astype(BF16), v, preferred_element_type=F32)
                 + jnp.dot((q * qdec).astype(BF16), s.astype(BF16), preferred_element_type=F32))
            s_ref[0, h] = gl * s + lax.dot_general((k * kdec).astype(BF16), v, TN_DIMS,
                                                   preferred_element_type=F32)
            mix_scr[rows, cols] = _gated_group_norm(o, rg_ref[rows, cols]).astype(BF16)

    lam = _lambda(lam_ref, lam_i)
    ii = lax.broadcasted_iota(jnp.int32, (tq, tq), 0)
    jj = lax.broadcasted_iota(jnp.int32, (tq, tq), 1)
    rel = (ii - jj).astype(F32)
    lane = lax.broadcasted_iota(jnp.int32, (tq, HEAD_W), 1)
    subg = subg_ref[...]
    for h in range(n_heads):
        cols = slice(h * HEAD_W, (h + 1) * HEAD_W)
        slope = 2.0 ** (-8.0 / n_heads * (h + 1))
        qh = dq_ref[:, cols]
        zero = jnp.zeros_like(qh)
        qq = jnp.concatenate([jnp.where(lane < HEAD_W // 2, qh, zero),
                              jnp.where(lane >= HEAD_W // 2, qh, zero)], axis=0)

        mx_scr[...] = jnp.full(mx_scr.shape, FINITE_MIN, F32)
        l_scr[...] = jnp.zeros_like(l_scr)
        acc_scr[...] = jnp.zeros_like(acc_scr)

        def scores(kb, carry):
            s = jnp.dot(qq, kt_ref[kb, cols, :], preferred_element_type=F32)
            dist = rel + ((t - kb) * tq).astype(F32)
            bias = jnp.where(dist >= 0, -slope * dist, NEG_INF)
            s1 = s[:tq] + bias
            s2 = s[tq:] + bias
            sc_scr[kb, 0] = s1
            sc_scr[kb, 1] = s2
            mx_scr[0] = jnp.maximum(mx_scr[0], s1)
            mx_scr[1] = jnp.maximum(mx_scr[1], s2)
            return carry

        lax.fori_loop(0, t + 1, scores, 0)
        m1 = jnp.broadcast_to(jnp.max(mx_scr[0], axis=-1, keepdims=True), (tq, tq))
        m2 = jnp.broadcast_to(jnp.max(mx_scr[1], axis=-1, keepdims=True), (tq, tq))

        def weighted(kb, carry):
            e1 = jnp.exp(sc_scr[kb, 0] - m1)
            e2 = jnp.exp(sc_scr[kb, 1] - m2)
            l_scr[0] += e1
            l_scr[1] += e2
            vr = vb_ref[pl.ds(pl.multiple_of(kb * tq, tq), tq), cols]
            e = jnp.concatenate([e1.astype(BF16), e2.astype(BF16)], axis=0)
            acc_scr[...] += jnp.dot(e, vr, preferred_element_type=F32)
            return carry

        lax.fori_loop(0, t + 1, weighted, 0)
        l1 = jnp.sum(l_scr[0], axis=-1, keepdims=True)
        l2 = jnp.sum(l_scr[1], axis=-1, keepdims=True)
        o = acc_scr[:tq] / l1 - lam * (acc_scr[tq:] / l2)
        o = _rms(o, subg) * (1.0 - lam_i)
        mix_scr[:, ret_w + h * HEAD_W:ret_w + (h + 1) * HEAD_W] = o.astype(BF16)

    mo = jnp.dot(mix_scr[...], wo_ref[...], preferred_element_type=F32)
    y_ref[...] = x_ref[...] + _rms(mo, gpost_ref[...])


def _mixer(lamv, rq, rk, rv, rg, dq, ktb, dvb, x2, wo_bf, gpost, subg, batch, seq, n_heads, lam_i):
    m, d = x2.shape
    w = rq.shape[1]
    tq = MIX_TILE
    nt = seq // tq
    tile = lambda b, t: (b * nt + t, 0)
    whole = lambda b, t: (b, 0)
    body = functools.partial(_mixer_body, tq=tq, n_heads=n_heads, lam_i=lam_i)
    return pl.pallas_call(
        body,
        grid=(batch, nt),
        in_specs=[
            _const_spec(lamv.shape),
            pl.BlockSpec((tq, w), tile), pl.BlockSpec((tq, w), tile), pl.BlockSpec((tq, w), tile),
            pl.BlockSpec((tq, w), tile), pl.BlockSpec((tq, w), tile),
            pl.BlockSpec((nt, w, tq), lambda b, t: (b, 0, 0)), pl.BlockSpec((seq, w), whole),
            pl.BlockSpec((tq, d), tile),
            _const_spec(wo_bf.shape), _const_spec((1, d)), _const_spec((1, HEAD_W)),
        ],
        out_specs=[pl.BlockSpec((tq, d), tile),
                   pl.BlockSpec((1, n_heads, HEAD_W, HEAD_W), lambda b, t: (b, 0, 0, 0))],
        out_shape=[jax.ShapeDtypeStruct((m, d), F32),
                   jax.ShapeDtypeStruct((batch, n_heads, HEAD_W, HEAD_W), F32)],
        scratch_shapes=[
            pltpu.VMEM((tq, 2 * w), BF16),
            pltpu.VMEM((nt, 2, tq, tq), F32),
            pltpu.VMEM((2, tq, tq), F32),
            pltpu.VMEM((2, tq, tq), F32),
            pltpu.VMEM((2 * tq, HEAD_W), F32),
        ],
        compiler_params=_params("arbitrary", "arbitrary"),
        name="mixer",
    )(lamv, rq, rk, rv, rg, dq, ktb, dvb, x2, wo_bf, gpost, subg)


def _conv_gate(g, u, g1, g2, cw_ref, cb_ref, cols):
    c = cw_ref[0:1, cols] * g2 + cw_ref[1:2, cols] * g1 + cw_ref[2:3, cols] * g + cb_ref[:, cols]
    return (jax.nn.gelu(c) * u).astype(BF16)


def _ffn_body(x_ref, gpre_ref, win_ref, cw_ref, cb_ref, wout_ref, gpost_ref, y_ref, cs_ref,
              gbuf, carry, act, *, tf, d_ff, ck):
    t = pl.program_id(1)

    @pl.when(t == 0)
    def _():
        carry[...] = jnp.zeros_like(carry)

    x = x_ref[...]
    h = _rms(x, gpre_ref[...]).astype(BF16)
    for j in range(d_ff // ck):
        cols = slice(j * ck, (j + 1) * ck)
        g = jnp.dot(h, win_ref[:, cols], preferred_element_type=F32)
        u = jnp.dot(h, win_ref[:, d_ff + j * ck:d_ff + (j + 1) * ck], preferred_element_type=F32)
        gbuf[0:8, :] = carry[:, cols]
        gbuf[8:8 + tf, :] = g
        act[:, cols] = _conv_gate(g, u, gbuf[7:7 + tf, :], gbuf[6:6 + tf, :], cw_ref, cb_ref, cols)
        carry[:, cols] = gbuf[tf:tf + 8, :]
    f = jnp.dot(act[...], wout_ref[...], preferred_element_type=F32)
    y_ref[...] = x + _rms(f, gpost_ref[...])
    cs_ref[0] = carry[...]


def _ffn(x2, gpre, win_bf, cw, cb, wout_bf, gpost, batch, seq):
    m, d = x2.shape
    d_ff = wout_bf.shape[0]
    tf = FFN_TILE
    nt = seq // tf
    tile = lambda b, t: (b * nt + t, 0)
    body = functools.partial(_ffn_body, tf=tf, d_ff=d_ff, ck=FFN_CHUNK)
    return pl.pallas_call(
        body,
        grid=(batch, nt),
        in_specs=[pl.BlockSpec((tf, d), tile), _const_spec((1, d)), _const_spec(win_bf.shape),
                  _const_spec(cw.shape), _const_spec(cb.shape), _const_spec(wout_bf.shape),
                  _const_spec((1, d))],
        out_specs=[pl.BlockSpec((tf, d), tile), pl.BlockSpec((1, 8, d_ff), lambda b, t: (b, 0, 0))],
        out_shape=[jax.ShapeDtypeStruct((m, d), F32), jax.ShapeDtypeStruct((batch, 8, d_ff), F32)],
        scratch_shapes=[pltpu.VMEM((tf + 8, FFN_CHUNK), F32), pltpu.VMEM((8, d_ff), F32),
                        pltpu.VMEM((tf, d_ff), BF16)],
        compiler_params=_params("arbitrary", "arbitrary"),
        name="conv_ffn",
    )(x2, gpre, win_bf, cw, cb, wout_bf, gpost)


def _ret_dec_body(rq_ref, rk_ref, rv_ref, rg_ref, s_ref, a_ref, sn_ref, *, grp, n_heads):
    row = lax.broadcasted_iota(jnp.int32, (grp, grp * HEAD_W), 0)
    lane = lax.broadcasted_iota(jnp.int32, (grp, grp * HEAD_W), 1)
    own = (lane // HEAD_W) == row

    def block_diag(x):
        return jnp.where(own, jnp.tile(x, (1, grp)), 0.0).astype(BF16)

    for h in range(n_heads):
        cols = slice(h * HEAD_W, (h + 1) * HEAD_W)
        gamma = math.exp(_ret_log_decay(h))
        q = rq_ref[:, cols]
        k = rk_ref[:, cols]
        v = rv_ref[:, cols]
        qb = q.astype(BF16).astype(F32)
        kb = k.astype(BF16).astype(F32)
        att = jnp.sum(qb * kb, axis=-1, keepdims=True).astype(BF16).astype(F32)
        s = s_ref[:, h].reshape(grp * HEAD_W, HEAD_W)
        o = att * v.astype(F32) + jnp.dot(block_diag(q * gamma), s.astype(BF16),
                                          preferred_element_type=F32)
        upd = lax.dot_general(block_diag(k), v, TN_DIMS, preferred_element_type=F32)
        sn_ref[:, h] = (gamma * s + upd).reshape(grp, HEAD_W, HEAD_W)
        a_ref[:, cols] = _gated_group_norm(o, rg_ref[:, cols]).astype(BF16)


def _ret_decode(rq, rk, rv, rg, state, n_heads):
    n, w = rq.shape
    grp = DEC_GROUP
    row = lambda i: (i, 0)
    st = lambda i: (i, 0, 0, 0)
    spec = pl.BlockSpec((grp, w), row)
    st_spec = pl.BlockSpec((grp, n_heads, HEAD_W, HEAD_W), st)
    return pl.pallas_call(
        functools.partial(_ret_dec_body, grp=grp, n_heads=n_heads),
        grid=(n // grp,),
        in_specs=[spec, spec, spec, spec, st_spec],
        out_specs=[spec, st_spec],
        out_shape=[jax.ShapeDtypeStruct((n, w), BF16), jax.ShapeDtypeStruct(state.shape, F32)],
        compiler_params=_params("arbitrary"),
        name="ret_decode",
    )(rq, rk, rv, rg, state)


def _page_copies(pt_ref, ck_hbm, cv_hbm, kbuf, vbuf, sem, b, slot, n_pages):
    out = []
    for p in range(n_pages):
        page = pt_ref[b * n_pages + p]
        out.append(pltpu.make_async_copy(ck_hbm.at[page], kbuf.at[slot, p], sem.at[0, slot]))
        out.append(pltpu.make_async_copy(cv_hbm.at[page], vbuf.at[slot, p], sem.at[1, slot]))
    return out


def _paged_attn_body(pt_ref, lam_ref, q_ref, kn_ref, vn_ref, ck_hbm, cv_hbm, o_ref,
                     kbuf, vbuf, sem, sc_scr, *, n_pages, page, n_heads, lam_i):
    b = pl.program_id(0)
    nb = pl.num_programs(0)
    slot = b % 2
    w = n_heads * HEAD_W
    nc = 2 * n_heads
    past = n_pages * page
    copies = functools.partial(_page_copies, pt_ref, ck_hbm, cv_hbm, kbuf, vbuf, sem, n_pages=n_pages)

    @pl.when(b == 0)
    def _():
        for c in copies(b=0, slot=0):
            c.start()

    @pl.when(b + 1 < nb)
    def _():
        for c in copies(b=b + 1, slot=1 - slot):
            c.start()

    for c in copies(b=b, slot=slot):
        c.wait()

    ci = lax.broadcasted_iota(jnp.int32, (nc, w), 0)
    ri = lax.broadcasted_iota(jnp.int32, (nc, w), 1)
    col_of = ri // HEAD_W + n_heads * ((ri % HEAD_W) // (HEAD_W // 2))
    q_sel = jnp.where(ci == col_of, jnp.broadcast_to(q_ref[0], (nc, w)), 0.0)
    q_sel_bf = q_sel.astype(BF16)
    rowc = lax.broadcasted_iota(jnp.int32, (nc, page), 0)
    key_i = lax.broadcasted_iota(jnp.int32, (nc, page), 1)
    slope = jnp.zeros((nc, page), F32)
    for h in range(n_heads):
        slope = jnp.where(rowc % n_heads == h, 2.0 ** (-8.0 / n_heads * (h + 1)), slope)

    def scores(p, mx):
        dist = (past - (p * page + key_i)).astype(F32)
        s = jnp.dot(q_sel_bf, kbuf[slot, p].astype(BF16), preferred_element_type=F32) - slope * dist
        sc_scr[p] = s
        return jnp.maximum(mx, s)

    mx = lax.fori_loop(0, n_pages, scores, jnp.full((nc, page), FINITE_MIN, F32))
    kn = kn_ref[0].astype(BF16).astype(F32)
    s_self = jnp.broadcast_to(jnp.sum(q_sel * kn, axis=-1, keepdims=True), (nc, page))
    m = jnp.maximum(jnp.broadcast_to(jnp.max(mx, axis=-1, keepdims=True), (nc, page)), s_self)

    def expsum(p, l):
        e = jnp.exp(sc_scr[p] - m)
        sc_scr[p] = e
        return l + e

    lp = lax.fori_loop(0, n_pages, expsum, jnp.zeros((nc, page), F32))
    e_self = jnp.exp(s_self - m)
    l = jnp.broadcast_to(jnp.sum(lp, axis=-1, keepdims=True), (nc, page)) + e_self
    r = jnp.where(rowc < n_heads, 1.0, _lambda(lam_ref, lam_i)) / l

    def combine(pn):
        return (pn - pltpu.roll(pn, n_heads, axis=0)).astype(BF16)

    def weighted(p, accs):
        a = combine(sc_scr[p] * r)
        return tuple(acc + jnp.dot(a, vbuf[slot, p, :, h, :].astype(BF16), preferred_element_type=F32)
                     for h, acc in enumerate(accs))

    accs = lax.fori_loop(0, n_pages, weighted, tuple(jnp.zeros((nc, HEAD_W), F32) for _ in range(n_heads)))
    a_self = combine(e_self * r).astype(F32)
    vn = vn_ref[0].astype(BF16).astype(F32)
    o_ref[0] = jnp.concatenate(
        [accs[h][h:h + 1, :] + a_self[h:h + 1, 0:HEAD_W] * vn[:, h * HEAD_W:(h + 1) * HEAD_W]
         for h in range(n_heads)], axis=-1)


def _paged_attn(page_table, lamv, dq, dk, dv, cache_kt, cache_v, n_heads, lam_i):
    n, w = dq.shape
    n_pages = page_table.shape[1]
    page = cache_v.shape[1]
    row = lambda b, pt: (b, 0, 0)
    rspec = pl.BlockSpec((1, 1, w), row)
    body = functools.partial(_paged_attn_body, n_pages=n_pages, page=page, n_heads=n_heads, lam_i=lam_i)
    grid_spec = pltpu.PrefetchScalarGridSpec(
        num_scalar_prefetch=1,
        grid=(n,),
        in_specs=[pl.BlockSpec(lamv.shape, lambda b, pt: (0, 0)), rspec, rspec, rspec,
                  pl.BlockSpec(memory_space=pl.ANY), pl.BlockSpec(memory_space=pl.ANY)],
        out_specs=rspec,
        scratch_shapes=[pltpu.VMEM((2, n_pages, w, page), F32),
                        pltpu.VMEM((2, n_pages, page, n_heads, HEAD_W), F32),
                        pltpu.SemaphoreType.DMA((2, 2)), pltpu.VMEM((n_pages, 2 * n_heads, page), F32)],
    )
    out = pl.pallas_call(
        body,
        grid_spec=grid_spec,
        out_shape=jax.ShapeDtypeStruct((n, 1, w), F32),
        compiler_params=_params("arbitrary"),
        name="paged_attn",
    )(page_table.reshape(-1), lamv, dq.astype(F32).reshape(n, 1, w), dk.reshape(n, 1, w),
      dv.reshape(n, 1, w), cache_kt, cache_v)
    return out.reshape(n, w)


def _dec_ffn_body(a_ref, do_ref, x_ref, wo_ref, gmix_ref, subg_ref, gpre_ref, wg_ref, wu_ref, cw_ref,
                  cb_ref, sc_ref, wout_ref, gpost_ref, y_ref, g_ref, xmid, hbuf, acc, *, lam_i):
    j = pl.program_id(0)
    ret_w = a_ref.shape[1]

    @pl.when(j == 0)
    def _():
        mo = jnp.dot(a_ref[...], wo_ref[0:ret_w, :], preferred_element_type=F32)
        for h in range(do_ref.shape[1] // HEAD_W):
            cols = slice(h * HEAD_W, (h + 1) * HEAD_W)
            bh = (_rms(do_ref[:, cols], subg_ref[...]) * (1.0 - lam_i)).astype(BF16)
            mo += jnp.dot(bh, wo_ref[ret_w + h * HEAD_W:ret_w + (h + 1) * HEAD_W, :],
                          preferred_element_type=F32)
        xm = x_ref[...] + _rms(mo, gmix_ref[...])
        xmid[...] = xm
        hbuf[...] = _rms(xm, gpre_ref[...]).astype(BF16)
        acc[...] = jnp.zeros_like(acc)

    h = hbuf[...]
    g = jnp.dot(h, wg_ref[...], preferred_element_type=F32)
    u = jnp.dot(h, wu_ref[...], preferred_element_type=F32)
    g_ref[...] = g
    c = cw_ref[0:1, :] * sc_ref[0] + cw_ref[1:2, :] * sc_ref[1] + cw_ref[2:3, :] * g + cb_ref[...]
    acc[...] += jnp.dot((jax.nn.gelu(c) * u).astype(BF16), wout_ref[...], preferred_element_type=F32)

    @pl.when(j == pl.num_programs(0) - 1)
    def _():
        y_ref[...] = xmid[...] + _rms(acc[...], gpost_ref[...])


def _dec_ffn(a, do, x2, wo_bf, gmix, subg, gpre, win_bf, cw, cb, conv_state_t, wout_bf, gpost, lam_i):
    n, d = x2.shape
    d_ff = wout_bf.shape[0]
    ck = FFN_CHUNK
    nj = d_ff // ck
    full = lambda shape: pl.BlockSpec(shape, lambda j: (0,) * len(shape))
    return pl.pallas_call(
        functools.partial(_dec_ffn_body, lam_i=lam_i),
        grid=(nj,),
        in_specs=[full(a.shape), full(do.shape), full(x2.shape), full(wo_bf.shape), full((1, d)),
                  full((1, HEAD_W)), full((1, d)),
                  pl.BlockSpec((d, ck), lambda j: (0, j)), pl.BlockSpec((d, ck), lambda j: (0, nj + j)),
                  pl.BlockSpec((3, ck), lambda j: (0, j)), pl.BlockSpec((1, ck), lambda j: (0, j)),
                  pl.BlockSpec((2, n, ck), lambda j: (0, 0, j)),
                  pl.BlockSpec((ck, d), lambda j: (j, 0)), full((1, d))],
        out_specs=[full((n, d)), pl.BlockSpec((n, ck), lambda j: (0, j))],
        out_shape=[jax.ShapeDtypeStruct((n, d), F32), jax.ShapeDtypeStruct((n, d_ff), F32)],
        scratch_shapes=[pltpu.VMEM((n, d), F32), pltpu.VMEM((n, d), BF16), pltpu.VMEM((n, d), F32)],
        compiler_params=_params("arbitrary"),
        name="dec_ffn",
    )(a, do, x2, wo_bf, gmix, subg, gpre, win_bf, win_bf, cw, cb, conv_state_t, wout_bf, gpost)


def kernel(x_prompt, x_sample, state_ret, cache_k, cache_v, state_conv, page_table,
           norm_mix_pre, norm_mix_post, w_in, w_o, lambda_q1, lambda_k1, lambda_q2, lambda_k2,
           subln_g, norm_ffn_pre, norm_ffn_post, w_ffn_in, conv_w, conv_b, w_ffn_out):
    batch, seq, d = x_prompt.shape
    n_dec = x_sample.shape[0]
    assert x_sample.shape[1] == 1, "the sample group is one token per row"
    depth = w_in.shape[0]
    n_heads = state_ret.shape[2]
    dk_ret = state_ret.shape[3]
    dh_diff = cache_k.shape[-1]
    d_ff = w_ffn_out.shape[1]
    w = n_heads * HEAD_W
    assert w_in.shape[2] == 7 * w and cache_v.shape[-1] == HEAD_W and 2 * dh_diff == HEAD_W
    assert seq % MIX_TILE == 0 and seq % FFN_TILE == 0 and d_ff % FFN_CHUNK == 0
    assert n_dec % DEC_GROUP == 0 and n_dec % 128 == 0 and seq % IN_TILE == 0 and IN_TILE % MIX_TILE == 0

    xp = x_prompt.reshape(batch * seq, d)
    xs = x_sample.reshape(n_dec, d)
    outs = [[] for _ in range(8)]
    for l in range(depth):
        lam_i = 0.8 - 0.6 * math.exp(-0.3 * l)
        lamv = jnp.stack([lambda_q1[l], lambda_k1[l], lambda_q2[l], lambda_k2[l]]).astype(F32)
        row = lambda v: v.reshape(1, -1)
        w_in_bf = w_in[l].astype(BF16)
        w_o_bf = w_o[l].astype(BF16)
        w_ffn_in_bf = w_ffn_in[l].astype(BF16)
        w_ffn_out_bf = w_ffn_out[l].astype(BF16)
        g_pre, g_post = row(norm_mix_pre[l]), row(norm_mix_post[l])
        f_pre, f_post = row(norm_ffn_pre[l]), row(norm_ffn_post[l])
        subg = row(subln_g[l])
        cw, cb = conv_w[l], row(conv_b[l])

        wkt_bf = w_in[l][:, 5 * w:6 * w].T.astype(BF16)
        cache_kt = jnp.transpose(cache_k[l], (0, 2, 3, 4, 1)).reshape(cache_k.shape[1], w, cache_k.shape[2])
        proj = functools.partial(_inproj, n_heads=n_heads, dk_ret=dk_ret, dh_diff=dh_diff)

        rq, rk, rv, rg, dq, kt, v4, ktb, dvb = proj(xp, g_pre, w_in_bf, wkt_bf, batch, seq, IN_TILE,
                                                     key_block=MIX_TILE)
        xp, s_fin = _mixer(lamv, rq, rk, rv, rg, dq, ktb, dvb, xp, w_o_bf, g_post, subg,
                           batch, seq, n_heads, lam_i)
        xp, cs = _ffn(xp, f_pre, w_ffn_in_bf, cw, cb, w_ffn_out_bf, f_post, batch, seq)
        outs[0].append(s_fin)
        outs[2].append(jnp.transpose(kt.reshape(batch, n_heads, 2, dh_diff, seq), (0, 4, 1, 2, 3)))
        outs[3].append(v4.reshape(batch, seq, n_heads, HEAD_W))
        outs[6].append(cs[:, 6:8, :])

        rq, rk, rv, rg, dq, kt, v4, dk, dv = proj(xs, g_pre, w_in_bf, wkt_bf, 1, n_dec, n_dec, key_block=None)
        a, s_new = _ret_decode(rq, rk, rv, rg, state_ret[l], n_heads)
        do = _paged_attn(page_table, lamv, dq, dk, dv, cache_kt, cache_v[l], n_heads, lam_i)
        conv_t = jnp.swapaxes(state_conv[l], 0, 1)
        xs, g_new = _dec_ffn(a, do, xs, w_o_bf, g_post, subg, f_pre, w_ffn_in_bf, cw, cb, conv_t,
                             w_ffn_out_bf, f_post, lam_i)
        outs[1].append(s_new)
        outs[4].append(jnp.transpose(kt.reshape(n_heads, 2, dh_diff, n_dec), (3, 0, 1, 2))[:, None])
        outs[5].append(v4.reshape(n_dec, 1, n_heads, HEAD_W))
        outs[7].append(jnp.stack([state_conv[l][:, 1, :], g_new], axis=1))

    st = [jnp.stack(o) for o in outs]
    return (xp.reshape(batch, seq, d), xs.reshape(n_dec, 1, d), st[0], st[1], st[2], st[3], st[4], st[5],
            st[6], st[7])
```

```python
import functools
import math

import jax
import jax.numpy as jnp
from jax import lax
from jax.experimental import pallas as pl
from jax.experimental.pallas import tpu as pltpu

F32 = jnp.float32
BF16 = jnp.bfloat16
EPS = 1e-6
NEG_INF = -1e30
FINITE_MIN = -3e38

V7X_VMEM_BYTES = 64 * 1024 * 1024
VMEM_LIMIT_BYTES = V7X_VMEM_BYTES * 7 // 8

RET_CHUNK = 128
HEAD_W = 128
IN_TILE = 512
MIX_TILE = 256
FFN_TILE = 256
FFN_CHUNK = 256
DEC_GROUP = 16
NT_DIMS = (((1,), (1,)), ((), ()))
TN_DIMS = (((0,), (0,)), ((), ()))


def _rms(x, g):
    return x * lax.rsqrt(jnp.mean(x * x, axis=-1, keepdims=True) + EPS) * g


def _params(*sem):
    return pltpu.CompilerParams(dimension_semantics=sem, vmem_limit_bytes=VMEM_LIMIT_BYTES)


def _const_spec(shape):
    n = len(shape)
    return pl.BlockSpec(shape, lambda *_: (0,) * n, pipeline_mode=pl.Buffered(1))


def _ret_log_decay(h):
    return math.log(1.0 - 2.0 ** (-5.0 - h))


def _lambda(lam_ref, lam_i):
    a = jnp.sum(lam_ref[0:1, :] * lam_ref[1:2, :], axis=-1, keepdims=True)
    b = jnp.sum(lam_ref[2:3, :] * lam_ref[3:4, :], axis=-1, keepdims=True)
    return jnp.exp(a) - jnp.exp(b) + lam_i


def _inproj_body(x_ref, g_ref, w_ref, wkt_ref, rq_ref, rk_ref, rv_ref, rg_ref, dq_ref, kt_ref, v4_ref,
                 kx_ref, vx_ref, *, width, n_heads, k_scale, q_scale, key_block):
    h = _rms(x_ref[...], g_ref[...]).astype(BF16)

    def col(j):
        return jnp.dot(h, w_ref[:, j * width:(j + 1) * width], preferred_element_type=F32)

    rq_ref[...] = col(0)
    rk_ref[...] = col(1) * k_scale
    rv_ref[...] = col(2).astype(BF16)
    rg_ref[...] = col(3)
    dq_ref[...] = (col(4) * q_scale).astype(BF16)
    kt = lax.dot_general(wkt_ref[...], h, NT_DIMS, preferred_element_type=F32)
    kt_ref[0] = kt
    dv = col(6)
    for hh in range(n_heads):
        v4_ref[:, hh, :] = dv[:, hh * HEAD_W:(hh + 1) * HEAD_W]
    if key_block is None:
        kx_ref[...] = col(5)
        vx_ref[...] = dv
    else:
        for c in range(kx_ref.shape[0]):
            kx_ref[c] = kt[:, c * key_block:(c + 1) * key_block].astype(BF16)
        vx_ref[...] = dv.astype(BF16)


def _inproj(x2, g, w_bf, wkt_bf, batch, seq, tile, n_heads, dk_ret, dh_diff, key_block):
    m, d = x2.shape
    width = wkt_bf.shape[0]
    nt = seq // tile
    row = lambda i: (i, 0)
    f32_out = jax.ShapeDtypeStruct((m, width), F32)
    bf_out = jax.ShapeDtypeStruct((m, width), BF16)
    spec = pl.BlockSpec((tile, width), row)
    if key_block is None:
        kx_shape, kx_spec, vx_shape = f32_out, spec, f32_out
    else:
        kx_shape = jax.ShapeDtypeStruct((m // key_block, width, key_block), BF16)
        kx_spec = pl.BlockSpec((tile // key_block, width, key_block), lambda i: (i, 0, 0))
        vx_shape = bf_out
    body = functools.partial(_inproj_body, width=width, n_heads=n_heads, k_scale=dk_ret ** -0.5,
                             q_scale=dh_diff ** -0.5, key_block=key_block)
    return pl.pallas_call(
        body,
        grid=(m // tile,),
        in_specs=[pl.BlockSpec((tile, d), row), _const_spec((1, d)), _const_spec(w_bf.shape),
                  _const_spec(wkt_bf.shape)],
        out_specs=[spec] * 5 + [pl.BlockSpec((1, width, tile), lambda i: (i // nt, 0, i % nt)),
                                pl.BlockSpec((tile, n_heads, HEAD_W), lambda i: (i, 0, 0)), kx_spec, spec],
        out_shape=[f32_out, f32_out, bf_out, f32_out, bf_out,
                   jax.ShapeDtypeStruct((batch, width, seq), F32),
                   jax.ShapeDtypeStruct((m, n_heads, HEAD_W), F32), kx_shape, vx_shape],
        compiler_params=_params("arbitrary"),
        name="inproj",
    )(x2, g, w_bf, wkt_bf)


def _retention_consts(length, n_heads):
    i = lax.broadcasted_iota(jnp.int32, (length, length), 0)
    j = lax.broadcasted_iota(jnp.int32, (length, length), 1)
    diff = (i - j).astype(F32)
    causal = diff >= 0
    ri = lax.broadcasted_iota(jnp.int32, (length, 1), 0).astype(F32)
    out = []
    for h in range(n_heads):
        lg = _ret_log_decay(h)
        decay = jnp.where(causal, jnp.exp(jnp.where(causal, diff, 0.0) * lg), 0.0)
        qdec = jnp.exp((ri + 1.0) * lg)
        kdec = jnp.exp((length - 1.0 - ri) * lg)
        out.append((decay, qdec, kdec, math.exp(length * lg)))
    return out


def _gated_group_norm(o, g):
    on = o * lax.rsqrt(jnp.mean(o * o, axis=-1, keepdims=True) + EPS)
    return on * (g * jax.nn.sigmoid(g))


def _mixer_body(lam_ref, rq_ref, rk_ref, rv_ref, rg_ref, dq_ref, kt_ref, vb_ref, x_ref, wo_ref,
                gpost_ref, subg_ref, y_ref, s_ref, mix_scr, sc_scr, mx_scr, l_scr, acc_scr,
                *, tq, n_heads, lam_i):
    t = pl.program_id(1)
    ret_w = n_heads * HEAD_W

    @pl.when(t == 0)
    def _():
        s_ref[...] = jnp.zeros_like(s_ref)

    consts = _retention_consts(RET_CHUNK, n_heads)
    for c in range(tq // RET_CHUNK):
        rows = slice(c * RET_CHUNK, (c + 1) * RET_CHUNK)
        for h in range(n_heads):
            cols = slice(h * HEAD_W, (h + 1) * HEAD_W)
            decay, qdec, kdec, gl = consts[h]
            q = rq_ref[rows, cols]
            k = rk_ref[rows, cols]
            v = rv_ref[rows, cols]
            s = s_ref[0, h]
            att = lax.dot_general(q.astype(BF16), k.astype(BF16), NT_DIMS,
                                  preferred_element_type=F32) * decay
            o = (jnp.dot(att.astype(BF16), v, preferred_element_type=F32)
                 + jnp.dot((q * qdec).astype(BF16), s.astype(BF16), preferred_element_type=F32))
            s_ref[0, h] = gl * s + lax.dot_general((k * kdec).astype(BF16), v, TN_DIMS,
                                                   preferred_element_type=F32)
            mix_scr[rows, cols] = _gated_group_norm(o, rg_ref[rows, cols]).astype(BF16)

    lam = _lambda(lam_ref, lam_i)
    ii = lax.broadcasted_iota(jnp.int32, (tq, tq), 0)
    jj = lax.broadcasted_iota(jnp.int32, (tq, tq), 1)
    rel = (ii - jj).astype(F32)
    lane = lax.broadcasted_iota(jnp.int32, (tq, HEAD_W), 1)
    subg = subg_ref[...]
    for h in range(n_heads):
        cols = slice(h * HEAD_W, (h + 1) * HEAD_W)
        slope = 2.0 ** (-8.0 / n_heads * (h + 1))
        qh = dq_ref[:, cols]
        zero = jnp.zeros_like(qh)
        qq = jnp.concatenate([jnp.where(lane < HEAD_W // 2, qh, zero),
                              jnp.where(lane >= HEAD_W // 2, qh, zero)], axis=0)

        mx_scr[...] = jnp.full(mx_scr.shape, FINITE_MIN, F32)
        l_scr[...] = jnp.zeros_like(l_scr)
        acc_scr[...] = jnp.zeros_like(acc_scr)

        def scores(kb, carry):
            s = jnp.dot(qq, kt_ref[kb, cols, :], preferred_element_type=F32)
            dist = rel + ((t - kb) * tq).astype(F32)
            bias = jnp.where(dist >= 0, -slope * dist, NEG_INF)
            s1 = s[:tq] + bias
            s2 = s[tq:] + bias
            sc_scr[kb, 0] = s1
            sc_scr[kb, 1] = s2
            mx_scr[0] = jnp.maximum(mx_scr[0], s1)
            mx_scr[1] = jnp.maximum(mx_scr[1], s2)
            return carry

        lax.fori_loop(0, t + 1, scores, 0)
        m1 = jnp.broadcast_to(jnp.max(mx_scr[0], axis=-1, keepdims=True), (tq, tq))
        m2 = jnp.broadcast_to(jnp.max(mx_scr[1], axis=-1, keepdims=True), (tq, tq))

        def weighted(kb, carry):
            e1 = jnp.exp(sc_scr[kb, 0] - m1)
            e2 = jnp.exp(sc_scr[kb, 1] - m2)
            l_scr[0] += e1
            l_scr[1] += e2
            vr = vb_ref[pl.ds(pl.multiple_of(kb * tq, tq), tq), cols]
            e = jnp.concatenate([e1.astype(BF16), e2.astype(BF16)], axis=0)
            acc_scr[...] += jnp.dot(e, vr, preferred_element_type=F32)
            return carry

        lax.fori_loop(0, t + 1, weighted, 0)
        l1 = jnp.sum(l_scr[0], axis=-1, keepdims=True)
        l2 = jnp.sum(l_scr[1], axis=-1, keepdims=True)
        o = acc_scr[:tq] / l1 - lam * (acc_scr[tq:] / l2)
        o = _rms(o, subg) * (1.0 - lam_i)
        mix_scr[:, ret_w + h * HEAD_W:ret_w + (h + 1) * HEAD_W] = o.astype(BF16)

    mo = jnp.dot(mix_scr[...], wo_ref[...], preferred_element_type=F32)
    y_ref[...] = x_ref[...] + _rms(mo, gpost_ref[...])


def _mixer(lamv, rq, rk, rv, rg, dq, ktb, dvb, x2, wo_bf, gpost, subg, batch, seq, n_heads, lam_i):
    m, d = x2.shape
    w = rq.shape[1]
    tq = MIX_TILE
    nt = seq // tq
    tile = lambda b, t: (b * nt + t, 0)
    whole = lambda b, t: (b, 0)
    body = functools.partial(_mixer_body, tq=tq, n_heads=n_heads, lam_i=lam_i)
    return pl.pallas_call(
        body,
        grid=(batch, nt),
        in_specs=[
            _const_spec(lamv.shape),
            pl.BlockSpec((tq, w), tile), pl.BlockSpec((tq, w), tile), pl.BlockSpec((tq, w), tile),
            pl.BlockSpec((tq, w), tile), pl.BlockSpec((tq, w), tile),
            pl.BlockSpec((nt, w, tq), lambda b, t: (b, 0, 0)), pl.BlockSpec((seq, w), whole),
            pl.BlockSpec((tq, d), tile),
            _const_spec(wo_bf.shape), _const_spec((1, d)), _const_spec((1, HEAD_W)),
        ],
        out_specs=[pl.BlockSpec((tq, d), tile),
                   pl.BlockSpec((1, n_heads, HEAD_W, HEAD_W), lambda b, t: (b, 0, 0, 0))],
        out_shape=[jax.ShapeDtypeStruct((m, d), F32),
                   jax.ShapeDtypeStruct((batch, n_heads, HEAD_W, HEAD_W), F32)],
        scratch_shapes=[
            pltpu.VMEM((tq, 2 * w), BF16),
            pltpu.VMEM((nt, 2, tq, tq), F32),
            pltpu.VMEM((2, tq, tq), F32),
            pltpu.VMEM((2, tq, tq), F32),
            pltpu.VMEM((2 * tq, HEAD_W), F32),
        ],
        compiler_params=_params("arbitrary", "arbitrary"),
        name="mixer",
    )(lamv, rq, rk, rv, rg, dq, ktb, dvb, x2, wo_bf, gpost, subg)


def _conv_gate(g, u, g1, g2, cw_ref, cb_ref, cols):
    c = cw_ref[0:1, cols] * g2 + cw_ref[1:2, cols] * g1 + cw_ref[2:3, cols] * g + cb_ref[:, cols]
    return (jax.nn.gelu(c) * u).astype(BF16)


def _ffn_body(x_ref, gpre_ref, win_ref, cw_ref, cb_ref, wout_ref, gpost_ref, y_ref, cs_ref,
              gbuf, carry, act, *, tf, d_ff, ck):
    t = pl.program_id(1)

    @pl.when(t == 0)
    def _():
        carry[...] = jnp.zeros_like(carry)

    x = x_ref[...]
    h = _rms(x, gpre_ref[...]).astype(BF16)
    for j in range(d_ff // ck):
        cols = slice(j * ck, (j + 1) * ck)
        g = jnp.dot(h, win_ref[:, cols], preferred_element_type=F32)
        u = jnp.dot(h, win_ref[:, d_ff + j * ck:d_ff + (j + 1) * ck], preferred_element_type=F32)
        gbuf[0:8, :] = carry[:, cols]
        gbuf[8:8 + tf, :] = g
        act[:, cols] = _conv_gate(g, u, gbuf[7:7 + tf, :], gbuf[6:6 + tf, :], cw_ref, cb_ref, cols)
        carry[:, cols] = gbuf[tf:tf + 8, :]
    f = jnp.dot(act[...], wout_ref[...], preferred_element_type=F32)
    y_ref[...] = x + _rms(f, gpost_ref[...])
    cs_ref[0] = carry[...]


def _ffn(x2, gpre, win_bf, cw, cb, wout_bf, gpost, batch, seq):
    m, d = x2.shape
    d_ff = wout_bf.shape[0]
    tf = FFN_TILE
    nt = seq // tf
    tile = lambda b, t: (b * nt + t, 0)
    body = functools.partial(_ffn_body, tf=tf, d_ff=d_ff, ck=FFN_CHUNK)
    return pl.pallas_call(
        body,
        grid=(batch, nt),
        in_specs=[pl.BlockSpec((tf, d), tile), _const_spec((1, d)), _const_spec(win_bf.shape),
                  _const_spec(cw.shape), _const_spec(cb.shape), _const_spec(wout_bf.shape),
                  _const_spec((1, d))],
        out_specs=[pl.BlockSpec((tf, d), tile), pl.BlockSpec((1, 8, d_ff), lambda b, t: (b, 0, 0))],
        out_shape=[jax.ShapeDtypeStruct((m, d), F32), jax.ShapeDtypeStruct((batch, 8, d_ff), F32)],
        scratch_shapes=[pltpu.VMEM((tf + 8, FFN_CHUNK), F32), pltpu.VMEM((8, d_ff), F32),
                        pltpu.VMEM((tf, d_ff), BF16)],
        compiler_params=_params("arbitrary", "arbitrary"),
        name="conv_ffn",
    )(x2, gpre, win_bf, cw, cb, wout_bf, gpost)


def _ret_dec_body(rq_ref, rk_ref, rv_ref, rg_ref, s_ref, a_ref, sn_ref, *, grp, n_heads):
    row = lax.broadcasted_iota(jnp.int32, (grp, grp * HEAD_W), 0)
    lane = lax.broadcasted_iota(jnp.int32, (grp, grp * HEAD_W), 1)
    own = (lane // HEAD_W) == row

    def block_diag(x):
        return jnp.where(own, jnp.tile(x, (1, grp)), 0.0).astype(BF16)

    for h in range(n_heads):
        cols = slice(h * HEAD_W, (h + 1) * HEAD_W)
        gamma = math.exp(_ret_log_decay(h))
        q = rq_ref[:, cols]
        k = rk_ref[:, cols]
        v = rv_ref[:, cols]
        qb = q.astype(BF16).astype(F32)
        kb = k.astype(BF16).astype(F32)
        att = jnp.sum(qb * kb, axis=-1, keepdims=True).astype(BF16).astype(F32)
        s = s_ref[:, h].reshape(grp * HEAD_W, HEAD_W)
        o = att * v.astype(F32) + jnp.dot(block_diag(q * gamma), s.astype(BF16),
                                          preferred_element_type=F32)
        upd = lax.dot_general(block_diag(k), v, TN_DIMS, preferred_element_type=F32)
        sn_ref[:, h] = (gamma * s + upd).reshape(grp, HEAD_W, HEAD_W)
        a_ref[:, cols] = _gated_group_norm(o, rg_ref[:, cols]).astype(BF16)


def _ret_decode(rq, rk, rv, rg, state, n_heads):
    n, w = rq.shape
    grp = DEC_GROUP
    row = lambda i: (i, 0)
    st = lambda i: (i, 0, 0, 0)
    spec = pl.BlockSpec((grp, w), row)
    st_spec = pl.BlockSpec((grp, n_heads, HEAD_W, HEAD_W), st)
    return pl.pallas_call(
        functools.partial(_ret_dec_body, grp=grp, n_heads=n_heads),
        grid=(n // grp,),
        in_specs=[spec, spec, spec, spec, st_spec],
        out_specs=[spec, st_spec],
        out_shape=[jax.ShapeDtypeStruct((n, w), BF16), jax.ShapeDtypeStruct(state.shape, F32)],
        compiler_params=_params("arbitrary"),
        name="ret_decode",
    )(rq, rk, rv, rg, state)


def _page_copies(pt_ref, ck_hbm, cv_hbm, kbuf, vbuf, sem, b, slot, n_pages):
    out = []
    for p in range(n_pages):
        page = pt_ref[b * n_pages + p]
        out.append(pltpu.make_async_copy(ck_hbm.at[page], kbuf.at[slot, p], sem.at[0, slot]))
        out.append(pltpu.make_async_copy(cv_hbm.at[page], vbuf.at[slot, p], sem.at[1, slot]))
    return out


def _paged_attn_body(pt_ref, lam_ref, q_ref, kn_ref, vn_ref, ck_hbm, cv_hbm, o_ref,
                     kbuf, vbuf, sem, sc_scr, *, n_pages, page, n_heads, lam_i):
    b = pl.program_id(0)
    nb = pl.num_programs(0)
    slot = b % 2
    w = n_heads * HEAD_W
    nc = 2 * n_heads
    past = n_pages * page
    copies = functools.partial(_page_copies, pt_ref, ck_hbm, cv_hbm, kbuf, vbuf, sem, n_pages=n_pages)

    @pl.when(b == 0)
    def _():
        for c in copies(b=0, slot=0):
            c.start()

    @pl.when(b + 1 < nb)
    def _():
        for c in copies(b=b + 1, slot=1 - slot):
            c.start()

    for c in copies(b=b, slot=slot):
        c.wait()

    ci = lax.broadcasted_iota(jnp.int32, (nc, w), 0)
    ri = lax.broadcasted_iota(jnp.int32, (nc, w), 1)
    col_of = ri // HEAD_W + n_heads * ((ri % HEAD_W) // (HEAD_W // 2))
    q_sel = jnp.where(ci == col_of, jnp.broadcast_to(q_ref[0], (nc, w)), 0.0)
    q_sel_bf = q_sel.astype(BF16)
    rowc = lax.broadcasted_iota(jnp.int32, (nc, page), 0)
    key_i = lax.broadcasted_iota(jnp.int32, (nc, page), 1)
    slope = jnp.zeros((nc, page), F32)
    for h in range(n_heads):
        slope = jnp.where(rowc % n_heads == h, 2.0 ** (-8.0 / n_heads * (h + 1)), slope)

    def scores(p, mx):
        dist = (past - (p * page + key_i)).astype(F32)
        s = jnp.dot(q_sel_bf, kbuf[slot, p].astype(BF16), preferred_element_type=F32) - slope * dist
        sc_scr[p] = s
        return jnp.maximum(mx, s)

    mx = lax.fori_loop(0, n_pages, scores, jnp.full((nc, page), FINITE_MIN, F32), unroll=True)
    kn = kn_ref[0].astype(BF16).astype(F32)
    s_self = jnp.broadcast_to(jnp.sum(q_sel * kn, axis=-1, keepdims=True), (nc, page))
    m = jnp.maximum(jnp.broadcast_to(jnp.max(mx, axis=-1, keepdims=True), (nc, page)), s_self)

    def expsum(p, l):
        e = jnp.exp(sc_scr[p] - m)
        sc_scr[p] = e
        return l + e

    lp = lax.fori_loop(0, n_pages, expsum, jnp.zeros((nc, page), F32), unroll=True)
    e_self = jnp.exp(s_self - m)
    l = jnp.broadcast_to(jnp.sum(lp, axis=-1, keepdims=True), (nc, page)) + e_self
    r = jnp.where(rowc < n_heads, 1.0, _lambda(lam_ref, lam_i)) / l

    def combine(pn):
        return (pn - pltpu.roll(pn, n_heads, axis=0)).astype(BF16)

    def weighted(p, accs):
        a = combine(sc_scr[p] * r)
        return tuple(acc + jnp.dot(a, vbuf[slot, p, pl.ds(h, page, stride=n_heads), :].astype(BF16),
                                   preferred_element_type=F32)
                     for h, acc in enumerate(accs))

    accs = lax.fori_loop(0, n_pages, weighted, tuple(jnp.zeros((nc, HEAD_W), F32) for _ in range(n_heads)),
                         unroll=True)
    a_self = combine(e_self * r).astype(F32)
    vn = vn_ref[0].astype(BF16).astype(F32)
    o_ref[0] = jnp.concatenate(
        [accs[h][h:h + 1, :] + a_self[h:h + 1, 0:HEAD_W] * vn[:, h * HEAD_W:(h + 1) * HEAD_W]
         for h in range(n_heads)], axis=-1)


def _paged_attn(page_table, lamv, dq, dk, dv, cache_kt, cache_v, n_heads, lam_i):
    n, w = dq.shape
    n_pages = page_table.shape[1]
    page = cache_kt.shape[2]
    row = lambda b, pt: (b, 0, 0)
    rspec = pl.BlockSpec((1, 1, w), row)
    body = functools.partial(_paged_attn_body, n_pages=n_pages, page=page, n_heads=n_heads, lam_i=lam_i)
    grid_spec = pltpu.PrefetchScalarGridSpec(
        num_scalar_prefetch=1,
        grid=(n,),
        in_specs=[pl.BlockSpec(lamv.shape, lambda b, pt: (0, 0)), rspec, rspec, rspec,
                  pl.BlockSpec(memory_space=pl.ANY), pl.BlockSpec(memory_space=pl.ANY)],
        out_specs=rspec,
        scratch_shapes=[pltpu.VMEM((2, n_pages, w, page), F32),
                        pltpu.VMEM((2, n_pages, page * n_heads, HEAD_W), F32),
                        pltpu.SemaphoreType.DMA((2, 2)), pltpu.VMEM((n_pages, 2 * n_heads, page), F32)],
    )
    out = pl.pallas_call(
        body,
        grid_spec=grid_spec,
        out_shape=jax.ShapeDtypeStruct((n, 1, w), F32),
        compiler_params=_params("arbitrary"),
        name="paged_attn",
    )(page_table.reshape(-1), lamv, dq.astype(F32).reshape(n, 1, w), dk.reshape(n, 1, w),
      dv.reshape(n, 1, w), cache_kt, cache_v)
    return out.reshape(n, w)


def _dec_ffn_body(a_ref, do_ref, x_ref, wo_ref, gmix_ref, subg_ref, gpre_ref, wg_ref, wu_ref, cw_ref,
                  cb_ref, sc_ref, wout_ref, gpost_ref, y_ref, g_ref, xmid, hbuf, acc, *, lam_i):
    j = pl.program_id(0)
    ret_w = a_ref.shape[1]

    @pl.when(j == 0)
    def _():
        mo = jnp.dot(a_ref[...], wo_ref[0:ret_w, :], preferred_element_type=F32)
        for h in range(do_ref.shape[1] // HEAD_W):
            cols = slice(h * HEAD_W, (h + 1) * HEAD_W)
            bh = (_rms(do_ref[:, cols], subg_ref[...]) * (1.0 - lam_i)).astype(BF16)
            mo += jnp.dot(bh, wo_ref[ret_w + h * HEAD_W:ret_w + (h + 1) * HEAD_W, :],
                          preferred_element_type=F32)
        xm = x_ref[...] + _rms(mo, gmix_ref[...])
        xmid[...] = xm
        hbuf[...] = _rms(xm, gpre_ref[...]).astype(BF16)
        acc[...] = jnp.zeros_like(acc)

    h = hbuf[...]
    g = jnp.dot(h, wg_ref[...], preferred_element_type=F32)
    u = jnp.dot(h, wu_ref[...], preferred_element_type=F32)
    g_ref[...] = g
    c = cw_ref[0:1, :] * sc_ref[0] + cw_ref[1:2, :] * sc_ref[1] + cw_ref[2:3, :] * g + cb_ref[...]
    acc[...] += jnp.dot((jax.nn.gelu(c) * u).astype(BF16), wout_ref[...], preferred_element_type=F32)

    @pl.when(j == pl.num_programs(0) - 1)
    def _():
        y_ref[...] = xmid[...] + _rms(acc[...], gpost_ref[...])


def _dec_ffn(a, do, x2, wo_bf, gmix, subg, gpre, win_bf, cw, cb, conv_state_t, wout_bf, gpost, lam_i):
    n, d = x2.shape
    d_ff = wout_bf.shape[0]
    ck = FFN_CHUNK
    nj = d_ff // ck
    full = lambda shape: pl.BlockSpec(shape, lambda j: (0,) * len(shape))
    return pl.pallas_call(
        functools.partial(_dec_ffn_body, lam_i=lam_i),
        grid=(nj,),
        in_specs=[full(a.shape), full(do.shape), full(x2.shape), full(wo_bf.shape), full((1, d)),
                  full((1, HEAD_W)), full((1, d)),
                  pl.BlockSpec((d, ck), lambda j: (0, j)), pl.BlockSpec((d, ck), lambda j: (0, nj + j)),
                  pl.BlockSpec((3, ck), lambda j: (0, j)), pl.BlockSpec((1, ck), lambda j: (0, j)),
                  pl.BlockSpec((2, n, ck), lambda j: (0, 0, j)),
                  pl.BlockSpec((ck, d), lambda j: (j, 0)), full((1, d))],
        out_specs=[full((n, d)), pl.BlockSpec((n, ck), lambda j: (0, j))],
        out_shape=[jax.ShapeDtypeStruct((n, d), F32), jax.ShapeDtypeStruct((n, d_ff), F32)],
        scratch_shapes=[pltpu.VMEM((n, d), F32), pltpu.VMEM((n, d), BF16), pltpu.VMEM((n, d), F32)],
        compiler_params=_params("arbitrary"),
        name="dec_ffn",
    )(a, do, x2, wo_bf, gmix, subg, gpre, win_bf, win_bf, cw, cb, conv_state_t, wout_bf, gpost)


def kernel(x_prompt, x_sample, state_ret, cache_k, cache_v, state_conv, page_table,
           norm_mix_pre, norm_mix_post, w_in, w_o, lambda_q1, lambda_k1, lambda_q2, lambda_k2,
           subln_g, norm_ffn_pre, norm_ffn_post, w_ffn_in, conv_w, conv_b, w_ffn_out):
    batch, seq, d = x_prompt.shape
    n_dec = x_sample.shape[0]
    assert x_sample.shape[1] == 1, "the sample group is one token per row"
    depth = w_in.shape[0]
    n_heads = state_ret.shape[2]
    dk_ret = state_ret.shape[3]
    dh_diff = cache_k.shape[-1]
    d_ff = w_ffn_out.shape[1]
    w = n_heads * HEAD_W
    assert w_in.shape[2] == 7 * w and cache_v.shape[-1] == HEAD_W and 2 * dh_diff == HEAD_W
    assert seq % MIX_TILE == 0 and seq % FFN_TILE == 0 and d_ff % FFN_CHUNK == 0
    assert n_dec % DEC_GROUP == 0 and n_dec % 128 == 0 and seq % IN_TILE == 0 and IN_TILE % MIX_TILE == 0

    xp = x_prompt.reshape(batch * seq, d)
    xs = x_sample.reshape(n_dec, d)
    outs = [[] for _ in range(8)]
    for l in range(depth):
        lam_i = 0.8 - 0.6 * math.exp(-0.3 * l)
        lamv = jnp.stack([lambda_q1[l], lambda_k1[l], lambda_q2[l], lambda_k2[l]]).astype(F32)
        row = lambda v: v.reshape(1, -1)
        w_in_bf = w_in[l].astype(BF16)
        w_o_bf = w_o[l].astype(BF16)
        w_ffn_in_bf = w_ffn_in[l].astype(BF16)
        w_ffn_out_bf = w_ffn_out[l].astype(BF16)
        g_pre, g_post = row(norm_mix_pre[l]), row(norm_mix_post[l])
        f_pre, f_post = row(norm_ffn_pre[l]), row(norm_ffn_post[l])
        subg = row(subln_g[l])
        cw, cb = conv_w[l], row(conv_b[l])

        wkt_bf = w_in[l][:, 5 * w:6 * w].T.astype(BF16)
        cache_kt = jnp.transpose(cache_k[l], (0, 2, 3, 4, 1)).reshape(cache_k.shape[1], w, cache_k.shape[2])
        cache_vr = cache_v[l].reshape(cache_v.shape[1], cache_v.shape[2] * n_heads, HEAD_W)
        proj = functools.partial(_inproj, n_heads=n_heads, dk_ret=dk_ret, dh_diff=dh_diff)

        rq, rk, rv, rg, dq, kt, v4, ktb, dvb = proj(xp, g_pre, w_in_bf, wkt_bf, batch, seq, IN_TILE,
                                                     key_block=MIX_TILE)
        xp, s_fin = _mixer(lamv, rq, rk, rv, rg, dq, ktb, dvb, xp, w_o_bf, g_post, subg,
                           batch, seq, n_heads, lam_i)
        xp, cs = _ffn(xp, f_pre, w_ffn_in_bf, cw, cb, w_ffn_out_bf, f_post, batch, seq)
        outs[0].append(s_fin)
        outs[2].append(jnp.transpose(kt.reshape(batch, n_heads, 2, dh_diff, seq), (0, 4, 1, 2, 3)))
        outs[3].append(v4.reshape(batch, seq, n_heads, HEAD_W))
        outs[6].append(cs[:, 6:8, :])

        rq, rk, rv, rg, dq, kt, v4, dk, dv = proj(xs, g_pre, w_in_bf, wkt_bf, 1, n_dec, n_dec, key_block=None)
        a, s_new = _ret_decode(rq, rk, rv, rg, state_ret[l], n_heads)
        do = _paged_attn(page_table, lamv, dq, dk, dv, cache_kt, cache_vr, n_heads, lam_i)
        conv_t = jnp.swapaxes(state_conv[l], 0, 1)
        xs, g_new = _dec_ffn(a, do, xs, w_o_bf, g_post, subg, f_pre, w_ffn_in_bf, cw, cb, conv_t,
                             w_ffn_out_bf, f_post, lam_i)
        outs[1].append(s_new)
        outs[4].append(jnp.transpose(kt.reshape(n_heads, 2, dh_diff, n_dec), (3, 0, 1, 2))[:, None])
        outs[5].append(v4.reshape(n_dec, 1, n_heads, HEAD_W))
        outs[7].append(jnp.stack([state_conv[l][:, 1, :], g_new], axis=1))

    st = [jnp.stack(o) for o in outs]
    return (xp.reshape(batch, seq, d), xs.reshape(n_dec, 1, d), st[0], st[1], st[2], st[3], st[4], st[5],
            st[6], st[7])
```

```python
import functools
import math

import jax
import jax.numpy as jnp
from jax import lax
from jax.experimental import pallas as pl
from jax.experimental.pallas import tpu as pltpu

F32 = jnp.float32
BF16 = jnp.bfloat16
EPS = 1e-6
NEG_INF = -1e30
FINITE_MIN = -3e38

V7X_VMEM_BYTES = 64 * 1024 * 1024
VMEM_LIMIT_BYTES = V7X_VMEM_BYTES * 7 // 8

RET_CHUNK = 128
HEAD_W = 128
IN_TILE = 512
MIX_TILE = 256
FFN_TILE = 256
FFN_CHUNK = 256
DEC_GROUP = 16
NT_DIMS = (((1,), (1,)), ((), ()))
TN_DIMS = (((0,), (0,)), ((), ()))


def _rms(x, g):
    return x * lax.rsqrt(jnp.mean(x * x, axis=-1, keepdims=True) + EPS) * g


def _params(*sem):
    return pltpu.CompilerParams(dimension_semantics=sem, vmem_limit_bytes=VMEM_LIMIT_BYTES)


def _const_spec(shape):
    n = len(shape)
    return pl.BlockSpec(shape, lambda *_: (0,) * n, pipeline_mode=pl.Buffered(1))


def _ret_log_decay(h):
    return math.log(1.0 - 2.0 ** (-5.0 - h))


def _lambda(lam_ref, lam_i):
    a = jnp.sum(lam_ref[0:1, :] * lam_ref[1:2, :], axis=-1, keepdims=True)
    b = jnp.sum(lam_ref[2:3, :] * lam_ref[3:4, :], axis=-1, keepdims=True)
    return jnp.exp(a) - jnp.exp(b) + lam_i


def _inproj_body(x_ref, g_ref, w_ref, wkt_ref, rq_ref, rk_ref, rv_ref, rg_ref, dq_ref, kt_ref, v4_ref,
                 kx_ref, vx_ref, *, width, n_heads, k_scale, q_scale, key_block):
    h = _rms(x_ref[...], g_ref[...]).astype(BF16)

    def col(j):
        return jnp.dot(h, w_ref[:, j * width:(j + 1) * width], preferred_element_type=F32)

    rq_ref[...] = col(0)
    rk_ref[...] = col(1) * k_scale
    rv_ref[...] = col(2).astype(BF16)
    rg_ref[...] = col(3)
    dq_ref[...] = (col(4) * q_scale).astype(BF16)
    kt = lax.dot_general(wkt_ref[...], h, NT_DIMS, preferred_element_type=F32)
    kt_ref[0] = kt
    dv = col(6)
    for hh in range(n_heads):
        v4_ref[:, hh, :] = dv[:, hh * HEAD_W:(hh + 1) * HEAD_W]
    if key_block is None:
        kx_ref[...] = col(5)
        vx_ref[...] = dv
    else:
        for c in range(kx_ref.shape[0]):
            kx_ref[c] = kt[:, c * key_block:(c + 1) * key_block].astype(BF16)
        vx_ref[...] = dv.astype(BF16)


def _inproj(x2, g, w_bf, wkt_bf, batch, seq, tile, n_heads, dk_ret, dh_diff, key_block):
    m, d = x2.shape
    width = wkt_bf.shape[0]
    nt = seq // tile
    row = lambda i: (i, 0)
    f32_out = jax.ShapeDtypeStruct((m, width), F32)
    bf_out = jax.ShapeDtypeStruct((m, width), BF16)
    spec = pl.BlockSpec((tile, width), row)
    if key_block is None:
        kx_shape, kx_spec, vx_shape = f32_out, spec, f32_out
    else:
        kx_shape = jax.ShapeDtypeStruct((m // key_block, width, key_block), BF16)
        kx_spec = pl.BlockSpec((tile // key_block, width, key_block), lambda i: (i, 0, 0))
        vx_shape = bf_out
    body = functools.partial(_inproj_body, width=width, n_heads=n_heads, k_scale=dk_ret ** -0.5,
                             q_scale=dh_diff ** -0.5, key_block=key_block)
    return pl.pallas_call(
        body,
        grid=(m // tile,),
        in_specs=[pl.BlockSpec((tile, d), row), _const_spec((1, d)), _const_spec(w_bf.shape),
                  _const_spec(wkt_bf.shape)],
        out_specs=[spec] * 5 + [pl.BlockSpec((1, width, tile), lambda i: (i // nt, 0, i % nt)),
                                pl.BlockSpec((tile, n_heads, HEAD_W), lambda i: (i, 0, 0)), kx_spec, spec],
        out_shape=[f32_out, f32_out, bf_out, f32_out, bf_out,
                   jax.ShapeDtypeStruct((batch, width, seq), F32),
                   jax.ShapeDtypeStruct((m, n_heads, HEAD_W), F32), kx_shape, vx_shape],
        compiler_params=_params("arbitrary"),
        name="inproj",
    )(x2, g, w_bf, wkt_bf)


def _retention_consts(length, n_heads):
    i = lax.broadcasted_iota(jnp.int32, (length, length), 0)
    j = lax.broadcasted_iota(jnp.int32, (length, length), 1)
    diff = (i - j).astype(F32)
    causal = diff >= 0
    ri = lax.broadcasted_iota(jnp.int32, (length, 1), 0).astype(F32)
    out = []
    for h in range(n_heads):
        lg = _ret_log_decay(h)
        decay = jnp.where(causal, jnp.exp(jnp.where(causal, diff, 0.0) * lg), 0.0)
        qdec = jnp.exp((ri + 1.0) * lg)
        kdec = jnp.exp((length - 1.0 - ri) * lg)
        out.append((decay, qdec, kdec, math.exp(length * lg)))
    return out


def _gated_group_norm(o, g):
    on = o * lax.rsqrt(jnp.mean(o * o, axis=-1, keepdims=True) + EPS)
    return on * (g * jax.nn.sigmoid(g))


def _mixer_body(lam_ref, rq_ref, rk_ref, rv_ref, rg_ref, dq_ref, kt_ref, vb_ref, x_ref, wo_ref,
                gpost_ref, subg_ref, y_ref, s_ref, mix_scr, qq_scr, sc_scr, mx_scr, l_scr, acc_scr,
                *, tq, n_heads, lam_i):
    t = pl.program_id(1)
    ret_w = n_heads * HEAD_W

    @pl.when(t == 0)
    def _():
        s_ref[...] = jnp.zeros_like(s_ref)

    consts = _retention_consts(RET_CHUNK, n_heads)
    for c in range(tq // RET_CHUNK):
        rows = slice(c * RET_CHUNK, (c + 1) * RET_CHUNK)
        for h in range(n_heads):
            cols = slice(h * HEAD_W, (h + 1) * HEAD_W)
            decay, qdec, kdec, gl = consts[h]
            q = rq_ref[rows, cols]
            k = rk_ref[rows, cols]
            v = rv_ref[rows, cols]
            s = s_ref[0, h]
            att = lax.dot_general(q.astype(BF16), k.astype(BF16), NT_DIMS,
                                  preferred_element_type=F32) * decay
            o = (jnp.dot(att.astype(BF16), v, preferred_element_type=F32)
                 + jnp.dot((q * qdec).astype(BF16), s.astype(BF16), preferred_element_type=F32))
            s_ref[0, h] = gl * s + lax.dot_general((k * kdec).astype(BF16), v, TN_DIMS,
                                                   preferred_element_type=F32)
            mix_scr[rows, cols] = _gated_group_norm(o, rg_ref[rows, cols]).astype(BF16)

    lane = lax.broadcasted_iota(jnp.int32, (tq, HEAD_W), 1)
    for h in range(n_heads):
        qh = dq_ref[:, h * HEAD_W:(h + 1) * HEAD_W]
        zero = jnp.zeros_like(qh)
        qq_scr[h, 0:tq] = jnp.where(lane < HEAD_W // 2, qh, zero)
        qq_scr[h, tq:2 * tq] = jnp.where(lane >= HEAD_W // 2, qh, zero)
    mx_scr[...] = jnp.full(mx_scr.shape, FINITE_MIN, F32)
    l_scr[...] = jnp.zeros_like(l_scr)
    acc_scr[...] = jnp.zeros_like(acc_scr)
    key_j = lax.broadcasted_iota(jnp.int32, (1, tq), 1).astype(F32)
    lane_tiles = [slice(c * HEAD_W, (c + 1) * HEAD_W) for c in range(tq // HEAD_W)]

    def fold(x, op):
        return functools.reduce(op, [x[:, c] for c in lane_tiles])

    def scores(kb, causal):
        key_pos = key_j + ((kb - t) * tq).astype(F32)
        for h in range(n_heads):
            slope = 2.0 ** (-8.0 / n_heads * (h + 1))
            s = jnp.dot(qq_scr[h], kt_ref[kb, h * HEAD_W:(h + 1) * HEAD_W, :], preferred_element_type=F32)
            s = s + slope * key_pos
            if causal is not None:
                s = jnp.where(causal, s, NEG_INF)
            sc_scr[kb, h] = s
            mx_scr[h] = jnp.maximum(mx_scr[h], fold(s, jnp.maximum))

    def off_diagonal(kb, carry):
        scores(kb, None)
        return carry

    lax.fori_loop(0, t, off_diagonal, 0)
    ii = lax.broadcasted_iota(jnp.int32, (2 * tq, tq), 0)
    jj = lax.broadcasted_iota(jnp.int32, (2 * tq, tq), 1)
    scores(t, jnp.where(ii >= tq, ii - tq, ii) >= jj)

    for h in range(n_heads):
        mx_scr[h] = jnp.broadcast_to(jnp.max(mx_scr[h], axis=-1, keepdims=True), (2 * tq, HEAD_W))

    def weighted(kb, carry):
        for h in range(n_heads):
            m = mx_scr[h]
            e = [jnp.exp(sc_scr[kb, h, :, c] - m) for c in lane_tiles]
            l_scr[h] += functools.reduce(jnp.add, e)
            vr = vb_ref[pl.ds(pl.multiple_of(kb * tq, tq), tq), h * HEAD_W:(h + 1) * HEAD_W]
            acc_scr[h] += jnp.dot(jnp.concatenate([x.astype(BF16) for x in e], axis=1), vr,
                                  preferred_element_type=F32)
        return carry

    lax.fori_loop(0, t + 1, weighted, 0)
    lam = _lambda(lam_ref, lam_i)
    subg = subg_ref[...]
    for h in range(n_heads):
        on = acc_scr[h] / jnp.sum(l_scr[h], axis=-1, keepdims=True)
        o = _rms(on[:tq] - lam * on[tq:], subg) * (1.0 - lam_i)
        mix_scr[:, ret_w + h * HEAD_W:ret_w + (h + 1) * HEAD_W] = o.astype(BF16)

    mo = jnp.dot(mix_scr[...], wo_ref[...], preferred_element_type=F32)
    y_ref[...] = x_ref[...] + _rms(mo, gpost_ref[...])


def _mixer(lamv, rq, rk, rv, rg, dq, ktb, dvb, x2, wo_bf, gpost, subg, batch, seq, n_heads, lam_i):
    m, d = x2.shape
    w = rq.shape[1]
    tq = MIX_TILE
    nt = seq // tq
    tile = lambda b, t: (b * nt + t, 0)
    whole = lambda b, t: (b, 0)
    body = functools.partial(_mixer_body, tq=tq, n_heads=n_heads, lam_i=lam_i)
    return pl.pallas_call(
        body,
        grid=(batch, nt),
        in_specs=[
            _const_spec(lamv.shape),
            pl.BlockSpec((tq, w), tile), pl.BlockSpec((tq, w), tile), pl.BlockSpec((tq, w), tile),
            pl.BlockSpec((tq, w), tile), pl.BlockSpec((tq, w), tile),
            pl.BlockSpec((nt, w, tq), lambda b, t: (b, 0, 0)), pl.BlockSpec((seq, w), whole),
            pl.BlockSpec((tq, d), tile),
            _const_spec(wo_bf.shape), _const_spec((1, d)), _const_spec((1, HEAD_W)),
        ],
        out_specs=[pl.BlockSpec((tq, d), tile),
                   pl.BlockSpec((1, n_heads, HEAD_W, HEAD_W), lambda b, t: (b, 0, 0, 0))],
        out_shape=[jax.ShapeDtypeStruct((m, d), F32),
                   jax.ShapeDtypeStruct((batch, n_heads, HEAD_W, HEAD_W), F32)],
        scratch_shapes=[
            pltpu.VMEM((tq, 2 * w), BF16),
            pltpu.VMEM((n_heads, 2 * tq, HEAD_W), BF16),
            pltpu.VMEM((nt, n_heads, 2 * tq, tq), F32),
            pltpu.VMEM((n_heads, 2 * tq, HEAD_W), F32),
            pltpu.VMEM((n_heads, 2 * tq, HEAD_W), F32),
            pltpu.VMEM((n_heads, 2 * tq, HEAD_W), F32),
        ],
        compiler_params=_params("arbitrary", "arbitrary"),
        name="mixer",
    )(lamv, rq, rk, rv, rg, dq, ktb, dvb, x2, wo_bf, gpost, subg)


def _conv_gate(g, u, g1, g2, cw_ref, cb_ref, cols):
    c = cw_ref[0:1, cols] * g2 + cw_ref[1:2, cols] * g1 + cw_ref[2:3, cols] * g + cb_ref[:, cols]
    return (jax.nn.gelu(c) * u).astype(BF16)


def _ffn_body(x_ref, gpre_ref, win_ref, cw_ref, cb_ref, wout_ref, gpost_ref, y_ref, cs_ref,
              gbuf, carry, act, *, tf, d_ff, ck):
    t = pl.program_id(1)

    @pl.when(t == 0)
    def _():
        carry[...] = jnp.zeros_like(carry)

    x = x_ref[...]
    h = _rms(x, gpre_ref[...]).astype(BF16)
    for j in range(d_ff // ck):
        cols = slice(j * ck, (j + 1) * ck)
        g = jnp.dot(h, win_ref[:, cols], preferred_element_type=F32)
        u = jnp.dot(h, win_ref[:, d_ff + j * ck:d_ff + (j + 1) * ck], preferred_element_type=F32)
        gbuf[0:8, :] = carry[:, cols]
        gbuf[8:8 + tf, :] = g
        act[:, cols] = _conv_gate(g, u, gbuf[7:7 + tf, :], gbuf[6:6 + tf, :], cw_ref, cb_ref, cols)
        carry[:, cols] = gbuf[tf:tf + 8, :]
    f = jnp.dot(act[...], wout_ref[...], preferred_element_type=F32)
    y_ref[...] = x + _rms(f, gpost_ref[...])
    cs_ref[0] = carry[...]


def _ffn(x2, gpre, win_bf, cw, cb, wout_bf, gpost, batch, seq):
    m, d = x2.shape
    d_ff = wout_bf.shape[0]
    tf = FFN_TILE
    nt = seq // tf
    tile = lambda b, t: (b * nt + t, 0)
    body = functools.partial(_ffn_body, tf=tf, d_ff=d_ff, ck=FFN_CHUNK)
    return pl.pallas_call(
        body,
        grid=(batch, nt),
        in_specs=[pl.BlockSpec((tf, d), tile), _const_spec((1, d)), _const_spec(win_bf.shape),
                  _const_spec(cw.shape), _const_spec(cb.shape), _const_spec(wout_bf.shape),
                  _const_spec((1, d))],
        out_specs=[pl.BlockSpec((tf, d), tile), pl.BlockSpec((1, 8, d_ff), lambda b, t: (b, 0, 0))],
        out_shape=[jax.ShapeDtypeStruct((m, d), F32), jax.ShapeDtypeStruct((batch, 8, d_ff), F32)],
        scratch_shapes=[pltpu.VMEM((tf + 8, FFN_CHUNK), F32), pltpu.VMEM((8, d_ff), F32),
                        pltpu.VMEM((tf, d_ff), BF16)],
        compiler_params=_params("arbitrary", "arbitrary"),
        name="conv_ffn",
    )(x2, gpre, win_bf, cw, cb, wout_bf, gpost)


def _ret_dec_body(rq_ref, rk_ref, rv_ref, rg_ref, s_ref, a_ref, sn_ref, *, grp, n_heads):
    row = lax.broadcasted_iota(jnp.int32, (grp, grp * HEAD_W), 0)
    lane = lax.broadcasted_iota(jnp.int32, (grp, grp * HEAD_W), 1)
    own = (lane // HEAD_W) == row

    def block_diag(x):
        return jnp.where(own, jnp.tile(x, (1, grp)), 0.0).astype(BF16)

    for h in range(n_heads):
        cols = slice(h * HEAD_W, (h + 1) * HEAD_W)
        gamma = math.exp(_ret_log_decay(h))
        q = rq_ref[:, cols]
        k = rk_ref[:, cols]
        v = rv_ref[:, cols]
        qb = q.astype(BF16).astype(F32)
        kb = k.astype(BF16).astype(F32)
        att = jnp.sum(qb * kb, axis=-1, keepdims=True).astype(BF16).astype(F32)
        s = s_ref[:, h].reshape(grp * HEAD_W, HEAD_W)
        o = att * v.astype(F32) + jnp.dot(block_diag(q * gamma), s.astype(BF16),
                                          preferred_element_type=F32)
        upd = lax.dot_general(block_diag(k), v, TN_DIMS, preferred_element_type=F32)
        sn_ref[:, h] = (gamma * s + upd).reshape(grp, HEAD_W, HEAD_W)
        a_ref[:, cols] = _gated_group_norm(o, rg_ref[:, cols]).astype(BF16)


def _ret_decode(rq, rk, rv, rg, state, n_heads):
    n, w = rq.shape
    grp = DEC_GROUP
    row = lambda i: (i, 0)
    st = lambda i: (i, 0, 0, 0)
    spec = pl.BlockSpec((grp, w), row)
    st_spec = pl.BlockSpec((grp, n_heads, HEAD_W, HEAD_W), st)
    return pl.pallas_call(
        functools.partial(_ret_dec_body, grp=grp, n_heads=n_heads),
        grid=(n // grp,),
        in_specs=[spec, spec, spec, spec, st_spec],
        out_specs=[spec, st_spec],
        out_shape=[jax.ShapeDtypeStruct((n, w), BF16), jax.ShapeDtypeStruct(state.shape, F32)],
        compiler_params=_params("arbitrary"),
        name="ret_decode",
    )(rq, rk, rv, rg, state)


def _page_copies(pt_ref, ck_hbm, cv_hbm, kbuf, vbuf, sem, b, slot, n_pages):
    out = []
    for p in range(n_pages):
        page = pt_ref[b * n_pages + p]
        out.append(pltpu.make_async_copy(ck_hbm.at[page], kbuf.at[slot, p], sem.at[0, slot]))
        out.append(pltpu.make_async_copy(cv_hbm.at[page], vbuf.at[slot, p], sem.at[1, slot]))
    return out


def _paged_attn_body(pt_ref, lam_ref, q_ref, kn_ref, vn_ref, ck_hbm, cv_hbm, o_ref,
                     kbuf, vbuf, sem, sc_scr, *, n_pages, page, n_heads, lam_i):
    b = pl.program_id(0)
    nb = pl.num_programs(0)
    slot = b % 2
    w = n_heads * HEAD_W
    nc = 2 * n_heads
    past = n_pages * page
    copies = functools.partial(_page_copies, pt_ref, ck_hbm, cv_hbm, kbuf, vbuf, sem, n_pages=n_pages)

    @pl.when(b == 0)
    def _():
        for c in copies(b=0, slot=0):
            c.start()

    @pl.when(b + 1 < nb)
    def _():
        for c in copies(b=b + 1, slot=1 - slot):
            c.start()

    for c in copies(b=b, slot=slot):
        c.wait()

    ci = lax.broadcasted_iota(jnp.int32, (nc, w), 0)
    ri = lax.broadcasted_iota(jnp.int32, (nc, w), 1)
    col_of = ri // HEAD_W + n_heads * ((ri % HEAD_W) // (HEAD_W // 2))
    q_sel = jnp.where(ci == col_of, jnp.broadcast_to(q_ref[0], (nc, w)), 0.0)
    q_sel_bf = q_sel.astype(BF16)
    rowc = lax.broadcasted_iota(jnp.int32, (nc, page), 0)
    key_i = lax.broadcasted_iota(jnp.int32, (nc, page), 1)
    slope = jnp.zeros((nc, page), F32)
    for h in range(n_heads):
        slope = jnp.where(rowc % n_heads == h, 2.0 ** (-8.0 / n_heads * (h + 1)), slope)

    def scores(p, mx):
        dist = (past - (p * page + key_i)).astype(F32)
        s = jnp.dot(q_sel_bf, kbuf[slot, p].astype(BF16), preferred_element_type=F32) - slope * dist
        sc_scr[p] = s
        return jnp.maximum(mx, s)

    mx = lax.fori_loop(0, n_pages, scores, jnp.full((nc, page), FINITE_MIN, F32), unroll=True)
    kn = kn_ref[0].astype(BF16).astype(F32)
    s_self = jnp.broadcast_to(jnp.sum(q_sel * kn, axis=-1, keepdims=True), (nc, page))
    m = jnp.maximum(jnp.broadcast_to(jnp.max(mx, axis=-1, keepdims=True), (nc, page)), s_self)

    def expsum(p, l):
        e = jnp.exp(sc_scr[p] - m)
        sc_scr[p] = e
        return l + e

    lp = lax.fori_loop(0, n_pages, expsum, jnp.zeros((nc, page), F32), unroll=True)
    e_self = jnp.exp(s_self - m)
    l = jnp.broadcast_to(jnp.sum(lp, axis=-1, keepdims=True), (nc, page)) + e_self
    r = jnp.where(rowc < n_heads, 1.0, _lambda(lam_ref, lam_i)) / l

    def combine(pn):
        return (pn - pltpu.roll(pn, n_heads, axis=0)).astype(BF16)

    def weighted(p, accs):
        a = combine(sc_scr[p] * r)
        return tuple(acc + jnp.dot(a, vbuf[slot, p, pl.ds(h, page, stride=n_heads), :].astype(BF16),
                                   preferred_element_type=F32)
                     for h, acc in enumerate(accs))

    accs = lax.fori_loop(0, n_pages, weighted, tuple(jnp.zeros((nc, HEAD_W), F32) for _ in range(n_heads)),
                         unroll=True)
    a_self = combine(e_self * r).astype(F32)
    vn = vn_ref[0].astype(BF16).astype(F32)
    o_ref[0] = jnp.concatenate(
        [accs[h][h:h + 1, :] + a_self[h:h + 1, 0:HEAD_W] * vn[:, h * HEAD_W:(h + 1) * HEAD_W]
         for h in range(n_heads)], axis=-1)


def _paged_attn(page_table, lamv, dq, dk, dv, cache_kt, cache_v, n_heads, lam_i):
    n, w = dq.shape
    n_pages = page_table.shape[1]
    page = cache_kt.shape[2]
    row = lambda b, pt: (b, 0, 0)
    rspec = pl.BlockSpec((1, 1, w), row)
    body = functools.partial(_paged_attn_body, n_pages=n_pages, page=page, n_heads=n_heads, lam_i=lam_i)
    grid_spec = pltpu.PrefetchScalarGridSpec(
        num_scalar_prefetch=1,
        grid=(n,),
        in_specs=[pl.BlockSpec(lamv.shape, lambda b, pt: (0, 0)), rspec, rspec, rspec,
                  pl.BlockSpec(memory_space=pl.ANY), pl.BlockSpec(memory_space=pl.ANY)],
        out_specs=rspec,
        scratch_shapes=[pltpu.VMEM((2, n_pages, w, page), F32),
                        pltpu.VMEM((2, n_pages, page * n_heads, HEAD_W), F32),
                        pltpu.SemaphoreType.DMA((2, 2)), pltpu.VMEM((n_pages, 2 * n_heads, page), F32)],
    )
    out = pl.pallas_call(
        body,
        grid_spec=grid_spec,
        out_shape=jax.ShapeDtypeStruct((n, 1, w), F32),
        compiler_params=_params("arbitrary"),
        name="paged_attn",
    )(page_table.reshape(-1), lamv, dq.astype(F32).reshape(n, 1, w), dk.reshape(n, 1, w),
      dv.reshape(n, 1, w), cache_kt, cache_v)
    return out.reshape(n, w)


def _dec_ffn_body(a_ref, do_ref, x_ref, wo_ref, gmix_ref, subg_ref, gpre_ref, wg_ref, wu_ref, cw_ref,
                  cb_ref, sc_ref, wout_ref, gpost_ref, y_ref, g_ref, xmid, hbuf, acc, *, lam_i):
    j = pl.program_id(0)
    ret_w = a_ref.shape[1]

    @pl.when(j == 0)
    def _():
        mo = jnp.dot(a_ref[...], wo_ref[0:ret_w, :], preferred_element_type=F32)
        for h in range(do_ref.shape[1] // HEAD_W):
            cols = slice(h * HEAD_W, (h + 1) * HEAD_W)
            bh = (_rms(do_ref[:, cols], subg_ref[...]) * (1.0 - lam_i)).astype(BF16)
            mo += jnp.dot(bh, wo_ref[ret_w + h * HEAD_W:ret_w + (h + 1) * HEAD_W, :],
                          preferred_element_type=F32)
        xm = x_ref[...] + _rms(mo, gmix_ref[...])
        xmid[...] = xm
        hbuf[...] = _rms(xm, gpre_ref[...]).astype(BF16)
        acc[...] = jnp.zeros_like(acc)

    h = hbuf[...]
    g = jnp.dot(h, wg_ref[...], preferred_element_type=F32)
    u = jnp.dot(h, wu_ref[...], preferred_element_type=F32)
    g_ref[...] = g
    c = cw_ref[0:1, :] * sc_ref[0] + cw_ref[1:2, :] * sc_ref[1] + cw_ref[2:3, :] * g + cb_ref[...]
    acc[...] += jnp.dot((jax.nn.gelu(c) * u).astype(BF16), wout_ref[...], preferred_element_type=F32)

    @pl.when(j == pl.num_programs(0) - 1)
    def _():
        y_ref[...] = xmid[...] + _rms(acc[...], gpost_ref[...])


def _dec_ffn(a, do, x2, wo_bf, gmix, subg, gpre, win_bf, cw, cb, conv_state_t, wout_bf, gpost, lam_i):
    n, d = x2.shape
    d_ff = wout_bf.shape[0]
    ck = FFN_CHUNK
    nj = d_ff // ck
    full = lambda shape: pl.BlockSpec(shape, lambda j: (0,) * len(shape))
    return pl.pallas_call(
        functools.partial(_dec_ffn_body, lam_i=lam_i),
        grid=(nj,),
        in_specs=[full(a.shape), full(do.shape), full(x2.shape), full(wo_bf.shape), full((1, d)),
                  full((1, HEAD_W)), full((1, d)),
                  pl.BlockSpec((d, ck), lambda j: (0, j)), pl.BlockSpec((d, ck), lambda j: (0, nj + j)),
                  pl.BlockSpec((3, ck), lambda j: (0, j)), pl.BlockSpec((1, ck), lambda j: (0, j)),
                  pl.BlockSpec((2, n, ck), lambda j: (0, 0, j)),
                  pl.BlockSpec((ck, d), lambda j: (j, 0)), full((1, d))],
        out_specs=[full((n, d)), pl.BlockSpec((n, ck), lambda j: (0, j))],
        out_shape=[jax.ShapeDtypeStruct((n, d), F32), jax.ShapeDtypeStruct((n, d_ff), F32)],
        scratch_shapes=[pltpu.VMEM((n, d), F32), pltpu.VMEM((n, d), BF16), pltpu.VMEM((n, d), F32)],
        compiler_params=_params("arbitrary"),
        name="dec_ffn",
    )(a, do, x2, wo_bf, gmix, subg, gpre, win_bf, win_bf, cw, cb, conv_state_t, wout_bf, gpost)


def kernel(x_prompt, x_sample, state_ret, cache_k, cache_v, state_conv, page_table,
           norm_mix_pre, norm_mix_post, w_in, w_o, lambda_q1, lambda_k1, lambda_q2, lambda_k2,
           subln_g, norm_ffn_pre, norm_ffn_post, w_ffn_in, conv_w, conv_b, w_ffn_out):
    batch, seq, d = x_prompt.shape
    n_dec = x_sample.shape[0]
    assert x_sample.shape[1] == 1, "the sample group is one token per row"
    depth = w_in.shape[0]
    n_heads = state_ret.shape[2]
    dk_ret = state_ret.shape[3]
    dh_diff = cache_k.shape[-1]
    d_ff = w_ffn_out.shape[1]
    w = n_heads * HEAD_W
    assert w_in.shape[2] == 7 * w and cache_v.shape[-1] == HEAD_W and 2 * dh_diff == HEAD_W
    assert seq % MIX_TILE == 0 and seq % FFN_TILE == 0 and d_ff % FFN_CHUNK == 0
    assert n_dec % DEC_GROUP == 0 and n_dec % 128 == 0 and seq % IN_TILE == 0 and IN_TILE % MIX_TILE == 0

    xp = x_prompt.reshape(batch * seq, d)
    xs = x_sample.reshape(n_dec, d)
    outs = [[] for _ in range(8)]
    for l in range(depth):
        lam_i = 0.8 - 0.6 * math.exp(-0.3 * l)
        lamv = jnp.stack([lambda_q1[l], lambda_k1[l], lambda_q2[l], lambda_k2[l]]).astype(F32)
        row = lambda v: v.reshape(1, -1)
        w_in_bf = w_in[l].astype(BF16)
        w_o_bf = w_o[l].astype(BF16)
        w_ffn_in_bf = w_ffn_in[l].astype(BF16)
        w_ffn_out_bf = w_ffn_out[l].astype(BF16)
        g_pre, g_post = row(norm_mix_pre[l]), row(norm_mix_post[l])
        f_pre, f_post = row(norm_ffn_pre[l]), row(norm_ffn_post[l])
        subg = row(subln_g[l])
        cw, cb = conv_w[l], row(conv_b[l])

        wkt_bf = w_in[l][:, 5 * w:6 * w].T.astype(BF16)
        cache_kt = jnp.transpose(cache_k[l], (0, 2, 3, 4, 1)).reshape(cache_k.shape[1], w, cache_k.shape[2])
        cache_vr = cache_v[l].reshape(cache_v.shape[1], cache_v.shape[2] * n_heads, HEAD_W)
        proj = functools.partial(_inproj, n_heads=n_heads, dk_ret=dk_ret, dh_diff=dh_diff)

        rq, rk, rv, rg, dq, kt, v4, ktb, dvb = proj(xp, g_pre, w_in_bf, wkt_bf, batch, seq, IN_TILE,
                                                     key_block=MIX_TILE)
        xp, s_fin = _mixer(lamv, rq, rk, rv, rg, dq, ktb, dvb, xp, w_o_bf, g_post, subg,
                           batch, seq, n_heads, lam_i)
        xp, cs = _ffn(xp, f_pre, w_ffn_in_bf, cw, cb, w_ffn_out_bf, f_post, batch, seq)
        outs[0].append(s_fin)
        outs[2].append(jnp.transpose(kt.reshape(batch, n_heads, 2, dh_diff, seq), (0, 4, 1, 2, 3)))
        outs[3].append(v4.reshape(batch, seq, n_heads, HEAD_W))
        outs[6].append(cs[:, 6:8, :])

        rq, rk, rv, rg, dq, kt, v4, dk, dv = proj(xs, g_pre, w_in_bf, wkt_bf, 1, n_dec, n_dec, key_block=None)
        a, s_new = _ret_decode(rq, rk, rv, rg, state_ret[l], n_heads)
        do = _paged_attn(page_table, lamv, dq, dk, dv, cache_kt, cache_vr, n_heads, lam_i)
        conv_t = jnp.swapaxes(state_conv[l], 0, 1)
        xs, g_new = _dec_ffn(a, do, xs, w_o_bf, g_post, subg, f_pre, w_ffn_in_bf, cw, cb, conv_t,
                             w_ffn_out_bf, f_post, lam_i)
        outs[1].append(s_new)
        outs[4].append(jnp.transpose(kt.reshape(n_heads, 2, dh_diff, n_dec), (3, 0, 1, 2))[:, None])
        outs[5].append(v4.reshape(n_dec, 1, n_heads, HEAD_W))
        outs[7].append(jnp.stack([state_conv[l][:, 1, :], g_new], axis=1))

    st = [jnp.stack(o) for o in outs]
    return (xp.reshape(batch, seq, d), xs.reshape(n_dec, 1, d), st[0], st[1], st[2], st[3], st[4], st[5],
            st[6], st[7])
```

```python
import functools
import math

import jax
import jax.numpy as jnp
from jax import lax
from jax.experimental import pallas as pl
from jax.experimental.pallas import tpu as pltpu

F32 = jnp.float32
BF16 = jnp.bfloat16
EPS = 1e-6
NEG_INF = -1e30
FINITE_MIN = -3e38

V7X_VMEM_BYTES = 64 * 1024 * 1024
VMEM_LIMIT_BYTES = V7X_VMEM_BYTES * 7 // 8

RET_CHUNK = 128
HEAD_W = 128
IN_TILE = 512
MIX_TILE = 256
FFN_TILE = 256
FFN_CHUNK = 256
DEC_GROUP = 16
NT_DIMS = (((1,), (1,)), ((), ()))
TN_DIMS = (((0,), (0,)), ((), ()))


def _rms(x, g):
    return x * lax.rsqrt(jnp.mean(x * x, axis=-1, keepdims=True) + EPS) * g


def _params(*sem):
    return pltpu.CompilerParams(dimension_semantics=sem, vmem_limit_bytes=VMEM_LIMIT_BYTES)


def _const_spec(shape):
    n = len(shape)
    return pl.BlockSpec(shape, lambda *_: (0,) * n, pipeline_mode=pl.Buffered(1))


def _ret_log_decay(h):
    return math.log(1.0 - 2.0 ** (-5.0 - h))


def _lambda(lam_ref, lam_i):
    a = jnp.sum(lam_ref[0:1, :] * lam_ref[1:2, :], axis=-1, keepdims=True)
    b = jnp.sum(lam_ref[2:3, :] * lam_ref[3:4, :], axis=-1, keepdims=True)
    return jnp.exp(a) - jnp.exp(b) + lam_i


def _inproj_body(x_ref, g_ref, w_ref, wkt_ref, rq_ref, rk_ref, rv_ref, rg_ref, dq_ref, kt_ref, v4_ref,
                 kx_ref, vx_ref, *, width, n_heads, k_scale, q_scale, key_block):
    h = _rms(x_ref[...], g_ref[...]).astype(BF16)

    def col(j):
        return jnp.dot(h, w_ref[:, j * width:(j + 1) * width], preferred_element_type=F32)

    rq_ref[...] = col(0)
    rk_ref[...] = col(1) * k_scale
    rv_ref[...] = col(2).astype(BF16)
    rg_ref[...] = col(3)
    dq_ref[...] = (col(4) * q_scale).astype(BF16)
    kt = lax.dot_general(wkt_ref[...], h, NT_DIMS, preferred_element_type=F32)
    kt_ref[0] = kt
    dv = col(6)
    for hh in range(n_heads):
        v4_ref[:, hh, :] = dv[:, hh * HEAD_W:(hh + 1) * HEAD_W]
    if key_block is None:
        kx_ref[...] = col(5)
        vx_ref[...] = dv
    else:
        for c in range(kx_ref.shape[0]):
            kx_ref[c] = kt[:, c * key_block:(c + 1) * key_block].astype(BF16)
        vx_ref[...] = dv.astype(BF16)


def _inproj(x2, g, w_bf, wkt_bf, batch, seq, tile, n_heads, dk_ret, dh_diff, key_block):
    m, d = x2.shape
    width = wkt_bf.shape[0]
    nt = seq // tile
    row = lambda i: (i, 0)
    f32_out = jax.ShapeDtypeStruct((m, width), F32)
    bf_out = jax.ShapeDtypeStruct((m, width), BF16)
    spec = pl.BlockSpec((tile, width), row)
    if key_block is None:
        kx_shape, kx_spec, vx_shape = f32_out, spec, f32_out
    else:
        kx_shape = jax.ShapeDtypeStruct((m // key_block, width, key_block), BF16)
        kx_spec = pl.BlockSpec((tile // key_block, width, key_block), lambda i: (i, 0, 0))
        vx_shape = bf_out
    body = functools.partial(_inproj_body, width=width, n_heads=n_heads, k_scale=dk_ret ** -0.5,
                             q_scale=dh_diff ** -0.5, key_block=key_block)
    return pl.pallas_call(
        body,
        grid=(m // tile,),
        in_specs=[pl.BlockSpec((tile, d), row), _const_spec((1, d)), _const_spec(w_bf.shape),
                  _const_spec(wkt_bf.shape)],
        out_specs=[spec] * 5 + [pl.BlockSpec((1, width, tile), lambda i: (i // nt, 0, i % nt)),
                                pl.BlockSpec((tile, n_heads, HEAD_W), lambda i: (i, 0, 0)), kx_spec, spec],
        out_shape=[f32_out, f32_out, bf_out, f32_out, bf_out,
                   jax.ShapeDtypeStruct((batch, width, seq), F32),
                   jax.ShapeDtypeStruct((m, n_heads, HEAD_W), F32), kx_shape, vx_shape],
        compiler_params=_params("arbitrary"),
        name="inproj",
    )(x2, g, w_bf, wkt_bf)


def _retention_consts(length, n_heads):
    i = lax.broadcasted_iota(jnp.int32, (length, length), 0)
    j = lax.broadcasted_iota(jnp.int32, (length, length), 1)
    diff = (i - j).astype(F32)
    causal = diff >= 0
    ri = lax.broadcasted_iota(jnp.int32, (length, 1), 0).astype(F32)
    out = []
    for h in range(n_heads):
        lg = _ret_log_decay(h)
        decay = jnp.where(causal, jnp.exp(jnp.where(causal, diff, 0.0) * lg), 0.0)
        qdec = jnp.exp((ri + 1.0) * lg)
        kdec = jnp.exp((length - 1.0 - ri) * lg)
        out.append((decay, qdec, kdec, math.exp(length * lg)))
    return out


def _gated_group_norm(o, g):
    on = o * lax.rsqrt(jnp.mean(o * o, axis=-1, keepdims=True) + EPS)
    return on * (g * jax.nn.sigmoid(g))


def _mixer_body(lam_ref, rq_ref, rk_ref, rv_ref, rg_ref, dq_ref, kt_ref, vb_ref, x_ref, wo_ref,
                gpost_ref, subg_ref, y_ref, s_ref, mix_scr, qq_scr, sc_scr, mx_scr, l_scr, acc_scr,
                *, tq, n_heads, lam_i):
    t = pl.program_id(1)
    ret_w = n_heads * HEAD_W

    @pl.when(t == 0)
    def _():
        s_ref[...] = jnp.zeros_like(s_ref)

    consts = _retention_consts(RET_CHUNK, n_heads)
    for c in range(tq // RET_CHUNK):
        rows = slice(c * RET_CHUNK, (c + 1) * RET_CHUNK)
        for h in range(n_heads):
            cols = slice(h * HEAD_W, (h + 1) * HEAD_W)
            decay, qdec, kdec, gl = consts[h]
            q = rq_ref[rows, cols]
            k = rk_ref[rows, cols]
            v = rv_ref[rows, cols]
            s = s_ref[0, h]
            att = lax.dot_general(q.astype(BF16), k.astype(BF16), NT_DIMS,
                                  preferred_element_type=F32) * decay
            o = (jnp.dot(att.astype(BF16), v, preferred_element_type=F32)
                 + jnp.dot((q * qdec).astype(BF16), s.astype(BF16), preferred_element_type=F32))
            s_ref[0, h] = gl * s + lax.dot_general((k * kdec).astype(BF16), v, TN_DIMS,
                                                   preferred_element_type=F32)
            mix_scr[rows, cols] = _gated_group_norm(o, rg_ref[rows, cols]).astype(BF16)

    lane = lax.broadcasted_iota(jnp.int32, (tq, HEAD_W), 1)
    for h in range(n_heads):
        qh = dq_ref[:, h * HEAD_W:(h + 1) * HEAD_W]
        zero = jnp.zeros_like(qh)
        qq_scr[h, 0:tq] = jnp.where(lane < HEAD_W // 2, qh, zero)
        qq_scr[h, tq:2 * tq] = jnp.where(lane >= HEAD_W // 2, qh, zero)
    mx_scr[...] = jnp.full(mx_scr.shape, FINITE_MIN, F32)
    l_scr[...] = jnp.zeros_like(l_scr)
    acc_scr[...] = jnp.zeros_like(acc_scr)
    key_j = lax.broadcasted_iota(jnp.int32, (1, tq), 1).astype(F32)
    lane_tiles = [slice(c * HEAD_W, (c + 1) * HEAD_W) for c in range(tq // HEAD_W)]

    def fold(x, op):
        return functools.reduce(op, [x[:, c] for c in lane_tiles])

    def scores(kb, causal):
        key_pos = key_j + ((kb - t) * tq).astype(F32)
        for h in range(n_heads):
            slope = 2.0 ** (-8.0 / n_heads * (h + 1))
            s = jnp.dot(qq_scr[h], kt_ref[kb, h * HEAD_W:(h + 1) * HEAD_W, :], preferred_element_type=F32)
            s = s + slope * key_pos
            if causal is not None:
                s = jnp.where(causal, s, NEG_INF)
            sc_scr[kb, h] = s
            mx_scr[h] = jnp.maximum(mx_scr[h], fold(s, jnp.maximum))

    def off_diagonal(kb, carry):
        scores(kb, None)
        return carry

    lax.fori_loop(0, t, off_diagonal, 0)
    ii = lax.broadcasted_iota(jnp.int32, (2 * tq, tq), 0)
    jj = lax.broadcasted_iota(jnp.int32, (2 * tq, tq), 1)
    scores(t, jnp.where(ii >= tq, ii - tq, ii) >= jj)

    for h in range(n_heads):
        mx_scr[h] = jnp.broadcast_to(jnp.max(mx_scr[h], axis=-1, keepdims=True), (2 * tq, HEAD_W))

    def weighted(kb, carry):
        for h in range(n_heads):
            m = mx_scr[h]
            e = [jnp.exp(sc_scr[kb, h, :, c] - m) for c in lane_tiles]
            l_scr[h] += functools.reduce(jnp.add, e)
            vr = vb_ref[pl.ds(pl.multiple_of(kb * tq, tq), tq), h * HEAD_W:(h + 1) * HEAD_W]
            acc_scr[h] += jnp.dot(jnp.concatenate([x.astype(BF16) for x in e], axis=1), vr,
                                  preferred_element_type=F32)
        return carry

    lax.fori_loop(0, t + 1, weighted, 0)
    lam = _lambda(lam_ref, lam_i)
    subg = subg_ref[...]
    for h in range(n_heads):
        on = acc_scr[h] / jnp.sum(l_scr[h], axis=-1, keepdims=True)
        o = _rms(on[:tq] - lam * on[tq:], subg) * (1.0 - lam_i)
        mix_scr[:, ret_w + h * HEAD_W:ret_w + (h + 1) * HEAD_W] = o.astype(BF16)

    mo = jnp.dot(mix_scr[...], wo_ref[...], preferred_element_type=F32)
    y_ref[...] = x_ref[...] + _rms(mo, gpost_ref[...])


def _mixer(lamv, rq, rk, rv, rg, dq, ktb, dvb, x2, wo_bf, gpost, subg, batch, seq, n_heads, lam_i):
    m, d = x2.shape
    w = rq.shape[1]
    tq = MIX_TILE
    nt = seq // tq
    tile = lambda b, t: (b * nt + t, 0)
    whole = lambda b, t: (b, 0)
    body = functools.partial(_mixer_body, tq=tq, n_heads=n_heads, lam_i=lam_i)
    return pl.pallas_call(
        body,
        grid=(batch, nt),
        in_specs=[
            _const_spec(lamv.shape),
            pl.BlockSpec((tq, w), tile), pl.BlockSpec((tq, w), tile), pl.BlockSpec((tq, w), tile),
            pl.BlockSpec((tq, w), tile), pl.BlockSpec((tq, w), tile),
            pl.BlockSpec((nt, w, tq), lambda b, t: (b, 0, 0)), pl.BlockSpec((seq, w), whole),
            pl.BlockSpec((tq, d), tile),
            _const_spec(wo_bf.shape), _const_spec((1, d)), _const_spec((1, HEAD_W)),
        ],
        out_specs=[pl.BlockSpec((tq, d), tile),
                   pl.BlockSpec((1, n_heads, HEAD_W, HEAD_W), lambda b, t: (b, 0, 0, 0))],
        out_shape=[jax.ShapeDtypeStruct((m, d), F32),
                   jax.ShapeDtypeStruct((batch, n_heads, HEAD_W, HEAD_W), F32)],
        scratch_shapes=[
            pltpu.VMEM((tq, 2 * w), BF16),
            pltpu.VMEM((n_heads, 2 * tq, HEAD_W), BF16),
            pltpu.VMEM((nt, n_heads, 2 * tq, tq), F32),
            pltpu.VMEM((n_heads, 2 * tq, HEAD_W), F32),
            pltpu.VMEM((n_heads, 2 * tq, HEAD_W), F32),
            pltpu.VMEM((n_heads, 2 * tq, HEAD_W), F32),
        ],
        compiler_params=_params("arbitrary", "arbitrary"),
        name="mixer",
    )(lamv, rq, rk, rv, rg, dq, ktb, dvb, x2, wo_bf, gpost, subg)


def _conv_gate(g, u, g1, g2, cw_ref, cb_ref, cols):
    c = cw_ref[0:1, cols] * g2 + cw_ref[1:2, cols] * g1 + cw_ref[2:3, cols] * g + cb_ref[:, cols]
    return (jax.nn.gelu(c) * u).astype(BF16)


def _ffn_body(pt_ref, x_ref, gpre_ref, win_ref, cw_ref, cb_ref, wout_ref, gpost_ref,
              lam_ref, q_ref, kn_ref, vn_ref, ck_hbm, cv_hbm, y_ref, cs_ref, o_ref,
              gbuf, carry, act, kbuf, vbuf, sem, sc_scr, *, tf, d_ff, ck, rows, n_pages, page, n_heads, lam_i):
    t = pl.program_id(1)
    step = pl.program_id(0) * pl.num_programs(1) + t
    n_rows = rows * pl.num_programs(0) * pl.num_programs(1)
    copies = functools.partial(_page_copies, pt_ref, ck_hbm, cv_hbm, kbuf, vbuf, sem, n_pages=n_pages)

    @pl.when(step == 0)
    def _():
        for slot in range(2):
            for c in copies(b=slot, slot=slot):
                c.start()

    @pl.when(t == 0)
    def _():
        carry[...] = jnp.zeros_like(carry)

    lam = _lambda(lam_ref, lam_i)
    for i in range(rows):
        g = step * rows + i
        slot = i % 2
        for c in copies(b=g, slot=slot):
            c.wait()
        o_ref[i] = _paged_row(slot, q_ref[i], kn_ref[i], vn_ref[i], lam, kbuf, vbuf, sc_scr,
                              n_pages=n_pages, page=page, n_heads=n_heads)

        @pl.when(g + 2 < n_rows)
        def _():
            for c in copies(b=g + 2, slot=slot):
                c.start()

    x = x_ref[...]
    h = _rms(x, gpre_ref[...]).astype(BF16)
    for j in range(d_ff // ck):
        cols = slice(j * ck, (j + 1) * ck)
        g = jnp.dot(h, win_ref[:, cols], preferred_element_type=F32)
        u = jnp.dot(h, win_ref[:, d_ff + j * ck:d_ff + (j + 1) * ck], preferred_element_type=F32)
        gbuf[0:8, :] = carry[:, cols]
        gbuf[8:8 + tf, :] = g
        act[:, cols] = _conv_gate(g, u, gbuf[7:7 + tf, :], gbuf[6:6 + tf, :], cw_ref, cb_ref, cols)
        carry[:, cols] = gbuf[tf:tf + 8, :]
    f = jnp.dot(act[...], wout_ref[...], preferred_element_type=F32)
    y_ref[...] = x + _rms(f, gpost_ref[...])
    cs_ref[0] = carry[...]


def _ffn_paged(x2, gpre, win_bf, cw, cb, wout_bf, gpost, batch, seq,
               page_table, lamv, dq, dk, dv, cache_kt, cache_v, n_heads, lam_i):
    m, d = x2.shape
    d_ff = wout_bf.shape[0]
    tf = FFN_TILE
    nt = seq // tf
    n, w = dq.shape
    rows = n // (batch * nt)
    assert rows * batch * nt == n and rows % 2 == 0, "sample rows must split evenly, in pairs, over the FFN steps"
    n_pages = page_table.shape[1]
    page = cache_kt.shape[2]
    tile = lambda b, t, pt: (b * nt + t, 0)
    const = _const_spec
    rspec = pl.BlockSpec((rows, 1, w), lambda b, t, pt: (b * nt + t, 0, 0))
    body = functools.partial(_ffn_body, tf=tf, d_ff=d_ff, ck=FFN_CHUNK, rows=rows, n_pages=n_pages, page=page,
                             n_heads=n_heads, lam_i=lam_i)
    grid_spec = pltpu.PrefetchScalarGridSpec(
        num_scalar_prefetch=1,
        grid=(batch, nt),
        in_specs=[pl.BlockSpec((tf, d), tile), const((1, d)), const(win_bf.shape), const(cw.shape),
                  const(cb.shape), const(wout_bf.shape), const((1, d)),
                  const(lamv.shape), rspec, rspec, rspec,
                  pl.BlockSpec(memory_space=pl.ANY), pl.BlockSpec(memory_space=pl.ANY)],
        out_specs=[pl.BlockSpec((tf, d), tile), pl.BlockSpec((1, 8, d_ff), lambda b, t, pt: (b, 0, 0)), rspec],
        scratch_shapes=[pltpu.VMEM((tf + 8, FFN_CHUNK), F32), pltpu.VMEM((8, d_ff), F32),
                        pltpu.VMEM((tf, d_ff), BF16),
                        pltpu.VMEM((2, n_pages, w, page), F32),
                        pltpu.VMEM((2, n_pages, page * n_heads, HEAD_W), F32),
                        pltpu.SemaphoreType.DMA((2, 2)), pltpu.VMEM((n_pages, 2 * n_heads, page), F32)],
    )
    y, cs, o = pl.pallas_call(
        body,
        grid_spec=grid_spec,
        out_shape=[jax.ShapeDtypeStruct((m, d), F32), jax.ShapeDtypeStruct((batch, 8, d_ff), F32),
                   jax.ShapeDtypeStruct((n, 1, w), F32)],
        compiler_params=_params("arbitrary", "arbitrary"),
        name="conv_ffn_paged_attn",
    )(page_table.reshape(-1), x2, gpre, win_bf, cw, cb, wout_bf, gpost,
      lamv, dq.astype(F32).reshape(n, 1, w), dk.reshape(n, 1, w), dv.reshape(n, 1, w), cache_kt, cache_v)
    return y, cs, o.reshape(n, w)


def _ret_dec_body(rq_ref, rk_ref, rv_ref, rg_ref, s_ref, a_ref, sn_ref, *, grp, n_heads):
    row = lax.broadcasted_iota(jnp.int32, (grp, grp * HEAD_W), 0)
    lane = lax.broadcasted_iota(jnp.int32, (grp, grp * HEAD_W), 1)
    own = (lane // HEAD_W) == row

    def block_diag(x):
        return jnp.where(own, jnp.tile(x, (1, grp)), 0.0).astype(BF16)

    for h in range(n_heads):
        cols = slice(h * HEAD_W, (h + 1) * HEAD_W)
        gamma = math.exp(_ret_log_decay(h))
        q = rq_ref[:, cols]
        k = rk_ref[:, cols]
        v = rv_ref[:, cols]
        qb = q.astype(BF16).astype(F32)
        kb = k.astype(BF16).astype(F32)
        att = jnp.sum(qb * kb, axis=-1, keepdims=True).astype(BF16).astype(F32)
        s = s_ref[:, h].reshape(grp * HEAD_W, HEAD_W)
        o = att * v.astype(F32) + jnp.dot(block_diag(q * gamma), s.astype(BF16),
                                          preferred_element_type=F32)
        upd = lax.dot_general(block_diag(k), v, TN_DIMS, preferred_element_type=F32)
        sn_ref[:, h] = (gamma * s + upd).reshape(grp, HEAD_W, HEAD_W)
        a_ref[:, cols] = _gated_group_norm(o, rg_ref[:, cols]).astype(BF16)


def _ret_decode(rq, rk, rv, rg, state, n_heads):
    n, w = rq.shape
    grp = DEC_GROUP
    row = lambda i: (i, 0)
    st = lambda i: (i, 0, 0, 0)
    spec = pl.BlockSpec((grp, w), row)
    st_spec = pl.BlockSpec((grp, n_heads, HEAD_W, HEAD_W), st)
    return pl.pallas_call(
        functools.partial(_ret_dec_body, grp=grp, n_heads=n_heads),
        grid=(n // grp,),
        in_specs=[spec, spec, spec, spec, st_spec],
        out_specs=[spec, st_spec],
        out_shape=[jax.ShapeDtypeStruct((n, w), BF16), jax.ShapeDtypeStruct(state.shape, F32)],
        compiler_params=_params("arbitrary"),
        name="ret_decode",
    )(rq, rk, rv, rg, state)


def _page_copies(pt_ref, ck_hbm, cv_hbm, kbuf, vbuf, sem, b, slot, n_pages):
    out = []
    for p in range(n_pages):
        page = pt_ref[b * n_pages + p]
        out.append(pltpu.make_async_copy(ck_hbm.at[page], kbuf.at[slot, p], sem.at[0, slot]))
        out.append(pltpu.make_async_copy(cv_hbm.at[page], vbuf.at[slot, p], sem.at[1, slot]))
    return out


def _paged_row(slot, q, kn_row, vn_row, lam, kbuf, vbuf, sc_scr, *, n_pages, page, n_heads):
    w = n_heads * HEAD_W
    nc = 2 * n_heads
    past = n_pages * page
    ci = lax.broadcasted_iota(jnp.int32, (nc, w), 0)
    ri = lax.broadcasted_iota(jnp.int32, (nc, w), 1)
    col_of = ri // HEAD_W + n_heads * ((ri % HEAD_W) // (HEAD_W // 2))
    q_sel = jnp.where(ci == col_of, jnp.broadcast_to(q, (nc, w)), 0.0)
    q_sel_bf = q_sel.astype(BF16)
    rowc = lax.broadcasted_iota(jnp.int32, (nc, page), 0)
    key_i = lax.broadcasted_iota(jnp.int32, (nc, page), 1)
    slope = jnp.zeros((nc, page), F32)
    for h in range(n_heads):
        slope = jnp.where(rowc % n_heads == h, 2.0 ** (-8.0 / n_heads * (h + 1)), slope)

    mx = jnp.full((nc, page), FINITE_MIN, F32)
    for p in range(n_pages):
        dist = (past - (p * page + key_i)).astype(F32)
        s = jnp.dot(q_sel_bf, kbuf[slot, p].astype(BF16), preferred_element_type=F32) - slope * dist
        sc_scr[p] = s
        mx = jnp.maximum(mx, s)
    kn = kn_row.astype(BF16).astype(F32)
    s_self = jnp.broadcast_to(jnp.sum(q_sel * kn, axis=-1, keepdims=True), (nc, page))
    m = jnp.maximum(jnp.broadcast_to(jnp.max(mx, axis=-1, keepdims=True), (nc, page)), s_self)

    lp = jnp.zeros((nc, page), F32)
    for p in range(n_pages):
        e = jnp.exp(sc_scr[p] - m)
        sc_scr[p] = e
        lp = lp + e
    e_self = jnp.exp(s_self - m)
    l = jnp.broadcast_to(jnp.sum(lp, axis=-1, keepdims=True), (nc, page)) + e_self
    r = jnp.where(rowc < n_heads, 1.0, lam) / l

    def combine(pn):
        return (pn - pltpu.roll(pn, n_heads, axis=0)).astype(BF16)

    accs = [jnp.zeros((nc, HEAD_W), F32) for _ in range(n_heads)]
    for p in range(n_pages):
        a = combine(sc_scr[p] * r)
        for h in range(n_heads):
            vh = vbuf[slot, p, pl.ds(h, page, stride=n_heads), :].astype(BF16)
            accs[h] = accs[h] + jnp.dot(a, vh, preferred_element_type=F32)
    a_self = combine(e_self * r).astype(F32)
    vn = vn_row.astype(BF16).astype(F32)
    return jnp.concatenate(
        [accs[h][h:h + 1, :] + a_self[h:h + 1, 0:HEAD_W] * vn[:, h * HEAD_W:(h + 1) * HEAD_W]
         for h in range(n_heads)], axis=-1)


def _dec_ffn_body(a_ref, do_ref, x_ref, wo_ref, gmix_ref, subg_ref, gpre_ref, wg_ref, wu_ref, cw_ref,
                  cb_ref, sc_ref, wout_ref, gpost_ref, y_ref, g_ref, xmid, hbuf, acc, *, lam_i):
    j = pl.program_id(0)
    ret_w = a_ref.shape[1]

    @pl.when(j == 0)
    def _():
        mo = jnp.dot(a_ref[...], wo_ref[0:ret_w, :], preferred_element_type=F32)
        for h in range(do_ref.shape[1] // HEAD_W):
            cols = slice(h * HEAD_W, (h + 1) * HEAD_W)
            bh = (_rms(do_ref[:, cols], subg_ref[...]) * (1.0 - lam_i)).astype(BF16)
            mo += jnp.dot(bh, wo_ref[ret_w + h * HEAD_W:ret_w + (h + 1) * HEAD_W, :],
                          preferred_element_type=F32)
        xm = x_ref[...] + _rms(mo, gmix_ref[...])
        xmid[...] = xm
        hbuf[...] = _rms(xm, gpre_ref[...]).astype(BF16)
        acc[...] = jnp.zeros_like(acc)

    h = hbuf[...]
    g = jnp.dot(h, wg_ref[...], preferred_element_type=F32)
    u = jnp.dot(h, wu_ref[...], preferred_element_type=F32)
    g_ref[...] = g
    c = cw_ref[0:1, :] * sc_ref[0] + cw_ref[1:2, :] * sc_ref[1] + cw_ref[2:3, :] * g + cb_ref[...]
    acc[...] += jnp.dot((jax.nn.gelu(c) * u).astype(BF16), wout_ref[...], preferred_element_type=F32)

    @pl.when(j == pl.num_programs(0) - 1)
    def _():
        y_ref[...] = xmid[...] + _rms(acc[...], gpost_ref[...])


def _dec_ffn(a, do, x2, wo_bf, gmix, subg, gpre, win_bf, cw, cb, conv_state_t, wout_bf, gpost, lam_i):
    n, d = x2.shape
    d_ff = wout_bf.shape[0]
    ck = FFN_CHUNK
    nj = d_ff // ck
    full = lambda shape: pl.BlockSpec(shape, lambda j: (0,) * len(shape))
    return pl.pallas_call(
        functools.partial(_dec_ffn_body, lam_i=lam_i),
        grid=(nj,),
        in_specs=[full(a.shape), full(do.shape), full(x2.shape), full(wo_bf.shape), full((1, d)),
                  full((1, HEAD_W)), full((1, d)),
                  pl.BlockSpec((d, ck), lambda j: (0, j)), pl.BlockSpec((d, ck), lambda j: (0, nj + j)),
                  pl.BlockSpec((3, ck), lambda j: (0, j)), pl.BlockSpec((1, ck), lambda j: (0, j)),
                  pl.BlockSpec((2, n, ck), lambda j: (0, 0, j)),
                  pl.BlockSpec((ck, d), lambda j: (j, 0)), full((1, d))],
        out_specs=[full((n, d)), pl.BlockSpec((n, ck), lambda j: (0, j))],
        out_shape=[jax.ShapeDtypeStruct((n, d), F32), jax.ShapeDtypeStruct((n, d_ff), F32)],
        scratch_shapes=[pltpu.VMEM((n, d), F32), pltpu.VMEM((n, d), BF16), pltpu.VMEM((n, d), F32)],
        compiler_params=_params("arbitrary"),
        name="dec_ffn",
    )(a, do, x2, wo_bf, gmix, subg, gpre, win_bf, win_bf, cw, cb, conv_state_t, wout_bf, gpost)


def kernel(x_prompt, x_sample, state_ret, cache_k, cache_v, state_conv, page_table,
           norm_mix_pre, norm_mix_post, w_in, w_o, lambda_q1, lambda_k1, lambda_q2, lambda_k2,
           subln_g, norm_ffn_pre, norm_ffn_post, w_ffn_in, conv_w, conv_b, w_ffn_out):
    batch, seq, d = x_prompt.shape
    n_dec = x_sample.shape[0]
    assert x_sample.shape[1] == 1, "the sample group is one token per row"
    depth = w_in.shape[0]
    n_heads = state_ret.shape[2]
    dk_ret = state_ret.shape[3]
    dh_diff = cache_k.shape[-1]
    d_ff = w_ffn_out.shape[1]
    w = n_heads * HEAD_W
    assert w_in.shape[2] == 7 * w and cache_v.shape[-1] == HEAD_W and 2 * dh_diff == HEAD_W
    assert seq % MIX_TILE == 0 and seq % FFN_TILE == 0 and d_ff % FFN_CHUNK == 0
    assert n_dec % DEC_GROUP == 0 and n_dec % 128 == 0 and seq % IN_TILE == 0 and IN_TILE % MIX_TILE == 0

    xp = x_prompt.reshape(batch * seq, d)
    xs = x_sample.reshape(n_dec, d)
    outs = [[] for _ in range(8)]
    for l in range(depth):
        lam_i = 0.8 - 0.6 * math.exp(-0.3 * l)
        lamv = jnp.stack([lambda_q1[l], lambda_k1[l], lambda_q2[l], lambda_k2[l]]).astype(F32)
        row = lambda v: v.reshape(1, -1)
        w_in_bf = w_in[l].astype(BF16)
        w_o_bf = w_o[l].astype(BF16)
        w_ffn_in_bf = w_ffn_in[l].astype(BF16)
        w_ffn_out_bf = w_ffn_out[l].astype(BF16)
        g_pre, g_post = row(norm_mix_pre[l]), row(norm_mix_post[l])
        f_pre, f_post = row(norm_ffn_pre[l]), row(norm_ffn_post[l])
        subg = row(subln_g[l])
        cw, cb = conv_w[l], row(conv_b[l])

        wkt_bf = w_in[l][:, 5 * w:6 * w].T.astype(BF16)
        cache_kt = jnp.transpose(cache_k[l], (0, 2, 3, 4, 1)).reshape(cache_k.shape[1], w, cache_k.shape[2])
        cache_vr = cache_v[l].reshape(cache_v.shape[1], cache_v.shape[2] * n_heads, HEAD_W)
        proj = functools.partial(_inproj, n_heads=n_heads, dk_ret=dk_ret, dh_diff=dh_diff)

        rq, rk, rv, rg, dq, kt, v4, ktb, dvb = proj(xp, g_pre, w_in_bf, wkt_bf, batch, seq, IN_TILE,
                                                     key_block=MIX_TILE)
        xp, s_fin = _mixer(lamv, rq, rk, rv, rg, dq, ktb, dvb, xp, w_o_bf, g_post, subg,
                           batch, seq, n_heads, lam_i)
        outs[0].append(s_fin)
        outs[2].append(jnp.transpose(kt.reshape(batch, n_heads, 2, dh_diff, seq), (0, 4, 1, 2, 3)))
        outs[3].append(v4.reshape(batch, seq, n_heads, HEAD_W))
        rq, rk, rv, rg, dq, kt, v4, dk, dv = proj(xs, g_pre, w_in_bf, wkt_bf, 1, n_dec, n_dec, key_block=None)
        a, s_new = _ret_decode(rq, rk, rv, rg, state_ret[l], n_heads)

        xp, cs, do = _ffn_paged(xp, f_pre, w_ffn_in_bf, cw, cb, w_ffn_out_bf, f_post, batch, seq,
                                page_table, lamv, dq, dk, dv, cache_kt, cache_vr, n_heads, lam_i)
        outs[6].append(cs[:, 6:8, :])

        conv_t = jnp.swapaxes(state_conv[l], 0, 1)
        xs, g_new = _dec_ffn(a, do, xs, w_o_bf, g_post, subg, f_pre, w_ffn_in_bf, cw, cb, conv_t,
                             w_ffn_out_bf, f_post, lam_i)
        outs[1].append(s_new)
        outs[4].append(jnp.transpose(kt.reshape(n_heads, 2, dh_diff, n_dec), (3, 0, 1, 2))[:, None])
        outs[5].append(v4.reshape(n_dec, 1, n_heads, HEAD_W))
        outs[7].append(jnp.stack([state_conv[l][:, 1, :], g_new], axis=1))

    st = [jnp.stack(o) for o in outs]
    return (xp.reshape(batch, seq, d), xs.reshape(n_dec, 1, d), st[0], st[1], st[2], st[3], st[4], st[5],
            st[6], st[7])
```

```python
import functools
import math

import jax
import jax.numpy as jnp
from jax import lax
from jax.experimental import pallas as pl
from jax.experimental.pallas import tpu as pltpu

F32 = jnp.float32
BF16 = jnp.bfloat16
EPS = 1e-6
NEG_INF = -1e30
FINITE_MIN = -3e38

V7X_VMEM_BYTES = 64 * 1024 * 1024
VMEM_LIMIT_BYTES = V7X_VMEM_BYTES * 7 // 8

RET_CHUNK = 256
HEAD_W = 128
IN_TILE = 512
MIX_TILE = 256
FFN_TILE = 256
FFN_CHUNK = 256
DEC_GROUP = 16
NT_DIMS = (((1,), (1,)), ((), ()))
TN_DIMS = (((0,), (0,)), ((), ()))


def _rms(x, g):
    return x * lax.rsqrt(jnp.mean(x * x, axis=-1, keepdims=True) + EPS) * g


def _params(*sem):
    return pltpu.CompilerParams(dimension_semantics=sem, vmem_limit_bytes=VMEM_LIMIT_BYTES)


def _const_spec(shape):
    n = len(shape)
    return pl.BlockSpec(shape, lambda *_: (0,) * n, pipeline_mode=pl.Buffered(1))


def _ret_log_decay(h):
    return math.log(1.0 - 2.0 ** (-5.0 - h))


def _lambda(lam_ref, lam_i):
    a = jnp.sum(lam_ref[0:1, :] * lam_ref[1:2, :], axis=-1, keepdims=True)
    b = jnp.sum(lam_ref[2:3, :] * lam_ref[3:4, :], axis=-1, keepdims=True)
    return jnp.exp(a) - jnp.exp(b) + lam_i


def _inproj_body(x_ref, g_ref, w_ref, wkt_ref, rq_ref, rk_ref, rv_ref, rg_ref, dq_ref, kt_ref, v4_ref,
                 kx_ref, vx_ref, *, width, n_heads, k_scale, q_scale, key_block):
    h = _rms(x_ref[...], g_ref[...]).astype(BF16)

    def col(j):
        return jnp.dot(h, w_ref[:, j * width:(j + 1) * width], preferred_element_type=F32)

    rq_ref[...] = col(0)
    rk_ref[...] = col(1) * k_scale
    rv_ref[...] = col(2).astype(BF16)
    rg_ref[...] = col(3)
    dq_ref[...] = (col(4) * q_scale).astype(BF16)
    kt = lax.dot_general(wkt_ref[...], h, NT_DIMS, preferred_element_type=F32)
    kt_ref[0] = kt
    dv = col(6)
    for hh in range(n_heads):
        v4_ref[:, hh, :] = dv[:, hh * HEAD_W:(hh + 1) * HEAD_W]
    if key_block is None:
        kx_ref[...] = col(5)
        vx_ref[...] = dv
    else:
        for c in range(kx_ref.shape[0]):
            kx_ref[c] = kt[:, c * key_block:(c + 1) * key_block].astype(BF16)
        vx_ref[...] = dv.astype(BF16)


def _inproj(x2, g, w_bf, wkt_bf, batch, seq, tile, n_heads, dk_ret, dh_diff, key_block):
    m, d = x2.shape
    width = wkt_bf.shape[0]
    nt = seq // tile
    row = lambda i: (i, 0)
    f32_out = jax.ShapeDtypeStruct((m, width), F32)
    bf_out = jax.ShapeDtypeStruct((m, width), BF16)
    spec = pl.BlockSpec((tile, width), row)
    if key_block is None:
        kx_shape, kx_spec, vx_shape = f32_out, spec, f32_out
    else:
        kx_shape = jax.ShapeDtypeStruct((m // key_block, width, key_block), BF16)
        kx_spec = pl.BlockSpec((tile // key_block, width, key_block), lambda i: (i, 0, 0))
        vx_shape = bf_out
    body = functools.partial(_inproj_body, width=width, n_heads=n_heads, k_scale=dk_ret ** -0.5,
                             q_scale=dh_diff ** -0.5, key_block=key_block)
    return pl.pallas_call(
        body,
        grid=(m // tile,),
        in_specs=[pl.BlockSpec((tile, d), row), _const_spec((1, d)), _const_spec(w_bf.shape),
                  _const_spec(wkt_bf.shape)],
        out_specs=[spec] * 5 + [pl.BlockSpec((1, width, tile), lambda i: (i // nt, 0, i % nt)),
                                pl.BlockSpec((tile, n_heads, HEAD_W), lambda i: (i, 0, 0)), kx_spec, spec],
        out_shape=[f32_out, f32_out, bf_out, f32_out, bf_out,
                   jax.ShapeDtypeStruct((batch, width, seq), F32),
                   jax.ShapeDtypeStruct((m, n_heads, HEAD_W), F32), kx_shape, vx_shape],
        compiler_params=_params("arbitrary"),
        name="inproj",
    )(x2, g, w_bf, wkt_bf)


def _retention_decay(length, h):
    i = lax.broadcasted_iota(jnp.int32, (length, length), 0)
    j = lax.broadcasted_iota(jnp.int32, (length, length), 1)
    diff = (i - j).astype(F32)
    causal = diff >= 0
    return jnp.where(causal, jnp.exp(jnp.where(causal, diff, 0.0) * _ret_log_decay(h)), 0.0)


def _retention_row_decays(length, h):
    lg = _ret_log_decay(h)
    ri = lax.broadcasted_iota(jnp.int32, (length, 1), 0).astype(F32)
    return jnp.exp((ri + 1.0) * lg), jnp.exp((length - 1.0 - ri) * lg), math.exp(length * lg)


def _gated_group_norm(o, g):
    on = o * lax.rsqrt(jnp.mean(o * o, axis=-1, keepdims=True) + EPS)
    return on * (g * jax.nn.sigmoid(g))


def _mixer_body(lam_ref, rq_ref, rk_ref, rv_ref, rg_ref, dq_ref, kt_ref, vb_ref, x_ref, wo_ref,
                gpost_ref, subg_ref, y_ref, s_ref, mix_scr, qq_scr, sc_scr, mx_scr, l_scr, acc_scr, decay_scr,
                *, tq, ret_chunk, n_heads, lam_i):
    t = pl.program_id(1)
    ret_w = n_heads * HEAD_W

    @pl.when(t == 0)
    def _():
        s_ref[...] = jnp.zeros_like(s_ref)

    @pl.when((pl.program_id(0) == 0) & (t == 0))
    def _():
        for h in range(n_heads):
            decay_scr[h] = _retention_decay(ret_chunk, h)

    for c in range(tq // ret_chunk):
        rows = slice(c * ret_chunk, (c + 1) * ret_chunk)
        for h in range(n_heads):
            cols = slice(h * HEAD_W, (h + 1) * HEAD_W)
            decay = decay_scr[h]
            qdec, kdec, gl = _retention_row_decays(ret_chunk, h)
            q = rq_ref[rows, cols]
            k = rk_ref[rows, cols]
            v = rv_ref[rows, cols]
            s = s_ref[0, h]
            att = lax.dot_general(q.astype(BF16), k.astype(BF16), NT_DIMS,
                                  preferred_element_type=F32) * decay
            o = (jnp.dot(att.astype(BF16), v, preferred_element_type=F32)
                 + jnp.dot((q * qdec).astype(BF16), s.astype(BF16), preferred_element_type=F32))
            s_ref[0, h] = gl * s + lax.dot_general((k * kdec).astype(BF16), v, TN_DIMS,
                                                   preferred_element_type=F32)
            mix_scr[rows, cols] = _gated_group_norm(o, rg_ref[rows, cols]).astype(BF16)

    lane = lax.broadcasted_iota(jnp.int32, (tq, HEAD_W), 1)
    for h in range(n_heads):
        qh = dq_ref[:, h * HEAD_W:(h + 1) * HEAD_W]
        zero = jnp.zeros_like(qh)
        qq_scr[h, 0:tq] = jnp.where(lane < HEAD_W // 2, qh, zero)
        qq_scr[h, tq:2 * tq] = jnp.where(lane >= HEAD_W // 2, qh, zero)
    mx_scr[...] = jnp.full(mx_scr.shape, FINITE_MIN, F32)
    l_scr[...] = jnp.zeros_like(l_scr)
    acc_scr[...] = jnp.zeros_like(acc_scr)
    key_j = lax.broadcasted_iota(jnp.int32, (1, tq), 1).astype(F32)
    lane_tiles = [slice(c * HEAD_W, (c + 1) * HEAD_W) for c in range(tq // HEAD_W)]

    def fold(x, op):
        return functools.reduce(op, [x[:, c] for c in lane_tiles])

    def scores(kb, causal):
        key_pos = key_j + ((kb - t) * tq).astype(F32)
        for h in range(n_heads):
            slope = 2.0 ** (-8.0 / n_heads * (h + 1))
            s = jnp.dot(qq_scr[h], kt_ref[kb, h * HEAD_W:(h + 1) * HEAD_W, :], preferred_element_type=F32)
            s = s + slope * key_pos
            if causal is not None:
                s = jnp.where(causal, s, NEG_INF)
            sc_scr[kb, h] = s
            mx_scr[h] = jnp.maximum(mx_scr[h], fold(s, jnp.maximum))

    def off_diagonal(kb, carry):
        scores(kb, None)
        return carry

    lax.fori_loop(0, t, off_diagonal, 0)
    ii = lax.broadcasted_iota(jnp.int32, (2 * tq, tq), 0)
    jj = lax.broadcasted_iota(jnp.int32, (2 * tq, tq), 1)
    scores(t, jnp.where(ii >= tq, ii - tq, ii) >= jj)

    for h in range(n_heads):
        mx_scr[h] = jnp.broadcast_to(jnp.max(mx_scr[h], axis=-1, keepdims=True), (2 * tq, HEAD_W))

    def weighted(kb, carry):
        for h in range(n_heads):
            m = mx_scr[h]
            e = [jnp.exp(sc_scr[kb, h, :, c] - m) for c in lane_tiles]
            l_scr[h] += functools.reduce(jnp.add, e)
            vr = vb_ref[pl.ds(pl.multiple_of(kb * tq, tq), tq), h * HEAD_W:(h + 1) * HEAD_W]
            acc_scr[h] += jnp.dot(jnp.concatenate([x.astype(BF16) for x in e], axis=1), vr,
                                  preferred_element_type=F32)
        return carry

    lax.fori_loop(0, t + 1, weighted, 0)
    lam = _lambda(lam_ref, lam_i)
    subg = subg_ref[...]
    for h in range(n_heads):
        on = acc_scr[h] / jnp.sum(l_scr[h], axis=-1, keepdims=True)
        o = _rms(on[:tq] - lam * on[tq:], subg) * (1.0 - lam_i)
        mix_scr[:, ret_w + h * HEAD_W:ret_w + (h + 1) * HEAD_W] = o.astype(BF16)

    mo = jnp.dot(mix_scr[...], wo_ref[...], preferred_element_type=F32)
    y_ref[...] = x_ref[...] + _rms(mo, gpost_ref[...])


def _mixer(lamv, rq, rk, rv, rg, dq, ktb, dvb, x2, wo_bf, gpost, subg, batch, seq, n_heads, lam_i):
    m, d = x2.shape
    w = rq.shape[1]
    tq = MIX_TILE
    nt = seq // tq
    tile = lambda b, t: (b * nt + t, 0)
    whole = lambda b, t: (b, 0)
    body = functools.partial(_mixer_body, tq=tq, ret_chunk=RET_CHUNK, n_heads=n_heads, lam_i=lam_i)
    return pl.pallas_call(
        body,
        grid=(batch, nt),
        in_specs=[
            _const_spec(lamv.shape),
            pl.BlockSpec((tq, w), tile), pl.BlockSpec((tq, w), tile), pl.BlockSpec((tq, w), tile),
            pl.BlockSpec((tq, w), tile), pl.BlockSpec((tq, w), tile),
            pl.BlockSpec((nt, w, tq), lambda b, t: (b, 0, 0)), pl.BlockSpec((seq, w), whole),
            pl.BlockSpec((tq, d), tile),
            _const_spec(wo_bf.shape), _const_spec((1, d)), _const_spec((1, HEAD_W)),
        ],
        out_specs=[pl.BlockSpec((tq, d), tile),
                   pl.BlockSpec((1, n_heads, HEAD_W, HEAD_W), lambda b, t: (b, 0, 0, 0))],
        out_shape=[jax.ShapeDtypeStruct((m, d), F32),
                   jax.ShapeDtypeStruct((batch, n_heads, HEAD_W, HEAD_W), F32)],
        scratch_shapes=[
            pltpu.VMEM((tq, 2 * w), BF16),
            pltpu.VMEM((n_heads, 2 * tq, HEAD_W), BF16),
            pltpu.VMEM((nt, n_heads, 2 * tq, tq), F32),
            pltpu.VMEM((n_heads, 2 * tq, HEAD_W), F32),
            pltpu.VMEM((n_heads, 2 * tq, HEAD_W), F32),
            pltpu.VMEM((n_heads, 2 * tq, HEAD_W), F32),
            pltpu.VMEM((n_heads, RET_CHUNK, RET_CHUNK), F32),
        ],
        compiler_params=_params("arbitrary", "arbitrary"),
        name="mixer",
    )(lamv, rq, rk, rv, rg, dq, ktb, dvb, x2, wo_bf, gpost, subg)


def _conv_gate(g, u, g1, g2, cw_ref, cb_ref, cols):
    c = cw_ref[0:1, cols] * g2 + cw_ref[1:2, cols] * g1 + cw_ref[2:3, cols] * g + cb_ref[:, cols]
    return (jax.nn.gelu(c) * u).astype(BF16)


def _ffn_body(pt_ref, x_ref, gpre_ref, win_ref, cw_ref, cb_ref, wout_ref, gpost_ref,
              lam_ref, q_ref, kn_ref, vn_ref, ck_hbm, cv_hbm, y_ref, cs_ref, o_ref,
              gbuf, carry, act, kbuf, vbuf, sem, sc_scr, *, tf, d_ff, ck, rows, n_pages, page, n_heads, lam_i):
    t = pl.program_id(1)
    step = pl.program_id(0) * pl.num_programs(1) + t
    n_rows = rows * pl.num_programs(0) * pl.num_programs(1)
    copies = functools.partial(_page_copies, pt_ref, ck_hbm, cv_hbm, kbuf, vbuf, sem, n_pages=n_pages)

    @pl.when(step == 0)
    def _():
        for slot in range(2):
            for c in copies(b=slot, slot=slot):
                c.start()

    @pl.when(t == 0)
    def _():
        carry[...] = jnp.zeros_like(carry)

    lam = _lambda(lam_ref, lam_i)
    for i in range(rows):
        g = step * rows + i
        slot = i % 2
        for c in copies(b=g, slot=slot):
            c.wait()
        o_ref[i] = _paged_row(slot, q_ref[i], kn_ref[i], vn_ref[i], lam, kbuf, vbuf, sc_scr,
                              n_pages=n_pages, page=page, n_heads=n_heads)

        @pl.when(g + 2 < n_rows)
        def _():
            for c in copies(b=g + 2, slot=slot):
                c.start()

    x = x_ref[...]
    h = _rms(x, gpre_ref[...]).astype(BF16)
    for j in range(d_ff // ck):
        cols = slice(j * ck, (j + 1) * ck)
        g = jnp.dot(h, win_ref[:, cols], preferred_element_type=F32)
        u = jnp.dot(h, win_ref[:, d_ff + j * ck:d_ff + (j + 1) * ck], preferred_element_type=F32)
        gbuf[0:8, :] = carry[:, cols]
        gbuf[8:8 + tf, :] = g
        act[:, cols] = _conv_gate(g, u, gbuf[7:7 + tf, :], gbuf[6:6 + tf, :], cw_ref, cb_ref, cols)
        carry[:, cols] = gbuf[tf:tf + 8, :]
    f = jnp.dot(act[...], wout_ref[...], preferred_element_type=F32)
    y_ref[...] = x + _rms(f, gpost_ref[...])
    cs_ref[0] = carry[...]


def _ffn_paged(x2, gpre, win_bf, cw, cb, wout_bf, gpost, batch, seq,
               page_table, lamv, dq, dk, dv, cache_kt, cache_v, n_heads, lam_i):
    m, d = x2.shape
    d_ff = wout_bf.shape[0]
    tf = FFN_TILE
    nt = seq // tf
    n, w = dq.shape
    rows = n // (batch * nt)
    assert rows * batch * nt == n and rows % 2 == 0, "sample rows must split evenly, in pairs, over the FFN steps"
    n_pages = page_table.shape[1]
    page = cache_kt.shape[2]
    tile = lambda b, t, pt: (b * nt + t, 0)
    const = _const_spec
    rspec = pl.BlockSpec((rows, 1, w), lambda b, t, pt: (b * nt + t, 0, 0))
    body = functools.partial(_ffn_body, tf=tf, d_ff=d_ff, ck=FFN_CHUNK, rows=rows, n_pages=n_pages, page=page,
                             n_heads=n_heads, lam_i=lam_i)
    grid_spec = pltpu.PrefetchScalarGridSpec(
        num_scalar_prefetch=1,
        grid=(batch, nt),
        in_specs=[pl.BlockSpec((tf, d), tile), const((1, d)), const(win_bf.shape), const(cw.shape),
                  const(cb.shape), const(wout_bf.shape), const((1, d)),
                  const(lamv.shape), rspec, rspec, rspec,
                  pl.BlockSpec(memory_space=pl.ANY), pl.BlockSpec(memory_space=pl.ANY)],
        out_specs=[pl.BlockSpec((tf, d), tile), pl.BlockSpec((1, 8, d_ff), lambda b, t, pt: (b, 0, 0)), rspec],
        scratch_shapes=[pltpu.VMEM((tf + 8, FFN_CHUNK), F32), pltpu.VMEM((8, d_ff), F32),
                        pltpu.VMEM((tf, d_ff), BF16),
                        pltpu.VMEM((2, n_pages, w, page), F32),
                        pltpu.VMEM((2, n_pages, page * n_heads, HEAD_W), F32),
                        pltpu.SemaphoreType.DMA((2, 2)), pltpu.VMEM((n_pages, 2 * n_heads, page), F32)],
    )
    y, cs, o = pl.pallas_call(
        body,
        grid_spec=grid_spec,
        out_shape=[jax.ShapeDtypeStruct((m, d), F32), jax.ShapeDtypeStruct((batch, 8, d_ff), F32),
                   jax.ShapeDtypeStruct((n, 1, w), F32)],
        compiler_params=_params("arbitrary", "arbitrary"),
        name="conv_ffn_paged_attn",
    )(page_table.reshape(-1), x2, gpre, win_bf, cw, cb, wout_bf, gpost,
      lamv, dq.astype(F32).reshape(n, 1, w), dk.reshape(n, 1, w), dv.reshape(n, 1, w), cache_kt, cache_v)
    return y, cs, o.reshape(n, w)


def _ret_dec_body(rq_ref, rk_ref, rv_ref, rg_ref, s_ref, a_ref, sn_ref, *, grp, n_heads):
    row = lax.broadcasted_iota(jnp.int32, (grp, grp * HEAD_W), 0)
    lane = lax.broadcasted_iota(jnp.int32, (grp, grp * HEAD_W), 1)
    own = (lane // HEAD_W) == row

    def block_diag(x):
        return jnp.where(own, jnp.tile(x, (1, grp)), 0.0).astype(BF16)

    for h in range(n_heads):
        cols = slice(h * HEAD_W, (h + 1) * HEAD_W)
        gamma = math.exp(_ret_log_decay(h))
        q = rq_ref[:, cols]
        k = rk_ref[:, cols]
        v = rv_ref[:, cols]
        qb = q.astype(BF16).astype(F32)
        kb = k.astype(BF16).astype(F32)
        att = jnp.sum(qb * kb, axis=-1, keepdims=True).astype(BF16).astype(F32)
        s = s_ref[:, h].reshape(grp * HEAD_W, HEAD_W)
        o = att * v.astype(F32) + jnp.dot(block_diag(q * gamma), s.astype(BF16),
                                          preferred_element_type=F32)
        upd = lax.dot_general(block_diag(k), v, TN_DIMS, preferred_element_type=F32)
        sn_ref[:, h] = (gamma * s + upd).reshape(grp, HEAD_W, HEAD_W)
        a_ref[:, cols] = _gated_group_norm(o, rg_ref[:, cols]).astype(BF16)


def _ret_decode(rq, rk, rv, rg, state, n_heads):
    n, w = rq.shape
    grp = DEC_GROUP
    row = lambda i: (i, 0)
    st = lambda i: (i, 0, 0, 0)
    spec = pl.BlockSpec((grp, w), row)
    st_spec = pl.BlockSpec((grp, n_heads, HEAD_W, HEAD_W), st)
    return pl.pallas_call(
        functools.partial(_ret_dec_body, grp=grp, n_heads=n_heads),
        grid=(n // grp,),
        in_specs=[spec, spec, spec, spec, st_spec],
        out_specs=[spec, st_spec],
        out_shape=[jax.ShapeDtypeStruct((n, w), BF16), jax.ShapeDtypeStruct(state.shape, F32)],
        compiler_params=_params("arbitrary"),
        name="ret_decode",
    )(rq, rk, rv, rg, state)


def _page_copies(pt_ref, ck_hbm, cv_hbm, kbuf, vbuf, sem, b, slot, n_pages):
    out = []
    for p in range(n_pages):
        page = pt_ref[b * n_pages + p]
        out.append(pltpu.make_async_copy(ck_hbm.at[page], kbuf.at[slot, p], sem.at[0, slot]))
        out.append(pltpu.make_async_copy(cv_hbm.at[page], vbuf.at[slot, p], sem.at[1, slot]))
    return out


def _paged_row(slot, q, kn_row, vn_row, lam, kbuf, vbuf, sc_scr, *, n_pages, page, n_heads):
    w = n_heads * HEAD_W
    nc = 2 * n_heads
    past = n_pages * page
    ci = lax.broadcasted_iota(jnp.int32, (nc, w), 0)
    ri = lax.broadcasted_iota(jnp.int32, (nc, w), 1)
    col_of = ri // HEAD_W + n_heads * ((ri % HEAD_W) // (HEAD_W // 2))
    q_sel = jnp.where(ci == col_of, jnp.broadcast_to(q, (nc, w)), 0.0)
    q_sel_bf = q_sel.astype(BF16)
    rowc = lax.broadcasted_iota(jnp.int32, (nc, page), 0)
    key_i = lax.broadcasted_iota(jnp.int32, (nc, page), 1)
    slope = jnp.zeros((nc, page), F32)
    for h in range(n_heads):
        slope = jnp.where(rowc % n_heads == h, 2.0 ** (-8.0 / n_heads * (h + 1)), slope)

    mx = jnp.full((nc, page), FINITE_MIN, F32)
    for p in range(n_pages):
        dist = (past - (p * page + key_i)).astype(F32)
        s = jnp.dot(q_sel_bf, kbuf[slot, p].astype(BF16), preferred_element_type=F32) - slope * dist
        sc_scr[p] = s
        mx = jnp.maximum(mx, s)
    kn = kn_row.astype(BF16).astype(F32)
    s_self = jnp.broadcast_to(jnp.sum(q_sel * kn, axis=-1, keepdims=True), (nc, page))
    m = jnp.maximum(jnp.broadcast_to(jnp.max(mx, axis=-1, keepdims=True), (nc, page)), s_self)

    lp = jnp.zeros((nc, page), F32)
    for p in range(n_pages):
        e = jnp.exp(sc_scr[p] - m)
        sc_scr[p] = e
        lp = lp + e
    e_self = jnp.exp(s_self - m)
    l = jnp.broadcast_to(jnp.sum(lp, axis=-1, keepdims=True), (nc, page)) + e_self
    r = jnp.where(rowc < n_heads, 1.0, lam) / l

    def combine(pn):
        return (pn - pltpu.roll(pn, n_heads, axis=0)).astype(BF16)

    accs = [jnp.zeros((nc, HEAD_W), F32) for _ in range(n_heads)]
    for p in range(n_pages):
        a = combine(sc_scr[p] * r)
        for h in range(n_heads):
            vh = vbuf[slot, p, pl.ds(h, page, stride=n_heads), :].astype(BF16)
            accs[h] = accs[h] + jnp.dot(a, vh, preferred_element_type=F32)
    a_self = combine(e_self * r).astype(F32)
    vn = vn_row.astype(BF16).astype(F32)
    return jnp.concatenate(
        [accs[h][h:h + 1, :] + a_self[h:h + 1, 0:HEAD_W] * vn[:, h * HEAD_W:(h + 1) * HEAD_W]
         for h in range(n_heads)], axis=-1)


def _dec_ffn_body(a_ref, do_ref, x_ref, wo_ref, gmix_ref, subg_ref, gpre_ref, wg_ref, wu_ref, cw_ref,
                  cb_ref, sc_ref, wout_ref, gpost_ref, y_ref, g_ref, xmid, hbuf, acc, *, lam_i):
    j = pl.program_id(0)
    ret_w = a_ref.shape[1]

    @pl.when(j == 0)
    def _():
        mo = jnp.dot(a_ref[...], wo_ref[0:ret_w, :], preferred_element_type=F32)
        for h in range(do_ref.shape[1] // HEAD_W):
            cols = slice(h * HEAD_W, (h + 1) * HEAD_W)
            bh = (_rms(do_ref[:, cols], subg_ref[...]) * (1.0 - lam_i)).astype(BF16)
            mo += jnp.dot(bh, wo_ref[ret_w + h * HEAD_W:ret_w + (h + 1) * HEAD_W, :],
                          preferred_element_type=F32)
        xm = x_ref[...] + _rms(mo, gmix_ref[...])
        xmid[...] = xm
        hbuf[...] = _rms(xm, gpre_ref[...]).astype(BF16)
        acc[...] = jnp.zeros_like(acc)

    h = hbuf[...]
    g = jnp.dot(h, wg_ref[...], preferred_element_type=F32)
    u = jnp.dot(h, wu_ref[...], preferred_element_type=F32)
    g_ref[...] = g
    c = cw_ref[0:1, :] * sc_ref[0] + cw_ref[1:2, :] * sc_ref[1] + cw_ref[2:3, :] * g + cb_ref[...]
    acc[...] += jnp.dot((jax.nn.gelu(c) * u).astype(BF16), wout_ref[...], preferred_element_type=F32)

    @pl.when(j == pl.num_programs(0) - 1)
    def _():
        y_ref[...] = xmid[...] + _rms(acc[...], gpost_ref[...])


def _dec_ffn(a, do, x2, wo_bf, gmix, subg, gpre, win_bf, cw, cb, conv_state_t, wout_bf, gpost, lam_i):
    n, d = x2.shape
    d_ff = wout_bf.shape[0]
    ck = FFN_CHUNK
    nj = d_ff // ck
    full = lambda shape: pl.BlockSpec(shape, lambda j: (0,) * len(shape))
    return pl.pallas_call(
        functools.partial(_dec_ffn_body, lam_i=lam_i),
        grid=(nj,),
        in_specs=[full(a.shape), full(do.shape), full(x2.shape), full(wo_bf.shape), full((1, d)),
                  full((1, HEAD_W)), full((1, d)),
                  pl.BlockSpec((d, ck), lambda j: (0, j)), pl.BlockSpec((d, ck), lambda j: (0, nj + j)),
                  pl.BlockSpec((3, ck), lambda j: (0, j)), pl.BlockSpec((1, ck), lambda j: (0, j)),
                  pl.BlockSpec((2, n, ck), lambda j: (0, 0, j)),
                  pl.BlockSpec((ck, d), lambda j: (j, 0)), full((1, d))],
        out_specs=[full((n, d)), pl.BlockSpec((n, ck), lambda j: (0, j))],
        out_shape=[jax.ShapeDtypeStruct((n, d), F32), jax.ShapeDtypeStruct((n, d_ff), F32)],
        scratch_shapes=[pltpu.VMEM((n, d), F32), pltpu.VMEM((n, d), BF16), pltpu.VMEM((n, d), F32)],
        compiler_params=_params("arbitrary"),
        name="dec_ffn",
    )(a, do, x2, wo_bf, gmix, subg, gpre, win_bf, win_bf, cw, cb, conv_state_t, wout_bf, gpost)


def kernel(x_prompt, x_sample, state_ret, cache_k, cache_v, state_conv, page_table,
           norm_mix_pre, norm_mix_post, w_in, w_o, lambda_q1, lambda_k1, lambda_q2, lambda_k2,
           subln_g, norm_ffn_pre, norm_ffn_post, w_ffn_in, conv_w, conv_b, w_ffn_out):
    batch, seq, d = x_prompt.shape
    n_dec = x_sample.shape[0]
    assert x_sample.shape[1] == 1, "the sample group is one token per row"
    depth = w_in.shape[0]
    n_heads = state_ret.shape[2]
    dk_ret = state_ret.shape[3]
    dh_diff = cache_k.shape[-1]
    d_ff = w_ffn_out.shape[1]
    w = n_heads * HEAD_W
    assert w_in.shape[2] == 7 * w and cache_v.shape[-1] == HEAD_W and 2 * dh_diff == HEAD_W
    assert seq % MIX_TILE == 0 and seq % FFN_TILE == 0 and d_ff % FFN_CHUNK == 0
    assert n_dec % DEC_GROUP == 0 and n_dec % 128 == 0 and seq % IN_TILE == 0 and IN_TILE % MIX_TILE == 0

    xp = x_prompt.reshape(batch * seq, d)
    xs = x_sample.reshape(n_dec, d)
    outs = [[] for _ in range(8)]
    for l in range(depth):
        lam_i = 0.8 - 0.6 * math.exp(-0.3 * l)
        lamv = jnp.stack([lambda_q1[l], lambda_k1[l], lambda_q2[l], lambda_k2[l]]).astype(F32)
        row = lambda v: v.reshape(1, -1)
        w_in_bf = w_in[l].astype(BF16)
        w_o_bf = w_o[l].astype(BF16)
        w_ffn_in_bf = w_ffn_in[l].astype(BF16)
        w_ffn_out_bf = w_ffn_out[l].astype(BF16)
        g_pre, g_post = row(norm_mix_pre[l]), row(norm_mix_post[l])
        f_pre, f_post = row(norm_ffn_pre[l]), row(norm_ffn_post[l])
        subg = row(subln_g[l])
        cw, cb = conv_w[l], row(conv_b[l])

        wkt_bf = w_in[l][:, 5 * w:6 * w].T.astype(BF16)
        cache_kt = jnp.transpose(cache_k[l], (0, 2, 3, 4, 1)).reshape(cache_k.shape[1], w, cache_k.shape[2])
        cache_vr = cache_v[l].reshape(cache_v.shape[1], cache_v.shape[2] * n_heads, HEAD_W)
        proj = functools.partial(_inproj, n_heads=n_heads, dk_ret=dk_ret, dh_diff=dh_diff)

        rq, rk, rv, rg, dq, kt, v4, ktb, dvb = proj(xp, g_pre, w_in_bf, wkt_bf, batch, seq, IN_TILE,
                                                     key_block=MIX_TILE)
        xp, s_fin = _mixer(lamv, rq, rk, rv, rg, dq, ktb, dvb, xp, w_o_bf, g_post, subg,
                           batch, seq, n_heads, lam_i)
        outs[0].append(s_fin)
        outs[2].append(jnp.transpose(kt.reshape(batch, n_heads, 2, dh_diff, seq), (0, 4, 1, 2, 3)))
        outs[3].append(v4.reshape(batch, seq, n_heads, HEAD_W))
        rq, rk, rv, rg, dq, kt, v4, dk, dv = proj(xs, g_pre, w_in_bf, wkt_bf, 1, n_dec, n_dec, key_block=None)
        a, s_new = _ret_decode(rq, rk, rv, rg, state_ret[l], n_heads)

        xp, cs, do = _ffn_paged(xp, f_pre, w_ffn_in_bf, cw, cb, w_ffn_out_bf, f_post, batch, seq,
                                page_table, lamv, dq, dk, dv, cache_kt, cache_vr, n_heads, lam_i)
        outs[6].append(cs[:, 6:8, :])

        conv_t = jnp.swapaxes(state_conv[l], 0, 1)
        xs, g_new = _dec_ffn(a, do, xs, w_o_bf, g_post, subg, f_pre, w_ffn_in_bf, cw, cb, conv_t,
                             w_ffn_out_bf, f_post, lam_i)
        outs[1].append(s_new)
        outs[4].append(jnp.transpose(kt.reshape(n_heads, 2, dh_diff, n_dec), (3, 0, 1, 2))[:, None])
        outs[5].append(v4.reshape(n_dec, 1, n_heads, HEAD_W))
        outs[7].append(jnp.stack([state_conv[l][:, 1, :], g_new], axis=1))

    st = [jnp.stack(o) for o in outs]
    return (xp.reshape(batch, seq, d), xs.reshape(n_dec, 1, d), st[0], st[1], st[2], st[3], st[4], st[5],
            st[6], st[7])
```

```python
import functools
import math

import jax
import jax.numpy as jnp
from jax import lax
from jax.experimental import pallas as pl
from jax.experimental.pallas import tpu as pltpu

F32 = jnp.float32
BF16 = jnp.bfloat16
EPS = 1e-6
NEG_INF = -1e30
FINITE_MIN = -3e38

V7X_VMEM_BYTES = 64 * 1024 * 1024
VMEM_LIMIT_BYTES = V7X_VMEM_BYTES * 7 // 8

RET_CHUNK = 256
HEAD_W = 128
IN_TILE = 512
MIX_TILE = 256
FFN_TILE = 256
FFN_CHUNK = 256
DEC_GROUP = 16
PAGE_GROUP = 2
NT_DIMS = (((1,), (1,)), ((), ()))
TN_DIMS = (((0,), (0,)), ((), ()))


def _rms(x, g):
    return x * lax.rsqrt(jnp.mean(x * x, axis=-1, keepdims=True) + EPS) * g


def _params(*sem):
    return pltpu.CompilerParams(dimension_semantics=sem, vmem_limit_bytes=VMEM_LIMIT_BYTES)


def _const_spec(shape):
    n = len(shape)
    return pl.BlockSpec(shape, lambda *_: (0,) * n, pipeline_mode=pl.Buffered(1))


def _ret_log_decay(h):
    return math.log(1.0 - 2.0 ** (-5.0 - h))


def _lambda(lam_ref, lam_i):
    a = jnp.sum(lam_ref[0:1, :] * lam_ref[1:2, :], axis=-1, keepdims=True)
    b = jnp.sum(lam_ref[2:3, :] * lam_ref[3:4, :], axis=-1, keepdims=True)
    return jnp.exp(a) - jnp.exp(b) + lam_i


def _inproj_body(x_ref, g_ref, w_ref, wkt_ref, rq_ref, rk_ref, rv_ref, rg_ref, dq_ref, kt_ref, v4_ref,
                 kx_ref, vx_ref, *, width, n_heads, k_scale, q_scale, key_block):
    h = _rms(x_ref[...], g_ref[...]).astype(BF16)

    def col(j):
        return jnp.dot(h, w_ref[:, j * width:(j + 1) * width], preferred_element_type=F32)

    rq_ref[...] = col(0)
    rk_ref[...] = col(1) * k_scale
    rv_ref[...] = col(2).astype(BF16)
    rg_ref[...] = col(3)
    dq_ref[...] = (col(4) * q_scale).astype(BF16)
    kt = lax.dot_general(wkt_ref[...], h, NT_DIMS, preferred_element_type=F32)
    kt_ref[0] = kt
    dv = col(6)
    for hh in range(n_heads):
        v4_ref[:, hh, :] = dv[:, hh * HEAD_W:(hh + 1) * HEAD_W]
    if key_block is None:
        kx_ref[...] = col(5)
        vx_ref[...] = dv
    else:
        for c in range(kx_ref.shape[0]):
            kx_ref[c] = kt[:, c * key_block:(c + 1) * key_block].astype(BF16)
        vx_ref[...] = dv.astype(BF16)


def _inproj(x2, g, w_bf, wkt_bf, batch, seq, tile, n_heads, dk_ret, dh_diff, key_block):
    m, d = x2.shape
    width = wkt_bf.shape[0]
    nt = seq // tile
    row = lambda i: (i, 0)
    f32_out = jax.ShapeDtypeStruct((m, width), F32)
    bf_out = jax.ShapeDtypeStruct((m, width), BF16)
    spec = pl.BlockSpec((tile, width), row)
    if key_block is None:
        kx_shape, kx_spec, vx_shape = f32_out, spec, f32_out
    else:
        kx_shape = jax.ShapeDtypeStruct((m // key_block, width, key_block), BF16)
        kx_spec = pl.BlockSpec((tile // key_block, width, key_block), lambda i: (i, 0, 0))
        vx_shape = bf_out
    body = functools.partial(_inproj_body, width=width, n_heads=n_heads, k_scale=dk_ret ** -0.5,
                             q_scale=dh_diff ** -0.5, key_block=key_block)
    return pl.pallas_call(
        body,
        grid=(m // tile,),
        in_specs=[pl.BlockSpec((tile, d), row), _const_spec((1, d)), _const_spec(w_bf.shape),
                  _const_spec(wkt_bf.shape)],
        out_specs=[spec] * 5 + [pl.BlockSpec((1, width, tile), lambda i: (i // nt, 0, i % nt)),
                                pl.BlockSpec((tile, n_heads, HEAD_W), lambda i: (i, 0, 0)), kx_spec, spec],
        out_shape=[f32_out, f32_out, bf_out, f32_out, bf_out,
                   jax.ShapeDtypeStruct((batch, width, seq), F32),
                   jax.ShapeDtypeStruct((m, n_heads, HEAD_W), F32), kx_shape, vx_shape],
        compiler_params=_params("arbitrary"),
        name="inproj",
    )(x2, g, w_bf, wkt_bf)


def _retention_decay(length, h):
    i = lax.broadcasted_iota(jnp.int32, (length, length), 0)
    j = lax.broadcasted_iota(jnp.int32, (length, length), 1)
    diff = (i - j).astype(F32)
    causal = diff >= 0
    return jnp.where(causal, jnp.exp(jnp.where(causal, diff, 0.0) * _ret_log_decay(h)), 0.0)


def _retention_row_decays(length, h):
    lg = _ret_log_decay(h)
    ri = lax.broadcasted_iota(jnp.int32, (length, 1), 0).astype(F32)
    return jnp.exp((ri + 1.0) * lg), jnp.exp((length - 1.0 - ri) * lg), math.exp(length * lg)


def _gated_group_norm(o, g):
    on = o * lax.rsqrt(jnp.mean(o * o, axis=-1, keepdims=True) + EPS)
    return on * (g * jax.nn.sigmoid(g))


def _mixer_body(lam_ref, rq_ref, rk_ref, rv_ref, rg_ref, dq_ref, kt_ref, vb_ref, x_ref, wo_ref,
                gpost_ref, subg_ref, y_ref, s_ref, mix_scr, qq_scr, sc_scr, mx_scr, l_scr, acc_scr, decay_scr,
                *, tq, ret_chunk, n_heads, lam_i):
    t = pl.program_id(1)
    ret_w = n_heads * HEAD_W

    @pl.when(t == 0)
    def _():
        s_ref[...] = jnp.zeros_like(s_ref)

    @pl.when((pl.program_id(0) == 0) & (t == 0))
    def _():
        for h in range(n_heads):
            decay_scr[h] = _retention_decay(ret_chunk, h)

    for c in range(tq // ret_chunk):
        rows = slice(c * ret_chunk, (c + 1) * ret_chunk)
        for h in range(n_heads):
            cols = slice(h * HEAD_W, (h + 1) * HEAD_W)
            decay = decay_scr[h]
            qdec, kdec, gl = _retention_row_decays(ret_chunk, h)
            q = rq_ref[rows, cols]
            k = rk_ref[rows, cols]
            v = rv_ref[rows, cols]
            s = s_ref[0, h]
            att = lax.dot_general(q.astype(BF16), k.astype(BF16), NT_DIMS,
                                  preferred_element_type=F32) * decay
            o = (jnp.dot(att.astype(BF16), v, preferred_element_type=F32)
                 + jnp.dot((q * qdec).astype(BF16), s.astype(BF16), preferred_element_type=F32))
            s_ref[0, h] = gl * s + lax.dot_general((k * kdec).astype(BF16), v, TN_DIMS,
                                                   preferred_element_type=F32)
            mix_scr[rows, cols] = _gated_group_norm(o, rg_ref[rows, cols]).astype(BF16)

    lane = lax.broadcasted_iota(jnp.int32, (tq, HEAD_W), 1)
    for h in range(n_heads):
        qh = dq_ref[:, h * HEAD_W:(h + 1) * HEAD_W]
        zero = jnp.zeros_like(qh)
        qq_scr[h, 0:tq] = jnp.where(lane < HEAD_W // 2, qh, zero)
        qq_scr[h, tq:2 * tq] = jnp.where(lane >= HEAD_W // 2, qh, zero)
    mx_scr[...] = jnp.full(mx_scr.shape, FINITE_MIN, F32)
    l_scr[...] = jnp.zeros_like(l_scr)
    acc_scr[...] = jnp.zeros_like(acc_scr)
    key_j = lax.broadcasted_iota(jnp.int32, (1, tq), 1).astype(F32)
    lane_tiles = [slice(c * HEAD_W, (c + 1) * HEAD_W) for c in range(tq // HEAD_W)]

    def fold(x, op):
        return functools.reduce(op, [x[:, c] for c in lane_tiles])

    def scores(kb, causal):
        key_pos = key_j + ((kb - t) * tq).astype(F32)
        for h in range(n_heads):
            slope = 2.0 ** (-8.0 / n_heads * (h + 1))
            s = jnp.dot(qq_scr[h], kt_ref[kb, h * HEAD_W:(h + 1) * HEAD_W, :], preferred_element_type=F32)
            s = s + slope * key_pos
            if causal is not None:
                s = jnp.where(causal, s, NEG_INF)
            sc_scr[kb, h] = s
            mx_scr[h] = jnp.maximum(mx_scr[h], fold(s, jnp.maximum))

    def off_diagonal(kb, carry):
        scores(kb, None)
        return carry

    lax.fori_loop(0, t, off_diagonal, 0)
    ii = lax.broadcasted_iota(jnp.int32, (2 * tq, tq), 0)
    jj = lax.broadcasted_iota(jnp.int32, (2 * tq, tq), 1)
    scores(t, jnp.where(ii >= tq, ii - tq, ii) >= jj)

    for h in range(n_heads):
        mx_scr[h] = jnp.broadcast_to(jnp.max(mx_scr[h], axis=-1, keepdims=True), (2 * tq, HEAD_W))

    def weighted(kb, carry):
        for h in range(n_heads):
            m = mx_scr[h]
            e = [jnp.exp(sc_scr[kb, h, :, c] - m) for c in lane_tiles]
            l_scr[h] += functools.reduce(jnp.add, e)
            vr = vb_ref[pl.ds(pl.multiple_of(kb * tq, tq), tq), h * HEAD_W:(h + 1) * HEAD_W]
            acc_scr[h] += jnp.dot(jnp.concatenate([x.astype(BF16) for x in e], axis=1), vr,
                                  preferred_element_type=F32)
        return carry

    lax.fori_loop(0, t + 1, weighted, 0)
    lam = _lambda(lam_ref, lam_i)
    subg = subg_ref[...]
    for h in range(n_heads):
        on = acc_scr[h] / jnp.sum(l_scr[h], axis=-1, keepdims=True)
        o = _rms(on[:tq] - lam * on[tq:], subg) * (1.0 - lam_i)
        mix_scr[:, ret_w + h * HEAD_W:ret_w + (h + 1) * HEAD_W] = o.astype(BF16)

    mo = jnp.dot(mix_scr[...], wo_ref[...], preferred_element_type=F32)
    y_ref[...] = x_ref[...] + _rms(mo, gpost_ref[...])


def _mixer(lamv, rq, rk, rv, rg, dq, ktb, dvb, x2, wo_bf, gpost, subg, batch, seq, n_heads, lam_i):
    m, d = x2.shape
    w = rq.shape[1]
    tq = MIX_TILE
    nt = seq // tq
    tile = lambda b, t: (b * nt + t, 0)
    whole = lambda b, t: (b, 0)
    body = functools.partial(_mixer_body, tq=tq, ret_chunk=RET_CHUNK, n_heads=n_heads, lam_i=lam_i)
    return pl.pallas_call(
        body,
        grid=(batch, nt),
        in_specs=[
            _const_spec(lamv.shape),
            pl.BlockSpec((tq, w), tile), pl.BlockSpec((tq, w), tile), pl.BlockSpec((tq, w), tile),
            pl.BlockSpec((tq, w), tile), pl.BlockSpec((tq, w), tile),
            pl.BlockSpec((nt, w, tq), lambda b, t: (b, 0, 0)), pl.BlockSpec((seq, w), whole),
            pl.BlockSpec((tq, d), tile),
            _const_spec(wo_bf.shape), _const_spec((1, d)), _const_spec((1, HEAD_W)),
        ],
        out_specs=[pl.BlockSpec((tq, d), tile),
                   pl.BlockSpec((1, n_heads, HEAD_W, HEAD_W), lambda b, t: (b, 0, 0, 0))],
        out_shape=[jax.ShapeDtypeStruct((m, d), F32),
                   jax.ShapeDtypeStruct((batch, n_heads, HEAD_W, HEAD_W), F32)],
        scratch_shapes=[
            pltpu.VMEM((tq, 2 * w), BF16),
            pltpu.VMEM((n_heads, 2 * tq, HEAD_W), BF16),
            pltpu.VMEM((nt, n_heads, 2 * tq, tq), F32),
            pltpu.VMEM((n_heads, 2 * tq, HEAD_W), F32),
            pltpu.VMEM((n_heads, 2 * tq, HEAD_W), F32),
            pltpu.VMEM((n_heads, 2 * tq, HEAD_W), F32),
            pltpu.VMEM((n_heads, RET_CHUNK, RET_CHUNK), F32),
        ],
        compiler_params=_params("arbitrary", "arbitrary"),
        name="mixer",
    )(lamv, rq, rk, rv, rg, dq, ktb, dvb, x2, wo_bf, gpost, subg)


def _conv_gate(g, u, g1, g2, cw_ref, cb_ref, cols):
    c = cw_ref[0:1, cols] * g2 + cw_ref[1:2, cols] * g1 + cw_ref[2:3, cols] * g + cb_ref[:, cols]
    return (jax.nn.gelu(c) * u).astype(BF16)


def _ffn_body(pt_ref, x_ref, gpre_ref, win_ref, cw_ref, cb_ref, wout_ref, gpost_ref,
              lam_ref, q_ref, kn_ref, vn_ref, ck_hbm, cv_hbm, y_ref, cs_ref, o_ref,
              gbuf, carry, act, kbuf, vbuf, sem, sc_scr, *, tf, d_ff, ck, rows, n_pages, page, n_heads, lam_i):
    t = pl.program_id(1)
    step = pl.program_id(0) * pl.num_programs(1) + t
    n_rows = rows * pl.num_programs(0) * pl.num_programs(1)
    copies = functools.partial(_page_copies, pt_ref, ck_hbm, cv_hbm, kbuf, vbuf, sem, n_pages=n_pages)

    @pl.when(step == 0)
    def _():
        for slot in range(2):
            for c in copies(b=slot, slot=slot):
                c.start()

    @pl.when(t == 0)
    def _():
        carry[...] = jnp.zeros_like(carry)

    lam = _lambda(lam_ref, lam_i)
    for i in range(rows):
        g = step * rows + i
        slot = i % 2
        for c in copies(b=g, slot=slot):
            c.wait()
        o_ref[i] = _paged_row(slot, q_ref[i], kn_ref[i], vn_ref[i], lam, kbuf, vbuf, sc_scr,
                              n_pages=n_pages, page=page, n_heads=n_heads)

        @pl.when(g + 2 < n_rows)
        def _():
            for c in copies(b=g + 2, slot=slot):
                c.start()

    x = x_ref[...]
    h = _rms(x, gpre_ref[...]).astype(BF16)
    for j in range(d_ff // ck):
        cols = slice(j * ck, (j + 1) * ck)
        g = jnp.dot(h, win_ref[:, cols], preferred_element_type=F32)
        u = jnp.dot(h, win_ref[:, d_ff + j * ck:d_ff + (j + 1) * ck], preferred_element_type=F32)
        gbuf[0:8, :] = carry[:, cols]
        gbuf[8:8 + tf, :] = g
        act[:, cols] = _conv_gate(g, u, gbuf[7:7 + tf, :], gbuf[6:6 + tf, :], cw_ref, cb_ref, cols)
        carry[:, cols] = gbuf[tf:tf + 8, :]
    f = jnp.dot(act[...], wout_ref[...], preferred_element_type=F32)
    y_ref[...] = x + _rms(f, gpost_ref[...])
    cs_ref[0] = carry[...]


def _ffn_paged(x2, gpre, win_bf, cw, cb, wout_bf, gpost, batch, seq,
               page_table, lamv, dq, dk, dv, cache_kt, cache_v, n_heads, lam_i):
    m, d = x2.shape
    d_ff = wout_bf.shape[0]
    tf = FFN_TILE
    nt = seq // tf
    n, w = dq.shape
    rows = n // (batch * nt)
    assert rows * batch * nt == n and rows % 2 == 0, "sample rows must split evenly, in pairs, over the FFN steps"
    n_pages = page_table.shape[1]
    page = cache_kt.shape[2]
    assert n_pages % PAGE_GROUP == 0
    tile = lambda b, t, pt: (b * nt + t, 0)
    const = _const_spec
    rspec = pl.BlockSpec((rows, 1, w), lambda b, t, pt: (b * nt + t, 0, 0))
    body = functools.partial(_ffn_body, tf=tf, d_ff=d_ff, ck=FFN_CHUNK, rows=rows, n_pages=n_pages, page=page,
                             n_heads=n_heads, lam_i=lam_i)
    grid_spec = pltpu.PrefetchScalarGridSpec(
        num_scalar_prefetch=1,
        grid=(batch, nt),
        in_specs=[pl.BlockSpec((tf, d), tile), const((1, d)), const(win_bf.shape), const(cw.shape),
                  const(cb.shape), const(wout_bf.shape), const((1, d)),
                  const(lamv.shape), rspec, rspec, rspec,
                  pl.BlockSpec(memory_space=pl.ANY), pl.BlockSpec(memory_space=pl.ANY)],
        out_specs=[pl.BlockSpec((tf, d), tile), pl.BlockSpec((1, 8, d_ff), lambda b, t, pt: (b, 0, 0)), rspec],
        scratch_shapes=[pltpu.VMEM((tf + 8, FFN_CHUNK), F32), pltpu.VMEM((8, d_ff), F32),
                        pltpu.VMEM((tf, d_ff), BF16),
                        pltpu.VMEM((2, n_pages, w, page), F32),
                        pltpu.VMEM((2, n_pages, page * n_heads, HEAD_W), F32),
                        pltpu.SemaphoreType.DMA((2, 2)),
                        pltpu.VMEM((n_pages // PAGE_GROUP, 2 * n_heads, PAGE_GROUP * page), F32)],
    )
    y, cs, o = pl.pallas_call(
        body,
        grid_spec=grid_spec,
        out_shape=[jax.ShapeDtypeStruct((m, d), F32), jax.ShapeDtypeStruct((batch, 8, d_ff), F32),
                   jax.ShapeDtypeStruct((n, 1, w), F32)],
        compiler_params=_params("arbitrary", "arbitrary"),
        name="conv_ffn_paged_attn",
    )(page_table.reshape(-1), x2, gpre, win_bf, cw, cb, wout_bf, gpost,
      lamv, dq.astype(F32).reshape(n, 1, w), dk.reshape(n, 1, w), dv.reshape(n, 1, w), cache_kt, cache_v)
    return y, cs, o.reshape(n, w)


def _ret_dec_body(rq_ref, rk_ref, rv_ref, rg_ref, s_ref, a_ref, sn_ref, *, grp, n_heads):
    row = lax.broadcasted_iota(jnp.int32, (grp, grp * HEAD_W), 0)
    lane = lax.broadcasted_iota(jnp.int32, (grp, grp * HEAD_W), 1)
    own = (lane // HEAD_W) == row

    def block_diag(x):
        return jnp.where(own, jnp.tile(x, (1, grp)), 0.0).astype(BF16)

    for h in range(n_heads):
        cols = slice(h * HEAD_W, (h + 1) * HEAD_W)
        gamma = math.exp(_ret_log_decay(h))
        q = rq_ref[:, cols]
        k = rk_ref[:, cols]
        v = rv_ref[:, cols]
        qb = q.astype(BF16).astype(F32)
        kb = k.astype(BF16).astype(F32)
        att = jnp.sum(qb * kb, axis=-1, keepdims=True).astype(BF16).astype(F32)
        s = s_ref[:, h].reshape(grp * HEAD_W, HEAD_W)
        o = att * v.astype(F32) + jnp.dot(block_diag(q * gamma), s.astype(BF16),
                                          preferred_element_type=F32)
        upd = lax.dot_general(block_diag(k), v, TN_DIMS, preferred_element_type=F32)
        sn_ref[:, h] = (gamma * s + upd).reshape(grp, HEAD_W, HEAD_W)
        a_ref[:, cols] = _gated_group_norm(o, rg_ref[:, cols]).astype(BF16)


def _ret_decode(rq, rk, rv, rg, state, n_heads):
    n, w = rq.shape
    grp = DEC_GROUP
    row = lambda i: (i, 0)
    st = lambda i: (i, 0, 0, 0)
    spec = pl.BlockSpec((grp, w), row)
    st_spec = pl.BlockSpec((grp, n_heads, HEAD_W, HEAD_W), st)
    return pl.pallas_call(
        functools.partial(_ret_dec_body, grp=grp, n_heads=n_heads),
        grid=(n // grp,),
        in_specs=[spec, spec, spec, spec, st_spec],
        out_specs=[spec, st_spec],
        out_shape=[jax.ShapeDtypeStruct((n, w), BF16), jax.ShapeDtypeStruct(state.shape, F32)],
        compiler_params=_params("arbitrary"),
        name="ret_decode",
    )(rq, rk, rv, rg, state)


def _page_copies(pt_ref, ck_hbm, cv_hbm, kbuf, vbuf, sem, b, slot, n_pages):
    out = []
    for p in range(n_pages):
        page = pt_ref[b * n_pages + p]
        out.append(pltpu.make_async_copy(ck_hbm.at[page], kbuf.at[slot, p], sem.at[0, slot]))
        out.append(pltpu.make_async_copy(cv_hbm.at[page], vbuf.at[slot, p], sem.at[1, slot]))
    return out


def _paged_row(slot, q, kn_row, vn_row, lam, kbuf, vbuf, sc_scr, *, n_pages, page, n_heads):
    w = n_heads * HEAD_W
    nc = 2 * n_heads
    past = n_pages * page
    ci = lax.broadcasted_iota(jnp.int32, (nc, w), 0)
    ri = lax.broadcasted_iota(jnp.int32, (nc, w), 1)
    col_of = ri // HEAD_W + n_heads * ((ri % HEAD_W) // (HEAD_W // 2))
    q_sel = jnp.where(ci == col_of, jnp.broadcast_to(q, (nc, w)), 0.0)
    q_sel_bf = q_sel.astype(BF16)
    grp = PAGE_GROUP
    span = grp * page
    rowc = lax.broadcasted_iota(jnp.int32, (nc, span), 0)
    key_i = lax.broadcasted_iota(jnp.int32, (nc, span), 1)
    slope = jnp.zeros((nc, span), F32)
    for h in range(n_heads):
        slope = jnp.where(rowc % n_heads == h, 2.0 ** (-8.0 / n_heads * (h + 1)), slope)

    mx = jnp.full((nc, span), FINITE_MIN, F32)
    for pg in range(n_pages // grp):
        kt = jnp.concatenate([kbuf[slot, grp * pg + i] for i in range(grp)], axis=1).astype(BF16)
        dist = (past - (pg * span + key_i)).astype(F32)
        s = jnp.dot(q_sel_bf, kt, preferred_element_type=F32) - slope * dist
        sc_scr[pg] = s
        mx = jnp.maximum(mx, s)
    kn = kn_row.astype(BF16).astype(F32)
    s_self = jnp.broadcast_to(jnp.sum(q_sel * kn, axis=-1, keepdims=True), (nc, span))
    m = jnp.maximum(jnp.broadcast_to(jnp.max(mx, axis=-1, keepdims=True), (nc, span)), s_self)

    lp = jnp.zeros((nc, span), F32)
    for pg in range(n_pages // grp):
        e = jnp.exp(sc_scr[pg] - m)
        sc_scr[pg] = e
        lp = lp + e
    e_self = jnp.exp(s_self - m)
    l = jnp.broadcast_to(jnp.sum(lp, axis=-1, keepdims=True), (nc, span)) + e_self
    r = jnp.where(rowc < n_heads, 1.0, lam) / l

    def combine(pn):
        return (pn - pltpu.roll(pn, n_heads, axis=0)).astype(BF16)

    def v_rows(p):
        return jnp.concatenate([vbuf[slot, p, pl.ds(h, page, stride=n_heads), :] for h in range(n_heads)],
                               axis=1)

    acc = jnp.zeros((nc, w), F32)
    for pg in range(n_pages // grp):
        v = jnp.concatenate([v_rows(grp * pg + i) for i in range(grp)], axis=0).astype(BF16)
        acc = acc + jnp.dot(combine(sc_scr[pg] * r), v, preferred_element_type=F32)
    a_self = combine(e_self * r).astype(F32)
    vn = vn_row.astype(BF16).astype(F32)
    return jnp.concatenate(
        [acc[h:h + 1, h * HEAD_W:(h + 1) * HEAD_W]
         + a_self[h:h + 1, 0:HEAD_W] * vn[:, h * HEAD_W:(h + 1) * HEAD_W] for h in range(n_heads)], axis=-1)


def _dec_ffn_body(a_ref, do_ref, x_ref, wo_ref, gmix_ref, subg_ref, gpre_ref, wg_ref, wu_ref, cw_ref,
                  cb_ref, sc_ref, wout_ref, gpost_ref, y_ref, g_ref, xmid, hbuf, acc, *, lam_i):
    j = pl.program_id(0)
    ret_w = a_ref.shape[1]

    @pl.when(j == 0)
    def _():
        mo = jnp.dot(a_ref[...], wo_ref[0:ret_w, :], preferred_element_type=F32)
        for h in range(do_ref.shape[1] // HEAD_W):
            cols = slice(h * HEAD_W, (h + 1) * HEAD_W)
            bh = (_rms(do_ref[:, cols], subg_ref[...]) * (1.0 - lam_i)).astype(BF16)
            mo += jnp.dot(bh, wo_ref[ret_w + h * HEAD_W:ret_w + (h + 1) * HEAD_W, :],
                          preferred_element_type=F32)
        xm = x_ref[...] + _rms(mo, gmix_ref[...])
        xmid[...] = xm
        hbuf[...] = _rms(xm, gpre_ref[...]).astype(BF16)
        acc[...] = jnp.zeros_like(acc)

    h = hbuf[...]
    g = jnp.dot(h, wg_ref[...], preferred_element_type=F32)
    u = jnp.dot(h, wu_ref[...], preferred_element_type=F32)
    g_ref[...] = g
    c = cw_ref[0:1, :] * sc_ref[0] + cw_ref[1:2, :] * sc_ref[1] + cw_ref[2:3, :] * g + cb_ref[...]
    acc[...] += jnp.dot((jax.nn.gelu(c) * u).astype(BF16), wout_ref[...], preferred_element_type=F32)

    @pl.when(j == pl.num_programs(0) - 1)
    def _():
        y_ref[...] = xmid[...] + _rms(acc[...], gpost_ref[...])


def _dec_ffn(a, do, x2, wo_bf, gmix, subg, gpre, win_bf, cw, cb, conv_state_t, wout_bf, gpost, lam_i):
    n, d = x2.shape
    d_ff = wout_bf.shape[0]
    ck = FFN_CHUNK
    nj = d_ff // ck
    full = lambda shape: pl.BlockSpec(shape, lambda j: (0,) * len(shape))
    return pl.pallas_call(
        functools.partial(_dec_ffn_body, lam_i=lam_i),
        grid=(nj,),
        in_specs=[full(a.shape), full(do.shape), full(x2.shape), full(wo_bf.shape), full((1, d)),
                  full((1, HEAD_W)), full((1, d)),
                  pl.BlockSpec((d, ck), lambda j: (0, j)), pl.BlockSpec((d, ck), lambda j: (0, nj + j)),
                  pl.BlockSpec((3, ck), lambda j: (0, j)), pl.BlockSpec((1, ck), lambda j: (0, j)),
                  pl.BlockSpec((2, n, ck), lambda j: (0, 0, j)),
                  pl.BlockSpec((ck, d), lambda j: (j, 0)), full((1, d))],
        out_specs=[full((n, d)), pl.BlockSpec((n, ck), lambda j: (0, j))],
        out_shape=[jax.ShapeDtypeStruct((n, d), F32), jax.ShapeDtypeStruct((n, d_ff), F32)],
        scratch_shapes=[pltpu.VMEM((n, d), F32), pltpu.VMEM((n, d), BF16), pltpu.VMEM((n, d), F32)],
        compiler_params=_params("arbitrary"),
        name="dec_ffn",
    )(a, do, x2, wo_bf, gmix, subg, gpre, win_bf, win_bf, cw, cb, conv_state_t, wout_bf, gpost)


def kernel(x_prompt, x_sample, state_ret, cache_k, cache_v, state_conv, page_table,
           norm_mix_pre, norm_mix_post, w_in, w_o, lambda_q1, lambda_k1, lambda_q2, lambda_k2,
           subln_g, norm_ffn_pre, norm_ffn_post, w_ffn_in, conv_w, conv_b, w_ffn_out):
    batch, seq, d = x_prompt.shape
    n_dec = x_sample.shape[0]
    assert x_sample.shape[1] == 1, "the sample group is one token per row"
    depth = w_in.shape[0]
    n_heads = state_ret.shape[2]
    dk_ret = state_ret.shape[3]
    dh_diff = cache_k.shape[-1]
    d_ff = w_ffn_out.shape[1]
    w = n_heads * HEAD_W
    assert w_in.shape[2] == 7 * w and cache_v.shape[-1] == HEAD_W and 2 * dh_diff == HEAD_W
    assert seq % MIX_TILE == 0 and seq % FFN_TILE == 0 and d_ff % FFN_CHUNK == 0
    assert n_dec % DEC_GROUP == 0 and n_dec % 128 == 0 and seq % IN_TILE == 0 and IN_TILE % MIX_TILE == 0

    xp = x_prompt.reshape(batch * seq, d)
    xs = x_sample.reshape(n_dec, d)
    outs = [[] for _ in range(8)]
    for l in range(depth):
        lam_i = 0.8 - 0.6 * math.exp(-0.3 * l)
        lamv = jnp.stack([lambda_q1[l], lambda_k1[l], lambda_q2[l], lambda_k2[l]]).astype(F32)
        row = lambda v: v.reshape(1, -1)
        w_in_bf = w_in[l].astype(BF16)
        w_o_bf = w_o[l].astype(BF16)
        w_ffn_in_bf = w_ffn_in[l].astype(BF16)
        w_ffn_out_bf = w_ffn_out[l].astype(BF16)
        g_pre, g_post = row(norm_mix_pre[l]), row(norm_mix_post[l])
        f_pre, f_post = row(norm_ffn_pre[l]), row(norm_ffn_post[l])
        subg = row(subln_g[l])
        cw, cb = conv_w[l], row(conv_b[l])

        wkt_bf = w_in[l][:, 5 * w:6 * w].T.astype(BF16)
        cache_kt = jnp.transpose(cache_k[l], (0, 2, 3, 4, 1)).reshape(cache_k.shape[1], w, cache_k.shape[2])
        cache_vr = cache_v[l].reshape(cache_v.shape[1], cache_v.shape[2] * n_heads, HEAD_W)
        proj = functools.partial(_inproj, n_heads=n_heads, dk_ret=dk_ret, dh_diff=dh_diff)

        rq, rk, rv, rg, dq, kt, v4, ktb, dvb = proj(xp, g_pre, w_in_bf, wkt_bf, batch, seq, IN_TILE,
                                                     key_block=MIX_TILE)
        xp, s_fin = _mixer(lamv, rq, rk, rv, rg, dq, ktb, dvb, xp, w_o_bf, g_post, subg,
                           batch, seq, n_heads, lam_i)
        outs[0].append(s_fin)
        outs[2].append(jnp.transpose(kt.reshape(batch, n_heads, 2, dh_diff, seq), (0, 4, 1, 2, 3)))
        outs[3].append(v4.reshape(batch, seq, n_heads, HEAD_W))
        rq, rk, rv, rg, dq, kt, v4, dk, dv = proj(xs, g_pre, w_in_bf, wkt_bf, 1, n_dec, n_dec, key_block=None)
        a, s_new = _ret_decode(rq, rk, rv, rg, state_ret[l], n_heads)

        xp, cs, do = _ffn_paged(xp, f_pre, w_ffn_in_bf, cw, cb, w_ffn_out_bf, f_post, batch, seq,
                                page_table, lamv, dq, dk, dv, cache_kt, cache_vr, n_heads, lam_i)
        outs[6].append(cs[:, 6:8, :])

        conv_t = jnp.swapaxes(state_conv[l], 0, 1)
        xs, g_new = _dec_ffn(a, do, xs, w_o_bf, g_post, subg, f_pre, w_ffn_in_bf, cw, cb, conv_t,
                             w_ffn_out_bf, f_post, lam_i)
        outs[1].append(s_new)
        outs[4].append(jnp.transpose(kt.reshape(n_heads, 2, dh_diff, n_dec), (3, 0, 1, 2))[:, None])
        outs[5].append(v4.reshape(n_dec, 1, n_heads, HEAD_W))
        outs[7].append(jnp.stack([state_conv[l][:, 1, :], g_new], axis=1))

    st = [jnp.stack(o) for o in outs]
    return (xp.reshape(batch, seq, d), xs.reshape(n_dec, 1, d), st[0], st[1], st[2], st[3], st[4], st[5],
            st[6], st[7])
```

```python
import functools
import math

import jax
import jax.numpy as jnp
from jax import lax
from jax.experimental import pallas as pl
from jax.experimental.pallas import tpu as pltpu

F32 = jnp.float32
BF16 = jnp.bfloat16
EPS = 1e-6
NEG_INF = -1e30
FINITE_MIN = -3e38

V7X_VMEM_BYTES = 64 * 1024 * 1024
VMEM_LIMIT_BYTES = V7X_VMEM_BYTES * 7 // 8

RET_CHUNK = 256
HEAD_W = 128
IN_TILE = 512
MIX_TILE = 256
FFN_TILE = 256
FFN_CHUNK = 256
DEC_GROUP = 16
PAGE_GROUP = 2
NT_DIMS = (((1,), (1,)), ((), ()))
TN_DIMS = (((0,), (0,)), ((), ()))


def _rms(x, g):
    return x * lax.rsqrt(jnp.mean(x * x, axis=-1, keepdims=True) + EPS) * g


def _params(*sem):
    return pltpu.CompilerParams(dimension_semantics=sem, vmem_limit_bytes=VMEM_LIMIT_BYTES)


def _const_spec(shape):
    n = len(shape)
    return pl.BlockSpec(shape, lambda *_: (0,) * n, pipeline_mode=pl.Buffered(1))


def _ret_log_decay(h):
    return math.log(1.0 - 2.0 ** (-5.0 - h))


def _lambda(lam_ref, lam_i):
    a = jnp.sum(lam_ref[0:1, :] * lam_ref[1:2, :], axis=-1, keepdims=True)
    b = jnp.sum(lam_ref[2:3, :] * lam_ref[3:4, :], axis=-1, keepdims=True)
    return jnp.exp(a) - jnp.exp(b) + lam_i


def _inproj_body(x_ref, g_ref, w_ref, rq_ref, rk_ref, rv_ref, rg_ref, dq_ref, kt_ref, v4_ref,
                 kx_ref, vx_ref, *, width, n_heads, k_scale, q_scale, key_block):
    h = _rms(x_ref[...], g_ref[...]).astype(BF16)

    def col(j):
        return jnp.dot(h, w_ref[:, j * width:(j + 1) * width], preferred_element_type=F32)

    dk = col(5)
    kt = dk.T
    kt_ref[0] = kt
    if key_block is None:
        kx_ref[...] = dk
    else:
        for c in range(kx_ref.shape[0]):
            kx_ref[c] = kt[:, c * key_block:(c + 1) * key_block].astype(BF16)
    dv = col(6)
    for hh in range(n_heads):
        v4_ref[:, hh, :] = dv[:, hh * HEAD_W:(hh + 1) * HEAD_W]
    vx_ref[...] = dv if key_block is None else dv.astype(BF16)
    rq_ref[...] = col(0)
    rk_ref[...] = col(1) * k_scale
    rv_ref[...] = col(2).astype(BF16)
    rg_ref[...] = col(3)
    dq_ref[...] = (col(4) * q_scale).astype(BF16)


def _inproj(x2, g, w_bf, batch, seq, tile, n_heads, dk_ret, dh_diff, key_block):
    m, d = x2.shape
    width = n_heads * HEAD_W
    nt = seq // tile
    row = lambda i: (i, 0)
    f32_out = jax.ShapeDtypeStruct((m, width), F32)
    bf_out = jax.ShapeDtypeStruct((m, width), BF16)
    spec = pl.BlockSpec((tile, width), row)
    if key_block is None:
        kx_shape, kx_spec, vx_shape = f32_out, spec, f32_out
    else:
        kx_shape = jax.ShapeDtypeStruct((m // key_block, width, key_block), BF16)
        kx_spec = pl.BlockSpec((tile // key_block, width, key_block), lambda i: (i, 0, 0))
        vx_shape = bf_out
    body = functools.partial(_inproj_body, width=width, n_heads=n_heads, k_scale=dk_ret ** -0.5,
                             q_scale=dh_diff ** -0.5, key_block=key_block)
    return pl.pallas_call(
        body,
        grid=(m // tile,),
        in_specs=[pl.BlockSpec((tile, d), row), _const_spec((1, d)), _const_spec(w_bf.shape)],
        out_specs=[spec] * 5 + [pl.BlockSpec((1, width, tile), lambda i: (i // nt, 0, i % nt)),
                                pl.BlockSpec((tile, n_heads, HEAD_W), lambda i: (i, 0, 0)), kx_spec, spec],
        out_shape=[f32_out, f32_out, bf_out, f32_out, bf_out,
                   jax.ShapeDtypeStruct((batch, width, seq), F32),
                   jax.ShapeDtypeStruct((m, n_heads, HEAD_W), F32), kx_shape, vx_shape],
        compiler_params=_params("arbitrary"),
        name="inproj",
    )(x2, g, w_bf)


def _retention_decay(length, h):
    i = lax.broadcasted_iota(jnp.int32, (length, length), 0)
    j = lax.broadcasted_iota(jnp.int32, (length, length), 1)
    diff = (i - j).astype(F32)
    causal = diff >= 0
    return jnp.where(causal, jnp.exp(jnp.where(causal, diff, 0.0) * _ret_log_decay(h)), 0.0)


def _retention_row_decays(length, h):
    lg = _ret_log_decay(h)
    ri = lax.broadcasted_iota(jnp.int32, (length, 1), 0).astype(F32)
    return jnp.exp((ri + 1.0) * lg), jnp.exp((length - 1.0 - ri) * lg), math.exp(length * lg)


def _pair_loop(n, body):
    pairs = lax.shift_right_logical(n, 1)

    def two(i, carry):
        body(2 * i)
        body(2 * i + 1)
        return carry

    def one(i, carry):
        body(i)
        return carry

    lax.fori_loop(0, pairs, two, 0)
    lax.fori_loop(2 * pairs, n, one, 0)


def _gated_group_norm(o, g):
    on = o * lax.rsqrt(jnp.mean(o * o, axis=-1, keepdims=True) + EPS)
    return on * (g * jax.nn.sigmoid(g))


def _mixer_body(lam_ref, rq_ref, rk_ref, rv_ref, rg_ref, dq_ref, kt_ref, vb_ref, x_ref, wo_ref,
                gpost_ref, subg_ref, y_ref, s_ref, mix_scr, qq_scr, sc_scr, mx_scr, l_scr, acc_scr, decay_scr,
                *, tq, ret_chunk, n_heads, lam_i):
    t = pl.program_id(1)
    ret_w = n_heads * HEAD_W

    @pl.when(t == 0)
    def _():
        s_ref[...] = jnp.zeros_like(s_ref)

    @pl.when((pl.program_id(0) == 0) & (t == 0))
    def _():
        for h in range(n_heads):
            decay_scr[h] = _retention_decay(ret_chunk, h)

    for c in range(tq // ret_chunk):
        rows = slice(c * ret_chunk, (c + 1) * ret_chunk)
        for h in range(n_heads):
            cols = slice(h * HEAD_W, (h + 1) * HEAD_W)
            decay = decay_scr[h]
            qdec, kdec, gl = _retention_row_decays(ret_chunk, h)
            q = rq_ref[rows, cols]
            k = rk_ref[rows, cols]
            v = rv_ref[rows, cols]
            s = s_ref[0, h]
            att = lax.dot_general(q.astype(BF16), k.astype(BF16), NT_DIMS,
                                  preferred_element_type=F32) * decay
            o = (jnp.dot(att.astype(BF16), v, preferred_element_type=F32)
                 + jnp.dot((q * qdec).astype(BF16), s.astype(BF16), preferred_element_type=F32))
            s_ref[0, h] = gl * s + lax.dot_general((k * kdec).astype(BF16), v, TN_DIMS,
                                                   preferred_element_type=F32)
            mix_scr[rows, cols] = _gated_group_norm(o, rg_ref[rows, cols]).astype(BF16)

    lane = lax.broadcasted_iota(jnp.int32, (tq, HEAD_W), 1)
    for h in range(n_heads):
        qh = dq_ref[:, h * HEAD_W:(h + 1) * HEAD_W]
        zero = jnp.zeros_like(qh)
        qq_scr[h, 0:tq] = jnp.where(lane < HEAD_W // 2, qh, zero)
        qq_scr[h, tq:2 * tq] = jnp.where(lane >= HEAD_W // 2, qh, zero)
    mx_scr[...] = jnp.full(mx_scr.shape, FINITE_MIN, F32)
    l_scr[...] = jnp.zeros_like(l_scr)
    acc_scr[...] = jnp.zeros_like(acc_scr)
    key_j = lax.broadcasted_iota(jnp.int32, (1, tq), 1).astype(F32)
    lane_tiles = [slice(c * HEAD_W, (c + 1) * HEAD_W) for c in range(tq // HEAD_W)]

    def fold(x, op):
        return functools.reduce(op, [x[:, c] for c in lane_tiles])

    def scores(kb, causal):
        key_pos = key_j + ((kb - t) * tq).astype(F32)
        for h in range(n_heads):
            slope = 2.0 ** (-8.0 / n_heads * (h + 1))
            s = jnp.dot(qq_scr[h], kt_ref[kb, h * HEAD_W:(h + 1) * HEAD_W, :], preferred_element_type=F32)
            s = s + slope * key_pos
            if causal is not None:
                s = jnp.where(causal, s, NEG_INF)
            sc_scr[kb, h] = s
            mx_scr[h] = jnp.maximum(mx_scr[h], fold(s, jnp.maximum))

    _pair_loop(t, lambda kb: scores(kb, None))
    ii = lax.broadcasted_iota(jnp.int32, (2 * tq, tq), 0)
    jj = lax.broadcasted_iota(jnp.int32, (2 * tq, tq), 1)
    scores(t, jnp.where(ii >= tq, ii - tq, ii) >= jj)

    for h in range(n_heads):
        mx_scr[h] = jnp.broadcast_to(jnp.max(mx_scr[h], axis=-1, keepdims=True), (2 * tq, HEAD_W))

    def weighted(kb):
        for h in range(n_heads):
            m = mx_scr[h]
            e = [jnp.exp(sc_scr[kb, h, :, c] - m) for c in lane_tiles]
            l_scr[h] += functools.reduce(jnp.add, e)
            vr = vb_ref[pl.ds(pl.multiple_of(kb * tq, tq), tq), h * HEAD_W:(h + 1) * HEAD_W]
            acc_scr[h] += jnp.dot(jnp.concatenate([x.astype(BF16) for x in e], axis=1), vr,
                                  preferred_element_type=F32)

    _pair_loop(t + 1, weighted)
    lam = _lambda(lam_ref, lam_i)
    subg = subg_ref[...]
    for h in range(n_heads):
        on = acc_scr[h] / jnp.sum(l_scr[h], axis=-1, keepdims=True)
        o = _rms(on[:tq] - lam * on[tq:], subg) * (1.0 - lam_i)
        mix_scr[:, ret_w + h * HEAD_W:ret_w + (h + 1) * HEAD_W] = o.astype(BF16)

    mo = jnp.dot(mix_scr[...], wo_ref[...], preferred_element_type=F32)
    y_ref[...] = x_ref[...] + _rms(mo, gpost_ref[...])


def _mixer(lamv, rq, rk, rv, rg, dq, ktb, dvb, x2, wo_bf, gpost, subg, batch, seq, n_heads, lam_i):
    m, d = x2.shape
    w = rq.shape[1]
    tq = MIX_TILE
    nt = seq // tq
    tile = lambda b, t: (b * nt + t, 0)
    whole = lambda b, t: (b, 0)
    body = functools.partial(_mixer_body, tq=tq, ret_chunk=RET_CHUNK, n_heads=n_heads, lam_i=lam_i)
    return pl.pallas_call(
        body,
        grid=(batch, nt),
        in_specs=[
            _const_spec(lamv.shape),
            pl.BlockSpec((tq, w), tile), pl.BlockSpec((tq, w), tile), pl.BlockSpec((tq, w), tile),
            pl.BlockSpec((tq, w), tile), pl.BlockSpec((tq, w), tile),
            pl.BlockSpec((nt, w, tq), lambda b, t: (b, 0, 0)), pl.BlockSpec((seq, w), whole),
            pl.BlockSpec((tq, d), tile),
            _const_spec(wo_bf.shape), _const_spec((1, d)), _const_spec((1, HEAD_W)),
        ],
        out_specs=[pl.BlockSpec((tq, d), tile),
                   pl.BlockSpec((1, n_heads, HEAD_W, HEAD_W), lambda b, t: (b, 0, 0, 0))],
        out_shape=[jax.ShapeDtypeStruct((m, d), F32),
                   jax.ShapeDtypeStruct((batch, n_heads, HEAD_W, HEAD_W), F32)],
        scratch_shapes=[
            pltpu.VMEM((tq, 2 * w), BF16),
            pltpu.VMEM((n_heads, 2 * tq, HEAD_W), BF16),
            pltpu.VMEM((nt, n_heads, 2 * tq, tq), F32),
            pltpu.VMEM((n_heads, 2 * tq, HEAD_W), F32),
            pltpu.VMEM((n_heads, 2 * tq, HEAD_W), F32),
            pltpu.VMEM((n_heads, 2 * tq, HEAD_W), F32),
            pltpu.VMEM((n_heads, RET_CHUNK, RET_CHUNK), F32),
        ],
        compiler_params=_params("arbitrary", "arbitrary"),
        name="mixer",
    )(lamv, rq, rk, rv, rg, dq, ktb, dvb, x2, wo_bf, gpost, subg)


def _conv_gate(g, u, g1, g2, cw_ref, cb_ref, cols):
    c = cw_ref[0:1, cols] * g2 + cw_ref[1:2, cols] * g1 + cw_ref[2:3, cols] * g + cb_ref[:, cols]
    return (jax.nn.gelu(c) * u).astype(BF16)


def _ffn_body(pt_ref, x_ref, gpre_ref, win_ref, cw_ref, cb_ref, wout_ref, gpost_ref,
              lam_ref, q_ref, kn_ref, vn_ref, ck_hbm, cv_hbm, y_ref, cs_ref, o_ref,
              gbuf, carry, act, kbuf, vbuf, sem, sc_scr, *, tf, d_ff, ck, rows, n_pages, page, n_heads, lam_i):
    t = pl.program_id(1)
    step = pl.program_id(0) * pl.num_programs(1) + t
    n_rows = rows * pl.num_programs(0) * pl.num_programs(1)
    copies = functools.partial(_page_copies, pt_ref, ck_hbm, cv_hbm, kbuf, vbuf, sem, n_pages=n_pages)

    @pl.when(step == 0)
    def _():
        for slot in range(2):
            for c in copies(b=slot, slot=slot):
                c.start()

    @pl.when(t == 0)
    def _():
        carry[...] = jnp.zeros_like(carry)

    lam = _lambda(lam_ref, lam_i)
    for i in range(rows):
        g = step * rows + i
        slot = i % 2
        for c in copies(b=g, slot=slot):
            c.wait()
        o_ref[i] = _paged_row(slot, q_ref[i], kn_ref[i], vn_ref[i], lam, kbuf, vbuf, sc_scr,
                              n_pages=n_pages, page=page, n_heads=n_heads)

        @pl.when(g + 2 < n_rows)
        def _():
            for c in copies(b=g + 2, slot=slot):
                c.start()

    x = x_ref[...]
    h = _rms(x, gpre_ref[...]).astype(BF16)
    for j in range(d_ff // ck):
        cols = slice(j * ck, (j + 1) * ck)
        g = jnp.dot(h, win_ref[:, cols], preferred_element_type=F32)
        u = jnp.dot(h, win_ref[:, d_ff + j * ck:d_ff + (j + 1) * ck], preferred_element_type=F32)
        gbuf[0:8, :] = carry[:, cols]
        gbuf[8:8 + tf, :] = g
        act[:, cols] = _conv_gate(g, u, gbuf[7:7 + tf, :], gbuf[6:6 + tf, :], cw_ref, cb_ref, cols)
        carry[:, cols] = gbuf[tf:tf + 8, :]
    f = jnp.dot(act[...], wout_ref[...], preferred_element_type=F32)
    y_ref[...] = x + _rms(f, gpost_ref[...])
    cs_ref[0] = carry[...]


def _ffn_paged(x2, gpre, win_bf, cw, cb, wout_bf, gpost, batch, seq,
               page_table, lamv, dq, dk, dv, cache_kt, cache_v, n_heads, lam_i):
    m, d = x2.shape
    d_ff = wout_bf.shape[0]
    tf = FFN_TILE
    nt = seq // tf
    n, w = dq.shape
    rows = n // (batch * nt)
    assert rows * batch * nt == n and rows % 2 == 0, "sample rows must split evenly, in pairs, over the FFN steps"
    n_pages = page_table.shape[1]
    page = cache_kt.shape[2]
    assert n_pages % PAGE_GROUP == 0
    tile = lambda b, t, pt: (b * nt + t, 0)
    const = _const_spec
    rspec = pl.BlockSpec((rows, 1, w), lambda b, t, pt: (b * nt + t, 0, 0))
    body = functools.partial(_ffn_body, tf=tf, d_ff=d_ff, ck=FFN_CHUNK, rows=rows, n_pages=n_pages, page=page,
                             n_heads=n_heads, lam_i=lam_i)
    grid_spec = pltpu.PrefetchScalarGridSpec(
        num_scalar_prefetch=1,
        grid=(batch, nt),
        in_specs=[pl.BlockSpec((tf, d), tile), const((1, d)), const(win_bf.shape), const(cw.shape),
                  const(cb.shape), const(wout_bf.shape), const((1, d)),
                  const(lamv.shape), rspec, rspec, rspec,
                  pl.BlockSpec(memory_space=pl.ANY), pl.BlockSpec(memory_space=pl.ANY)],
        out_specs=[pl.BlockSpec((tf, d), tile), pl.BlockSpec((1, 8, d_ff), lambda b, t, pt: (b, 0, 0)), rspec],
        scratch_shapes=[pltpu.VMEM((tf + 8, FFN_CHUNK), F32), pltpu.VMEM((8, d_ff), F32),
                        pltpu.VMEM((tf, d_ff), BF16),
                        pltpu.VMEM((2, n_pages, w, page), F32),
                        pltpu.VMEM((2, n_pages, page * n_heads, HEAD_W), F32),
                        pltpu.SemaphoreType.DMA((2, 2)),
                        pltpu.VMEM((n_pages // PAGE_GROUP, 2 * n_heads, PAGE_GROUP * page), F32)],
    )
    y, cs, o = pl.pallas_call(
        body,
        grid_spec=grid_spec,
        out_shape=[jax.ShapeDtypeStruct((m, d), F32), jax.ShapeDtypeStruct((batch, 8, d_ff), F32),
                   jax.ShapeDtypeStruct((n, 1, w), F32)],
        compiler_params=_params("arbitrary", "arbitrary"),
        name="conv_ffn_paged_attn",
    )(page_table.reshape(-1), x2, gpre, win_bf, cw, cb, wout_bf, gpost,
      lamv, dq.astype(F32).reshape(n, 1, w), dk.reshape(n, 1, w), dv.reshape(n, 1, w), cache_kt, cache_v)
    return y, cs, o.reshape(n, w)


def _ret_dec_body(rq_ref, rk_ref, rv_ref, rg_ref, s_ref, a_ref, sn_ref, *, grp, n_heads):
    row = lax.broadcasted_iota(jnp.int32, (grp, grp * HEAD_W), 0)
    lane = lax.broadcasted_iota(jnp.int32, (grp, grp * HEAD_W), 1)
    own = (lane // HEAD_W) == row

    def block_diag(x):
        return jnp.where(own, jnp.tile(x, (1, grp)), 0.0).astype(BF16)

    for h in range(n_heads):
        cols = slice(h * HEAD_W, (h + 1) * HEAD_W)
        gamma = math.exp(_ret_log_decay(h))
        q = rq_ref[:, cols]
        k = rk_ref[:, cols]
        v = rv_ref[:, cols]
        qb = q.astype(BF16).astype(F32)
        kb = k.astype(BF16).astype(F32)
        att = jnp.sum(qb * kb, axis=-1, keepdims=True).astype(BF16).astype(F32)
        s = s_ref[:, h].reshape(grp * HEAD_W, HEAD_W)
        o = att * v.astype(F32) + jnp.dot(block_diag(q * gamma), s.astype(BF16),
                                          preferred_element_type=F32)
        upd = lax.dot_general(block_diag(k), v, TN_DIMS, preferred_element_type=F32)
        sn_ref[:, h] = (gamma * s + upd).reshape(grp, HEAD_W, HEAD_W)
        a_ref[:, cols] = _gated_group_norm(o, rg_ref[:, cols]).astype(BF16)


def _ret_decode(rq, rk, rv, rg, state, n_heads):
    n, w = rq.shape
    grp = DEC_GROUP
    row = lambda i: (i, 0)
    st = lambda i: (i, 0, 0, 0)
    spec = pl.BlockSpec((grp, w), row)
    st_spec = pl.BlockSpec((grp, n_heads, HEAD_W, HEAD_W), st)
    return pl.pallas_call(
        functools.partial(_ret_dec_body, grp=grp, n_heads=n_heads),
        grid=(n // grp,),
        in_specs=[spec, spec, spec, spec, st_spec],
        out_specs=[spec, st_spec],
        out_shape=[jax.ShapeDtypeStruct((n, w), BF16), jax.ShapeDtypeStruct(state.shape, F32)],
        compiler_params=_params("arbitrary"),
        name="ret_decode",
    )(rq, rk, rv, rg, state)


def _page_copies(pt_ref, ck_hbm, cv_hbm, kbuf, vbuf, sem, b, slot, n_pages):
    out = []
    for p in range(n_pages):
        page = pt_ref[b * n_pages + p]
        out.append(pltpu.make_async_copy(ck_hbm.at[page], kbuf.at[slot, p], sem.at[0, slot]))
        out.append(pltpu.make_async_copy(cv_hbm.at[page], vbuf.at[slot, p], sem.at[1, slot]))
    return out


def _paged_row(slot, q, kn_row, vn_row, lam, kbuf, vbuf, sc_scr, *, n_pages, page, n_heads):
    w = n_heads * HEAD_W
    nc = 2 * n_heads
    past = n_pages * page
    ci = lax.broadcasted_iota(jnp.int32, (nc, w), 0)
    ri = lax.broadcasted_iota(jnp.int32, (nc, w), 1)
    col_of = ri // HEAD_W + n_heads * ((ri % HEAD_W) // (HEAD_W // 2))
    q_sel = jnp.where(ci == col_of, jnp.broadcast_to(q, (nc, w)), 0.0)
    q_sel_bf = q_sel.astype(BF16)
    grp = PAGE_GROUP
    span = grp * page
    rowc = lax.broadcasted_iota(jnp.int32, (nc, span), 0)
    key_i = lax.broadcasted_iota(jnp.int32, (nc, span), 1)
    slope = jnp.zeros((nc, span), F32)
    for h in range(n_heads):
        slope = jnp.where(rowc % n_heads == h, 2.0 ** (-8.0 / n_heads * (h + 1)), slope)

    mx = jnp.full((nc, span), FINITE_MIN, F32)
    for pg in range(n_pages // grp):
        kt = jnp.concatenate([kbuf[slot, grp * pg + i] for i in range(grp)], axis=1).astype(BF16)
        dist = (past - (pg * span + key_i)).astype(F32)
        s = jnp.dot(q_sel_bf, kt, preferred_element_type=F32) - slope * dist
        sc_scr[pg] = s
        mx = jnp.maximum(mx, s)
    kn = kn_row.astype(BF16).astype(F32)
    s_self = jnp.broadcast_to(jnp.sum(q_sel * kn, axis=-1, keepdims=True), (nc, span))
    m = jnp.maximum(jnp.broadcast_to(jnp.max(mx, axis=-1, keepdims=True), (nc, span)), s_self)

    lp = jnp.zeros((nc, span), F32)
    for pg in range(n_pages // grp):
        e = jnp.exp(sc_scr[pg] - m)
        sc_scr[pg] = e
        lp = lp + e
    e_self = jnp.exp(s_self - m)
    l = jnp.broadcast_to(jnp.sum(lp, axis=-1, keepdims=True), (nc, span)) + e_self
    r = jnp.where(rowc < n_heads, 1.0, lam) / l

    def combine(pn):
        return (pn - pltpu.roll(pn, n_heads, axis=0)).astype(BF16)

    def v_rows(p):
        return jnp.concatenate([vbuf[slot, p, pl.ds(h, page, stride=n_heads), :] for h in range(n_heads)],
                               axis=1)

    acc = jnp.zeros((nc, w), F32)
    for pg in range(n_pages // grp):
        v = jnp.concatenate([v_rows(grp * pg + i) for i in range(grp)], axis=0).astype(BF16)
        acc = acc + jnp.dot(combine(sc_scr[pg] * r), v, preferred_element_type=F32)
    a_self = combine(e_self * r).astype(F32)
    vn = vn_row.astype(BF16).astype(F32)
    return jnp.concatenate(
        [acc[h:h + 1, h * HEAD_W:(h + 1) * HEAD_W]
         + a_self[h:h + 1, 0:HEAD_W] * vn[:, h * HEAD_W:(h + 1) * HEAD_W] for h in range(n_heads)], axis=-1)


def _dec_ffn_body(a_ref, do_ref, x_ref, wo_ref, gmix_ref, subg_ref, gpre_ref, wg_ref, wu_ref, cw_ref,
                  cb_ref, sc_ref, wout_ref, gpost_ref, y_ref, g_ref, xmid, hbuf, acc, *, lam_i):
    j = pl.program_id(0)
    ret_w = a_ref.shape[1]

    @pl.when(j == 0)
    def _():
        mo = jnp.dot(a_ref[...], wo_ref[0:ret_w, :], preferred_element_type=F32)
        for h in range(do_ref.shape[1] // HEAD_W):
            cols = slice(h * HEAD_W, (h + 1) * HEAD_W)
            bh = (_rms(do_ref[:, cols], subg_ref[...]) * (1.0 - lam_i)).astype(BF16)
            mo += jnp.dot(bh, wo_ref[ret_w + h * HEAD_W:ret_w + (h + 1) * HEAD_W, :],
                          preferred_element_type=F32)
        xm = x_ref[...] + _rms(mo, gmix_ref[...])
        xmid[...] = xm
        hbuf[...] = _rms(xm, gpre_ref[...]).astype(BF16)
        acc[...] = jnp.zeros_like(acc)

    h = hbuf[...]
    g = jnp.dot(h, wg_ref[...], preferred_element_type=F32)
    u = jnp.dot(h, wu_ref[...], preferred_element_type=F32)
    g_ref[...] = g
    c = cw_ref[0:1, :] * sc_ref[0] + cw_ref[1:2, :] * sc_ref[1] + cw_ref[2:3, :] * g + cb_ref[...]
    acc[...] += jnp.dot((jax.nn.gelu(c) * u).astype(BF16), wout_ref[...], preferred_element_type=F32)

    @pl.when(j == pl.num_programs(0) - 1)
    def _():
        y_ref[...] = xmid[...] + _rms(acc[...], gpost_ref[...])


def _dec_ffn(a, do, x2, wo_bf, gmix, subg, gpre, win_bf, cw, cb, conv_state_t, wout_bf, gpost, lam_i):
    n, d = x2.shape
    d_ff = wout_bf.shape[0]
    ck = FFN_CHUNK
    nj = d_ff // ck
    full = lambda shape: pl.BlockSpec(shape, lambda j: (0,) * len(shape))
    return pl.pallas_call(
        functools.partial(_dec_ffn_body, lam_i=lam_i),
        grid=(nj,),
        in_specs=[full(a.shape), full(do.shape), full(x2.shape), full(wo_bf.shape), full((1, d)),
                  full((1, HEAD_W)), full((1, d)),
                  pl.BlockSpec((d, ck), lambda j: (0, j)), pl.BlockSpec((d, ck), lambda j: (0, nj + j)),
                  pl.BlockSpec((3, ck), lambda j: (0, j)), pl.BlockSpec((1, ck), lambda j: (0, j)),
                  pl.BlockSpec((2, n, ck), lambda j: (0, 0, j)),
                  pl.BlockSpec((ck, d), lambda j: (j, 0)), full((1, d))],
        out_specs=[full((n, d)), pl.BlockSpec((n, ck), lambda j: (0, j))],
        out_shape=[jax.ShapeDtypeStruct((n, d), F32), jax.ShapeDtypeStruct((n, d_ff), F32)],
        scratch_shapes=[pltpu.VMEM((n, d), F32), pltpu.VMEM((n, d), BF16), pltpu.VMEM((n, d), F32)],
        compiler_params=_params("arbitrary"),
        name="dec_ffn",
    )(a, do, x2, wo_bf, gmix, subg, gpre, win_bf, win_bf, cw, cb, conv_state_t, wout_bf, gpost)


def kernel(x_prompt, x_sample, state_ret, cache_k, cache_v, state_conv, page_table,
           norm_mix_pre, norm_mix_post, w_in, w_o, lambda_q1, lambda_k1, lambda_q2, lambda_k2,
           subln_g, norm_ffn_pre, norm_ffn_post, w_ffn_in, conv_w, conv_b, w_ffn_out):
    batch, seq, d = x_prompt.shape
    n_dec = x_sample.shape[0]
    assert x_sample.shape[1] == 1, "the sample group is one token per row"
    depth = w_in.shape[0]
    n_heads = state_ret.shape[2]
    dk_ret = state_ret.shape[3]
    dh_diff = cache_k.shape[-1]
    d_ff = w_ffn_out.shape[1]
    w = n_heads * HEAD_W
    assert w_in.shape[2] == 7 * w and cache_v.shape[-1] == HEAD_W and 2 * dh_diff == HEAD_W
    assert seq % MIX_TILE == 0 and seq % FFN_TILE == 0 and d_ff % FFN_CHUNK == 0
    assert n_dec % DEC_GROUP == 0 and n_dec % 128 == 0 and seq % IN_TILE == 0 and IN_TILE % MIX_TILE == 0

    xp = x_prompt.reshape(batch * seq, d)
    xs = x_sample.reshape(n_dec, d)
    outs = [[] for _ in range(8)]
    for l in range(depth):
        lam_i = 0.8 - 0.6 * math.exp(-0.3 * l)
        lamv = jnp.stack([lambda_q1[l], lambda_k1[l], lambda_q2[l], lambda_k2[l]]).astype(F32)
        row = lambda v: v.reshape(1, -1)
        w_in_bf = w_in[l].astype(BF16)
        w_o_bf = w_o[l].astype(BF16)
        w_ffn_in_bf = w_ffn_in[l].astype(BF16)
        w_ffn_out_bf = w_ffn_out[l].astype(BF16)
        g_pre, g_post = row(norm_mix_pre[l]), row(norm_mix_post[l])
        f_pre, f_post = row(norm_ffn_pre[l]), row(norm_ffn_post[l])
        subg = row(subln_g[l])
        cw, cb = conv_w[l], row(conv_b[l])

        cache_kt = jnp.transpose(cache_k[l], (0, 2, 3, 4, 1)).reshape(cache_k.shape[1], w, cache_k.shape[2])
        cache_vr = cache_v[l].reshape(cache_v.shape[1], cache_v.shape[2] * n_heads, HEAD_W)
        proj = functools.partial(_inproj, n_heads=n_heads, dk_ret=dk_ret, dh_diff=dh_diff)

        rq, rk, rv, rg, dq, kt, v4, ktb, dvb = proj(xp, g_pre, w_in_bf, batch, seq, IN_TILE,
                                                     key_block=MIX_TILE)
        xp, s_fin = _mixer(lamv, rq, rk, rv, rg, dq, ktb, dvb, xp, w_o_bf, g_post, subg,
                           batch, seq, n_heads, lam_i)
        outs[0].append(s_fin)
        outs[2].append(jnp.transpose(kt.reshape(batch, n_heads, 2, dh_diff, seq), (0, 4, 1, 2, 3)))
        outs[3].append(v4.reshape(batch, seq, n_heads, HEAD_W))
        rq, rk, rv, rg, dq, kt, v4, dk, dv = proj(xs, g_pre, w_in_bf, 1, n_dec, n_dec, key_block=None)
        a, s_new = _ret_decode(rq, rk, rv, rg, state_ret[l], n_heads)

        xp, cs, do = _ffn_paged(xp, f_pre, w_ffn_in_bf, cw, cb, w_ffn_out_bf, f_post, batch, seq,
                                page_table, lamv, dq, dk, dv, cache_kt, cache_vr, n_heads, lam_i)
        outs[6].append(cs[:, 6:8, :])

        conv_t = jnp.swapaxes(state_conv[l], 0, 1)
        xs, g_new = _dec_ffn(a, do, xs, w_o_bf, g_post, subg, f_pre, w_ffn_in_bf, cw, cb, conv_t,
                             w_ffn_out_bf, f_post, lam_i)
        outs[1].append(s_new)
        outs[4].append(jnp.transpose(kt.reshape(n_heads, 2, dh_diff, n_dec), (3, 0, 1, 2))[:, None])
        outs[5].append(v4.reshape(n_dec, 1, n_heads, HEAD_W))
        outs[7].append(jnp.stack([state_conv[l][:, 1, :], g_new], axis=1))

    st = [jnp.stack(o) for o in outs]
    return (xp.reshape(batch, seq, d), xs.reshape(n_dec, 1, d), st[0], st[1], st[2], st[3], st[4], st[5],
            st[6], st[7])
```

```python
import functools
import math

import jax
import jax.numpy as jnp
from jax import lax
from jax.experimental import pallas as pl
from jax.experimental.pallas import tpu as pltpu

F32 = jnp.float32
BF16 = jnp.bfloat16
EPS = 1e-6
NEG_INF = -1e30
FINITE_MIN = -3e38

V7X_VMEM_BYTES = 64 * 1024 * 1024
VMEM_LIMIT_BYTES = V7X_VMEM_BYTES * 7 // 8

RET_CHUNK = 256
HEAD_W = 128
IN_TILE = 512
MIX_TILE = 256
FFN_TILE = 256
FFN_CHUNK = 256
DEC_GROUP = 16
DEC_FFN_STEPS = 2
PAGE_GROUP = 2
NT_DIMS = (((1,), (1,)), ((), ()))
TN_DIMS = (((0,), (0,)), ((), ()))


def _rms(x, g):
    return x * lax.rsqrt(jnp.mean(x * x, axis=-1, keepdims=True) + EPS) * g


def _params(*sem):
    return pltpu.CompilerParams(dimension_semantics=sem, vmem_limit_bytes=VMEM_LIMIT_BYTES)


def _const_spec(shape):
    n = len(shape)
    return pl.BlockSpec(shape, lambda *_: (0,) * n, pipeline_mode=pl.Buffered(1))


def _ret_log_decay(h):
    return math.log(1.0 - 2.0 ** (-5.0 - h))


def _lambda(lam_ref, lam_i):
    a = jnp.sum(lam_ref[0:1, :] * lam_ref[1:2, :], axis=-1, keepdims=True)
    b = jnp.sum(lam_ref[2:3, :] * lam_ref[3:4, :], axis=-1, keepdims=True)
    return jnp.exp(a) - jnp.exp(b) + lam_i


def _inproj_body(x_ref, g_ref, w_ref, rq_ref, rk_ref, rv_ref, rg_ref, dq_ref, kt_ref, v4_ref,
                 kx_ref, vx_ref, *, width, n_heads, k_scale, q_scale, key_block):
    h = _rms(x_ref[...], g_ref[...]).astype(BF16)

    def col(j):
        return jnp.dot(h, w_ref[:, j * width:(j + 1) * width], preferred_element_type=F32)

    dk = col(5)
    kt = dk.T
    kt_ref[0] = kt
    if key_block is None:
        kx_ref[...] = dk
    else:
        for c in range(kx_ref.shape[0]):
            kx_ref[c] = kt[:, c * key_block:(c + 1) * key_block].astype(BF16)
    dv = col(6)
    for hh in range(n_heads):
        v4_ref[:, hh, :] = dv[:, hh * HEAD_W:(hh + 1) * HEAD_W]
    vx_ref[...] = dv if key_block is None else dv.astype(BF16)
    rq_ref[...] = col(0)
    rk_ref[...] = col(1) * k_scale
    rv_ref[...] = col(2).astype(BF16)
    rg_ref[...] = col(3)
    dq_ref[...] = (col(4) * q_scale).astype(BF16)


def _inproj(x2, g, w_bf, batch, seq, tile, n_heads, dk_ret, dh_diff, key_block):
    m, d = x2.shape
    width = n_heads * HEAD_W
    nt = seq // tile
    row = lambda i: (i, 0)
    f32_out = jax.ShapeDtypeStruct((m, width), F32)
    bf_out = jax.ShapeDtypeStruct((m, width), BF16)
    spec = pl.BlockSpec((tile, width), row)
    if key_block is None:
        kx_shape, kx_spec, vx_shape = f32_out, spec, f32_out
    else:
        kx_shape = jax.ShapeDtypeStruct((m // key_block, width, key_block), BF16)
        kx_spec = pl.BlockSpec((tile // key_block, width, key_block), lambda i: (i, 0, 0))
        vx_shape = bf_out
    body = functools.partial(_inproj_body, width=width, n_heads=n_heads, k_scale=dk_ret ** -0.5,
                             q_scale=dh_diff ** -0.5, key_block=key_block)
    return pl.pallas_call(
        body,
        grid=(m // tile,),
        in_specs=[pl.BlockSpec((tile, d), row), _const_spec((1, d)), _const_spec(w_bf.shape)],
        out_specs=[spec] * 5 + [pl.BlockSpec((1, width, tile), lambda i: (i // nt, 0, i % nt)),
                                pl.BlockSpec((tile, n_heads, HEAD_W), lambda i: (i, 0, 0)), kx_spec, spec],
        out_shape=[f32_out, f32_out, bf_out, f32_out, bf_out,
                   jax.ShapeDtypeStruct((batch, width, seq), F32),
                   jax.ShapeDtypeStruct((m, n_heads, HEAD_W), F32), kx_shape, vx_shape],
        compiler_params=_params("arbitrary"),
        name="inproj",
    )(x2, g, w_bf)


def _retention_decay(length, h):
    i = lax.broadcasted_iota(jnp.int32, (length, length), 0)
    j = lax.broadcasted_iota(jnp.int32, (length, length), 1)
    diff = (i - j).astype(F32)
    causal = diff >= 0
    return jnp.where(causal, jnp.exp(jnp.where(causal, diff, 0.0) * _ret_log_decay(h)), 0.0)


def _retention_row_decays(length, h):
    lg = _ret_log_decay(h)
    ri = lax.broadcasted_iota(jnp.int32, (length, 1), 0).astype(F32)
    return jnp.exp((ri + 1.0) * lg), jnp.exp((length - 1.0 - ri) * lg), math.exp(length * lg)


def _pair_loop(n, body):
    pairs = lax.shift_right_logical(n, 1)

    def two(i, carry):
        body(2 * i)
        body(2 * i + 1)
        return carry

    def one(i, carry):
        body(i)
        return carry

    lax.fori_loop(0, pairs, two, 0)
    lax.fori_loop(2 * pairs, n, one, 0)


def _gated_group_norm(o, g):
    on = o * lax.rsqrt(jnp.mean(o * o, axis=-1, keepdims=True) + EPS)
    return on * (g * jax.nn.sigmoid(g))


def _mixer_body(lam_ref, rq_ref, rk_ref, rv_ref, rg_ref, dq_ref, kt_ref, vb_ref, x_ref, wo_ref,
                gpost_ref, subg_ref, y_ref, s_ref, mix_scr, qq_scr, sc_scr, mx_scr, l_scr, acc_scr, decay_scr,
                *, tq, ret_chunk, n_heads, lam_i):
    t = pl.program_id(1)
    ret_w = n_heads * HEAD_W

    @pl.when(t == 0)
    def _():
        s_ref[...] = jnp.zeros_like(s_ref)

    @pl.when((pl.program_id(0) == 0) & (t == 0))
    def _():
        for h in range(n_heads):
            decay_scr[h] = _retention_decay(ret_chunk, h)

    for c in range(tq // ret_chunk):
        rows = slice(c * ret_chunk, (c + 1) * ret_chunk)
        for h in range(n_heads):
            cols = slice(h * HEAD_W, (h + 1) * HEAD_W)
            decay = decay_scr[h]
            qdec, kdec, gl = _retention_row_decays(ret_chunk, h)
            q = rq_ref[rows, cols]
            k = rk_ref[rows, cols]
            v = rv_ref[rows, cols]
            s = s_ref[0, h]
            att = lax.dot_general(q.astype(BF16), k.astype(BF16), NT_DIMS,
                                  preferred_element_type=F32) * decay
            o = (jnp.dot(att.astype(BF16), v, preferred_element_type=F32)
                 + jnp.dot((q * qdec).astype(BF16), s.astype(BF16), preferred_element_type=F32))
            s_ref[0, h] = gl * s + lax.dot_general((k * kdec).astype(BF16), v, TN_DIMS,
                                                   preferred_element_type=F32)
            mix_scr[rows, cols] = _gated_group_norm(o, rg_ref[rows, cols]).astype(BF16)

    lane = lax.broadcasted_iota(jnp.int32, (tq, HEAD_W), 1)
    for h in range(n_heads):
        qh = dq_ref[:, h * HEAD_W:(h + 1) * HEAD_W]
        zero = jnp.zeros_like(qh)
        qq_scr[h, 0:tq] = jnp.where(lane < HEAD_W // 2, qh, zero)
        qq_scr[h, tq:2 * tq] = jnp.where(lane >= HEAD_W // 2, qh, zero)
    mx_scr[...] = jnp.full(mx_scr.shape, FINITE_MIN, F32)
    l_scr[...] = jnp.zeros_like(l_scr)
    acc_scr[...] = jnp.zeros_like(acc_scr)
    key_j = lax.broadcasted_iota(jnp.int32, (1, tq), 1).astype(F32)
    lane_tiles = [slice(c * HEAD_W, (c + 1) * HEAD_W) for c in range(tq // HEAD_W)]

    def fold(x, op):
        return functools.reduce(op, [x[:, c] for c in lane_tiles])

    def scores(kb, causal):
        key_pos = key_j + ((kb - t) * tq).astype(F32)
        for h in range(n_heads):
            slope = 2.0 ** (-8.0 / n_heads * (h + 1))
            s = jnp.dot(qq_scr[h], kt_ref[kb, h * HEAD_W:(h + 1) * HEAD_W, :], preferred_element_type=F32)
            s = s + slope * key_pos
            if causal is not None:
                s = jnp.where(causal, s, NEG_INF)
            sc_scr[kb, h] = s
            mx_scr[h] = jnp.maximum(mx_scr[h], fold(s, jnp.maximum))

    _pair_loop(t, lambda kb: scores(kb, None))
    ii = lax.broadcasted_iota(jnp.int32, (2 * tq, tq), 0)
    jj = lax.broadcasted_iota(jnp.int32, (2 * tq, tq), 1)
    scores(t, jnp.where(ii >= tq, ii - tq, ii) >= jj)

    for h in range(n_heads):
        mx_scr[h] = jnp.broadcast_to(jnp.max(mx_scr[h], axis=-1, keepdims=True), (2 * tq, HEAD_W))

    def weighted(kb):
        for h in range(n_heads):
            m = mx_scr[h]
            e = [jnp.exp(sc_scr[kb, h, :, c] - m) for c in lane_tiles]
            l_scr[h] += functools.reduce(jnp.add, e)
            vr = vb_ref[pl.ds(pl.multiple_of(kb * tq, tq), tq), h * HEAD_W:(h + 1) * HEAD_W]
            acc_scr[h] += jnp.dot(jnp.concatenate([x.astype(BF16) for x in e], axis=1), vr,
                                  preferred_element_type=F32)

    _pair_loop(t + 1, weighted)
    lam = _lambda(lam_ref, lam_i)
    subg = subg_ref[...]
    for h in range(n_heads):
        on = acc_scr[h] / jnp.sum(l_scr[h], axis=-1, keepdims=True)
        o = _rms(on[:tq] - lam * on[tq:], subg) * (1.0 - lam_i)
        mix_scr[:, ret_w + h * HEAD_W:ret_w + (h + 1) * HEAD_W] = o.astype(BF16)

    mo = jnp.dot(mix_scr[...], wo_ref[...], preferred_element_type=F32)
    y_ref[...] = x_ref[...] + _rms(mo, gpost_ref[...])


def _mixer(lamv, rq, rk, rv, rg, dq, ktb, dvb, x2, wo_bf, gpost, subg, batch, seq, n_heads, lam_i):
    m, d = x2.shape
    w = rq.shape[1]
    tq = MIX_TILE
    nt = seq // tq
    tile = lambda b, t: (b * nt + t, 0)
    whole = lambda b, t: (b, 0)
    body = functools.partial(_mixer_body, tq=tq, ret_chunk=RET_CHUNK, n_heads=n_heads, lam_i=lam_i)
    return pl.pallas_call(
        body,
        grid=(batch, nt),
        in_specs=[
            _const_spec(lamv.shape),
            pl.BlockSpec((tq, w), tile), pl.BlockSpec((tq, w), tile), pl.BlockSpec((tq, w), tile),
            pl.BlockSpec((tq, w), tile), pl.BlockSpec((tq, w), tile),
            pl.BlockSpec((nt, w, tq), lambda b, t: (b, 0, 0)), pl.BlockSpec((seq, w), whole),
            pl.BlockSpec((tq, d), tile),
            _const_spec(wo_bf.shape), _const_spec((1, d)), _const_spec((1, HEAD_W)),
        ],
        out_specs=[pl.BlockSpec((tq, d), tile),
                   pl.BlockSpec((1, n_heads, HEAD_W, HEAD_W), lambda b, t: (b, 0, 0, 0))],
        out_shape=[jax.ShapeDtypeStruct((m, d), F32),
                   jax.ShapeDtypeStruct((batch, n_heads, HEAD_W, HEAD_W), F32)],
        scratch_shapes=[
            pltpu.VMEM((tq, 2 * w), BF16),
            pltpu.VMEM((n_heads, 2 * tq, HEAD_W), BF16),
            pltpu.VMEM((nt, n_heads, 2 * tq, tq), F32),
            pltpu.VMEM((n_heads, 2 * tq, HEAD_W), F32),
            pltpu.VMEM((n_heads, 2 * tq, HEAD_W), F32),
            pltpu.VMEM((n_heads, 2 * tq, HEAD_W), F32),
            pltpu.VMEM((n_heads, RET_CHUNK, RET_CHUNK), F32),
        ],
        compiler_params=_params("arbitrary", "arbitrary"),
        name="mixer",
    )(lamv, rq, rk, rv, rg, dq, ktb, dvb, x2, wo_bf, gpost, subg)


def _conv_gate(g, u, g1, g2, cw_ref, cb_ref, cols):
    c = cw_ref[0:1, cols] * g2 + cw_ref[1:2, cols] * g1 + cw_ref[2:3, cols] * g + cb_ref[:, cols]
    return (jax.nn.gelu(c) * u).astype(BF16)


def _ffn_body(pt_ref, x_ref, gpre_ref, win_ref, cw_ref, cb_ref, wout_ref, gpost_ref,
              lam_ref, q_ref, kn_ref, vn_ref, ck_hbm, cv_hbm, y_ref, cs_ref, o_ref,
              gbuf, carry, act, kbuf, vbuf, sem, sc_scr, *, tf, d_ff, ck, rows, n_pages, page, n_heads, lam_i):
    t = pl.program_id(1)
    step = pl.program_id(0) * pl.num_programs(1) + t
    n_rows = rows * pl.num_programs(0) * pl.num_programs(1)
    copies = functools.partial(_page_copies, pt_ref, ck_hbm, cv_hbm, kbuf, vbuf, sem, n_pages=n_pages)

    @pl.when(step == 0)
    def _():
        for slot in range(2):
            for c in copies(b=slot, slot=slot):
                c.start()

    @pl.when(t == 0)
    def _():
        carry[...] = jnp.zeros_like(carry)

    lam = _lambda(lam_ref, lam_i)
    for i in range(rows):
        g = step * rows + i
        slot = i % 2
        for c in copies(b=g, slot=slot):
            c.wait()
        o_ref[i] = _paged_row(slot, q_ref[i], kn_ref[i], vn_ref[i], lam, kbuf, vbuf, sc_scr,
                              n_pages=n_pages, page=page, n_heads=n_heads)

        @pl.when(g + 2 < n_rows)
        def _():
            for c in copies(b=g + 2, slot=slot):
                c.start()

    x = x_ref[...]
    h = _rms(x, gpre_ref[...]).astype(BF16)
    for j in range(d_ff // ck):
        cols = slice(j * ck, (j + 1) * ck)
        g = jnp.dot(h, win_ref[:, cols], preferred_element_type=F32)
        u = jnp.dot(h, win_ref[:, d_ff + j * ck:d_ff + (j + 1) * ck], preferred_element_type=F32)
        gbuf[0:8, :] = carry[:, cols]
        gbuf[8:8 + tf, :] = g
        act[:, cols] = _conv_gate(g, u, gbuf[7:7 + tf, :], gbuf[6:6 + tf, :], cw_ref, cb_ref, cols)
        carry[:, cols] = gbuf[tf:tf + 8, :]
    f = jnp.dot(act[...], wout_ref[...], preferred_element_type=F32)
    y_ref[...] = x + _rms(f, gpost_ref[...])
    cs_ref[0] = carry[...]


def _ffn_paged(x2, gpre, win_bf, cw, cb, wout_bf, gpost, batch, seq,
               page_table, lamv, dq, dk, dv, cache_kt, cache_v, n_heads, lam_i):
    m, d = x2.shape
    d_ff = wout_bf.shape[0]
    tf = FFN_TILE
    nt = seq // tf
    n, w = dq.shape
    rows = n // (batch * nt)
    assert rows * batch * nt == n and rows % 2 == 0, "sample rows must split evenly, in pairs, over the FFN steps"
    n_pages = page_table.shape[1]
    page = cache_kt.shape[2]
    assert n_pages % PAGE_GROUP == 0
    tile = lambda b, t, pt: (b * nt + t, 0)
    const = _const_spec
    rspec = pl.BlockSpec((rows, 1, w), lambda b, t, pt: (b * nt + t, 0, 0))
    body = functools.partial(_ffn_body, tf=tf, d_ff=d_ff, ck=FFN_CHUNK, rows=rows, n_pages=n_pages, page=page,
                             n_heads=n_heads, lam_i=lam_i)
    grid_spec = pltpu.PrefetchScalarGridSpec(
        num_scalar_prefetch=1,
        grid=(batch, nt),
        in_specs=[pl.BlockSpec((tf, d), tile), const((1, d)), const(win_bf.shape), const(cw.shape),
                  const(cb.shape), const(wout_bf.shape), const((1, d)),
                  const(lamv.shape), rspec, rspec, rspec,
                  pl.BlockSpec(memory_space=pl.ANY), pl.BlockSpec(memory_space=pl.ANY)],
        out_specs=[pl.BlockSpec((tf, d), tile), pl.BlockSpec((1, 8, d_ff), lambda b, t, pt: (b, 0, 0)), rspec],
        scratch_shapes=[pltpu.VMEM((tf + 8, FFN_CHUNK), F32), pltpu.VMEM((8, d_ff), F32),
                        pltpu.VMEM((tf, d_ff), BF16),
                        pltpu.VMEM((2, n_pages, w, page), F32),
                        pltpu.VMEM((2, n_pages, page * n_heads, HEAD_W), F32),
                        pltpu.SemaphoreType.DMA((2, 2)),
                        pltpu.VMEM((n_pages // PAGE_GROUP, 2 * n_heads, PAGE_GROUP * page), F32)],
    )
    y, cs, o = pl.pallas_call(
        body,
        grid_spec=grid_spec,
        out_shape=[jax.ShapeDtypeStruct((m, d), F32), jax.ShapeDtypeStruct((batch, 8, d_ff), F32),
                   jax.ShapeDtypeStruct((n, 1, w), F32)],
        compiler_params=_params("arbitrary", "arbitrary"),
        name="conv_ffn_paged_attn",
    )(page_table.reshape(-1), x2, gpre, win_bf, cw, cb, wout_bf, gpost,
      lamv, dq.astype(F32).reshape(n, 1, w), dk.reshape(n, 1, w), dv.reshape(n, 1, w), cache_kt, cache_v)
    return y, cs, o.reshape(n, w)


def _ret_dec_body(rq_ref, rk_ref, rv_ref, rg_ref, s_ref, a_ref, sn_ref, *, grp, n_heads):
    row = lax.broadcasted_iota(jnp.int32, (grp, grp * HEAD_W), 0)
    lane = lax.broadcasted_iota(jnp.int32, (grp, grp * HEAD_W), 1)
    own = (lane // HEAD_W) == row

    def block_diag(x):
        return jnp.where(own, jnp.tile(x, (1, grp)), 0.0).astype(BF16)

    for h in range(n_heads):
        cols = slice(h * HEAD_W, (h + 1) * HEAD_W)
        gamma = math.exp(_ret_log_decay(h))
        q = rq_ref[:, cols]
        k = rk_ref[:, cols]
        v = rv_ref[:, cols]
        qb = q.astype(BF16).astype(F32)
        kb = k.astype(BF16).astype(F32)
        att = jnp.sum(qb * kb, axis=-1, keepdims=True).astype(BF16).astype(F32)
        s = s_ref[:, h].reshape(grp * HEAD_W, HEAD_W)
        o = att * v.astype(F32) + jnp.dot(block_diag(q * gamma), s.astype(BF16),
                                          preferred_element_type=F32)
        upd = lax.dot_general(block_diag(k), v, TN_DIMS, preferred_element_type=F32)
        sn_ref[:, h] = (gamma * s + upd).reshape(grp, HEAD_W, HEAD_W)
        a_ref[:, cols] = _gated_group_norm(o, rg_ref[:, cols]).astype(BF16)


def _ret_decode(rq, rk, rv, rg, state, n_heads):
    n, w = rq.shape
    grp = DEC_GROUP
    row = lambda i: (i, 0)
    st = lambda i: (i, 0, 0, 0)
    spec = pl.BlockSpec((grp, w), row)
    st_spec = pl.BlockSpec((grp, n_heads, HEAD_W, HEAD_W), st)
    return pl.pallas_call(
        functools.partial(_ret_dec_body, grp=grp, n_heads=n_heads),
        grid=(n // grp,),
        in_specs=[spec, spec, spec, spec, st_spec],
        out_specs=[spec, st_spec],
        out_shape=[jax.ShapeDtypeStruct((n, w), BF16), jax.ShapeDtypeStruct(state.shape, F32)],
        compiler_params=_params("arbitrary"),
        name="ret_decode",
    )(rq, rk, rv, rg, state)


def _page_copies(pt_ref, ck_hbm, cv_hbm, kbuf, vbuf, sem, b, slot, n_pages):
    out = []
    for p in range(n_pages):
        page = pt_ref[b * n_pages + p]
        out.append(pltpu.make_async_copy(ck_hbm.at[page], kbuf.at[slot, p], sem.at[0, slot]))
        out.append(pltpu.make_async_copy(cv_hbm.at[page], vbuf.at[slot, p], sem.at[1, slot]))
    return out


def _paged_row(slot, q, kn_row, vn_row, lam, kbuf, vbuf, sc_scr, *, n_pages, page, n_heads):
    w = n_heads * HEAD_W
    nc = 2 * n_heads
    past = n_pages * page
    ci = lax.broadcasted_iota(jnp.int32, (nc, w), 0)
    ri = lax.broadcasted_iota(jnp.int32, (nc, w), 1)
    col_of = ri // HEAD_W + n_heads * ((ri % HEAD_W) // (HEAD_W // 2))
    q_sel = jnp.where(ci == col_of, jnp.broadcast_to(q, (nc, w)), 0.0)
    q_sel_bf = q_sel.astype(BF16)
    grp = PAGE_GROUP
    span = grp * page
    rowc = lax.broadcasted_iota(jnp.int32, (nc, span), 0)
    key_i = lax.broadcasted_iota(jnp.int32, (nc, span), 1)
    slope = jnp.zeros((nc, span), F32)
    for h in range(n_heads):
        slope = jnp.where(rowc % n_heads == h, 2.0 ** (-8.0 / n_heads * (h + 1)), slope)

    mx = jnp.full((nc, span), FINITE_MIN, F32)
    for pg in range(n_pages // grp):
        kt = jnp.concatenate([kbuf[slot, grp * pg + i] for i in range(grp)], axis=1).astype(BF16)
        dist = (past - (pg * span + key_i)).astype(F32)
        s = jnp.dot(q_sel_bf, kt, preferred_element_type=F32) - slope * dist
        sc_scr[pg] = s
        mx = jnp.maximum(mx, s)
    kn = kn_row.astype(BF16).astype(F32)
    s_self = jnp.broadcast_to(jnp.sum(q_sel * kn, axis=-1, keepdims=True), (nc, span))
    m = jnp.maximum(jnp.broadcast_to(jnp.max(mx, axis=-1, keepdims=True), (nc, span)), s_self)

    lp = jnp.zeros((nc, span), F32)
    for pg in range(n_pages // grp):
        e = jnp.exp(sc_scr[pg] - m)
        sc_scr[pg] = e
        lp = lp + e
    e_self = jnp.exp(s_self - m)
    l = jnp.broadcast_to(jnp.sum(lp, axis=-1, keepdims=True), (nc, span)) + e_self
    r = jnp.where(rowc < n_heads, 1.0, lam) / l

    def combine(pn):
        return (pn - pltpu.roll(pn, n_heads, axis=0)).astype(BF16)

    def v_rows(p):
        return jnp.concatenate([vbuf[slot, p, pl.ds(h, page, stride=n_heads), :] for h in range(n_heads)],
                               axis=1)

    acc = jnp.zeros((nc, w), F32)
    for pg in range(n_pages // grp):
        v = jnp.concatenate([v_rows(grp * pg + i) for i in range(grp)], axis=0).astype(BF16)
        acc = acc + jnp.dot(combine(sc_scr[pg] * r), v, preferred_element_type=F32)
    a_self = combine(e_self * r).astype(F32)
    vn = vn_row.astype(BF16).astype(F32)
    return jnp.concatenate(
        [acc[h:h + 1, h * HEAD_W:(h + 1) * HEAD_W]
         + a_self[h:h + 1, 0:HEAD_W] * vn[:, h * HEAD_W:(h + 1) * HEAD_W] for h in range(n_heads)], axis=-1)


def _dec_ffn_body(a_ref, do_ref, x_ref, wo_ref, gmix_ref, subg_ref, gpre_ref, wg_ref, wu_ref, cw_ref,
                  cb_ref, sc_ref, wout_ref, gpost_ref, y_ref, g_ref, xmid, hbuf, acc, *, lam_i):
    j = pl.program_id(0)
    ret_w = a_ref.shape[1]

    @pl.when(j == 0)
    def _():
        mo = jnp.dot(a_ref[...], wo_ref[0:ret_w, :], preferred_element_type=F32)
        for h in range(do_ref.shape[1] // HEAD_W):
            cols = slice(h * HEAD_W, (h + 1) * HEAD_W)
            bh = (_rms(do_ref[:, cols], subg_ref[...]) * (1.0 - lam_i)).astype(BF16)
            mo += jnp.dot(bh, wo_ref[ret_w + h * HEAD_W:ret_w + (h + 1) * HEAD_W, :],
                          preferred_element_type=F32)
        xm = x_ref[...] + _rms(mo, gmix_ref[...])
        xmid[...] = xm
        hbuf[...] = _rms(xm, gpre_ref[...]).astype(BF16)
        acc[...] = jnp.zeros_like(acc)

    h = hbuf[...]
    g = jnp.dot(h, wg_ref[...], preferred_element_type=F32)
    u = jnp.dot(h, wu_ref[...], preferred_element_type=F32)
    g_ref[...] = g
    c = cw_ref[0:1, :] * sc_ref[0] + cw_ref[1:2, :] * sc_ref[1] + cw_ref[2:3, :] * g + cb_ref[...]
    acc[...] += jnp.dot((jax.nn.gelu(c) * u).astype(BF16), wout_ref[...], preferred_element_type=F32)

    @pl.when(j == pl.num_programs(0) - 1)
    def _():
        y_ref[...] = xmid[...] + _rms(acc[...], gpost_ref[...])


def _dec_ffn(a, do, x2, wo_bf, gmix, subg, gpre, win_bf, cw, cb, conv_state_t, wout_bf, gpost, lam_i):
    n, d = x2.shape
    d_ff = wout_bf.shape[0]
    ck = d_ff // DEC_FFN_STEPS
    assert ck * DEC_FFN_STEPS == d_ff and ck % HEAD_W == 0
    nj = DEC_FFN_STEPS
    full = lambda shape: pl.BlockSpec(shape, lambda j: (0,) * len(shape))
    return pl.pallas_call(
        functools.partial(_dec_ffn_body, lam_i=lam_i),
        grid=(nj,),
        in_specs=[full(a.shape), full(do.shape), full(x2.shape), full(wo_bf.shape), full((1, d)),
                  full((1, HEAD_W)), full((1, d)),
                  pl.BlockSpec((d, ck), lambda j: (0, j)), pl.BlockSpec((d, ck), lambda j: (0, nj + j)),
                  pl.BlockSpec((3, ck), lambda j: (0, j)), pl.BlockSpec((1, ck), lambda j: (0, j)),
                  pl.BlockSpec((2, n, ck), lambda j: (0, 0, j)),
                  pl.BlockSpec((ck, d), lambda j: (j, 0)), full((1, d))],
        out_specs=[full((n, d)), pl.BlockSpec((n, ck), lambda j: (0, j))],
        out_shape=[jax.ShapeDtypeStruct((n, d), F32), jax.ShapeDtypeStruct((n, d_ff), F32)],
        scratch_shapes=[pltpu.VMEM((n, d), F32), pltpu.VMEM((n, d), BF16), pltpu.VMEM((n, d), F32)],
        compiler_params=_params("arbitrary"),
        name="dec_ffn",
    )(a, do, x2, wo_bf, gmix, subg, gpre, win_bf, win_bf, cw, cb, conv_state_t, wout_bf, gpost)


def kernel(x_prompt, x_sample, state_ret, cache_k, cache_v, state_conv, page_table,
           norm_mix_pre, norm_mix_post, w_in, w_o, lambda_q1, lambda_k1, lambda_q2, lambda_k2,
           subln_g, norm_ffn_pre, norm_ffn_post, w_ffn_in, conv_w, conv_b, w_ffn_out):
    batch, seq, d = x_prompt.shape
    n_dec = x_sample.shape[0]
    assert x_sample.shape[1] == 1, "the sample group is one token per row"
    depth = w_in.shape[0]
    n_heads = state_ret.shape[2]
    dk_ret = state_ret.shape[3]
    dh_diff = cache_k.shape[-1]
    d_ff = w_ffn_out.shape[1]
    w = n_heads * HEAD_W
    assert w_in.shape[2] == 7 * w and cache_v.shape[-1] == HEAD_W and 2 * dh_diff == HEAD_W
    assert seq % MIX_TILE == 0 and seq % FFN_TILE == 0 and d_ff % FFN_CHUNK == 0
    assert n_dec % DEC_GROUP == 0 and n_dec % 128 == 0 and seq % IN_TILE == 0 and IN_TILE % MIX_TILE == 0

    xp = x_prompt.reshape(batch * seq, d)
    xs = x_sample.reshape(n_dec, d)
    outs = [[] for _ in range(8)]
    for l in range(depth):
        lam_i = 0.8 - 0.6 * math.exp(-0.3 * l)
        lamv = jnp.stack([lambda_q1[l], lambda_k1[l], lambda_q2[l], lambda_k2[l]]).astype(F32)
        row = lambda v: v.reshape(1, -1)
        w_in_bf = w_in[l].astype(BF16)
        w_o_bf = w_o[l].astype(BF16)
        w_ffn_in_bf = w_ffn_in[l].astype(BF16)
        w_ffn_out_bf = w_ffn_out[l].astype(BF16)
        g_pre, g_post = row(norm_mix_pre[l]), row(norm_mix_post[l])
        f_pre, f_post = row(norm_ffn_pre[l]), row(norm_ffn_post[l])
        subg = row(subln_g[l])
        cw, cb = conv_w[l], row(conv_b[l])

        cache_kt = jnp.transpose(cache_k[l], (0, 2, 3, 4, 1)).reshape(cache_k.shape[1], w, cache_k.shape[2])
        cache_vr = cache_v[l].reshape(cache_v.shape[1], cache_v.shape[2] * n_heads, HEAD_W)
        proj = functools.partial(_inproj, n_heads=n_heads, dk_ret=dk_ret, dh_diff=dh_diff)

        rq, rk, rv, rg, dq, kt, v4, ktb, dvb = proj(xp, g_pre, w_in_bf, batch, seq, IN_TILE,
                                                     key_block=MIX_TILE)
        xp, s_fin = _mixer(lamv, rq, rk, rv, rg, dq, ktb, dvb, xp, w_o_bf, g_post, subg,
                           batch, seq, n_heads, lam_i)
        outs[0].append(s_fin)
        outs[2].append(jnp.transpose(kt.reshape(batch, n_heads, 2, dh_diff, seq), (0, 4, 1, 2, 3)))
        outs[3].append(v4.reshape(batch, seq, n_heads, HEAD_W))
        rq, rk, rv, rg, dq, kt, v4, dk, dv = proj(xs, g_pre, w_in_bf, 1, n_dec, n_dec, key_block=None)
        a, s_new = _ret_decode(rq, rk, rv, rg, state_ret[l], n_heads)

        xp, cs, do = _ffn_paged(xp, f_pre, w_ffn_in_bf, cw, cb, w_ffn_out_bf, f_post, batch, seq,
                                page_table, lamv, dq, dk, dv, cache_kt, cache_vr, n_heads, lam_i)
        outs[6].append(cs[:, 6:8, :])

        conv_t = jnp.swapaxes(state_conv[l], 0, 1)
        xs, g_new = _dec_ffn(a, do, xs, w_o_bf, g_post, subg, f_pre, w_ffn_in_bf, cw, cb, conv_t,
                             w_ffn_out_bf, f_post, lam_i)
        outs[1].append(s_new)
        outs[4].append(jnp.transpose(kt.reshape(n_heads, 2, dh_diff, n_dec), (3, 0, 1, 2))[:, None])
        outs[5].append(v4.reshape(n_dec, 1, n_heads, HEAD_W))
        outs[7].append(jnp.stack([state_conv[l][:, 1, :], g_new], axis=1))

    st = [jnp.stack(o) for o in outs]
    return (xp.reshape(batch, seq, d), xs.reshape(n_dec, 1, d), st[0], st[1], st[2], st[3], st[4], st[5],
            st[6], st[7])
```

```python
import functools
import math

import jax
import jax.numpy as jnp
from jax import lax
from jax.experimental import pallas as pl
from jax.experimental.pallas import tpu as pltpu

F32 = jnp.float32
BF16 = jnp.bfloat16
EPS = 1e-6
NEG_INF = -1e30
FINITE_MIN = -3e38
LOG2_E = math.log2(math.e)

V7X_VMEM_BYTES = 64 * 1024 * 1024
VMEM_LIMIT_BYTES = V7X_VMEM_BYTES * 7 // 8

RET_CHUNK = 256
HEAD_W = 128
IN_TILE = 512
MIX_TILE = 256
FFN_TILE = 256
FFN_CHUNK = 256
DEC_GROUP = 16
DEC_FFN_STEPS = 2
PAGE_GROUP = 2
NT_DIMS = (((1,), (1,)), ((), ()))
TN_DIMS = (((0,), (0,)), ((), ()))


def _rms(x, g):
    return x * lax.rsqrt(jnp.mean(x * x, axis=-1, keepdims=True) + EPS) * g


def _params(*sem):
    return pltpu.CompilerParams(dimension_semantics=sem, vmem_limit_bytes=VMEM_LIMIT_BYTES)


def _const_spec(shape):
    n = len(shape)
    return pl.BlockSpec(shape, lambda *_: (0,) * n, pipeline_mode=pl.Buffered(1))


def _ret_log_decay(h):
    return math.log(1.0 - 2.0 ** (-5.0 - h))


def _lambda(lam_ref, lam_i):
    a = jnp.sum(lam_ref[0:1, :] * lam_ref[1:2, :], axis=-1, keepdims=True)
    b = jnp.sum(lam_ref[2:3, :] * lam_ref[3:4, :], axis=-1, keepdims=True)
    return jnp.exp(a) - jnp.exp(b) + lam_i


def _inproj_body(x_ref, g_ref, w_ref, rq_ref, rk_ref, rv_ref, rg_ref, dq_ref, kt_ref, v4_ref,
                 kx_ref, vx_ref, *, width, n_heads, k_scale, q_scale, key_block):
    h = _rms(x_ref[...], g_ref[...]).astype(BF16)

    def col(j):
        return jnp.dot(h, w_ref[:, j * width:(j + 1) * width], preferred_element_type=F32)

    dk = col(5)
    kt = dk.T
    kt_ref[0] = kt
    if key_block is None:
        kx_ref[...] = dk
    else:
        for c in range(kx_ref.shape[0]):
            kx_ref[c] = kt[:, c * key_block:(c + 1) * key_block].astype(BF16)
    dv = col(6)
    for hh in range(n_heads):
        v4_ref[:, hh, :] = dv[:, hh * HEAD_W:(hh + 1) * HEAD_W]
    vx_ref[...] = dv if key_block is None else dv.astype(BF16)
    rq_ref[...] = col(0)
    rk_ref[...] = col(1) * k_scale
    rv_ref[...] = col(2).astype(BF16)
    rg_ref[...] = col(3)
    dq_ref[...] = (col(4) * q_scale).astype(BF16)


def _inproj(x2, g, w_bf, batch, seq, tile, n_heads, dk_ret, dh_diff, key_block):
    m, d = x2.shape
    width = n_heads * HEAD_W
    nt = seq // tile
    row = lambda i: (i, 0)
    f32_out = jax.ShapeDtypeStruct((m, width), F32)
    bf_out = jax.ShapeDtypeStruct((m, width), BF16)
    spec = pl.BlockSpec((tile, width), row)
    if key_block is None:
        kx_shape, kx_spec, vx_shape = f32_out, spec, f32_out
    else:
        kx_shape = jax.ShapeDtypeStruct((m // key_block, width, key_block), BF16)
        kx_spec = pl.BlockSpec((tile // key_block, width, key_block), lambda i: (i, 0, 0))
        vx_shape = bf_out
    body = functools.partial(_inproj_body, width=width, n_heads=n_heads, k_scale=dk_ret ** -0.5,
                             q_scale=dh_diff ** -0.5, key_block=key_block)
    return pl.pallas_call(
        body,
        grid=(m // tile,),
        in_specs=[pl.BlockSpec((tile, d), row), _const_spec((1, d)), _const_spec(w_bf.shape)],
        out_specs=[spec] * 5 + [pl.BlockSpec((1, width, tile), lambda i: (i // nt, 0, i % nt)),
                                pl.BlockSpec((tile, n_heads, HEAD_W), lambda i: (i, 0, 0)), kx_spec, spec],
        out_shape=[f32_out, f32_out, bf_out, f32_out, bf_out,
                   jax.ShapeDtypeStruct((batch, width, seq), F32),
                   jax.ShapeDtypeStruct((m, n_heads, HEAD_W), F32), kx_shape, vx_shape],
        compiler_params=_params("arbitrary"),
        name="inproj",
    )(x2, g, w_bf)


def _retention_decay(length, h):
    i = lax.broadcasted_iota(jnp.int32, (length, length), 0)
    j = lax.broadcasted_iota(jnp.int32, (length, length), 1)
    diff = (i - j).astype(F32)
    causal = diff >= 0
    return jnp.where(causal, jnp.exp(jnp.where(causal, diff, 0.0) * _ret_log_decay(h)), 0.0)


def _retention_row_decays(length, h):
    lg = _ret_log_decay(h)
    ri = lax.broadcasted_iota(jnp.int32, (length, 1), 0).astype(F32)
    return jnp.exp((ri + 1.0) * lg), jnp.exp((length - 1.0 - ri) * lg), math.exp(length * lg)


def _pair_loop(n, body):
    pairs = lax.shift_right_logical(n, 1)

    def two(i, carry):
        body(2 * i)
        body(2 * i + 1)
        return carry

    def one(i, carry):
        body(i)
        return carry

    lax.fori_loop(0, pairs, two, 0)
    lax.fori_loop(2 * pairs, n, one, 0)


def _gated_group_norm(o, g):
    on = o * lax.rsqrt(jnp.mean(o * o, axis=-1, keepdims=True) + EPS)
    return on * (g * jax.nn.sigmoid(g))


def _mixer_body(lam_ref, rq_ref, rk_ref, rv_ref, rg_ref, dq_ref, kt_ref, vb_ref, x_ref, wo_ref,
                gpost_ref, subg_ref, y_ref, s_ref, mix_scr, qq_scr, sc_scr, mx_scr, l_scr, acc_scr, decay_scr,
                *, tq, ret_chunk, n_heads, lam_i):
    t = pl.program_id(1)
    ret_w = n_heads * HEAD_W

    @pl.when(t == 0)
    def _():
        s_ref[...] = jnp.zeros_like(s_ref)

    @pl.when((pl.program_id(0) == 0) & (t == 0))
    def _():
        for h in range(n_heads):
            decay_scr[h] = _retention_decay(ret_chunk, h)

    for c in range(tq // ret_chunk):
        rows = slice(c * ret_chunk, (c + 1) * ret_chunk)
        for h in range(n_heads):
            cols = slice(h * HEAD_W, (h + 1) * HEAD_W)
            decay = decay_scr[h]
            qdec, kdec, gl = _retention_row_decays(ret_chunk, h)
            q = rq_ref[rows, cols]
            k = rk_ref[rows, cols]
            v = rv_ref[rows, cols]
            s = s_ref[0, h]
            att = lax.dot_general(q.astype(BF16), k.astype(BF16), NT_DIMS,
                                  preferred_element_type=F32) * decay
            o = (jnp.dot(att.astype(BF16), v, preferred_element_type=F32)
                 + jnp.dot((q * qdec).astype(BF16), s.astype(BF16), preferred_element_type=F32))
            s_ref[0, h] = gl * s + lax.dot_general((k * kdec).astype(BF16), v, TN_DIMS,
                                                   preferred_element_type=F32)
            mix_scr[rows, cols] = _gated_group_norm(o, rg_ref[rows, cols]).astype(BF16)

    lane = lax.broadcasted_iota(jnp.int32, (tq, HEAD_W), 1)
    for h in range(n_heads):
        qh = dq_ref[:, h * HEAD_W:(h + 1) * HEAD_W]
        zero = jnp.zeros_like(qh)
        qq_scr[h, 0:tq] = jnp.where(lane < HEAD_W // 2, qh, zero)
        qq_scr[h, tq:2 * tq] = jnp.where(lane >= HEAD_W // 2, qh, zero)
    mx_scr[...] = jnp.full(mx_scr.shape, FINITE_MIN, F32)
    l_scr[...] = jnp.zeros_like(l_scr)
    acc_scr[...] = jnp.zeros_like(acc_scr)
    key_j = lax.broadcasted_iota(jnp.int32, (1, tq), 1).astype(F32)
    lane_tiles = [slice(c * HEAD_W, (c + 1) * HEAD_W) for c in range(tq // HEAD_W)]

    def fold(x, op):
        return functools.reduce(op, [x[:, c] for c in lane_tiles])

    def scores(kb, causal):
        key_pos = key_j + ((kb - t) * tq).astype(F32)
        for h in range(n_heads):
            slope = 2.0 ** (-8.0 / n_heads * (h + 1))
            s = jnp.dot(qq_scr[h], kt_ref[kb, h * HEAD_W:(h + 1) * HEAD_W, :], preferred_element_type=F32)
            s = (s + slope * key_pos) * LOG2_E
            if causal is not None:
                s = jnp.where(causal, s, NEG_INF)
            sc_scr[kb, h] = s
            mx_scr[h] = jnp.maximum(mx_scr[h], fold(s, jnp.maximum))

    _pair_loop(t, lambda kb: scores(kb, None))
    ii = lax.broadcasted_iota(jnp.int32, (2 * tq, tq), 0)
    jj = lax.broadcasted_iota(jnp.int32, (2 * tq, tq), 1)
    scores(t, jnp.where(ii >= tq, ii - tq, ii) >= jj)

    for h in range(n_heads):
        mx_scr[h] = jnp.broadcast_to(jnp.max(mx_scr[h], axis=-1, keepdims=True), (2 * tq, HEAD_W))

    def weighted(kb):
        for h in range(n_heads):
            m = mx_scr[h]
            e = [jnp.exp2(sc_scr[kb, h, :, c] - m) for c in lane_tiles]
            l_scr[h] += functools.reduce(jnp.add, e)
            vr = vb_ref[pl.ds(pl.multiple_of(kb * tq, tq), tq), h * HEAD_W:(h + 1) * HEAD_W]
            acc_scr[h] += jnp.dot(jnp.concatenate([x.astype(BF16) for x in e], axis=1), vr,
                                  preferred_element_type=F32)

    _pair_loop(t + 1, weighted)
    lam = _lambda(lam_ref, lam_i)
    subg = subg_ref[...]
    for h in range(n_heads):
        on = acc_scr[h] / jnp.sum(l_scr[h], axis=-1, keepdims=True)
        o = _rms(on[:tq] - lam * on[tq:], subg) * (1.0 - lam_i)
        mix_scr[:, ret_w + h * HEAD_W:ret_w + (h + 1) * HEAD_W] = o.astype(BF16)

    mo = jnp.dot(mix_scr[...], wo_ref[...], preferred_element_type=F32)
    y_ref[...] = x_ref[...] + _rms(mo, gpost_ref[...])


def _mixer(lamv, rq, rk, rv, rg, dq, ktb, dvb, x2, wo_bf, gpost, subg, batch, seq, n_heads, lam_i):
    m, d = x2.shape
    w = rq.shape[1]
    tq = MIX_TILE
    nt = seq // tq
    tile = lambda b, t: (b * nt + t, 0)
    whole = lambda b, t: (b, 0)
    body = functools.partial(_mixer_body, tq=tq, ret_chunk=RET_CHUNK, n_heads=n_heads, lam_i=lam_i)
    return pl.pallas_call(
        body,
        grid=(batch, nt),
        in_specs=[
            _const_spec(lamv.shape),
            pl.BlockSpec((tq, w), tile), pl.BlockSpec((tq, w), tile), pl.BlockSpec((tq, w), tile),
            pl.BlockSpec((tq, w), tile), pl.BlockSpec((tq, w), tile),
            pl.BlockSpec((nt, w, tq), lambda b, t: (b, 0, 0)), pl.BlockSpec((seq, w), whole),
            pl.BlockSpec((tq, d), tile),
            _const_spec(wo_bf.shape), _const_spec((1, d)), _const_spec((1, HEAD_W)),
        ],
        out_specs=[pl.BlockSpec((tq, d), tile),
                   pl.BlockSpec((1, n_heads, HEAD_W, HEAD_W), lambda b, t: (b, 0, 0, 0))],
        out_shape=[jax.ShapeDtypeStruct((m, d), F32),
                   jax.ShapeDtypeStruct((batch, n_heads, HEAD_W, HEAD_W), F32)],
        scratch_shapes=[
            pltpu.VMEM((tq, 2 * w), BF16),
            pltpu.VMEM((n_heads, 2 * tq, HEAD_W), BF16),
            pltpu.VMEM((nt, n_heads, 2 * tq, tq), F32),
            pltpu.VMEM((n_heads, 2 * tq, HEAD_W), F32),
            pltpu.VMEM((n_heads, 2 * tq, HEAD_W), F32),
            pltpu.VMEM((n_heads, 2 * tq, HEAD_W), F32),
            pltpu.VMEM((n_heads, RET_CHUNK, RET_CHUNK), F32),
        ],
        compiler_params=_params("arbitrary", "arbitrary"),
        name="mixer",
    )(lamv, rq, rk, rv, rg, dq, ktb, dvb, x2, wo_bf, gpost, subg)


def _conv_gate(g, u, g1, g2, cw_ref, cb_ref, cols):
    c = cw_ref[0:1, cols] * g2 + cw_ref[1:2, cols] * g1 + cw_ref[2:3, cols] * g + cb_ref[:, cols]
    return (jax.nn.gelu(c) * u).astype(BF16)


def _ffn_body(pt_ref, x_ref, gpre_ref, win_ref, cw_ref, cb_ref, wout_ref, gpost_ref,
              lam_ref, q_ref, kn_ref, vn_ref, ck_hbm, cv_hbm, y_ref, cs_ref, o_ref,
              gbuf, carry, act, kbuf, vbuf, sem, sc_scr, *, tf, d_ff, ck, rows, n_pages, page, n_heads, lam_i):
    t = pl.program_id(1)
    step = pl.program_id(0) * pl.num_programs(1) + t
    n_rows = rows * pl.num_programs(0) * pl.num_programs(1)
    copies = functools.partial(_page_copies, pt_ref, ck_hbm, cv_hbm, kbuf, vbuf, sem, n_rows=q_ref.shape[0],
                               n_pages=n_pages)

    @pl.when(step == 0)
    def _():
        for slot in range(2):
            for c in copies(b=slot, slot=slot):
                c.start()

    @pl.when(t == 0)
    def _():
        carry[...] = jnp.zeros_like(carry)

    lam = _lambda(lam_ref, lam_i)
    for i in range(rows):
        g = step * rows + i
        slot = i % 2
        for c in copies(b=g, slot=slot):
            c.wait()
        this = pl.ds(g, 1)
        o_ref[this, :] = _paged_row(slot, q_ref[this, :], kn_ref[this, :], vn_ref[this, :], lam, kbuf, vbuf,
                                    sc_scr, n_pages=n_pages, page=page, n_heads=n_heads)

        @pl.when(g + 2 < n_rows)
        def _():
            for c in copies(b=g + 2, slot=slot):
                c.start()

    x = x_ref[...]
    h = _rms(x, gpre_ref[...]).astype(BF16)
    for j in range(d_ff // ck):
        cols = slice(j * ck, (j + 1) * ck)
        g = jnp.dot(h, win_ref[:, cols], preferred_element_type=F32)
        u = jnp.dot(h, win_ref[:, d_ff + j * ck:d_ff + (j + 1) * ck], preferred_element_type=F32)
        gbuf[0:8, :] = carry[:, cols]
        gbuf[8:8 + tf, :] = g
        act[:, cols] = _conv_gate(g, u, gbuf[7:7 + tf, :], gbuf[6:6 + tf, :], cw_ref, cb_ref, cols)
        carry[:, cols] = gbuf[tf:tf + 8, :]
    f = jnp.dot(act[...], wout_ref[...], preferred_element_type=F32)
    y_ref[...] = x + _rms(f, gpost_ref[...])
    cs_ref[0] = carry[...]


def _ffn_paged(x2, gpre, win_bf, cw, cb, wout_bf, gpost, batch, seq,
               page_table, lamv, dq, dk, dv, cache_kt, cache_v, n_heads, lam_i):
    m, d = x2.shape
    d_ff = wout_bf.shape[0]
    tf = FFN_TILE
    nt = seq // tf
    n, w = dq.shape
    rows = n // (batch * nt)
    assert rows * batch * nt == n and rows % 2 == 0, "sample rows must split evenly, in pairs, over the FFN steps"
    n_pages = page_table.shape[1]
    page = cache_kt.shape[2]
    assert n_pages % PAGE_GROUP == 0
    tile = lambda b, t, pt: (b * nt + t, 0)
    const = _const_spec
    rspec = const((n, w))
    body = functools.partial(_ffn_body, tf=tf, d_ff=d_ff, ck=FFN_CHUNK, rows=rows, n_pages=n_pages, page=page,
                             n_heads=n_heads, lam_i=lam_i)
    grid_spec = pltpu.PrefetchScalarGridSpec(
        num_scalar_prefetch=1,
        grid=(batch, nt),
        in_specs=[pl.BlockSpec((tf, d), tile), const((1, d)), const(win_bf.shape), const(cw.shape),
                  const(cb.shape), const(wout_bf.shape), const((1, d)),
                  const(lamv.shape), rspec, rspec, rspec,
                  pl.BlockSpec(memory_space=pl.ANY), pl.BlockSpec(memory_space=pl.ANY)],
        out_specs=[pl.BlockSpec((tf, d), tile), pl.BlockSpec((1, 8, d_ff), lambda b, t, pt: (b, 0, 0)),
                   pl.BlockSpec((n, w), lambda b, t, pt: (0, 0))],
        scratch_shapes=[pltpu.VMEM((tf + 8, FFN_CHUNK), F32), pltpu.VMEM((8, d_ff), F32),
                        pltpu.VMEM((tf, d_ff), BF16),
                        pltpu.VMEM((2, n_pages, w, page), F32),
                        pltpu.VMEM((2, n_pages, page * n_heads, HEAD_W), F32),
                        pltpu.SemaphoreType.DMA((2, 2)),
                        pltpu.VMEM((n_pages // PAGE_GROUP, 2 * n_heads, PAGE_GROUP * page), F32)],
    )
    y, cs, o = pl.pallas_call(
        body,
        grid_spec=grid_spec,
        out_shape=[jax.ShapeDtypeStruct((m, d), F32), jax.ShapeDtypeStruct((batch, 8, d_ff), F32),
                   jax.ShapeDtypeStruct((n, w), F32)],
        compiler_params=_params("arbitrary", "arbitrary"),
        name="conv_ffn_paged_attn",
    )(page_table.T.reshape(-1), x2, gpre, win_bf, cw, cb, wout_bf, gpost,
      lamv, dq.astype(F32), dk, dv, cache_kt, cache_v)
    return y, cs, o


def _ret_dec_body(rq_ref, rk_ref, rv_ref, rg_ref, s_ref, a_ref, sn_ref, *, grp, n_heads):
    row = lax.broadcasted_iota(jnp.int32, (grp, grp * HEAD_W), 0)
    lane = lax.broadcasted_iota(jnp.int32, (grp, grp * HEAD_W), 1)
    own = (lane // HEAD_W) == row

    def block_diag(x):
        return jnp.where(own, jnp.tile(x, (1, grp)), 0.0).astype(BF16)

    for h in range(n_heads):
        cols = slice(h * HEAD_W, (h + 1) * HEAD_W)
        gamma = math.exp(_ret_log_decay(h))
        q = rq_ref[:, cols]
        k = rk_ref[:, cols]
        v = rv_ref[:, cols]
        qb = q.astype(BF16).astype(F32)
        kb = k.astype(BF16).astype(F32)
        att = jnp.sum(qb * kb, axis=-1, keepdims=True).astype(BF16).astype(F32)
        s = s_ref[:, h].reshape(grp * HEAD_W, HEAD_W)
        o = att * v.astype(F32) + jnp.dot(block_diag(q * gamma), s.astype(BF16),
                                          preferred_element_type=F32)
        upd = lax.dot_general(block_diag(k), v, TN_DIMS, preferred_element_type=F32)
        sn_ref[:, h] = (gamma * s + upd).reshape(grp, HEAD_W, HEAD_W)
        a_ref[:, cols] = _gated_group_norm(o, rg_ref[:, cols]).astype(BF16)


def _ret_decode(rq, rk, rv, rg, state, n_heads):
    n, w = rq.shape
    grp = DEC_GROUP
    row = lambda i: (i, 0)
    st = lambda i: (i, 0, 0, 0)
    spec = pl.BlockSpec((grp, w), row)
    st_spec = pl.BlockSpec((grp, n_heads, HEAD_W, HEAD_W), st)
    return pl.pallas_call(
        functools.partial(_ret_dec_body, grp=grp, n_heads=n_heads),
        grid=(n // grp,),
        in_specs=[spec, spec, spec, spec, st_spec],
        out_specs=[spec, st_spec],
        out_shape=[jax.ShapeDtypeStruct((n, w), BF16), jax.ShapeDtypeStruct(state.shape, F32)],
        compiler_params=_params("arbitrary"),
        name="ret_decode",
    )(rq, rk, rv, rg, state)


def _page_copies(pt_ref, ck_hbm, cv_hbm, kbuf, vbuf, sem, b, slot, n_rows, n_pages):
    out = []
    for p in range(n_pages):
        page = pt_ref[p * n_rows + b]
        out.append(pltpu.make_async_copy(ck_hbm.at[page], kbuf.at[slot, p], sem.at[0, slot]))
        out.append(pltpu.make_async_copy(cv_hbm.at[page], vbuf.at[slot, p], sem.at[1, slot]))
    return out


def _paged_row(slot, q, kn_row, vn_row, lam, kbuf, vbuf, sc_scr, *, n_pages, page, n_heads):
    w = n_heads * HEAD_W
    nc = 2 * n_heads
    past = n_pages * page
    ci = lax.broadcasted_iota(jnp.int32, (nc, w), 0)
    ri = lax.broadcasted_iota(jnp.int32, (nc, w), 1)
    col_of = ri // HEAD_W + n_heads * ((ri % HEAD_W) // (HEAD_W // 2))
    q_sel = jnp.where(ci == col_of, jnp.broadcast_to(q, (nc, w)), 0.0)
    q_sel_bf = q_sel.astype(BF16)
    grp = PAGE_GROUP
    span = grp * page
    rowc = lax.broadcasted_iota(jnp.int32, (nc, span), 0)
    key_i = lax.broadcasted_iota(jnp.int32, (nc, span), 1)
    slope = jnp.zeros((nc, span), F32)
    for h in range(n_heads):
        slope = jnp.where(rowc % n_heads == h, 2.0 ** (-8.0 / n_heads * (h + 1)), slope)

    mx = jnp.full((nc, span), FINITE_MIN, F32)
    for pg in range(n_pages // grp):
        kt = jnp.concatenate([kbuf[slot, grp * pg + i] for i in range(grp)], axis=1).astype(BF16)
        dist = (past - (pg * span + key_i)).astype(F32)
        s = jnp.dot(q_sel_bf, kt, preferred_element_type=F32) - slope * dist
        sc_scr[pg] = s
        mx = jnp.maximum(mx, s)
    kn = kn_row.astype(BF16).astype(F32)
    s_self = jnp.broadcast_to(jnp.sum(q_sel * kn, axis=-1, keepdims=True), (nc, span))
    m = jnp.maximum(jnp.broadcast_to(jnp.max(mx, axis=-1, keepdims=True), (nc, span)), s_self)

    lp = jnp.zeros((nc, span), F32)
    for pg in range(n_pages // grp):
        e = jnp.exp(sc_scr[pg] - m)
        sc_scr[pg] = e
        lp = lp + e
    e_self = jnp.exp(s_self - m)
    l = jnp.broadcast_to(jnp.sum(lp, axis=-1, keepdims=True), (nc, span)) + e_self
    r = jnp.where(rowc < n_heads, 1.0, lam) / l

    def combine(pn):
        return (pn - pltpu.roll(pn, n_heads, axis=0)).astype(BF16)

    def v_rows(p):
        return jnp.concatenate([vbuf[slot, p, pl.ds(h, page, stride=n_heads), :] for h in range(n_heads)],
                               axis=1)

    acc = jnp.zeros((nc, w), F32)
    for pg in range(n_pages // grp):
        v = jnp.concatenate([v_rows(grp * pg + i) for i in range(grp)], axis=0).astype(BF16)
        acc = acc + jnp.dot(combine(sc_scr[pg] * r), v, preferred_element_type=F32)
    a_self = combine(e_self * r).astype(F32)
    vn = vn_row.astype(BF16).astype(F32)
    return jnp.concatenate(
        [acc[h:h + 1, h * HEAD_W:(h + 1) * HEAD_W]
         + a_self[h:h + 1, 0:HEAD_W] * vn[:, h * HEAD_W:(h + 1) * HEAD_W] for h in range(n_heads)], axis=-1)


def _dec_ffn_body(a_ref, do_ref, x_ref, wo_ref, gmix_ref, subg_ref, gpre_ref, wg_ref, wu_ref, cw_ref,
                  cb_ref, sc_ref, wout_ref, gpost_ref, y_ref, g_ref, xmid, hbuf, acc, *, lam_i):
    j = pl.program_id(0)
    ret_w = a_ref.shape[1]

    @pl.when(j == 0)
    def _():
        mo = jnp.dot(a_ref[...], wo_ref[0:ret_w, :], preferred_element_type=F32)
        for h in range(do_ref.shape[1] // HEAD_W):
            cols = slice(h * HEAD_W, (h + 1) * HEAD_W)
            bh = (_rms(do_ref[:, cols], subg_ref[...]) * (1.0 - lam_i)).astype(BF16)
            mo += jnp.dot(bh, wo_ref[ret_w + h * HEAD_W:ret_w + (h + 1) * HEAD_W, :],
                          preferred_element_type=F32)
        xm = x_ref[...] + _rms(mo, gmix_ref[...])
        xmid[...] = xm
        hbuf[...] = _rms(xm, gpre_ref[...]).astype(BF16)
        acc[...] = jnp.zeros_like(acc)

    h = hbuf[...]
    g = jnp.dot(h, wg_ref[...], preferred_element_type=F32)
    u = jnp.dot(h, wu_ref[...], preferred_element_type=F32)
    g_ref[...] = g
    c = cw_ref[0:1, :] * sc_ref[0] + cw_ref[1:2, :] * sc_ref[1] + cw_ref[2:3, :] * g + cb_ref[...]
    acc[...] += jnp.dot((jax.nn.gelu(c) * u).astype(BF16), wout_ref[...], preferred_element_type=F32)

    @pl.when(j == pl.num_programs(0) - 1)
    def _():
        y_ref[...] = xmid[...] + _rms(acc[...], gpost_ref[...])


def _dec_ffn(a, do, x2, wo_bf, gmix, subg, gpre, win_bf, cw, cb, conv_state_t, wout_bf, gpost, lam_i):
    n, d = x2.shape
    d_ff = wout_bf.shape[0]
    ck = d_ff // DEC_FFN_STEPS
    assert ck * DEC_FFN_STEPS == d_ff and ck % HEAD_W == 0
    nj = DEC_FFN_STEPS
    full = lambda shape: pl.BlockSpec(shape, lambda j: (0,) * len(shape))
    return pl.pallas_call(
        functools.partial(_dec_ffn_body, lam_i=lam_i),
        grid=(nj,),
        in_specs=[full(a.shape), full(do.shape), full(x2.shape), full(wo_bf.shape), full((1, d)),
                  full((1, HEAD_W)), full((1, d)),
                  pl.BlockSpec((d, ck), lambda j: (0, j)), pl.BlockSpec((d, ck), lambda j: (0, nj + j)),
                  pl.BlockSpec((3, ck), lambda j: (0, j)), pl.BlockSpec((1, ck), lambda j: (0, j)),
                  pl.BlockSpec((2, n, ck), lambda j: (0, 0, j)),
                  pl.BlockSpec((ck, d), lambda j: (j, 0)), full((1, d))],
        out_specs=[full((n, d)), pl.BlockSpec((n, ck), lambda j: (0, j))],
        out_shape=[jax.ShapeDtypeStruct((n, d), F32), jax.ShapeDtypeStruct((n, d_ff), F32)],
        scratch_shapes=[pltpu.VMEM((n, d), F32), pltpu.VMEM((n, d), BF16), pltpu.VMEM((n, d), F32)],
        compiler_params=_params("arbitrary"),
        name="dec_ffn",
    )(a, do, x2, wo_bf, gmix, subg, gpre, win_bf, win_bf, cw, cb, conv_state_t, wout_bf, gpost)


def kernel(x_prompt, x_sample, state_ret, cache_k, cache_v, state_conv, page_table,
           norm_mix_pre, norm_mix_post, w_in, w_o, lambda_q1, lambda_k1, lambda_q2, lambda_k2,
           subln_g, norm_ffn_pre, norm_ffn_post, w_ffn_in, conv_w, conv_b, w_ffn_out):
    batch, seq, d = x_prompt.shape
    n_dec = x_sample.shape[0]
    assert x_sample.shape[1] == 1, "the sample group is one token per row"
    depth = w_in.shape[0]
    n_heads = state_ret.shape[2]
    dk_ret = state_ret.shape[3]
    dh_diff = cache_k.shape[-1]
    d_ff = w_ffn_out.shape[1]
    w = n_heads * HEAD_W
    assert w_in.shape[2] == 7 * w and cache_v.shape[-1] == HEAD_W and 2 * dh_diff == HEAD_W
    assert seq % MIX_TILE == 0 and seq % FFN_TILE == 0 and d_ff % FFN_CHUNK == 0
    assert n_dec % DEC_GROUP == 0 and n_dec % 128 == 0 and seq % IN_TILE == 0 and IN_TILE % MIX_TILE == 0

    xp = x_prompt.reshape(batch * seq, d)
    xs = x_sample.reshape(n_dec, d)
    outs = [[] for _ in range(8)]
    for l in range(depth):
        lam_i = 0.8 - 0.6 * math.exp(-0.3 * l)
        lamv = jnp.stack([lambda_q1[l], lambda_k1[l], lambda_q2[l], lambda_k2[l]]).astype(F32)
        row = lambda v: v.reshape(1, -1)
        w_in_bf = w_in[l].astype(BF16)
        w_o_bf = w_o[l].astype(BF16)
        w_ffn_in_bf = w_ffn_in[l].astype(BF16)
        w_ffn_out_bf = w_ffn_out[l].astype(BF16)
        g_pre, g_post = row(norm_mix_pre[l]), row(norm_mix_post[l])
        f_pre, f_post = row(norm_ffn_pre[l]), row(norm_ffn_post[l])
        subg = row(subln_g[l])
        cw, cb = conv_w[l], row(conv_b[l])

        cache_kt = jnp.transpose(cache_k[l], (0, 2, 3, 4, 1)).reshape(cache_k.shape[1], w, cache_k.shape[2])
        cache_vr = cache_v[l].reshape(cache_v.shape[1], cache_v.shape[2] * n_heads, HEAD_W)
        proj = functools.partial(_inproj, n_heads=n_heads, dk_ret=dk_ret, dh_diff=dh_diff)

        rq, rk, rv, rg, dq, kt, v4, ktb, dvb = proj(xp, g_pre, w_in_bf, batch, seq, IN_TILE,
                                                     key_block=MIX_TILE)
        xp, s_fin = _mixer(lamv, rq, rk, rv, rg, dq, ktb, dvb, xp, w_o_bf, g_post, subg,
                           batch, seq, n_heads, lam_i)
        outs[0].append(s_fin)
        outs[2].append(jnp.transpose(kt.reshape(batch, n_heads, 2, dh_diff, seq), (0, 4, 1, 2, 3)))
        outs[3].append(v4.reshape(batch, seq, n_heads, HEAD_W))
        rq, rk, rv, rg, dq, kt, v4, dk, dv = proj(xs, g_pre, w_in_bf, 1, n_dec, n_dec, key_block=None)
        a, s_new = _ret_decode(rq, rk, rv, rg, state_ret[l], n_heads)

        xp, cs, do = _ffn_paged(xp, f_pre, w_ffn_in_bf, cw, cb, w_ffn_out_bf, f_post, batch, seq,
                                page_table, lamv, dq, dk, dv, cache_kt, cache_vr, n_heads, lam_i)
        outs[6].append(cs[:, 6:8, :])

        conv_t = jnp.swapaxes(state_conv[l], 0, 1)
        xs, g_new = _dec_ffn(a, do, xs, w_o_bf, g_post, subg, f_pre, w_ffn_in_bf, cw, cb, conv_t,
                             w_ffn_out_bf, f_post, lam_i)
        outs[1].append(s_new)
        outs[4].append(jnp.transpose(kt.reshape(n_heads, 2, dh_diff, n_dec), (3, 0, 1, 2))[:, None])
        outs[5].append(v4.reshape(n_dec, 1, n_heads, HEAD_W))
        outs[7].append(jnp.stack([state_conv[l][:, 1, :], g_new], axis=1))

    st = [jnp.stack(o) for o in outs]
    return (xp.reshape(batch, seq, d), xs.reshape(n_dec, 1, d), st[0], st[1], st[2], st[3], st[4], st[5],
            st[6], st[7])
```

```python
import functools
import math

import jax
import jax.numpy as jnp
from jax import lax
from jax.experimental import pallas as pl
from jax.experimental.pallas import tpu as pltpu

F32 = jnp.float32
BF16 = jnp.bfloat16
EPS = 1e-6
NEG_INF = -1e30
FINITE_MIN = -3e38
LOG2_E = math.log2(math.e)

V7X_VMEM_BYTES = 64 * 1024 * 1024
VMEM_LIMIT_BYTES = V7X_VMEM_BYTES * 7 // 8

RET_CHUNK = 256
HEAD_W = 128
IN_TILE = 1024
MIX_TILE = 256
FFN_TILE = 256
FFN_CHUNK = 256
DEC_GROUP = 16
DEC_FFN_STEPS = 2
PAGE_GROUP = 2
NT_DIMS = (((1,), (1,)), ((), ()))
TN_DIMS = (((0,), (0,)), ((), ()))


def _rms(x, g):
    return x * lax.rsqrt(jnp.mean(x * x, axis=-1, keepdims=True) + EPS) * g


def _params(*sem):
    return pltpu.CompilerParams(dimension_semantics=sem, vmem_limit_bytes=VMEM_LIMIT_BYTES)


def _const_spec(shape):
    n = len(shape)
    return pl.BlockSpec(shape, lambda *_: (0,) * n, pipeline_mode=pl.Buffered(1))


def _ret_log_decay(h):
    return math.log(1.0 - 2.0 ** (-5.0 - h))


def _lambda(lam_ref, lam_i):
    a = jnp.sum(lam_ref[0:1, :] * lam_ref[1:2, :], axis=-1, keepdims=True)
    b = jnp.sum(lam_ref[2:3, :] * lam_ref[3:4, :], axis=-1, keepdims=True)
    return jnp.exp(a) - jnp.exp(b) + lam_i


def _inproj_body(x_ref, g_ref, w_ref, *rest, n_riders, width, n_heads, k_scale, q_scale, key_block):
    rider_in, rest = rest[:n_riders], rest[n_riders:]
    rq_ref, rk_ref, rv_ref, rg_ref, dq_ref, kt_ref, v4_ref, kx_ref, vx_ref = rest[:9]
    for src, dst in zip(rider_in, rest[9:]):
        dst[...] = src[...].astype(BF16)
    h = _rms(x_ref[...], g_ref[...]).astype(BF16)

    def col(j):
        return jnp.dot(h, w_ref[:, j * width:(j + 1) * width], preferred_element_type=F32)

    dk = col(5)
    kt = dk.T
    kt_ref[0] = kt
    if key_block is None:
        kx_ref[...] = dk
    else:
        for c in range(kx_ref.shape[0]):
            kx_ref[c] = kt[:, c * key_block:(c + 1) * key_block].astype(BF16)
    dv = col(6)
    for hh in range(n_heads):
        v4_ref[:, hh, :] = dv[:, hh * HEAD_W:(hh + 1) * HEAD_W]
    vx_ref[...] = dv if key_block is None else dv.astype(BF16)
    rq_ref[...] = col(0)
    rk_ref[...] = col(1) * k_scale
    rv_ref[...] = col(2).astype(BF16)
    rg_ref[...] = col(3)
    dq_ref[...] = (col(4) * q_scale).astype(BF16)


def _inproj(x2, g, w_bf, batch, seq, tile, n_heads, dk_ret, dh_diff, key_block, riders=()):
    m, d = x2.shape
    width = n_heads * HEAD_W
    nt = seq // tile
    steps = m // tile
    row = lambda i: (i, 0)
    assert all(r.shape[0] % (16 * steps) == 0 for r in riders), "rider row blocks must be bf16-tile aligned"
    rider_specs = [pl.BlockSpec((r.shape[0] // steps, r.shape[1]), row) for r in riders]
    f32_out = jax.ShapeDtypeStruct((m, width), F32)
    bf_out = jax.ShapeDtypeStruct((m, width), BF16)
    spec = pl.BlockSpec((tile, width), row)
    if key_block is None:
        kx_shape, kx_spec, vx_shape = f32_out, spec, f32_out
    else:
        kx_shape = jax.ShapeDtypeStruct((m // key_block, width, key_block), BF16)
        kx_spec = pl.BlockSpec((tile // key_block, width, key_block), lambda i: (i, 0, 0))
        vx_shape = bf_out
    body = functools.partial(_inproj_body, n_riders=len(riders), width=width, n_heads=n_heads,
                             k_scale=dk_ret ** -0.5, q_scale=dh_diff ** -0.5, key_block=key_block)
    return pl.pallas_call(
        body,
        grid=(steps,),
        in_specs=[pl.BlockSpec((tile, d), row), _const_spec((1, d)), _const_spec(w_bf.shape)] + rider_specs,
        out_specs=[spec] * 5 + [pl.BlockSpec((1, width, tile), lambda i: (i // nt, 0, i % nt)),
                                pl.BlockSpec((tile, n_heads, HEAD_W), lambda i: (i, 0, 0)), kx_spec, spec]
        + rider_specs,
        out_shape=[f32_out, f32_out, bf_out, f32_out, bf_out,
                   jax.ShapeDtypeStruct((batch, width, seq), F32),
                   jax.ShapeDtypeStruct((m, n_heads, HEAD_W), F32), kx_shape, vx_shape]
        + [jax.ShapeDtypeStruct(r.shape, BF16) for r in riders],
        compiler_params=_params("arbitrary"),
        name="inproj",
    )(x2, g, w_bf, *riders)


def _retention_decay(length, h):
    i = lax.broadcasted_iota(jnp.int32, (length, length), 0)
    j = lax.broadcasted_iota(jnp.int32, (length, length), 1)
    diff = (i - j).astype(F32)
    causal = diff >= 0
    return jnp.where(causal, jnp.exp(jnp.where(causal, diff, 0.0) * _ret_log_decay(h)), 0.0)


def _retention_row_decays(length, h):
    lg = _ret_log_decay(h)
    ri = lax.broadcasted_iota(jnp.int32, (length, 1), 0).astype(F32)
    return jnp.exp((ri + 1.0) * lg), jnp.exp((length - 1.0 - ri) * lg), math.exp(length * lg)


def _pair_loop(n, body):
    pairs = lax.shift_right_logical(n, 1)

    def two(i, carry):
        body(2 * i)
        body(2 * i + 1)
        return carry

    def one(i, carry):
        body(i)
        return carry

    lax.fori_loop(0, pairs, two, 0)
    lax.fori_loop(2 * pairs, n, one, 0)


def _gated_group_norm(o, g):
    on = o * lax.rsqrt(jnp.mean(o * o, axis=-1, keepdims=True) + EPS)
    return on * (g * jax.nn.sigmoid(g))


def _mixer_body(lam_ref, rq_ref, rk_ref, rv_ref, rg_ref, dq_ref, kt_ref, vb_ref, x_ref, wo_ref,
                gpost_ref, subg_ref, y_ref, s_ref, mix_scr, qq_scr, sc_scr, mx_scr, l_scr, acc_scr, decay_scr,
                *, tq, ret_chunk, n_heads, lam_i):
    t = pl.program_id(1)
    ret_w = n_heads * HEAD_W

    @pl.when(t == 0)
    def _():
        s_ref[...] = jnp.zeros_like(s_ref)

    @pl.when((pl.program_id(0) == 0) & (t == 0))
    def _():
        for h in range(n_heads):
            decay_scr[h] = _retention_decay(ret_chunk, h)

    for c in range(tq // ret_chunk):
        rows = slice(c * ret_chunk, (c + 1) * ret_chunk)
        for h in range(n_heads):
            cols = slice(h * HEAD_W, (h + 1) * HEAD_W)
            decay = decay_scr[h]
            qdec, kdec, gl = _retention_row_decays(ret_chunk, h)
            q = rq_ref[rows, cols]
            k = rk_ref[rows, cols]
            v = rv_ref[rows, cols]
            s = s_ref[0, h]
            att = lax.dot_general(q.astype(BF16), k.astype(BF16), NT_DIMS,
                                  preferred_element_type=F32) * decay
            o = (jnp.dot(att.astype(BF16), v, preferred_element_type=F32)
                 + jnp.dot((q * qdec).astype(BF16), s.astype(BF16), preferred_element_type=F32))
            s_ref[0, h] = gl * s + lax.dot_general((k * kdec).astype(BF16), v, TN_DIMS,
                                                   preferred_element_type=F32)
            mix_scr[rows, cols] = _gated_group_norm(o, rg_ref[rows, cols]).astype(BF16)

    lane = lax.broadcasted_iota(jnp.int32, (tq, HEAD_W), 1)
    for h in range(n_heads):
        qh = dq_ref[:, h * HEAD_W:(h + 1) * HEAD_W]
        zero = jnp.zeros_like(qh)
        qq_scr[h, 0:tq] = jnp.where(lane < HEAD_W // 2, qh, zero)
        qq_scr[h, tq:2 * tq] = jnp.where(lane >= HEAD_W // 2, qh, zero)
    mx_scr[...] = jnp.full(mx_scr.shape, FINITE_MIN, F32)
    l_scr[...] = jnp.zeros_like(l_scr)
    acc_scr[...] = jnp.zeros_like(acc_scr)
    key_j = lax.broadcasted_iota(jnp.int32, (1, tq), 1).astype(F32)
    lane_tiles = [slice(c * HEAD_W, (c + 1) * HEAD_W) for c in range(tq // HEAD_W)]

    def fold(x, op):
        return functools.reduce(op, [x[:, c] for c in lane_tiles])

    def scores(kb, causal):
        key_pos = key_j + ((kb - t) * tq).astype(F32)
        for h in range(n_heads):
            slope = 2.0 ** (-8.0 / n_heads * (h + 1))
            s = jnp.dot(qq_scr[h], kt_ref[kb, h * HEAD_W:(h + 1) * HEAD_W, :], preferred_element_type=F32)
            s = (s + slope * key_pos) * LOG2_E
            if causal is not None:
                s = jnp.where(causal, s, NEG_INF)
            sc_scr[kb, h] = s
            mx_scr[h] = jnp.maximum(mx_scr[h], fold(s, jnp.maximum))

    _pair_loop(t, lambda kb: scores(kb, None))
    ii = lax.broadcasted_iota(jnp.int32, (2 * tq, tq), 0)
    jj = lax.broadcasted_iota(jnp.int32, (2 * tq, tq), 1)
    scores(t, jnp.where(ii >= tq, ii - tq, ii) >= jj)

    for h in range(n_heads):
        mx_scr[h] = jnp.broadcast_to(jnp.max(mx_scr[h], axis=-1, keepdims=True), (2 * tq, HEAD_W))

    def weighted(kb):
        for h in range(n_heads):
            m = mx_scr[h]
            e = [jnp.exp2(sc_scr[kb, h, :, c] - m) for c in lane_tiles]
            l_scr[h] += functools.reduce(jnp.add, e)
            vr = vb_ref[pl.ds(pl.multiple_of(kb * tq, tq), tq), h * HEAD_W:(h + 1) * HEAD_W]
            acc_scr[h] += jnp.dot(jnp.concatenate([x.astype(BF16) for x in e], axis=1), vr,
                                  preferred_element_type=F32)

    _pair_loop(t + 1, weighted)
    lam = _lambda(lam_ref, lam_i)
    subg = subg_ref[...]
    for h in range(n_heads):
        on = acc_scr[h] / jnp.sum(l_scr[h], axis=-1, keepdims=True)
        o = _rms(on[:tq] - lam * on[tq:], subg) * (1.0 - lam_i)
        mix_scr[:, ret_w + h * HEAD_W:ret_w + (h + 1) * HEAD_W] = o.astype(BF16)

    mo = jnp.dot(mix_scr[...], wo_ref[...], preferred_element_type=F32)
    y_ref[...] = x_ref[...] + _rms(mo, gpost_ref[...])


def _mixer(lamv, rq, rk, rv, rg, dq, ktb, dvb, x2, wo_bf, gpost, subg, batch, seq, n_heads, lam_i):
    m, d = x2.shape
    w = rq.shape[1]
    tq = MIX_TILE
    nt = seq // tq
    tile = lambda b, t: (b * nt + t, 0)
    whole = lambda b, t: (b, 0)
    body = functools.partial(_mixer_body, tq=tq, ret_chunk=RET_CHUNK, n_heads=n_heads, lam_i=lam_i)
    return pl.pallas_call(
        body,
        grid=(batch, nt),
        in_specs=[
            _const_spec(lamv.shape),
            pl.BlockSpec((tq, w), tile), pl.BlockSpec((tq, w), tile), pl.BlockSpec((tq, w), tile),
            pl.BlockSpec((tq, w), tile), pl.BlockSpec((tq, w), tile),
            pl.BlockSpec((nt, w, tq), lambda b, t: (b, 0, 0)), pl.BlockSpec((seq, w), whole),
            pl.BlockSpec((tq, d), tile),
            _const_spec(wo_bf.shape), _const_spec((1, d)), _const_spec((1, HEAD_W)),
        ],
        out_specs=[pl.BlockSpec((tq, d), tile),
                   pl.BlockSpec((1, n_heads, HEAD_W, HEAD_W), lambda b, t: (b, 0, 0, 0))],
        out_shape=[jax.ShapeDtypeStruct((m, d), F32),
                   jax.ShapeDtypeStruct((batch, n_heads, HEAD_W, HEAD_W), F32)],
        scratch_shapes=[
            pltpu.VMEM((tq, 2 * w), BF16),
            pltpu.VMEM((n_heads, 2 * tq, HEAD_W), BF16),
            pltpu.VMEM((nt, n_heads, 2 * tq, tq), F32),
            pltpu.VMEM((n_heads, 2 * tq, HEAD_W), F32),
            pltpu.VMEM((n_heads, 2 * tq, HEAD_W), F32),
            pltpu.VMEM((n_heads, 2 * tq, HEAD_W), F32),
            pltpu.VMEM((n_heads, RET_CHUNK, RET_CHUNK), F32),
        ],
        compiler_params=_params("arbitrary", "arbitrary"),
        name="mixer",
    )(lamv, rq, rk, rv, rg, dq, ktb, dvb, x2, wo_bf, gpost, subg)


def _conv_gate(g, u, g1, g2, cw_ref, cb_ref, cols):
    c = cw_ref[0:1, cols] * g2 + cw_ref[1:2, cols] * g1 + cw_ref[2:3, cols] * g + cb_ref[:, cols]
    return (jax.nn.gelu(c) * u).astype(BF16)


def _ffn_body(pt_ref, x_ref, gpre_ref, win_ref, cw_ref, cb_ref, wout_ref, gpost_ref,
              lam_ref, q_ref, kn_ref, vn_ref, ck_hbm, cv_hbm, y_ref, cs_ref, o_ref,
              gbuf, carry, act, kbuf, vbuf, sem, sc_scr, *, tf, d_ff, ck, rows, n_pages, page, n_heads, lam_i):
    t = pl.program_id(1)
    step = pl.program_id(0) * pl.num_programs(1) + t
    n_rows = rows * pl.num_programs(0) * pl.num_programs(1)
    copies = functools.partial(_page_copies, pt_ref, ck_hbm, cv_hbm, kbuf, vbuf, sem, n_rows=q_ref.shape[0],
                               n_pages=n_pages)

    @pl.when(step == 0)
    def _():
        for slot in range(2):
            for c in copies(b=slot, slot=slot):
                c.start()

    @pl.when(t == 0)
    def _():
        carry[...] = jnp.zeros_like(carry)

    lam = _lambda(lam_ref, lam_i)
    for i in range(rows):
        g = step * rows + i
        slot = i % 2
        for c in copies(b=g, slot=slot):
            c.wait()
        this = pl.ds(g, 1)
        o_ref[this, :] = _paged_row(slot, q_ref[this, :], kn_ref[this, :], vn_ref[this, :], lam, kbuf, vbuf,
                                    sc_scr, n_pages=n_pages, page=page, n_heads=n_heads)

        @pl.when(g + 2 < n_rows)
        def _():
            for c in copies(b=g + 2, slot=slot):
                c.start()

    x = x_ref[...]
    h = _rms(x, gpre_ref[...]).astype(BF16)
    for j in range(d_ff // ck):
        cols = slice(j * ck, (j + 1) * ck)
        g = jnp.dot(h, win_ref[:, cols], preferred_element_type=F32)
        u = jnp.dot(h, win_ref[:, d_ff + j * ck:d_ff + (j + 1) * ck], preferred_element_type=F32)
        gbuf[0:8, :] = carry[:, cols]
        gbuf[8:8 + tf, :] = g
        act[:, cols] = _conv_gate(g, u, gbuf[7:7 + tf, :], gbuf[6:6 + tf, :], cw_ref, cb_ref, cols)
        carry[:, cols] = gbuf[tf:tf + 8, :]
    f = jnp.dot(act[...], wout_ref[...], preferred_element_type=F32)
    y_ref[...] = x + _rms(f, gpost_ref[...])
    cs_ref[0] = carry[...]


def _ffn_paged(x2, gpre, win_bf, cw, cb, wout_bf, gpost, batch, seq,
               page_table, lamv, dq, dk, dv, cache_kt, cache_v, n_heads, lam_i):
    m, d = x2.shape
    d_ff = wout_bf.shape[0]
    tf = FFN_TILE
    nt = seq // tf
    n, w = dq.shape
    rows = n // (batch * nt)
    assert rows * batch * nt == n and rows % 2 == 0, "sample rows must split evenly, in pairs, over the FFN steps"
    n_pages = page_table.shape[1]
    page = cache_kt.shape[2]
    assert n_pages % PAGE_GROUP == 0
    tile = lambda b, t, pt: (b * nt + t, 0)
    const = _const_spec
    rspec = const((n, w))
    body = functools.partial(_ffn_body, tf=tf, d_ff=d_ff, ck=FFN_CHUNK, rows=rows, n_pages=n_pages, page=page,
                             n_heads=n_heads, lam_i=lam_i)
    grid_spec = pltpu.PrefetchScalarGridSpec(
        num_scalar_prefetch=1,
        grid=(batch, nt),
        in_specs=[pl.BlockSpec((tf, d), tile), const((1, d)), const(win_bf.shape), const(cw.shape),
                  const(cb.shape), const(wout_bf.shape), const((1, d)),
                  const(lamv.shape), rspec, rspec, rspec,
                  pl.BlockSpec(memory_space=pl.ANY), pl.BlockSpec(memory_space=pl.ANY)],
        out_specs=[pl.BlockSpec((tf, d), tile), pl.BlockSpec((1, 8, d_ff), lambda b, t, pt: (b, 0, 0)),
                   pl.BlockSpec((n, w), lambda b, t, pt: (0, 0))],
        scratch_shapes=[pltpu.VMEM((tf + 8, FFN_CHUNK), F32), pltpu.VMEM((8, d_ff), F32),
                        pltpu.VMEM((tf, d_ff), BF16),
                        pltpu.VMEM((2, n_pages, w, page), F32),
                        pltpu.VMEM((2, n_pages, page * n_heads, HEAD_W), F32),
                        pltpu.SemaphoreType.DMA((2, 2)),
                        pltpu.VMEM((n_pages // PAGE_GROUP, 2 * n_heads, PAGE_GROUP * page), F32)],
    )
    y, cs, o = pl.pallas_call(
        body,
        grid_spec=grid_spec,
        out_shape=[jax.ShapeDtypeStruct((m, d), F32), jax.ShapeDtypeStruct((batch, 8, d_ff), F32),
                   jax.ShapeDtypeStruct((n, w), F32)],
        compiler_params=_params("arbitrary", "arbitrary"),
        name="conv_ffn_paged_attn",
    )(page_table.T.reshape(-1), x2, gpre, win_bf, cw, cb, wout_bf, gpost,
      lamv, dq.astype(F32), dk, dv, cache_kt, cache_v)
    return y, cs, o


def _ret_dec_body(rq_ref, rk_ref, rv_ref, rg_ref, s_ref, a_ref, sn_ref, *, grp, n_heads):
    row = lax.broadcasted_iota(jnp.int32, (grp, grp * HEAD_W), 0)
    lane = lax.broadcasted_iota(jnp.int32, (grp, grp * HEAD_W), 1)
    own = (lane // HEAD_W) == row

    def block_diag(x):
        return jnp.where(own, jnp.tile(x, (1, grp)), 0.0).astype(BF16)

    for h in range(n_heads):
        cols = slice(h * HEAD_W, (h + 1) * HEAD_W)
        gamma = math.exp(_ret_log_decay(h))
        q = rq_ref[:, cols]
        k = rk_ref[:, cols]
        v = rv_ref[:, cols]
        qb = q.astype(BF16).astype(F32)
        kb = k.astype(BF16).astype(F32)
        att = jnp.sum(qb * kb, axis=-1, keepdims=True).astype(BF16).astype(F32)
        s = s_ref[:, h].reshape(grp * HEAD_W, HEAD_W)
        o = att * v.astype(F32) + jnp.dot(block_diag(q * gamma), s.astype(BF16),
                                          preferred_element_type=F32)
        upd = lax.dot_general(block_diag(k), v, TN_DIMS, preferred_element_type=F32)
        sn_ref[:, h] = (gamma * s + upd).reshape(grp, HEAD_W, HEAD_W)
        a_ref[:, cols] = _gated_group_norm(o, rg_ref[:, cols]).astype(BF16)


def _ret_decode(rq, rk, rv, rg, state, n_heads):
    n, w = rq.shape
    grp = DEC_GROUP
    row = lambda i: (i, 0)
    st = lambda i: (i, 0, 0, 0)
    spec = pl.BlockSpec((grp, w), row)
    st_spec = pl.BlockSpec((grp, n_heads, HEAD_W, HEAD_W), st)
    return pl.pallas_call(
        functools.partial(_ret_dec_body, grp=grp, n_heads=n_heads),
        grid=(n // grp,),
        in_specs=[spec, spec, spec, spec, st_spec],
        out_specs=[spec, st_spec],
        out_shape=[jax.ShapeDtypeStruct((n, w), BF16), jax.ShapeDtypeStruct(state.shape, F32)],
        compiler_params=_params("arbitrary"),
        name="ret_decode",
    )(rq, rk, rv, rg, state)


def _page_copies(pt_ref, ck_hbm, cv_hbm, kbuf, vbuf, sem, b, slot, n_rows, n_pages):
    out = []
    for p in range(n_pages):
        page = pt_ref[p * n_rows + b]
        out.append(pltpu.make_async_copy(ck_hbm.at[page], kbuf.at[slot, p], sem.at[0, slot]))
        out.append(pltpu.make_async_copy(cv_hbm.at[page], vbuf.at[slot, p], sem.at[1, slot]))
    return out


def _paged_row(slot, q, kn_row, vn_row, lam, kbuf, vbuf, sc_scr, *, n_pages, page, n_heads):
    w = n_heads * HEAD_W
    nc = 2 * n_heads
    past = n_pages * page
    ci = lax.broadcasted_iota(jnp.int32, (nc, w), 0)
    ri = lax.broadcasted_iota(jnp.int32, (nc, w), 1)
    col_of = ri // HEAD_W + n_heads * ((ri % HEAD_W) // (HEAD_W // 2))
    q_sel = jnp.where(ci == col_of, jnp.broadcast_to(q, (nc, w)), 0.0)
    q_sel_bf = q_sel.astype(BF16)
    grp = PAGE_GROUP
    span = grp * page
    rowc = lax.broadcasted_iota(jnp.int32, (nc, span), 0)
    key_i = lax.broadcasted_iota(jnp.int32, (nc, span), 1)
    slope = jnp.zeros((nc, span), F32)
    for h in range(n_heads):
        slope = jnp.where(rowc % n_heads == h, 2.0 ** (-8.0 / n_heads * (h + 1)), slope)

    mx = jnp.full((nc, span), FINITE_MIN, F32)
    for pg in range(n_pages // grp):
        kt = jnp.concatenate([kbuf[slot, grp * pg + i] for i in range(grp)], axis=1).astype(BF16)
        dist = (past - (pg * span + key_i)).astype(F32)
        s = jnp.dot(q_sel_bf, kt, preferred_element_type=F32) - slope * dist
        sc_scr[pg] = s
        mx = jnp.maximum(mx, s)
    kn = kn_row.astype(BF16).astype(F32)
    s_self = jnp.broadcast_to(jnp.sum(q_sel * kn, axis=-1, keepdims=True), (nc, span))
    m = jnp.maximum(jnp.broadcast_to(jnp.max(mx, axis=-1, keepdims=True), (nc, span)), s_self)

    lp = jnp.zeros((nc, span), F32)
    for pg in range(n_pages // grp):
        e = jnp.exp(sc_scr[pg] - m)
        sc_scr[pg] = e
        lp = lp + e
    e_self = jnp.exp(s_self - m)
    l = jnp.broadcast_to(jnp.sum(lp, axis=-1, keepdims=True), (nc, span)) + e_self
    r = jnp.where(rowc < n_heads, 1.0, lam) / l

    def combine(pn):
        return (pn - pltpu.roll(pn, n_heads, axis=0)).astype(BF16)

    def v_rows(p):
        return jnp.concatenate([vbuf[slot, p, pl.ds(h, page, stride=n_heads), :] for h in range(n_heads)],
                               axis=1)

    acc = jnp.zeros((nc, w), F32)
    for pg in range(n_pages // grp):
        v = jnp.concatenate([v_rows(grp * pg + i) for i in range(grp)], axis=0).astype(BF16)
        acc = acc + jnp.dot(combine(sc_scr[pg] * r), v, preferred_element_type=F32)
    a_self = combine(e_self * r).astype(F32)
    vn = vn_row.astype(BF16).astype(F32)
    return jnp.concatenate(
        [acc[h:h + 1, h * HEAD_W:(h + 1) * HEAD_W]
         + a_self[h:h + 1, 0:HEAD_W] * vn[:, h * HEAD_W:(h + 1) * HEAD_W] for h in range(n_heads)], axis=-1)


def _dec_ffn_body(a_ref, do_ref, x_ref, wo_ref, gmix_ref, subg_ref, gpre_ref, wg_ref, wu_ref, cw_ref,
                  cb_ref, sc_ref, wout_ref, gpost_ref, y_ref, g_ref, xmid, hbuf, acc, *, lam_i):
    j = pl.program_id(0)
    ret_w = a_ref.shape[1]

    @pl.when(j == 0)
    def _():
        mo = jnp.dot(a_ref[...], wo_ref[0:ret_w, :], preferred_element_type=F32)
        for h in range(do_ref.shape[1] // HEAD_W):
            cols = slice(h * HEAD_W, (h + 1) * HEAD_W)
            bh = (_rms(do_ref[:, cols], subg_ref[...]) * (1.0 - lam_i)).astype(BF16)
            mo += jnp.dot(bh, wo_ref[ret_w + h * HEAD_W:ret_w + (h + 1) * HEAD_W, :],
                          preferred_element_type=F32)
        xm = x_ref[...] + _rms(mo, gmix_ref[...])
        xmid[...] = xm
        hbuf[...] = _rms(xm, gpre_ref[...]).astype(BF16)
        acc[...] = jnp.zeros_like(acc)

    h = hbuf[...]
    g = jnp.dot(h, wg_ref[...], preferred_element_type=F32)
    u = jnp.dot(h, wu_ref[...], preferred_element_type=F32)
    g_ref[...] = g
    c = cw_ref[0:1, :] * sc_ref[0] + cw_ref[1:2, :] * sc_ref[1] + cw_ref[2:3, :] * g + cb_ref[...]
    acc[...] += jnp.dot((jax.nn.gelu(c) * u).astype(BF16), wout_ref[...], preferred_element_type=F32)

    @pl.when(j == pl.num_programs(0) - 1)
    def _():
        y_ref[...] = xmid[...] + _rms(acc[...], gpost_ref[...])


def _dec_ffn(a, do, x2, wo_bf, gmix, subg, gpre, win_bf, cw, cb, conv_state_t, wout_bf, gpost, lam_i):
    n, d = x2.shape
    d_ff = wout_bf.shape[0]
    ck = d_ff // DEC_FFN_STEPS
    assert ck * DEC_FFN_STEPS == d_ff and ck % HEAD_W == 0
    nj = DEC_FFN_STEPS
    full = lambda shape: pl.BlockSpec(shape, lambda j: (0,) * len(shape))
    return pl.pallas_call(
        functools.partial(_dec_ffn_body, lam_i=lam_i),
        grid=(nj,),
        in_specs=[full(a.shape), full(do.shape), full(x2.shape), full(wo_bf.shape), full((1, d)),
                  full((1, HEAD_W)), full((1, d)),
                  pl.BlockSpec((d, ck), lambda j: (0, j)), pl.BlockSpec((d, ck), lambda j: (0, nj + j)),
                  pl.BlockSpec((3, ck), lambda j: (0, j)), pl.BlockSpec((1, ck), lambda j: (0, j)),
                  pl.BlockSpec((2, n, ck), lambda j: (0, 0, j)),
                  pl.BlockSpec((ck, d), lambda j: (j, 0)), full((1, d))],
        out_specs=[full((n, d)), pl.BlockSpec((n, ck), lambda j: (0, j))],
        out_shape=[jax.ShapeDtypeStruct((n, d), F32), jax.ShapeDtypeStruct((n, d_ff), F32)],
        scratch_shapes=[pltpu.VMEM((n, d), F32), pltpu.VMEM((n, d), BF16), pltpu.VMEM((n, d), F32)],
        compiler_params=_params("arbitrary"),
        name="dec_ffn",
    )(a, do, x2, wo_bf, gmix, subg, gpre, win_bf, win_bf, cw, cb, conv_state_t, wout_bf, gpost)


def kernel(x_prompt, x_sample, state_ret, cache_k, cache_v, state_conv, page_table,
           norm_mix_pre, norm_mix_post, w_in, w_o, lambda_q1, lambda_k1, lambda_q2, lambda_k2,
           subln_g, norm_ffn_pre, norm_ffn_post, w_ffn_in, conv_w, conv_b, w_ffn_out):
    batch, seq, d = x_prompt.shape
    n_dec = x_sample.shape[0]
    assert x_sample.shape[1] == 1, "the sample group is one token per row"
    depth = w_in.shape[0]
    n_heads = state_ret.shape[2]
    dk_ret = state_ret.shape[3]
    dh_diff = cache_k.shape[-1]
    d_ff = w_ffn_out.shape[1]
    w = n_heads * HEAD_W
    assert w_in.shape[2] == 7 * w and cache_v.shape[-1] == HEAD_W and 2 * dh_diff == HEAD_W
    assert seq % MIX_TILE == 0 and seq % FFN_TILE == 0 and d_ff % FFN_CHUNK == 0
    assert n_dec % DEC_GROUP == 0 and n_dec % 128 == 0 and seq % IN_TILE == 0 and IN_TILE % MIX_TILE == 0

    xp = x_prompt.reshape(batch * seq, d)
    xs = x_sample.reshape(n_dec, d)
    outs = [[] for _ in range(8)]
    for l in range(depth):
        lam_i = 0.8 - 0.6 * math.exp(-0.3 * l)
        lamv = jnp.stack([lambda_q1[l], lambda_k1[l], lambda_q2[l], lambda_k2[l]]).astype(F32)
        row = lambda v: v.reshape(1, -1)
        w_in_bf = w_in[l].astype(BF16)
        g_pre, g_post = row(norm_mix_pre[l]), row(norm_mix_post[l])
        f_pre, f_post = row(norm_ffn_pre[l]), row(norm_ffn_post[l])
        subg = row(subln_g[l])
        cw, cb = conv_w[l], row(conv_b[l])

        cache_kt = jnp.transpose(cache_k[l], (0, 2, 3, 4, 1)).reshape(cache_k.shape[1], w, cache_k.shape[2])
        cache_vr = cache_v[l].reshape(cache_v.shape[1], cache_v.shape[2] * n_heads, HEAD_W)
        proj = functools.partial(_inproj, n_heads=n_heads, dk_ret=dk_ret, dh_diff=dh_diff)

        (rq, rk, rv, rg, dq, kt, v4, ktb, dvb, w_o_bf, w_ffn_in_bf, w_ffn_out_bf) = proj(
            xp, g_pre, w_in_bf, batch, seq, IN_TILE, key_block=MIX_TILE,
            riders=(w_o[l], w_ffn_in[l], w_ffn_out[l]))
        xp, s_fin = _mixer(lamv, rq, rk, rv, rg, dq, ktb, dvb, xp, w_o_bf, g_post, subg,
                           batch, seq, n_heads, lam_i)
        outs[0].append(s_fin)
        outs[2].append(jnp.transpose(kt.reshape(batch, n_heads, 2, dh_diff, seq), (0, 4, 1, 2, 3)))
        outs[3].append(v4.reshape(batch, seq, n_heads, HEAD_W))
        rq, rk, rv, rg, dq, kt, v4, dk, dv = proj(xs, g_pre, w_in_bf, 1, n_dec, n_dec, key_block=None)
        a, s_new = _ret_decode(rq, rk, rv, rg, state_ret[l], n_heads)

        xp, cs, do = _ffn_paged(xp, f_pre, w_ffn_in_bf, cw, cb, w_ffn_out_bf, f_post, batch, seq,
                                page_table, lamv, dq, dk, dv, cache_kt, cache_vr, n_heads, lam_i)
        outs[6].append(cs[:, 6:8, :])

        conv_t = jnp.swapaxes(state_conv[l], 0, 1)
        xs, g_new = _dec_ffn(a, do, xs, w_o_bf, g_post, subg, f_pre, w_ffn_in_bf, cw, cb, conv_t,
                             w_ffn_out_bf, f_post, lam_i)
        outs[1].append(s_new)
        outs[4].append(jnp.transpose(kt.reshape(n_heads, 2, dh_diff, n_dec), (3, 0, 1, 2))[:, None])
        outs[5].append(v4.reshape(n_dec, 1, n_heads, HEAD_W))
        outs[7].append(jnp.stack([state_conv[l][:, 1, :], g_new], axis=1))

    st = [jnp.stack(o) for o in outs]
    return (xp.reshape(batch, seq, d), xs.reshape(n_dec, 1, d), st[0], st[1], st[2], st[3], st[4], st[5],
            st[6], st[7])
```

```python
import functools
import math

import jax
import jax.numpy as jnp
from jax import lax
from jax.experimental import pallas as pl
from jax.experimental.pallas import tpu as pltpu

F32 = jnp.float32
BF16 = jnp.bfloat16
EPS = 1e-6
NEG_INF = -1e30
FINITE_MIN = -3e38
LOG2_E = math.log2(math.e)

V7X_VMEM_BYTES = 64 * 1024 * 1024
VMEM_LIMIT_BYTES = V7X_VMEM_BYTES * 7 // 8

RET_CHUNK = 256
HEAD_W = 128
IN_TILE = 1024
MIX_TILE = 256
FFN_TILE = 256
FFN_CHUNK = 256
DEC_GROUP = 8
DEC_FFN_STEPS = 2
PAGE_GROUP = 2
NT_DIMS = (((1,), (1,)), ((), ()))
TN_DIMS = (((0,), (0,)), ((), ()))


def _rms(x, g):
    return x * lax.rsqrt(jnp.mean(x * x, axis=-1, keepdims=True) + EPS) * g


def _params(*sem):
    return pltpu.CompilerParams(dimension_semantics=sem, vmem_limit_bytes=VMEM_LIMIT_BYTES)


def _const_spec(shape):
    n = len(shape)
    return pl.BlockSpec(shape, lambda *_: (0,) * n, pipeline_mode=pl.Buffered(1))


def _ret_log_decay(h):
    return math.log(1.0 - 2.0 ** (-5.0 - h))


def _lambda(lam_ref, lam_i):
    a = jnp.sum(lam_ref[0:1, :] * lam_ref[1:2, :], axis=-1, keepdims=True)
    b = jnp.sum(lam_ref[2:3, :] * lam_ref[3:4, :], axis=-1, keepdims=True)
    return jnp.exp(a) - jnp.exp(b) + lam_i


def _inproj_body(x_ref, g_ref, w_ref, *rest, n_riders, width, n_heads, k_scale, q_scale, key_block):
    rider_in, rest = rest[:n_riders], rest[n_riders:]
    rq_ref, rk_ref, rv_ref, rg_ref, dq_ref, kt_ref, v4_ref, kx_ref, vx_ref = rest[:9]
    for src, dst in zip(rider_in, rest[9:]):
        dst[...] = src[...].astype(BF16)
    h = _rms(x_ref[...], g_ref[...]).astype(BF16)

    def col(j):
        return jnp.dot(h, w_ref[:, j * width:(j + 1) * width], preferred_element_type=F32)

    dk = col(5)
    kt = dk.T
    kt_ref[0] = kt
    if key_block is None:
        kx_ref[...] = dk
    else:
        for c in range(kx_ref.shape[0]):
            kx_ref[c] = kt[:, c * key_block:(c + 1) * key_block].astype(BF16)
    dv = col(6)
    for hh in range(n_heads):
        v4_ref[:, hh, :] = dv[:, hh * HEAD_W:(hh + 1) * HEAD_W]
    vx_ref[...] = dv if key_block is None else dv.astype(BF16)
    rq_ref[...] = col(0)
    rk_ref[...] = col(1) * k_scale
    rv_ref[...] = col(2).astype(BF16)
    rg_ref[...] = col(3)
    dq_ref[...] = (col(4) * q_scale).astype(BF16)


def _inproj(x2, g, w_bf, batch, seq, tile, n_heads, dk_ret, dh_diff, key_block, riders=()):
    m, d = x2.shape
    width = n_heads * HEAD_W
    nt = seq // tile
    steps = m // tile
    row = lambda i: (i, 0)
    assert all(r.shape[0] % (16 * steps) == 0 for r in riders), "rider row blocks must be bf16-tile aligned"
    rider_specs = [pl.BlockSpec((r.shape[0] // steps, r.shape[1]), row) for r in riders]
    f32_out = jax.ShapeDtypeStruct((m, width), F32)
    bf_out = jax.ShapeDtypeStruct((m, width), BF16)
    spec = pl.BlockSpec((tile, width), row)
    if key_block is None:
        kx_shape, kx_spec, vx_shape = f32_out, spec, f32_out
    else:
        kx_shape = jax.ShapeDtypeStruct((m // key_block, width, key_block), BF16)
        kx_spec = pl.BlockSpec((tile // key_block, width, key_block), lambda i: (i, 0, 0))
        vx_shape = bf_out
    body = functools.partial(_inproj_body, n_riders=len(riders), width=width, n_heads=n_heads,
                             k_scale=dk_ret ** -0.5, q_scale=dh_diff ** -0.5, key_block=key_block)
    return pl.pallas_call(
        body,
        grid=(steps,),
        in_specs=[pl.BlockSpec((tile, d), row), _const_spec((1, d)), _const_spec(w_bf.shape)] + rider_specs,
        out_specs=[spec] * 5 + [pl.BlockSpec((1, width, tile), lambda i: (i // nt, 0, i % nt)),
                                pl.BlockSpec((tile, n_heads, HEAD_W), lambda i: (i, 0, 0)), kx_spec, spec]
        + rider_specs,
        out_shape=[f32_out, f32_out, bf_out, f32_out, bf_out,
                   jax.ShapeDtypeStruct((batch, width, seq), F32),
                   jax.ShapeDtypeStruct((m, n_heads, HEAD_W), F32), kx_shape, vx_shape]
        + [jax.ShapeDtypeStruct(r.shape, BF16) for r in riders],
        compiler_params=_params("arbitrary"),
        name="inproj",
    )(x2, g, w_bf, *riders)


def _retention_decay(length, h):
    i = lax.broadcasted_iota(jnp.int32, (length, length), 0)
    j = lax.broadcasted_iota(jnp.int32, (length, length), 1)
    diff = (i - j).astype(F32)
    causal = diff >= 0
    return jnp.where(causal, jnp.exp(jnp.where(causal, diff, 0.0) * _ret_log_decay(h)), 0.0)


def _retention_row_decays(length, h):
    lg = _ret_log_decay(h)
    ri = lax.broadcasted_iota(jnp.int32, (length, 1), 0).astype(F32)
    return jnp.exp((ri + 1.0) * lg), jnp.exp((length - 1.0 - ri) * lg), math.exp(length * lg)


def _pair_loop(n, body):
    pairs = lax.shift_right_logical(n, 1)

    def two(i, carry):
        body(2 * i)
        body(2 * i + 1)
        return carry

    def one(i, carry):
        body(i)
        return carry

    lax.fori_loop(0, pairs, two, 0)
    lax.fori_loop(2 * pairs, n, one, 0)


def _gated_group_norm(o, g):
    on = o * lax.rsqrt(jnp.mean(o * o, axis=-1, keepdims=True) + EPS)
    return on * (g * jax.nn.sigmoid(g))


def _mixer_body(lam_ref, rq_ref, rk_ref, rv_ref, rg_ref, dq_ref, kt_ref, vb_ref, x_ref, wo_ref,
                gpost_ref, subg_ref, srq_ref, srk_ref, srv_ref, srg_ref, sst_ref,
                y_ref, s_ref, sa_ref, ssn_ref,
                mix_scr, qq_scr, sc_scr, mx_scr, l_scr, acc_scr, decay_scr,
                *, tq, ret_chunk, n_heads, lam_i, dec_period):
    t = pl.program_id(1)
    ret_w = n_heads * HEAD_W

    @pl.when((pl.program_id(0) * pl.num_programs(1) + t) % dec_period == 0)
    def _():
        _ret_decode_rows(srq_ref, srk_ref, srv_ref, srg_ref, sst_ref, sa_ref, ssn_ref,
                         grp=sa_ref.shape[0], n_heads=n_heads)

    @pl.when(t == 0)
    def _():
        s_ref[...] = jnp.zeros_like(s_ref)

    @pl.when((pl.program_id(0) == 0) & (t == 0))
    def _():
        for h in range(n_heads):
            decay_scr[h] = _retention_decay(ret_chunk, h)

    for c in range(tq // ret_chunk):
        rows = slice(c * ret_chunk, (c + 1) * ret_chunk)
        for h in range(n_heads):
            cols = slice(h * HEAD_W, (h + 1) * HEAD_W)
            decay = decay_scr[h]
            qdec, kdec, gl = _retention_row_decays(ret_chunk, h)
            q = rq_ref[rows, cols]
            k = rk_ref[rows, cols]
            v = rv_ref[rows, cols]
            s = s_ref[0, h]
            att = lax.dot_general(q.astype(BF16), k.astype(BF16), NT_DIMS,
                                  preferred_element_type=F32) * decay
            o = (jnp.dot(att.astype(BF16), v, preferred_element_type=F32)
                 + jnp.dot((q * qdec).astype(BF16), s.astype(BF16), preferred_element_type=F32))
            s_ref[0, h] = gl * s + lax.dot_general((k * kdec).astype(BF16), v, TN_DIMS,
                                                   preferred_element_type=F32)
            mix_scr[rows, cols] = _gated_group_norm(o, rg_ref[rows, cols]).astype(BF16)

    lane = lax.broadcasted_iota(jnp.int32, (tq, HEAD_W), 1)
    for h in range(n_heads):
        qh = dq_ref[:, h * HEAD_W:(h + 1) * HEAD_W]
        zero = jnp.zeros_like(qh)
        qq_scr[h, 0:tq] = jnp.where(lane < HEAD_W // 2, qh, zero)
        qq_scr[h, tq:2 * tq] = jnp.where(lane >= HEAD_W // 2, qh, zero)
    mx_scr[...] = jnp.full(mx_scr.shape, FINITE_MIN, F32)
    l_scr[...] = jnp.zeros_like(l_scr)
    acc_scr[...] = jnp.zeros_like(acc_scr)
    key_j = lax.broadcasted_iota(jnp.int32, (1, tq), 1).astype(F32)
    lane_tiles = [slice(c * HEAD_W, (c + 1) * HEAD_W) for c in range(tq // HEAD_W)]

    def fold(x, op):
        return functools.reduce(op, [x[:, c] for c in lane_tiles])

    def scores(kb, causal):
        key_pos = key_j + ((kb - t) * tq).astype(F32)
        for h in range(n_heads):
            slope = 2.0 ** (-8.0 / n_heads * (h + 1))
            s = jnp.dot(qq_scr[h], kt_ref[kb, h * HEAD_W:(h + 1) * HEAD_W, :], preferred_element_type=F32)
            s = (s + slope * key_pos) * LOG2_E
            if causal is not None:
                s = jnp.where(causal, s, NEG_INF)
            sc_scr[kb, h] = s
            mx_scr[h] = jnp.maximum(mx_scr[h], fold(s, jnp.maximum))

    _pair_loop(t, lambda kb: scores(kb, None))
    ii = lax.broadcasted_iota(jnp.int32, (2 * tq, tq), 0)
    jj = lax.broadcasted_iota(jnp.int32, (2 * tq, tq), 1)
    scores(t, jnp.where(ii >= tq, ii - tq, ii) >= jj)

    for h in range(n_heads):
        mx_scr[h] = jnp.broadcast_to(jnp.max(mx_scr[h], axis=-1, keepdims=True), (2 * tq, HEAD_W))

    def weighted(kb):
        for h in range(n_heads):
            m = mx_scr[h]
            e = [jnp.exp2(sc_scr[kb, h, :, c] - m) for c in lane_tiles]
            l_scr[h] += functools.reduce(jnp.add, e)
            vr = vb_ref[pl.ds(pl.multiple_of(kb * tq, tq), tq), h * HEAD_W:(h + 1) * HEAD_W]
            acc_scr[h] += jnp.dot(jnp.concatenate([x.astype(BF16) for x in e], axis=1), vr,
                                  preferred_element_type=F32)

    _pair_loop(t + 1, weighted)
    lam = _lambda(lam_ref, lam_i)
    subg = subg_ref[...]
    for h in range(n_heads):
        on = acc_scr[h] / jnp.sum(l_scr[h], axis=-1, keepdims=True)
        o = _rms(on[:tq] - lam * on[tq:], subg) * (1.0 - lam_i)
        mix_scr[:, ret_w + h * HEAD_W:ret_w + (h + 1) * HEAD_W] = o.astype(BF16)

    mo = jnp.dot(mix_scr[...], wo_ref[...], preferred_element_type=F32)
    y_ref[...] = x_ref[...] + _rms(mo, gpost_ref[...])


def _mixer(lamv, rq, rk, rv, rg, dq, ktb, dvb, x2, wo_bf, gpost, subg, batch, seq, n_heads, lam_i,
           srq, srk, srv, srg, sstate):
    m, d = x2.shape
    w = rq.shape[1]
    tq = MIX_TILE
    nt = seq // tq
    n_dec = srq.shape[0]
    grp = DEC_GROUP
    dec_period = batch * nt * grp // n_dec
    assert dec_period * n_dec == batch * nt * grp, "sample row groups must spread evenly over the mixer steps"
    tile = lambda b, t: (b * nt + t, 0)
    whole = lambda b, t: (b, 0)
    dec_rows = pl.BlockSpec((grp, w), lambda b, t: ((b * nt + t) // dec_period, 0))
    dec_state = pl.BlockSpec((grp, n_heads, HEAD_W, HEAD_W), lambda b, t: ((b * nt + t) // dec_period, 0, 0, 0))
    body = functools.partial(_mixer_body, tq=tq, ret_chunk=RET_CHUNK, n_heads=n_heads, lam_i=lam_i,
                             dec_period=dec_period)
    return pl.pallas_call(
        body,
        grid=(batch, nt),
        in_specs=[
            _const_spec(lamv.shape),
            pl.BlockSpec((tq, w), tile), pl.BlockSpec((tq, w), tile), pl.BlockSpec((tq, w), tile),
            pl.BlockSpec((tq, w), tile), pl.BlockSpec((tq, w), tile),
            pl.BlockSpec((nt, w, tq), lambda b, t: (b, 0, 0)), pl.BlockSpec((seq, w), whole),
            pl.BlockSpec((tq, d), tile),
            _const_spec(wo_bf.shape), _const_spec((1, d)), _const_spec((1, HEAD_W)),
            dec_rows, dec_rows, dec_rows, dec_rows, dec_state,
        ],
        out_specs=[pl.BlockSpec((tq, d), tile),
                   pl.BlockSpec((1, n_heads, HEAD_W, HEAD_W), lambda b, t: (b, 0, 0, 0)),
                   dec_rows, dec_state],
        out_shape=[jax.ShapeDtypeStruct((m, d), F32),
                   jax.ShapeDtypeStruct((batch, n_heads, HEAD_W, HEAD_W), F32),
                   jax.ShapeDtypeStruct((n_dec, w), BF16), jax.ShapeDtypeStruct(sstate.shape, F32)],
        scratch_shapes=[
            pltpu.VMEM((tq, 2 * w), BF16),
            pltpu.VMEM((n_heads, 2 * tq, HEAD_W), BF16),
            pltpu.VMEM((nt, n_heads, 2 * tq, tq), F32),
            pltpu.VMEM((n_heads, 2 * tq, HEAD_W), F32),
            pltpu.VMEM((n_heads, 2 * tq, HEAD_W), F32),
            pltpu.VMEM((n_heads, 2 * tq, HEAD_W), F32),
            pltpu.VMEM((n_heads, RET_CHUNK, RET_CHUNK), F32),
        ],
        compiler_params=_params("arbitrary", "arbitrary"),
        name="mixer",
    )(lamv, rq, rk, rv, rg, dq, ktb, dvb, x2, wo_bf, gpost, subg, srq, srk, srv, srg, sstate)


def _conv_gate(g, u, g1, g2, cw_ref, cb_ref, cols):
    c = cw_ref[0:1, cols] * g2 + cw_ref[1:2, cols] * g1 + cw_ref[2:3, cols] * g + cb_ref[:, cols]
    return (jax.nn.gelu(c) * u).astype(BF16)


def _ffn_body(pt_ref, x_ref, gpre_ref, win_ref, cw_ref, cb_ref, wout_ref, gpost_ref,
              lam_ref, q_ref, kn_ref, vn_ref, ck_hbm, cv_hbm, y_ref, cs_ref, o_ref,
              gbuf, carry, act, kbuf, vbuf, sem, sc_scr, *, tf, d_ff, ck, rows, n_pages, page, n_heads, lam_i):
    t = pl.program_id(1)
    step = pl.program_id(0) * pl.num_programs(1) + t
    n_rows = rows * pl.num_programs(0) * pl.num_programs(1)
    copies = functools.partial(_page_copies, pt_ref, ck_hbm, cv_hbm, kbuf, vbuf, sem, n_rows=q_ref.shape[0],
                               n_pages=n_pages)

    @pl.when(step == 0)
    def _():
        for slot in range(2):
            for c in copies(b=slot, slot=slot):
                c.start()

    @pl.when(t == 0)
    def _():
        carry[...] = jnp.zeros_like(carry)

    lam = _lambda(lam_ref, lam_i)
    for i in range(rows):
        g = step * rows + i
        slot = i % 2
        for c in copies(b=g, slot=slot):
            c.wait()
        this = pl.ds(g, 1)
        o_ref[this, :] = _paged_row(slot, q_ref[this, :], kn_ref[this, :], vn_ref[this, :], lam, kbuf, vbuf,
                                    sc_scr, n_pages=n_pages, page=page, n_heads=n_heads)

        @pl.when(g + 2 < n_rows)
        def _():
            for c in copies(b=g + 2, slot=slot):
                c.start()

    x = x_ref[...]
    h = _rms(x, gpre_ref[...]).astype(BF16)
    for j in range(d_ff // ck):
        cols = slice(j * ck, (j + 1) * ck)
        g = jnp.dot(h, win_ref[:, cols], preferred_element_type=F32)
        u = jnp.dot(h, win_ref[:, d_ff + j * ck:d_ff + (j + 1) * ck], preferred_element_type=F32)
        gbuf[0:8, :] = carry[:, cols]
        gbuf[8:8 + tf, :] = g
        act[:, cols] = _conv_gate(g, u, gbuf[7:7 + tf, :], gbuf[6:6 + tf, :], cw_ref, cb_ref, cols)
        carry[:, cols] = gbuf[tf:tf + 8, :]
    f = jnp.dot(act[...], wout_ref[...], preferred_element_type=F32)
    y_ref[...] = x + _rms(f, gpost_ref[...])
    cs_ref[0] = carry[...]


def _ffn_paged(x2, gpre, win_bf, cw, cb, wout_bf, gpost, batch, seq,
               page_table, lamv, dq, dk, dv, cache_kt, cache_v, n_heads, lam_i):
    m, d = x2.shape
    d_ff = wout_bf.shape[0]
    tf = FFN_TILE
    nt = seq // tf
    n, w = dq.shape
    rows = n // (batch * nt)
    assert rows * batch * nt == n and rows % 2 == 0, "sample rows must split evenly, in pairs, over the FFN steps"
    n_pages = page_table.shape[1]
    page = cache_kt.shape[2]
    assert n_pages % PAGE_GROUP == 0
    tile = lambda b, t, pt: (b * nt + t, 0)
    const = _const_spec
    rspec = const((n, w))
    body = functools.partial(_ffn_body, tf=tf, d_ff=d_ff, ck=FFN_CHUNK, rows=rows, n_pages=n_pages, page=page,
                             n_heads=n_heads, lam_i=lam_i)
    grid_spec = pltpu.PrefetchScalarGridSpec(
        num_scalar_prefetch=1,
        grid=(batch, nt),
        in_specs=[pl.BlockSpec((tf, d), tile), const((1, d)), const(win_bf.shape), const(cw.shape),
                  const(cb.shape), const(wout_bf.shape), const((1, d)),
                  const(lamv.shape), rspec, rspec, rspec,
                  pl.BlockSpec(memory_space=pl.ANY), pl.BlockSpec(memory_space=pl.ANY)],
        out_specs=[pl.BlockSpec((tf, d), tile), pl.BlockSpec((1, 8, d_ff), lambda b, t, pt: (b, 0, 0)),
                   pl.BlockSpec((n, w), lambda b, t, pt: (0, 0))],
        scratch_shapes=[pltpu.VMEM((tf + 8, FFN_CHUNK), F32), pltpu.VMEM((8, d_ff), F32),
                        pltpu.VMEM((tf, d_ff), BF16),
                        pltpu.VMEM((2, n_pages, w, page), F32),
                        pltpu.VMEM((2, n_pages, page * n_heads, HEAD_W), F32),
                        pltpu.SemaphoreType.DMA((2, 2)),
                        pltpu.VMEM((n_pages // PAGE_GROUP, 2 * n_heads, PAGE_GROUP * page), F32)],
    )
    y, cs, o = pl.pallas_call(
        body,
        grid_spec=grid_spec,
        out_shape=[jax.ShapeDtypeStruct((m, d), F32), jax.ShapeDtypeStruct((batch, 8, d_ff), F32),
                   jax.ShapeDtypeStruct((n, w), F32)],
        compiler_params=_params("arbitrary", "arbitrary"),
        name="conv_ffn_paged_attn",
    )(page_table.T.reshape(-1), x2, gpre, win_bf, cw, cb, wout_bf, gpost,
      lamv, dq.astype(F32), dk, dv, cache_kt, cache_v)
    return y, cs, o


def _ret_decode_rows(rq_ref, rk_ref, rv_ref, rg_ref, s_ref, a_ref, sn_ref, *, grp, n_heads):
    row = lax.broadcasted_iota(jnp.int32, (grp, grp * HEAD_W), 0)
    lane = lax.broadcasted_iota(jnp.int32, (grp, grp * HEAD_W), 1)
    own = (lane // HEAD_W) == row

    def block_diag(x):
        return jnp.where(own, jnp.tile(x, (1, grp)), 0.0).astype(BF16)

    for h in range(n_heads):
        cols = slice(h * HEAD_W, (h + 1) * HEAD_W)
        gamma = math.exp(_ret_log_decay(h))
        q = rq_ref[:, cols]
        k = rk_ref[:, cols]
        v = rv_ref[:, cols]
        qb = q.astype(BF16).astype(F32)
        kb = k.astype(BF16).astype(F32)
        att = jnp.sum(qb * kb, axis=-1, keepdims=True).astype(BF16).astype(F32)
        s = s_ref[:, h].reshape(grp * HEAD_W, HEAD_W)
        o = att * v.astype(F32) + jnp.dot(block_diag(q * gamma), s.astype(BF16),
                                          preferred_element_type=F32)
        upd = lax.dot_general(block_diag(k), v, TN_DIMS, preferred_element_type=F32)
        sn_ref[:, h] = (gamma * s + upd).reshape(grp, HEAD_W, HEAD_W)
        a_ref[:, cols] = _gated_group_norm(o, rg_ref[:, cols]).astype(BF16)


def _page_copies(pt_ref, ck_hbm, cv_hbm, kbuf, vbuf, sem, b, slot, n_rows, n_pages):
    out = []
    for p in range(n_pages):
        page = pt_ref[p * n_rows + b]
        out.append(pltpu.make_async_copy(ck_hbm.at[page], kbuf.at[slot, p], sem.at[0, slot]))
        out.append(pltpu.make_async_copy(cv_hbm.at[page], vbuf.at[slot, p], sem.at[1, slot]))
    return out


def _paged_row(slot, q, kn_row, vn_row, lam, kbuf, vbuf, sc_scr, *, n_pages, page, n_heads):
    w = n_heads * HEAD_W
    nc = 2 * n_heads
    past = n_pages * page
    ci = lax.broadcasted_iota(jnp.int32, (nc, w), 0)
    ri = lax.broadcasted_iota(jnp.int32, (nc, w), 1)
    col_of = ri // HEAD_W + n_heads * ((ri % HEAD_W) // (HEAD_W // 2))
    q_sel = jnp.where(ci == col_of, jnp.broadcast_to(q, (nc, w)), 0.0)
    q_sel_bf = q_sel.astype(BF16)
    grp = PAGE_GROUP
    span = grp * page
    rowc = lax.broadcasted_iota(jnp.int32, (nc, span), 0)
    key_i = lax.broadcasted_iota(jnp.int32, (nc, span), 1)
    slope = jnp.zeros((nc, span), F32)
    for h in range(n_heads):
        slope = jnp.where(rowc % n_heads == h, 2.0 ** (-8.0 / n_heads * (h + 1)), slope)

    mx = jnp.full((nc, span), FINITE_MIN, F32)
    for pg in range(n_pages // grp):
        kt = jnp.concatenate([kbuf[slot, grp * pg + i] for i in range(grp)], axis=1).astype(BF16)
        dist = (past - (pg * span + key_i)).astype(F32)
        s = jnp.dot(q_sel_bf, kt, preferred_element_type=F32) - slope * dist
        sc_scr[pg] = s
        mx = jnp.maximum(mx, s)
    kn = kn_row.astype(BF16).astype(F32)
    s_self = jnp.broadcast_to(jnp.sum(q_sel * kn, axis=-1, keepdims=True), (nc, span))
    m = jnp.maximum(jnp.broadcast_to(jnp.max(mx, axis=-1, keepdims=True), (nc, span)), s_self)

    lp = jnp.zeros((nc, span), F32)
    for pg in range(n_pages // grp):
        e = jnp.exp(sc_scr[pg] - m)
        sc_scr[pg] = e
        lp = lp + e
    e_self = jnp.exp(s_self - m)
    l = jnp.broadcast_to(jnp.sum(lp, axis=-1, keepdims=True), (nc, span)) + e_self
    r = jnp.where(rowc < n_heads, 1.0, lam) / l

    def combine(pn):
        return (pn - pltpu.roll(pn, n_heads, axis=0)).astype(BF16)

    def v_rows(p):
        return jnp.concatenate([vbuf[slot, p, pl.ds(h, page, stride=n_heads), :] for h in range(n_heads)],
                               axis=1)

    acc = jnp.zeros((nc, w), F32)
    for pg in range(n_pages // grp):
        v = jnp.concatenate([v_rows(grp * pg + i) for i in range(grp)], axis=0).astype(BF16)
        acc = acc + jnp.dot(combine(sc_scr[pg] * r), v, preferred_element_type=F32)
    a_self = combine(e_self * r).astype(F32)
    vn = vn_row.astype(BF16).astype(F32)
    return jnp.concatenate(
        [acc[h:h + 1, h * HEAD_W:(h + 1) * HEAD_W]
         + a_self[h:h + 1, 0:HEAD_W] * vn[:, h * HEAD_W:(h + 1) * HEAD_W] for h in range(n_heads)], axis=-1)


def _dec_ffn_body(a_ref, do_ref, x_ref, wo_ref, gmix_ref, subg_ref, gpre_ref, wg_ref, wu_ref, cw_ref,
                  cb_ref, sc_ref, wout_ref, gpost_ref, y_ref, g_ref, xmid, hbuf, acc, *, lam_i):
    j = pl.program_id(0)
    ret_w = a_ref.shape[1]

    @pl.when(j == 0)
    def _():
        mo = jnp.dot(a_ref[...], wo_ref[0:ret_w, :], preferred_element_type=F32)
        for h in range(do_ref.shape[1] // HEAD_W):
            cols = slice(h * HEAD_W, (h + 1) * HEAD_W)
            bh = (_rms(do_ref[:, cols], subg_ref[...]) * (1.0 - lam_i)).astype(BF16)
            mo += jnp.dot(bh, wo_ref[ret_w + h * HEAD_W:ret_w + (h + 1) * HEAD_W, :],
                          preferred_element_type=F32)
        xm = x_ref[...] + _rms(mo, gmix_ref[...])
        xmid[...] = xm
        hbuf[...] = _rms(xm, gpre_ref[...]).astype(BF16)
        acc[...] = jnp.zeros_like(acc)

    h = hbuf[...]
    g = jnp.dot(h, wg_ref[...], preferred_element_type=F32)
    u = jnp.dot(h, wu_ref[...], preferred_element_type=F32)
    g_ref[...] = g
    c = cw_ref[0:1, :] * sc_ref[0] + cw_ref[1:2, :] * sc_ref[1] + cw_ref[2:3, :] * g + cb_ref[...]
    acc[...] += jnp.dot((jax.nn.gelu(c) * u).astype(BF16), wout_ref[...], preferred_element_type=F32)

    @pl.when(j == pl.num_programs(0) - 1)
    def _():
        y_ref[...] = xmid[...] + _rms(acc[...], gpost_ref[...])


def _dec_ffn(a, do, x2, wo_bf, gmix, subg, gpre, win_bf, cw, cb, conv_state_t, wout_bf, gpost, lam_i):
    n, d = x2.shape
    d_ff = wout_bf.shape[0]
    ck = d_ff // DEC_FFN_STEPS
    assert ck * DEC_FFN_STEPS == d_ff and ck % HEAD_W == 0
    nj = DEC_FFN_STEPS
    full = lambda shape: pl.BlockSpec(shape, lambda j: (0,) * len(shape))
    return pl.pallas_call(
        functools.partial(_dec_ffn_body, lam_i=lam_i),
        grid=(nj,),
        in_specs=[full(a.shape), full(do.shape), full(x2.shape), full(wo_bf.shape), full((1, d)),
                  full((1, HEAD_W)), full((1, d)),
                  pl.BlockSpec((d, ck), lambda j: (0, j)), pl.BlockSpec((d, ck), lambda j: (0, nj + j)),
                  pl.BlockSpec((3, ck), lambda j: (0, j)), pl.BlockSpec((1, ck), lambda j: (0, j)),
                  pl.BlockSpec((2, n, ck), lambda j: (0, 0, j)),
                  pl.BlockSpec((ck, d), lambda j: (j, 0)), full((1, d))],
        out_specs=[full((n, d)), pl.BlockSpec((n, ck), lambda j: (0, j))],
        out_shape=[jax.ShapeDtypeStruct((n, d), F32), jax.ShapeDtypeStruct((n, d_ff), F32)],
        scratch_shapes=[pltpu.VMEM((n, d), F32), pltpu.VMEM((n, d), BF16), pltpu.VMEM((n, d), F32)],
        compiler_params=_params("arbitrary"),
        name="dec_ffn",
    )(a, do, x2, wo_bf, gmix, subg, gpre, win_bf, win_bf, cw, cb, conv_state_t, wout_bf, gpost)


def kernel(x_prompt, x_sample, state_ret, cache_k, cache_v, state_conv, page_table,
           norm_mix_pre, norm_mix_post, w_in, w_o, lambda_q1, lambda_k1, lambda_q2, lambda_k2,
           subln_g, norm_ffn_pre, norm_ffn_post, w_ffn_in, conv_w, conv_b, w_ffn_out):
    batch, seq, d = x_prompt.shape
    n_dec = x_sample.shape[0]
    assert x_sample.shape[1] == 1, "the sample group is one token per row"
    depth = w_in.shape[0]
    n_heads = state_ret.shape[2]
    dk_ret = state_ret.shape[3]
    dh_diff = cache_k.shape[-1]
    d_ff = w_ffn_out.shape[1]
    w = n_heads * HEAD_W
    assert w_in.shape[2] == 7 * w and cache_v.shape[-1] == HEAD_W and 2 * dh_diff == HEAD_W
    assert seq % MIX_TILE == 0 and seq % FFN_TILE == 0 and d_ff % FFN_CHUNK == 0
    assert n_dec % DEC_GROUP == 0 and n_dec % 128 == 0 and seq % IN_TILE == 0 and IN_TILE % MIX_TILE == 0

    xp = x_prompt.reshape(batch * seq, d)
    xs = x_sample.reshape(n_dec, d)
    outs = [[] for _ in range(8)]
    for l in range(depth):
        lam_i = 0.8 - 0.6 * math.exp(-0.3 * l)
        lamv = jnp.stack([lambda_q1[l], lambda_k1[l], lambda_q2[l], lambda_k2[l]]).astype(F32)
        row = lambda v: v.reshape(1, -1)
        w_in_bf = w_in[l].astype(BF16)
        g_pre, g_post = row(norm_mix_pre[l]), row(norm_mix_post[l])
        f_pre, f_post = row(norm_ffn_pre[l]), row(norm_ffn_post[l])
        subg = row(subln_g[l])
        cw, cb = conv_w[l], row(conv_b[l])

        cache_kt = jnp.transpose(cache_k[l], (0, 2, 3, 4, 1)).reshape(cache_k.shape[1], w, cache_k.shape[2])
        cache_vr = cache_v[l].reshape(cache_v.shape[1], cache_v.shape[2] * n_heads, HEAD_W)
        proj = functools.partial(_inproj, n_heads=n_heads, dk_ret=dk_ret, dh_diff=dh_diff)

        srq, srk, srv, srg, dq, skt, sv4, dk, dv = proj(xs, g_pre, w_in_bf, 1, n_dec, n_dec, key_block=None)
        (rq, rk, rv, rg, pdq, kt, v4, ktb, dvb, w_o_bf, w_ffn_in_bf, w_ffn_out_bf) = proj(
            xp, g_pre, w_in_bf, batch, seq, IN_TILE, key_block=MIX_TILE,
            riders=(w_o[l], w_ffn_in[l], w_ffn_out[l]))
        xp, s_fin, a, s_new = _mixer(lamv, rq, rk, rv, rg, pdq, ktb, dvb, xp, w_o_bf, g_post, subg,
                                     batch, seq, n_heads, lam_i, srq, srk, srv, srg, state_ret[l])
        outs[0].append(s_fin)
        outs[2].append(jnp.transpose(kt.reshape(batch, n_heads, 2, dh_diff, seq), (0, 4, 1, 2, 3)))
        outs[3].append(v4.reshape(batch, seq, n_heads, HEAD_W))

        xp, cs, do = _ffn_paged(xp, f_pre, w_ffn_in_bf, cw, cb, w_ffn_out_bf, f_post, batch, seq,
                                page_table, lamv, dq, dk, dv, cache_kt, cache_vr, n_heads, lam_i)
        outs[6].append(cs[:, 6:8, :])

        conv_t = jnp.swapaxes(state_conv[l], 0, 1)
        xs, g_new = _dec_ffn(a, do, xs, w_o_bf, g_post, subg, f_pre, w_ffn_in_bf, cw, cb, conv_t,
                             w_ffn_out_bf, f_post, lam_i)
        outs[1].append(s_new)
        outs[4].append(jnp.transpose(skt.reshape(n_heads, 2, dh_diff, n_dec), (3, 0, 1, 2))[:, None])
        outs[5].append(sv4.reshape(n_dec, 1, n_heads, HEAD_W))
        outs[7].append(jnp.stack([state_conv[l][:, 1, :], g_new], axis=1))

    st = [jnp.stack(o) for o in outs]
    return (xp.reshape(batch, seq, d), xs.reshape(n_dec, 1, d), st[0], st[1], st[2], st[3], st[4], st[5],
            st[6], st[7])
```

```python
import functools
import math

import jax
import jax.numpy as jnp
from jax import lax
from jax.experimental import pallas as pl
from jax.experimental.pallas import tpu as pltpu

F32 = jnp.float32
BF16 = jnp.bfloat16
EPS = 1e-6
NEG_INF = -1e30
FINITE_MIN = -3e38
LOG2_E = math.log2(math.e)

V7X_VMEM_BYTES = 64 * 1024 * 1024
VMEM_LIMIT_BYTES = V7X_VMEM_BYTES * 7 // 8

RET_CHUNK = 256
HEAD_W = 128
IN_TILE = 1024
MIX_TILE = 256
FFN_TILE = 256
FFN_CHUNK = 256
DEC_GROUP = 8
DEC_FFN_STEPS = 2
PAGE_GROUP = 2
NT_DIMS = (((1,), (1,)), ((), ()))
TN_DIMS = (((0,), (0,)), ((), ()))


def _rms(x, g):
    return x * lax.rsqrt(jnp.mean(x * x, axis=-1, keepdims=True) + EPS) * g


def _params(*sem):
    return pltpu.CompilerParams(dimension_semantics=sem, vmem_limit_bytes=VMEM_LIMIT_BYTES)


def _const_spec(shape):
    n = len(shape)
    return pl.BlockSpec(shape, lambda *_: (0,) * n, pipeline_mode=pl.Buffered(1))


def _ret_log_decay(h):
    return math.log(1.0 - 2.0 ** (-5.0 - h))


def _lambda(lam_ref, lam_i):
    a = jnp.sum(lam_ref[0:1, :] * lam_ref[1:2, :], axis=-1, keepdims=True)
    b = jnp.sum(lam_ref[2:3, :] * lam_ref[3:4, :], axis=-1, keepdims=True)
    return jnp.exp(a) - jnp.exp(b) + lam_i


def _inproj_body(x_ref, g_ref, w_ref, *rest, n_riders, width, n_heads, k_scale, q_scale, key_block):
    rider_in, rest = rest[:n_riders], rest[n_riders:]
    rq_ref, rk_ref, rv_ref, rg_ref, dq_ref, kt_ref, v4_ref, kx_ref, vx_ref = rest[:9]
    for src, dst in zip(rider_in, rest[9:]):
        dst[...] = src[...].astype(BF16)
    if w_ref.dtype == F32:
        w_bf_ref = rest[9 + n_riders]
        w_bf_ref[...] = w_ref[...].astype(BF16)
    else:
        w_bf_ref = w_ref
    h = _rms(x_ref[...], g_ref[...]).astype(BF16)

    def col(j):
        return jnp.dot(h, w_bf_ref[:, j * width:(j + 1) * width], preferred_element_type=F32)

    dk = col(5)
    kt = dk.T
    kt_ref[0] = kt
    if key_block is None:
        kx_ref[...] = dk
    else:
        for c in range(kx_ref.shape[0]):
            kx_ref[c] = kt[:, c * key_block:(c + 1) * key_block].astype(BF16)
    dv = col(6)
    for hh in range(n_heads):
        v4_ref[:, hh, :] = dv[:, hh * HEAD_W:(hh + 1) * HEAD_W]
    vx_ref[...] = dv if key_block is None else dv.astype(BF16)
    rq_ref[...] = col(0)
    rk_ref[...] = col(1) * k_scale
    rv_ref[...] = col(2).astype(BF16)
    rg_ref[...] = col(3)
    dq_ref[...] = (col(4) * q_scale).astype(BF16)


def _inproj(x2, g, w, batch, seq, tile, n_heads, dk_ret, dh_diff, key_block, riders=()):
    m, d = x2.shape
    width = n_heads * HEAD_W
    nt = seq // tile
    steps = m // tile
    row = lambda i: (i, 0)
    w_out_specs, w_out_shapes = [], []
    if w.dtype == F32:
        assert steps == 1
        w_out_specs = [pl.BlockSpec(w.shape, lambda i: (0, 0))]
        w_out_shapes = [jax.ShapeDtypeStruct(w.shape, BF16)]
    assert all(r.shape[0] % (16 * steps) == 0 for r in riders), "rider row blocks must be bf16-tile aligned"
    rider_specs = [pl.BlockSpec((r.shape[0] // steps, r.shape[1]), row) for r in riders]
    f32_out = jax.ShapeDtypeStruct((m, width), F32)
    bf_out = jax.ShapeDtypeStruct((m, width), BF16)
    spec = pl.BlockSpec((tile, width), row)
    if key_block is None:
        kx_shape, kx_spec, vx_shape = f32_out, spec, f32_out
    else:
        kx_shape = jax.ShapeDtypeStruct((m // key_block, width, key_block), BF16)
        kx_spec = pl.BlockSpec((tile // key_block, width, key_block), lambda i: (i, 0, 0))
        vx_shape = bf_out
    body = functools.partial(_inproj_body, n_riders=len(riders), width=width, n_heads=n_heads,
                             k_scale=dk_ret ** -0.5, q_scale=dh_diff ** -0.5, key_block=key_block)
    return pl.pallas_call(
        body,
        grid=(steps,),
        in_specs=[pl.BlockSpec((tile, d), row), _const_spec((1, d)), _const_spec(w.shape)] + rider_specs,
        out_specs=[spec] * 5 + [pl.BlockSpec((1, width, tile), lambda i: (i // nt, 0, i % nt)),
                                pl.BlockSpec((tile, n_heads, HEAD_W), lambda i: (i, 0, 0)), kx_spec, spec]
        + rider_specs + w_out_specs,
        out_shape=[f32_out, f32_out, bf_out, f32_out, bf_out,
                   jax.ShapeDtypeStruct((batch, width, seq), F32),
                   jax.ShapeDtypeStruct((m, n_heads, HEAD_W), F32), kx_shape, vx_shape]
        + [jax.ShapeDtypeStruct(r.shape, BF16) for r in riders] + w_out_shapes,
        compiler_params=_params("arbitrary"),
        name="inproj",
    )(x2, g, w, *riders)


def _retention_decay(length, h):
    i = lax.broadcasted_iota(jnp.int32, (length, length), 0)
    j = lax.broadcasted_iota(jnp.int32, (length, length), 1)
    diff = (i - j).astype(F32)
    causal = diff >= 0
    return jnp.where(causal, jnp.exp(jnp.where(causal, diff, 0.0) * _ret_log_decay(h)), 0.0)


def _retention_row_decays(length, h):
    lg = _ret_log_decay(h)
    ri = lax.broadcasted_iota(jnp.int32, (length, 1), 0).astype(F32)
    return jnp.exp((ri + 1.0) * lg), jnp.exp((length - 1.0 - ri) * lg), math.exp(length * lg)


def _pair_loop(n, body):
    pairs = lax.shift_right_logical(n, 1)

    def two(i, carry):
        body(2 * i)
        body(2 * i + 1)
        return carry

    def one(i, carry):
        body(i)
        return carry

    lax.fori_loop(0, pairs, two, 0)
    lax.fori_loop(2 * pairs, n, one, 0)


def _gated_group_norm(o, g):
    on = o * lax.rsqrt(jnp.mean(o * o, axis=-1, keepdims=True) + EPS)
    return on * (g * jax.nn.sigmoid(g))


def _mixer_body(lam_ref, rq_ref, rk_ref, rv_ref, rg_ref, dq_ref, kt_ref, vb_ref, x_ref, wo_ref,
                gpost_ref, subg_ref, srq_ref, srk_ref, srv_ref, srg_ref, sst_ref,
                y_ref, s_ref, sa_ref, ssn_ref,
                mix_scr, qq_scr, sc_scr, mx_scr, l_scr, acc_scr, decay_scr,
                *, tq, ret_chunk, n_heads, lam_i, dec_period):
    t = pl.program_id(1)
    ret_w = n_heads * HEAD_W

    @pl.when((pl.program_id(0) * pl.num_programs(1) + t) % dec_period == 0)
    def _():
        _ret_decode_rows(srq_ref, srk_ref, srv_ref, srg_ref, sst_ref, sa_ref, ssn_ref,
                         grp=sa_ref.shape[0], n_heads=n_heads)

    @pl.when(t == 0)
    def _():
        s_ref[...] = jnp.zeros_like(s_ref)

    @pl.when((pl.program_id(0) == 0) & (t == 0))
    def _():
        for h in range(n_heads):
            decay_scr[h] = _retention_decay(ret_chunk, h)

    for c in range(tq // ret_chunk):
        rows = slice(c * ret_chunk, (c + 1) * ret_chunk)
        for h in range(n_heads):
            cols = slice(h * HEAD_W, (h + 1) * HEAD_W)
            decay = decay_scr[h]
            qdec, kdec, gl = _retention_row_decays(ret_chunk, h)
            q = rq_ref[rows, cols]
            k = rk_ref[rows, cols]
            v = rv_ref[rows, cols]
            s = s_ref[0, h]
            att = lax.dot_general(q.astype(BF16), k.astype(BF16), NT_DIMS,
                                  preferred_element_type=F32) * decay
            o = (jnp.dot(att.astype(BF16), v, preferred_element_type=F32)
                 + jnp.dot((q * qdec).astype(BF16), s.astype(BF16), preferred_element_type=F32))
            s_ref[0, h] = gl * s + lax.dot_general((k * kdec).astype(BF16), v, TN_DIMS,
                                                   preferred_element_type=F32)
            mix_scr[rows, cols] = _gated_group_norm(o, rg_ref[rows, cols]).astype(BF16)

    lane = lax.broadcasted_iota(jnp.int32, (tq, HEAD_W), 1)
    for h in range(n_heads):
        qh = dq_ref[:, h * HEAD_W:(h + 1) * HEAD_W]
        zero = jnp.zeros_like(qh)
        qq_scr[h, 0:tq] = jnp.where(lane < HEAD_W // 2, qh, zero)
        qq_scr[h, tq:2 * tq] = jnp.where(lane >= HEAD_W // 2, qh, zero)
    mx_scr[...] = jnp.full(mx_scr.shape, FINITE_MIN, F32)
    l_scr[...] = jnp.zeros_like(l_scr)
    acc_scr[...] = jnp.zeros_like(acc_scr)
    key_j = lax.broadcasted_iota(jnp.int32, (1, tq), 1).astype(F32)
    lane_tiles = [slice(c * HEAD_W, (c + 1) * HEAD_W) for c in range(tq // HEAD_W)]

    def fold(x, op):
        return functools.reduce(op, [x[:, c] for c in lane_tiles])

    def scores(kb, causal):
        key_pos = key_j + ((kb - t) * tq).astype(F32)
        for h in range(n_heads):
            slope = 2.0 ** (-8.0 / n_heads * (h + 1))
            s = jnp.dot(qq_scr[h], kt_ref[kb, h * HEAD_W:(h + 1) * HEAD_W, :], preferred_element_type=F32)
            s = (s + slope * key_pos) * LOG2_E
            if causal is not None:
                s = jnp.where(causal, s, NEG_INF)
            sc_scr[kb, h] = s
            mx_scr[h] = jnp.maximum(mx_scr[h], fold(s, jnp.maximum))

    _pair_loop(t, lambda kb: scores(kb, None))
    ii = lax.broadcasted_iota(jnp.int32, (2 * tq, tq), 0)
    jj = lax.broadcasted_iota(jnp.int32, (2 * tq, tq), 1)
    scores(t, jnp.where(ii >= tq, ii - tq, ii) >= jj)

    for h in range(n_heads):
        mx_scr[h] = jnp.broadcast_to(jnp.max(mx_scr[h], axis=-1, keepdims=True), (2 * tq, HEAD_W))

    def weighted(kb):
        for h in range(n_heads):
            m = mx_scr[h]
            e = [jnp.exp2(sc_scr[kb, h, :, c] - m) for c in lane_tiles]
            l_scr[h] += functools.reduce(jnp.add, e)
            vr = vb_ref[pl.ds(pl.multiple_of(kb * tq, tq), tq), h * HEAD_W:(h + 1) * HEAD_W]
            acc_scr[h] += jnp.dot(jnp.concatenate([x.astype(BF16) for x in e], axis=1), vr,
                                  preferred_element_type=F32)

    _pair_loop(t + 1, weighted)
    lam = _lambda(lam_ref, lam_i)
    subg = subg_ref[...]
    for h in range(n_heads):
        on = acc_scr[h] / jnp.sum(l_scr[h], axis=-1, keepdims=True)
        o = _rms(on[:tq] - lam * on[tq:], subg) * (1.0 - lam_i)
        mix_scr[:, ret_w + h * HEAD_W:ret_w + (h + 1) * HEAD_W] = o.astype(BF16)

    mo = jnp.dot(mix_scr[...], wo_ref[...], preferred_element_type=F32)
    y_ref[...] = x_ref[...] + _rms(mo, gpost_ref[...])


def _mixer(lamv, rq, rk, rv, rg, dq, ktb, dvb, x2, wo_bf, gpost, subg, batch, seq, n_heads, lam_i,
           srq, srk, srv, srg, sstate):
    m, d = x2.shape
    w = rq.shape[1]
    tq = MIX_TILE
    nt = seq // tq
    n_dec = srq.shape[0]
    grp = DEC_GROUP
    dec_period = batch * nt * grp // n_dec
    assert dec_period * n_dec == batch * nt * grp, "sample row groups must spread evenly over the mixer steps"
    tile = lambda b, t: (b * nt + t, 0)
    whole = lambda b, t: (b, 0)
    dec_rows = pl.BlockSpec((grp, w), lambda b, t: ((b * nt + t) // dec_period, 0))
    dec_state = pl.BlockSpec((grp, n_heads, HEAD_W, HEAD_W), lambda b, t: ((b * nt + t) // dec_period, 0, 0, 0))
    body = functools.partial(_mixer_body, tq=tq, ret_chunk=RET_CHUNK, n_heads=n_heads, lam_i=lam_i,
                             dec_period=dec_period)
    return pl.pallas_call(
        body,
        grid=(batch, nt),
        in_specs=[
            _const_spec(lamv.shape),
            pl.BlockSpec((tq, w), tile), pl.BlockSpec((tq, w), tile), pl.BlockSpec((tq, w), tile),
            pl.BlockSpec((tq, w), tile), pl.BlockSpec((tq, w), tile),
            pl.BlockSpec((nt, w, tq), lambda b, t: (b, 0, 0)), pl.BlockSpec((seq, w), whole),
            pl.BlockSpec((tq, d), tile),
            _const_spec(wo_bf.shape), _const_spec((1, d)), _const_spec((1, HEAD_W)),
            dec_rows, dec_rows, dec_rows, dec_rows, dec_state,
        ],
        out_specs=[pl.BlockSpec((tq, d), tile),
                   pl.BlockSpec((1, n_heads, HEAD_W, HEAD_W), lambda b, t: (b, 0, 0, 0)),
                   dec_rows, dec_state],
        out_shape=[jax.ShapeDtypeStruct((m, d), F32),
                   jax.ShapeDtypeStruct((batch, n_heads, HEAD_W, HEAD_W), F32),
                   jax.ShapeDtypeStruct((n_dec, w), BF16), jax.ShapeDtypeStruct(sstate.shape, F32)],
        scratch_shapes=[
            pltpu.VMEM((tq, 2 * w), BF16),
            pltpu.VMEM((n_heads, 2 * tq, HEAD_W), BF16),
            pltpu.VMEM((nt, n_heads, 2 * tq, tq), F32),
            pltpu.VMEM((n_heads, 2 * tq, HEAD_W), F32),
            pltpu.VMEM((n_heads, 2 * tq, HEAD_W), F32),
            pltpu.VMEM((n_heads, 2 * tq, HEAD_W), F32),
            pltpu.VMEM((n_heads, RET_CHUNK, RET_CHUNK), F32),
        ],
        compiler_params=_params("arbitrary", "arbitrary"),
        name="mixer",
    )(lamv, rq, rk, rv, rg, dq, ktb, dvb, x2, wo_bf, gpost, subg, srq, srk, srv, srg, sstate)


def _conv_gate(g, u, g1, g2, cw_ref, cb_ref, cols):
    c = cw_ref[0:1, cols] * g2 + cw_ref[1:2, cols] * g1 + cw_ref[2:3, cols] * g + cb_ref[:, cols]
    return (jax.nn.gelu(c) * u).astype(BF16)


def _ffn_body(pt_ref, x_ref, gpre_ref, win_ref, cw_ref, cb_ref, wout_ref, gpost_ref,
              lam_ref, q_ref, kn_ref, vn_ref, ck_hbm, cv_hbm, y_ref, cs_ref, o_ref,
              gbuf, carry, act, kbuf, vbuf, sem, sc_scr, *, tf, d_ff, ck, rows, n_pages, page, n_heads, lam_i):
    t = pl.program_id(1)
    step = pl.program_id(0) * pl.num_programs(1) + t
    n_rows = rows * pl.num_programs(0) * pl.num_programs(1)
    copies = functools.partial(_page_copies, pt_ref, ck_hbm, cv_hbm, kbuf, vbuf, sem, n_rows=q_ref.shape[0],
                               n_pages=n_pages)

    @pl.when(step == 0)
    def _():
        for slot in range(2):
            for c in copies(b=slot, slot=slot):
                c.start()

    @pl.when(t == 0)
    def _():
        carry[...] = jnp.zeros_like(carry)

    lam = _lambda(lam_ref, lam_i)
    for i in range(rows):
        g = step * rows + i
        slot = i % 2
        for c in copies(b=g, slot=slot):
            c.wait()
        this = pl.ds(g, 1)
        o_ref[this, :] = _paged_row(slot, q_ref[this, :], kn_ref[this, :], vn_ref[this, :], lam, kbuf, vbuf,
                                    sc_scr, n_pages=n_pages, page=page, n_heads=n_heads)

        @pl.when(g + 2 < n_rows)
        def _():
            for c in copies(b=g + 2, slot=slot):
                c.start()

    x = x_ref[...]
    h = _rms(x, gpre_ref[...]).astype(BF16)
    for j in range(d_ff // ck):
        cols = slice(j * ck, (j + 1) * ck)
        g = jnp.dot(h, win_ref[:, cols], preferred_element_type=F32)
        u = jnp.dot(h, win_ref[:, d_ff + j * ck:d_ff + (j + 1) * ck], preferred_element_type=F32)
        gbuf[0:8, :] = carry[:, cols]
        gbuf[8:8 + tf, :] = g
        act[:, cols] = _conv_gate(g, u, gbuf[7:7 + tf, :], gbuf[6:6 + tf, :], cw_ref, cb_ref, cols)
        carry[:, cols] = gbuf[tf:tf + 8, :]
    f = jnp.dot(act[...], wout_ref[...], preferred_element_type=F32)
    y_ref[...] = x + _rms(f, gpost_ref[...])
    cs_ref[0] = carry[...]


def _ffn_paged(x2, gpre, win_bf, cw, cb, wout_bf, gpost, batch, seq,
               page_table, lamv, dq, dk, dv, cache_kt, cache_v, n_heads, lam_i):
    m, d = x2.shape
    d_ff = wout_bf.shape[0]
    tf = FFN_TILE
    nt = seq // tf
    n, w = dq.shape
    rows = n // (batch * nt)
    assert rows * batch * nt == n and rows % 2 == 0, "sample rows must split evenly, in pairs, over the FFN steps"
    n_pages = page_table.shape[1]
    page = cache_kt.shape[2]
    assert n_pages % PAGE_GROUP == 0
    tile = lambda b, t, pt: (b * nt + t, 0)
    const = _const_spec
    rspec = const((n, w))
    body = functools.partial(_ffn_body, tf=tf, d_ff=d_ff, ck=FFN_CHUNK, rows=rows, n_pages=n_pages, page=page,
                             n_heads=n_heads, lam_i=lam_i)
    grid_spec = pltpu.PrefetchScalarGridSpec(
        num_scalar_prefetch=1,
        grid=(batch, nt),
        in_specs=[pl.BlockSpec((tf, d), tile), const((1, d)), const(win_bf.shape), const(cw.shape),
                  const(cb.shape), const(wout_bf.shape), const((1, d)),
                  const(lamv.shape), rspec, rspec, rspec,
                  pl.BlockSpec(memory_space=pl.ANY), pl.BlockSpec(memory_space=pl.ANY)],
        out_specs=[pl.BlockSpec((tf, d), tile), pl.BlockSpec((1, 8, d_ff), lambda b, t, pt: (b, 0, 0)),
                   pl.BlockSpec((n, w), lambda b, t, pt: (0, 0))],
        scratch_shapes=[pltpu.VMEM((tf + 8, FFN_CHUNK), F32), pltpu.VMEM((8, d_ff), F32),
                        pltpu.VMEM((tf, d_ff), BF16),
                        pltpu.VMEM((2, n_pages, w, page), F32),
                        pltpu.VMEM((2, n_pages, page * n_heads, HEAD_W), F32),
                        pltpu.SemaphoreType.DMA((2, 2)),
                        pltpu.VMEM((n_pages // PAGE_GROUP, 2 * n_heads, PAGE_GROUP * page), F32)],
    )
    y, cs, o = pl.pallas_call(
        body,
        grid_spec=grid_spec,
        out_shape=[jax.ShapeDtypeStruct((m, d), F32), jax.ShapeDtypeStruct((batch, 8, d_ff), F32),
                   jax.ShapeDtypeStruct((n, w), F32)],
        compiler_params=_params("arbitrary", "arbitrary"),
        name="conv_ffn_paged_attn",
    )(page_table.T.reshape(-1), x2, gpre, win_bf, cw, cb, wout_bf, gpost,
      lamv, dq.astype(F32), dk, dv, cache_kt, cache_v)
    return y, cs, o


def _ret_decode_rows(rq_ref, rk_ref, rv_ref, rg_ref, s_ref, a_ref, sn_ref, *, grp, n_heads):
    row = lax.broadcasted_iota(jnp.int32, (grp, grp * HEAD_W), 0)
    lane = lax.broadcasted_iota(jnp.int32, (grp, grp * HEAD_W), 1)
    own = (lane // HEAD_W) == row

    def block_diag(x):
        return jnp.where(own, jnp.tile(x, (1, grp)), 0.0).astype(BF16)

    for h in range(n_heads):
        cols = slice(h * HEAD_W, (h + 1) * HEAD_W)
        gamma = math.exp(_ret_log_decay(h))
        q = rq_ref[:, cols]
        k = rk_ref[:, cols]
        v = rv_ref[:, cols]
        qb = q.astype(BF16).astype(F32)
        kb = k.astype(BF16).astype(F32)
        att = jnp.sum(qb * kb, axis=-1, keepdims=True).astype(BF16).astype(F32)
        s = s_ref[:, h].reshape(grp * HEAD_W, HEAD_W)
        o = att * v.astype(F32) + jnp.dot(block_diag(q * gamma), s.astype(BF16),
                                          preferred_element_type=F32)
        upd = lax.dot_general(block_diag(k), v, TN_DIMS, preferred_element_type=F32)
        sn_ref[:, h] = (gamma * s + upd).reshape(grp, HEAD_W, HEAD_W)
        a_ref[:, cols] = _gated_group_norm(o, rg_ref[:, cols]).astype(BF16)


def _page_copies(pt_ref, ck_hbm, cv_hbm, kbuf, vbuf, sem, b, slot, n_rows, n_pages):
    out = []
    for p in range(n_pages):
        page = pt_ref[p * n_rows + b]
        out.append(pltpu.make_async_copy(ck_hbm.at[page], kbuf.at[slot, p], sem.at[0, slot]))
        out.append(pltpu.make_async_copy(cv_hbm.at[page], vbuf.at[slot, p], sem.at[1, slot]))
    return out


def _paged_row(slot, q, kn_row, vn_row, lam, kbuf, vbuf, sc_scr, *, n_pages, page, n_heads):
    w = n_heads * HEAD_W
    nc = 2 * n_heads
    past = n_pages * page
    ci = lax.broadcasted_iota(jnp.int32, (nc, w), 0)
    ri = lax.broadcasted_iota(jnp.int32, (nc, w), 1)
    col_of = ri // HEAD_W + n_heads * ((ri % HEAD_W) // (HEAD_W // 2))
    q_sel = jnp.where(ci == col_of, jnp.broadcast_to(q, (nc, w)), 0.0)
    q_sel_bf = q_sel.astype(BF16)
    grp = PAGE_GROUP
    span = grp * page
    rowc = lax.broadcasted_iota(jnp.int32, (nc, span), 0)
    key_i = lax.broadcasted_iota(jnp.int32, (nc, span), 1)
    slope = jnp.zeros((nc, span), F32)
    for h in range(n_heads):
        slope = jnp.where(rowc % n_heads == h, 2.0 ** (-8.0 / n_heads * (h + 1)), slope)

    mx = jnp.full((nc, span), FINITE_MIN, F32)
    for pg in range(n_pages // grp):
        kt = jnp.concatenate([kbuf[slot, grp * pg + i] for i in range(grp)], axis=1).astype(BF16)
        dist = (past - (pg * span + key_i)).astype(F32)
        s = jnp.dot(q_sel_bf, kt, preferred_element_type=F32) - slope * dist
        sc_scr[pg] = s
        mx = jnp.maximum(mx, s)
    kn = kn_row.astype(BF16).astype(F32)
    s_self = jnp.broadcast_to(jnp.sum(q_sel * kn, axis=-1, keepdims=True), (nc, span))
    m = jnp.maximum(jnp.broadcast_to(jnp.max(mx, axis=-1, keepdims=True), (nc, span)), s_self)

    lp = jnp.zeros((nc, span), F32)
    for pg in range(n_pages // grp):
        e = jnp.exp(sc_scr[pg] - m)
        sc_scr[pg] = e
        lp = lp + e
    e_self = jnp.exp(s_self - m)
    l = jnp.broadcast_to(jnp.sum(lp, axis=-1, keepdims=True), (nc, span)) + e_self
    r = jnp.where(rowc < n_heads, 1.0, lam) / l

    def combine(pn):
        return (pn - pltpu.roll(pn, n_heads, axis=0)).astype(BF16)

    def v_rows(p):
        return jnp.concatenate([vbuf[slot, p, pl.ds(h, page, stride=n_heads), :] for h in range(n_heads)],
                               axis=1)

    acc = jnp.zeros((nc, w), F32)
    for pg in range(n_pages // grp):
        v = jnp.concatenate([v_rows(grp * pg + i) for i in range(grp)], axis=0).astype(BF16)
        acc = acc + jnp.dot(combine(sc_scr[pg] * r), v, preferred_element_type=F32)
    a_self = combine(e_self * r).astype(F32)
    vn = vn_row.astype(BF16).astype(F32)
    return jnp.concatenate(
        [acc[h:h + 1, h * HEAD_W:(h + 1) * HEAD_W]
         + a_self[h:h + 1, 0:HEAD_W] * vn[:, h * HEAD_W:(h + 1) * HEAD_W] for h in range(n_heads)], axis=-1)


def _dec_ffn_body(a_ref, do_ref, x_ref, wo_ref, gmix_ref, subg_ref, gpre_ref, wg_ref, wu_ref, cw_ref,
                  cb_ref, sc_ref, wout_ref, gpost_ref, y_ref, g_ref, xmid, hbuf, acc, *, lam_i):
    j = pl.program_id(0)
    ret_w = a_ref.shape[1]

    @pl.when(j == 0)
    def _():
        mo = jnp.dot(a_ref[...], wo_ref[0:ret_w, :], preferred_element_type=F32)
        for h in range(do_ref.shape[1] // HEAD_W):
            cols = slice(h * HEAD_W, (h + 1) * HEAD_W)
            bh = (_rms(do_ref[:, cols], subg_ref[...]) * (1.0 - lam_i)).astype(BF16)
            mo += jnp.dot(bh, wo_ref[ret_w + h * HEAD_W:ret_w + (h + 1) * HEAD_W, :],
                          preferred_element_type=F32)
        xm = x_ref[...] + _rms(mo, gmix_ref[...])
        xmid[...] = xm
        hbuf[...] = _rms(xm, gpre_ref[...]).astype(BF16)
        acc[...] = jnp.zeros_like(acc)

    h = hbuf[...]
    g = jnp.dot(h, wg_ref[...], preferred_element_type=F32)
    u = jnp.dot(h, wu_ref[...], preferred_element_type=F32)
    g_ref[...] = g
    c = cw_ref[0:1, :] * sc_ref[0] + cw_ref[1:2, :] * sc_ref[1] + cw_ref[2:3, :] * g + cb_ref[...]
    acc[...] += jnp.dot((jax.nn.gelu(c) * u).astype(BF16), wout_ref[...], preferred_element_type=F32)

    @pl.when(j == pl.num_programs(0) - 1)
    def _():
        y_ref[...] = xmid[...] + _rms(acc[...], gpost_ref[...])


def _dec_ffn(a, do, x2, wo_bf, gmix, subg, gpre, win_bf, cw, cb, conv_state_t, wout_bf, gpost, lam_i):
    n, d = x2.shape
    d_ff = wout_bf.shape[0]
    ck = d_ff // DEC_FFN_STEPS
    assert ck * DEC_FFN_STEPS == d_ff and ck % HEAD_W == 0
    nj = DEC_FFN_STEPS
    full = lambda shape: pl.BlockSpec(shape, lambda j: (0,) * len(shape))
    return pl.pallas_call(
        functools.partial(_dec_ffn_body, lam_i=lam_i),
        grid=(nj,),
        in_specs=[full(a.shape), full(do.shape), full(x2.shape), full(wo_bf.shape), full((1, d)),
                  full((1, HEAD_W)), full((1, d)),
                  pl.BlockSpec((d, ck), lambda j: (0, j)), pl.BlockSpec((d, ck), lambda j: (0, nj + j)),
                  pl.BlockSpec((3, ck), lambda j: (0, j)), pl.BlockSpec((1, ck), lambda j: (0, j)),
                  pl.BlockSpec((2, n, ck), lambda j: (0, 0, j)),
                  pl.BlockSpec((ck, d), lambda j: (j, 0)), full((1, d))],
        out_specs=[full((n, d)), pl.BlockSpec((n, ck), lambda j: (0, j))],
        out_shape=[jax.ShapeDtypeStruct((n, d), F32), jax.ShapeDtypeStruct((n, d_ff), F32)],
        scratch_shapes=[pltpu.VMEM((n, d), F32), pltpu.VMEM((n, d), BF16), pltpu.VMEM((n, d), F32)],
        compiler_params=_params("arbitrary"),
        name="dec_ffn",
    )(a, do, x2, wo_bf, gmix, subg, gpre, win_bf, win_bf, cw, cb, conv_state_t, wout_bf, gpost)


def kernel(x_prompt, x_sample, state_ret, cache_k, cache_v, state_conv, page_table,
           norm_mix_pre, norm_mix_post, w_in, w_o, lambda_q1, lambda_k1, lambda_q2, lambda_k2,
           subln_g, norm_ffn_pre, norm_ffn_post, w_ffn_in, conv_w, conv_b, w_ffn_out):
    batch, seq, d = x_prompt.shape
    n_dec = x_sample.shape[0]
    assert x_sample.shape[1] == 1, "the sample group is one token per row"
    depth = w_in.shape[0]
    n_heads = state_ret.shape[2]
    dk_ret = state_ret.shape[3]
    dh_diff = cache_k.shape[-1]
    d_ff = w_ffn_out.shape[1]
    w = n_heads * HEAD_W
    assert w_in.shape[2] == 7 * w and cache_v.shape[-1] == HEAD_W and 2 * dh_diff == HEAD_W
    assert seq % MIX_TILE == 0 and seq % FFN_TILE == 0 and d_ff % FFN_CHUNK == 0
    assert n_dec % DEC_GROUP == 0 and n_dec % 128 == 0 and seq % IN_TILE == 0 and IN_TILE % MIX_TILE == 0

    xp = x_prompt.reshape(batch * seq, d)
    xs = x_sample.reshape(n_dec, d)
    outs = [[] for _ in range(8)]
    for l in range(depth):
        lam_i = 0.8 - 0.6 * math.exp(-0.3 * l)
        lamv = jnp.stack([lambda_q1[l], lambda_k1[l], lambda_q2[l], lambda_k2[l]]).astype(F32)
        row = lambda v: v.reshape(1, -1)
        g_pre, g_post = row(norm_mix_pre[l]), row(norm_mix_post[l])
        f_pre, f_post = row(norm_ffn_pre[l]), row(norm_ffn_post[l])
        subg = row(subln_g[l])
        cw, cb = conv_w[l], row(conv_b[l])

        cache_kt = jnp.transpose(cache_k[l], (0, 2, 3, 4, 1)).reshape(cache_k.shape[1], w, cache_k.shape[2])
        cache_vr = cache_v[l].reshape(cache_v.shape[1], cache_v.shape[2] * n_heads, HEAD_W)
        proj = functools.partial(_inproj, n_heads=n_heads, dk_ret=dk_ret, dh_diff=dh_diff)

        srq, srk, srv, srg, dq, skt, sv4, dk, dv, w_in_bf = proj(xs, g_pre, w_in[l], 1, n_dec, n_dec,
                                                                 key_block=None)
        (rq, rk, rv, rg, pdq, kt, v4, ktb, dvb, w_o_bf, w_ffn_in_bf, w_ffn_out_bf) = proj(
            xp, g_pre, w_in_bf, batch, seq, IN_TILE, key_block=MIX_TILE,
            riders=(w_o[l], w_ffn_in[l], w_ffn_out[l]))
        xp, s_fin, a, s_new = _mixer(lamv, rq, rk, rv, rg, pdq, ktb, dvb, xp, w_o_bf, g_post, subg,
                                     batch, seq, n_heads, lam_i, srq, srk, srv, srg, state_ret[l])
        outs[0].append(s_fin)
        outs[2].append(jnp.transpose(kt.reshape(batch, n_heads, 2, dh_diff, seq), (0, 4, 1, 2, 3)))
        outs[3].append(v4.reshape(batch, seq, n_heads, HEAD_W))

        xp, cs, do = _ffn_paged(xp, f_pre, w_ffn_in_bf, cw, cb, w_ffn_out_bf, f_post, batch, seq,
                                page_table, lamv, dq, dk, dv, cache_kt, cache_vr, n_heads, lam_i)
        outs[6].append(cs[:, 6:8, :])

        conv_t = jnp.swapaxes(state_conv[l], 0, 1)
        xs, g_new = _dec_ffn(a, do, xs, w_o_bf, g_post, subg, f_pre, w_ffn_in_bf, cw, cb, conv_t,
                             w_ffn_out_bf, f_post, lam_i)
        outs[1].append(s_new)
        outs[4].append(jnp.transpose(skt.reshape(n_heads, 2, dh_diff, n_dec), (3, 0, 1, 2))[:, None])
        outs[5].append(sv4.reshape(n_dec, 1, n_heads, HEAD_W))
        outs[7].append(jnp.stack([state_conv[l][:, 1, :], g_new], axis=1))

    st = [jnp.stack(o) for o in outs]
    return (xp.reshape(batch, seq, d), xs.reshape(n_dec, 1, d), st[0], st[1], st[2], st[3], st[4], st[5],
            st[6], st[7])
```

```python
import functools
import math

import jax
import jax.numpy as jnp
from jax import lax
from jax.experimental import pallas as pl
from jax.experimental.pallas import tpu as pltpu

F32 = jnp.float32
BF16 = jnp.bfloat16
EPS = 1e-6
NEG_INF = -1e30
FINITE_MIN = -3e38
LOG2_E = math.log2(math.e)

V7X_VMEM_BYTES = 64 * 1024 * 1024
V7X_MXU_DIM = 256
V7X_LANES = 128
VMEM_LIMIT_BYTES = V7X_VMEM_BYTES * 7 // 8

HEAD_W = V7X_LANES
IN_TILE = 1024
MIX_TILE = V7X_MXU_DIM
RET_CHUNK = MIX_TILE
FFN_TILE = 256
FFN_CHUNK = V7X_MXU_DIM
DEC_GROUP = 8
DEC_FFN_STEPS = 2
NT_DIMS = (((1,), (1,)), ((), ()))
TN_DIMS = (((0,), (0,)), ((), ()))


def _page_group(page):
    return max(1, V7X_MXU_DIM // page)


def _rms(x, g):
    return x * lax.rsqrt(jnp.mean(x * x, axis=-1, keepdims=True) + EPS) * g


def _params(*sem):
    return pltpu.CompilerParams(dimension_semantics=sem, vmem_limit_bytes=VMEM_LIMIT_BYTES)


def _const_spec(shape):
    n = len(shape)
    return pl.BlockSpec(shape, lambda *_: (0,) * n, pipeline_mode=pl.Buffered(1))


def _ret_log_decay(h):
    return math.log(1.0 - 2.0 ** (-5.0 - h))


def _lambda(lam_ref, lam_i):
    a = jnp.sum(lam_ref[0:1, :] * lam_ref[1:2, :], axis=-1, keepdims=True)
    b = jnp.sum(lam_ref[2:3, :] * lam_ref[3:4, :], axis=-1, keepdims=True)
    return jnp.exp(a) - jnp.exp(b) + lam_i


def _inproj_body(x_ref, g_ref, w_ref, *rest, n_riders, width, n_heads, k_scale, q_scale, key_block):
    rider_in, rest = rest[:n_riders], rest[n_riders:]
    rq_ref, rk_ref, rv_ref, rg_ref, dq_ref, kt_ref, v4_ref, kx_ref, vx_ref = rest[:9]
    for src, dst in zip(rider_in, rest[9:]):
        dst[...] = src[...].astype(BF16)
    h = _rms(x_ref[...], g_ref[...]).astype(BF16)

    def col(j):
        return jnp.dot(h, w_ref[:, j * width:(j + 1) * width], preferred_element_type=F32)

    dk = col(5)
    kt = dk.T
    kt_ref[0] = kt
    if key_block is None:
        kx_ref[...] = dk
    else:
        for c in range(kx_ref.shape[0]):
            kx_ref[c] = kt[:, c * key_block:(c + 1) * key_block].astype(BF16)
    dv = col(6)
    for hh in range(n_heads):
        v4_ref[:, hh, :] = dv[:, hh * HEAD_W:(hh + 1) * HEAD_W]
    vx_ref[...] = dv if key_block is None else dv.astype(BF16)
    rq_ref[...] = col(0)
    rk_ref[...] = col(1) * k_scale
    rv_ref[...] = col(2).astype(BF16)
    rg_ref[...] = col(3)
    dq_ref[...] = (col(4) * q_scale).astype(BF16)


def _inproj(x2, g, w_bf, batch, seq, tile, n_heads, dk_ret, dh_diff, key_block, riders=()):
    m, d = x2.shape
    width = n_heads * HEAD_W
    nt = seq // tile
    steps = m // tile
    row = lambda i: (i, 0)
    assert all(r.shape[0] % (16 * steps) == 0 for r in riders), "rider row blocks must be bf16-tile aligned"
    rider_specs = [pl.BlockSpec((r.shape[0] // steps, r.shape[1]), row) for r in riders]
    f32_out = jax.ShapeDtypeStruct((m, width), F32)
    bf_out = jax.ShapeDtypeStruct((m, width), BF16)
    spec = pl.BlockSpec((tile, width), row)
    if key_block is None:
        kx_shape, kx_spec, vx_shape = f32_out, spec, f32_out
    else:
        kx_shape = jax.ShapeDtypeStruct((m // key_block, width, key_block), BF16)
        kx_spec = pl.BlockSpec((tile // key_block, width, key_block), lambda i: (i, 0, 0))
        vx_shape = bf_out
    body = functools.partial(_inproj_body, n_riders=len(riders), width=width, n_heads=n_heads,
                             k_scale=dk_ret ** -0.5, q_scale=dh_diff ** -0.5, key_block=key_block)
    return pl.pallas_call(
        body,
        grid=(steps,),
        in_specs=[pl.BlockSpec((tile, d), row), _const_spec((1, d)), _const_spec(w_bf.shape)] + rider_specs,
        out_specs=[spec] * 5 + [pl.BlockSpec((1, width, tile), lambda i: (i // nt, 0, i % nt)),
                                pl.BlockSpec((tile, n_heads, HEAD_W), lambda i: (i, 0, 0)), kx_spec, spec]
        + rider_specs,
        out_shape=[f32_out, f32_out, bf_out, f32_out, bf_out,
                   jax.ShapeDtypeStruct((batch, width, seq), F32),
                   jax.ShapeDtypeStruct((m, n_heads, HEAD_W), F32), kx_shape, vx_shape]
        + [jax.ShapeDtypeStruct(r.shape, BF16) for r in riders],
        compiler_params=_params("arbitrary"),
        name="inproj",
    )(x2, g, w_bf, *riders)


def _retention_decay(length, h):
    i = lax.broadcasted_iota(jnp.int32, (length, length), 0)
    j = lax.broadcasted_iota(jnp.int32, (length, length), 1)
    diff = (i - j).astype(F32)
    causal = diff >= 0
    return jnp.where(causal, jnp.exp(jnp.where(causal, diff, 0.0) * _ret_log_decay(h)), 0.0)


def _retention_row_decays(length, h):
    lg = _ret_log_decay(h)
    ri = lax.broadcasted_iota(jnp.int32, (length, 1), 0).astype(F32)
    return jnp.exp((ri + 1.0) * lg), jnp.exp((length - 1.0 - ri) * lg), math.exp(length * lg)


def _pair_loop(n, body):
    pairs = lax.shift_right_logical(n, 1)

    def two(i, carry):
        body(2 * i)
        body(2 * i + 1)
        return carry

    def one(i, carry):
        body(i)
        return carry

    lax.fori_loop(0, pairs, two, 0)
    lax.fori_loop(2 * pairs, n, one, 0)


def _gated_group_norm(o, g):
    on = o * lax.rsqrt(jnp.mean(o * o, axis=-1, keepdims=True) + EPS)
    return on * (g * jax.nn.sigmoid(g))


def _mixer_body(lam_ref, rq_ref, rk_ref, rv_ref, rg_ref, dq_ref, kt_ref, vb_ref, x_ref, wo_ref,
                gpost_ref, subg_ref, srq_ref, srk_ref, srv_ref, srg_ref, sst_ref,
                y_ref, s_ref, sa_ref, ssn_ref,
                mix_scr, qq_scr, sc_scr, mx_scr, acc_scr, decay_scr,
                *, tq, ret_chunk, n_heads, lam_i, dec_period):
    t = pl.program_id(1)
    ret_w = n_heads * HEAD_W

    @pl.when((pl.program_id(0) * pl.num_programs(1) + t) % dec_period == 0)
    def _():
        _ret_decode_rows(srq_ref, srk_ref, srv_ref, srg_ref, sst_ref, sa_ref, ssn_ref,
                         grp=sa_ref.shape[0], n_heads=n_heads)

    @pl.when(t == 0)
    def _():
        s_ref[...] = jnp.zeros_like(s_ref)

    @pl.when((pl.program_id(0) == 0) & (t == 0))
    def _():
        for h in range(n_heads):
            decay_scr[h] = _retention_decay(ret_chunk, h)

    for c in range(tq // ret_chunk):
        rows = slice(c * ret_chunk, (c + 1) * ret_chunk)
        for h in range(n_heads):
            cols = slice(h * HEAD_W, (h + 1) * HEAD_W)
            decay = decay_scr[h]
            qdec, kdec, gl = _retention_row_decays(ret_chunk, h)
            q = rq_ref[rows, cols]
            k = rk_ref[rows, cols]
            v = rv_ref[rows, cols]
            s = s_ref[0, h]
            att = lax.dot_general(q.astype(BF16), k.astype(BF16), NT_DIMS,
                                  preferred_element_type=F32) * decay
            o = (jnp.dot(att.astype(BF16), v, preferred_element_type=F32)
                 + jnp.dot((q * qdec).astype(BF16), s.astype(BF16), preferred_element_type=F32))
            s_ref[0, h] = gl * s + lax.dot_general((k * kdec).astype(BF16), v, TN_DIMS,
                                                   preferred_element_type=F32)
            mix_scr[rows, cols] = _gated_group_norm(o, rg_ref[rows, cols]).astype(BF16)

    lane = lax.broadcasted_iota(jnp.int32, (tq, HEAD_W), 1)
    for h in range(n_heads):
        qh = dq_ref[:, h * HEAD_W:(h + 1) * HEAD_W]
        zero = jnp.zeros_like(qh)
        qq_scr[h, 0:tq] = jnp.where(lane < HEAD_W // 2, qh, zero)
        qq_scr[h, tq:2 * tq] = jnp.where(lane >= HEAD_W // 2, qh, zero)
    mx_scr[...] = jnp.full(mx_scr.shape, FINITE_MIN, F32)
    acc_scr[...] = jnp.zeros_like(acc_scr)
    key_j = lax.broadcasted_iota(jnp.int32, (1, tq), 1).astype(F32)
    lane_tiles = [slice(c * HEAD_W, (c + 1) * HEAD_W) for c in range(tq // HEAD_W)]

    def fold(x, op):
        return functools.reduce(op, [x[:, c] for c in lane_tiles])

    def scores(kb, causal):
        key_pos = key_j + ((kb - t) * tq).astype(F32)
        for h in range(n_heads):
            slope = 2.0 ** (-8.0 / n_heads * (h + 1))
            s = jnp.dot(qq_scr[h], kt_ref[kb, h * HEAD_W:(h + 1) * HEAD_W, :], preferred_element_type=F32)
            s = (s + slope * key_pos) * LOG2_E
            if causal is not None:
                s = jnp.where(causal, s, NEG_INF)
            sc_scr[kb, h] = s
            mx_scr[h] = jnp.maximum(mx_scr[h], fold(s, jnp.maximum))

    _pair_loop(t, lambda kb: scores(kb, None))
    ii = lax.broadcasted_iota(jnp.int32, (2 * tq, tq), 0)
    jj = lax.broadcasted_iota(jnp.int32, (2 * tq, tq), 1)
    scores(t, jnp.where(ii >= tq, ii - tq, ii) >= jj)

    for h in range(n_heads):
        mx_scr[h] = jnp.broadcast_to(jnp.max(mx_scr[h], axis=-1, keepdims=True), (2 * tq, HEAD_W))

    ones = jnp.ones((tq, HEAD_W), BF16)

    def weighted(kb):
        for h in range(n_heads):
            m = mx_scr[h]
            e = [jnp.exp2(sc_scr[kb, h, :, c] - m).astype(BF16) for c in lane_tiles]
            vr = vb_ref[pl.ds(pl.multiple_of(kb * tq, tq), tq), h * HEAD_W:(h + 1) * HEAD_W]
            acc_scr[h] += jnp.dot(jnp.concatenate(e, axis=1), jnp.concatenate([vr, ones], axis=1),
                                  preferred_element_type=F32)

    _pair_loop(t + 1, weighted)
    lam = _lambda(lam_ref, lam_i)
    subg = subg_ref[...]
    for h in range(n_heads):
        on = acc_scr[h, :, 0:HEAD_W] / acc_scr[h, :, HEAD_W:2 * HEAD_W]
        o = _rms(on[:tq] - lam * on[tq:], subg) * (1.0 - lam_i)
        mix_scr[:, ret_w + h * HEAD_W:ret_w + (h + 1) * HEAD_W] = o.astype(BF16)

    mo = jnp.dot(mix_scr[...], wo_ref[...], preferred_element_type=F32)
    y_ref[...] = x_ref[...] + _rms(mo, gpost_ref[...])


def _mixer(lamv, rq, rk, rv, rg, dq, ktb, dvb, x2, wo_bf, gpost, subg, batch, seq, n_heads, lam_i,
           srq, srk, srv, srg, sstate):
    m, d = x2.shape
    w = rq.shape[1]
    tq = MIX_TILE
    nt = seq // tq
    n_dec = srq.shape[0]
    grp = DEC_GROUP
    dec_period = batch * nt * grp // n_dec
    assert dec_period * n_dec == batch * nt * grp, "sample row groups must spread evenly over the mixer steps"
    tile = lambda b, t: (b * nt + t, 0)
    whole = lambda b, t: (b, 0)
    dec_rows = pl.BlockSpec((grp, w), lambda b, t: ((b * nt + t) // dec_period, 0))
    dec_state = pl.BlockSpec((grp, n_heads, HEAD_W, HEAD_W), lambda b, t: ((b * nt + t) // dec_period, 0, 0, 0))
    body = functools.partial(_mixer_body, tq=tq, ret_chunk=RET_CHUNK, n_heads=n_heads, lam_i=lam_i,
                             dec_period=dec_period)
    return pl.pallas_call(
        body,
        grid=(batch, nt),
        in_specs=[
            _const_spec(lamv.shape),
            pl.BlockSpec((tq, w), tile), pl.BlockSpec((tq, w), tile), pl.BlockSpec((tq, w), tile),
            pl.BlockSpec((tq, w), tile), pl.BlockSpec((tq, w), tile),
            pl.BlockSpec((nt, w, tq), lambda b, t: (b, 0, 0)), pl.BlockSpec((seq, w), whole),
            pl.BlockSpec((tq, d), tile),
            _const_spec(wo_bf.shape), _const_spec((1, d)), _const_spec((1, HEAD_W)),
            dec_rows, dec_rows, dec_rows, dec_rows, dec_state,
        ],
        out_specs=[pl.BlockSpec((tq, d), tile),
                   pl.BlockSpec((1, n_heads, HEAD_W, HEAD_W), lambda b, t: (b, 0, 0, 0)),
                   dec_rows, dec_state],
        out_shape=[jax.ShapeDtypeStruct((m, d), F32),
                   jax.ShapeDtypeStruct((batch, n_heads, HEAD_W, HEAD_W), F32),
                   jax.ShapeDtypeStruct((n_dec, w), BF16), jax.ShapeDtypeStruct(sstate.shape, F32)],
        scratch_shapes=[
            pltpu.VMEM((tq, 2 * w), BF16),
            pltpu.VMEM((n_heads, 2 * tq, HEAD_W), BF16),
            pltpu.VMEM((nt, n_heads, 2 * tq, tq), F32),
            pltpu.VMEM((n_heads, 2 * tq, HEAD_W), F32),
            pltpu.VMEM((n_heads, 2 * tq, 2 * HEAD_W), F32),
            pltpu.VMEM((n_heads, RET_CHUNK, RET_CHUNK), F32),
        ],
        compiler_params=_params("arbitrary", "arbitrary"),
        name="mixer",
    )(lamv, rq, rk, rv, rg, dq, ktb, dvb, x2, wo_bf, gpost, subg, srq, srk, srv, srg, sstate)


def _conv_gate(g, u, g1, g2, cw_ref, cb_ref, cols):
    c = cw_ref[0:1, cols] * g2 + cw_ref[1:2, cols] * g1 + cw_ref[2:3, cols] * g + cb_ref[:, cols]
    return (jax.nn.gelu(c) * u).astype(BF16)


def _ffn_body(pt_ref, x_ref, gpre_ref, win_ref, cw_ref, cb_ref, wout_ref, gpost_ref,
              lam_ref, q_ref, kn_ref, vn_ref, ck_hbm, cv_hbm, y_ref, cs_ref, o_ref,
              gbuf, carry, act, kbuf, vbuf, sem, sc_scr, *, tf, d_ff, ck, rows, n_pages, page, n_heads, lam_i):
    t = pl.program_id(1)
    step = pl.program_id(0) * pl.num_programs(1) + t
    n_rows = rows * pl.num_programs(0) * pl.num_programs(1)
    copies = functools.partial(_page_copies, pt_ref, ck_hbm, cv_hbm, kbuf, vbuf, sem, n_rows=q_ref.shape[0],
                               n_pages=n_pages)

    @pl.when(step == 0)
    def _():
        for slot in range(2):
            for c in copies(b=slot, slot=slot):
                c.start()

    @pl.when(t == 0)
    def _():
        carry[...] = jnp.zeros_like(carry)

    lam = _lambda(lam_ref, lam_i)
    for i in range(rows):
        g = step * rows + i
        slot = i % 2
        for c in copies(b=g, slot=slot):
            c.wait()
        this = pl.ds(g, 1)
        o_ref[this, :] = _paged_row(slot, q_ref[this, :], kn_ref[this, :], vn_ref[this, :], lam, kbuf, vbuf,
                                    sc_scr, n_pages=n_pages, page=page, n_heads=n_heads)

        @pl.when(g + 2 < n_rows)
        def _():
            for c in copies(b=g + 2, slot=slot):
                c.start()

    x = x_ref[...]
    h = _rms(x, gpre_ref[...]).astype(BF16)
    for j in range(d_ff // ck):
        cols = slice(j * ck, (j + 1) * ck)
        g = jnp.dot(h, win_ref[:, cols], preferred_element_type=F32)
        u = jnp.dot(h, win_ref[:, d_ff + j * ck:d_ff + (j + 1) * ck], preferred_element_type=F32)
        gbuf[0:8, :] = carry[:, cols]
        gbuf[8:8 + tf, :] = g
        act[:, cols] = _conv_gate(g, u, gbuf[7:7 + tf, :], gbuf[6:6 + tf, :], cw_ref, cb_ref, cols)
        carry[:, cols] = gbuf[tf:tf + 8, :]
    f = jnp.dot(act[...], wout_ref[...], preferred_element_type=F32)
    y_ref[...] = x + _rms(f, gpost_ref[...])
    cs_ref[0] = carry[...]


def _ffn_paged(x2, gpre, win_bf, cw, cb, wout_bf, gpost, batch, seq,
               page_table, lamv, dq, dk, dv, cache_kt, cache_v, n_heads, lam_i):
    m, d = x2.shape
    d_ff = wout_bf.shape[0]
    tf = FFN_TILE
    nt = seq // tf
    n, w = dq.shape
    rows = n // (batch * nt)
    assert rows * batch * nt == n and rows % 2 == 0, "sample rows must split evenly, in pairs, over the FFN steps"
    n_pages = page_table.shape[1]
    page = cache_kt.shape[2]
    grp = _page_group(page)
    assert n_pages % grp == 0
    tile = lambda b, t, pt: (b * nt + t, 0)
    const = _const_spec
    rspec = const((n, w))
    body = functools.partial(_ffn_body, tf=tf, d_ff=d_ff, ck=FFN_CHUNK, rows=rows, n_pages=n_pages, page=page,
                             n_heads=n_heads, lam_i=lam_i)
    grid_spec = pltpu.PrefetchScalarGridSpec(
        num_scalar_prefetch=1,
        grid=(batch, nt),
        in_specs=[pl.BlockSpec((tf, d), tile), const((1, d)), const(win_bf.shape), const(cw.shape),
                  const(cb.shape), const(wout_bf.shape), const((1, d)),
                  const(lamv.shape), rspec, rspec, rspec,
                  pl.BlockSpec(memory_space=pl.ANY), pl.BlockSpec(memory_space=pl.ANY)],
        out_specs=[pl.BlockSpec((tf, d), tile), pl.BlockSpec((1, 8, d_ff), lambda b, t, pt: (b, 0, 0)),
                   pl.BlockSpec((n, w), lambda b, t, pt: (0, 0))],
        scratch_shapes=[pltpu.VMEM((tf + 8, FFN_CHUNK), F32), pltpu.VMEM((8, d_ff), F32),
                        pltpu.VMEM((tf, d_ff), BF16),
                        pltpu.VMEM((2, n_pages, w, page), F32),
                        pltpu.VMEM((2, n_pages, page * n_heads, HEAD_W), F32),
                        pltpu.SemaphoreType.DMA((2, 2)),
                        pltpu.VMEM((n_pages // grp, 2 * n_heads, grp * page), F32)],
    )
    y, cs, o = pl.pallas_call(
        body,
        grid_spec=grid_spec,
        out_shape=[jax.ShapeDtypeStruct((m, d), F32), jax.ShapeDtypeStruct((batch, 8, d_ff), F32),
                   jax.ShapeDtypeStruct((n, w), F32)],
        compiler_params=_params("arbitrary", "arbitrary"),
        name="conv_ffn_paged_attn",
    )(page_table.T.reshape(-1), x2, gpre, win_bf, cw, cb, wout_bf, gpost,
      lamv, dq.astype(F32), dk, dv, cache_kt, cache_v)
    return y, cs, o


def _ret_decode_rows(rq_ref, rk_ref, rv_ref, rg_ref, s_ref, a_ref, sn_ref, *, grp, n_heads):
    row = lax.broadcasted_iota(jnp.int32, (grp, grp * HEAD_W), 0)
    lane = lax.broadcasted_iota(jnp.int32, (grp, grp * HEAD_W), 1)
    own = (lane // HEAD_W) == row

    def block_diag(x):
        return jnp.where(own, jnp.tile(x, (1, grp)), 0.0).astype(BF16)

    for h in range(n_heads):
        cols = slice(h * HEAD_W, (h + 1) * HEAD_W)
        gamma = math.exp(_ret_log_decay(h))
        q = rq_ref[:, cols]
        k = rk_ref[:, cols]
        v = rv_ref[:, cols]
        qb = q.astype(BF16).astype(F32)
        kb = k.astype(BF16).astype(F32)
        att = jnp.sum(qb * kb, axis=-1, keepdims=True).astype(BF16).astype(F32)
        s = s_ref[:, h].reshape(grp * HEAD_W, HEAD_W)
        o = att * v.astype(F32) + jnp.dot(block_diag(q * gamma), s.astype(BF16),
                                          preferred_element_type=F32)
        upd = lax.dot_general(block_diag(k), v, TN_DIMS, preferred_element_type=F32)
        sn_ref[:, h] = (gamma * s + upd).reshape(grp, HEAD_W, HEAD_W)
        a_ref[:, cols] = _gated_group_norm(o, rg_ref[:, cols]).astype(BF16)


def _page_copies(pt_ref, ck_hbm, cv_hbm, kbuf, vbuf, sem, b, slot, n_rows, n_pages):
    out = []
    for p in range(n_pages):
        page = pt_ref[p * n_rows + b]
        out.append(pltpu.make_async_copy(ck_hbm.at[page], kbuf.at[slot, p], sem.at[0, slot]))
        out.append(pltpu.make_async_copy(cv_hbm.at[page], vbuf.at[slot, p], sem.at[1, slot]))
    return out


def _paged_row(slot, q, kn_row, vn_row, lam, kbuf, vbuf, sc_scr, *, n_pages, page, n_heads):
    w = n_heads * HEAD_W
    nc = 2 * n_heads
    past = n_pages * page
    ci = lax.broadcasted_iota(jnp.int32, (nc, w), 0)
    ri = lax.broadcasted_iota(jnp.int32, (nc, w), 1)
    col_of = ri // HEAD_W + n_heads * ((ri % HEAD_W) // (HEAD_W // 2))
    q_sel = jnp.where(ci == col_of, jnp.broadcast_to(q, (nc, w)), 0.0)
    q_sel_bf = q_sel.astype(BF16)
    grp = _page_group(page)
    span = grp * page
    rowc = lax.broadcasted_iota(jnp.int32, (nc, span), 0)
    key_i = lax.broadcasted_iota(jnp.int32, (nc, span), 1)
    slope = jnp.zeros((nc, span), F32)
    for h in range(n_heads):
        slope = jnp.where(rowc % n_heads == h, 2.0 ** (-8.0 / n_heads * (h + 1)), slope)

    mx = jnp.full((nc, span), FINITE_MIN, F32)
    for pg in range(n_pages // grp):
        kt = jnp.concatenate([kbuf[slot, grp * pg + i] for i in range(grp)], axis=1).astype(BF16)
        dist = (past - (pg * span + key_i)).astype(F32)
        s = jnp.dot(q_sel_bf, kt, preferred_element_type=F32) - slope * dist
        sc_scr[pg] = s
        mx = jnp.maximum(mx, s)
    kn = kn_row.astype(BF16).astype(F32)
    s_self = jnp.broadcast_to(jnp.sum(q_sel * kn, axis=-1, keepdims=True), (nc, span))
    m = jnp.maximum(jnp.broadcast_to(jnp.max(mx, axis=-1, keepdims=True), (nc, span)), s_self)

    lp = jnp.zeros((nc, span), F32)
    for pg in range(n_pages // grp):
        e = jnp.exp(sc_scr[pg] - m)
        sc_scr[pg] = e
        lp = lp + e
    e_self = jnp.exp(s_self - m)
    l = jnp.broadcast_to(jnp.sum(lp, axis=-1, keepdims=True), (nc, span)) + e_self
    r = jnp.where(rowc < n_heads, 1.0, lam) / l

    def combine(pn):
        return (pn - pltpu.roll(pn, n_heads, axis=0)).astype(BF16)

    def v_rows(p):
        return jnp.concatenate([vbuf[slot, p, pl.ds(h, page, stride=n_heads), :] for h in range(n_heads)],
                               axis=1)

    acc = jnp.zeros((nc, w), F32)
    for pg in range(n_pages // grp):
        v = jnp.concatenate([v_rows(grp * pg + i) for i in range(grp)], axis=0).astype(BF16)
        acc = acc + jnp.dot(combine(sc_scr[pg] * r), v, preferred_element_type=F32)
    a_self = combine(e_self * r).astype(F32)
    vn = vn_row.astype(BF16).astype(F32)
    return jnp.concatenate(
        [acc[h:h + 1, h * HEAD_W:(h + 1) * HEAD_W]
         + a_self[h:h + 1, 0:HEAD_W] * vn[:, h * HEAD_W:(h + 1) * HEAD_W] for h in range(n_heads)], axis=-1)


def _dec_ffn_body(a_ref, do_ref, x_ref, wo_ref, gmix_ref, subg_ref, gpre_ref, wg_ref, wu_ref, cw_ref,
                  cb_ref, sc_ref, wout_ref, gpost_ref, y_ref, g_ref, xmid, hbuf, acc, *, lam_i):
    j = pl.program_id(0)
    ret_w = a_ref.shape[1]

    @pl.when(j == 0)
    def _():
        mo = jnp.dot(a_ref[...], wo_ref[0:ret_w, :], preferred_element_type=F32)
        for h in range(do_ref.shape[1] // HEAD_W):
            cols = slice(h * HEAD_W, (h + 1) * HEAD_W)
            bh = (_rms(do_ref[:, cols], subg_ref[...]) * (1.0 - lam_i)).astype(BF16)
            mo += jnp.dot(bh, wo_ref[ret_w + h * HEAD_W:ret_w + (h + 1) * HEAD_W, :],
                          preferred_element_type=F32)
        xm = x_ref[...] + _rms(mo, gmix_ref[...])
        xmid[...] = xm
        hbuf[...] = _rms(xm, gpre_ref[...]).astype(BF16)
        acc[...] = jnp.zeros_like(acc)

    h = hbuf[...]
    g = jnp.dot(h, wg_ref[...], preferred_element_type=F32)
    u = jnp.dot(h, wu_ref[...], preferred_element_type=F32)
    g_ref[...] = g
    c = cw_ref[0:1, :] * sc_ref[0] + cw_ref[1:2, :] * sc_ref[1] + cw_ref[2:3, :] * g + cb_ref[...]
    acc[...] += jnp.dot((jax.nn.gelu(c) * u).astype(BF16), wout_ref[...], preferred_element_type=F32)

    @pl.when(j == pl.num_programs(0) - 1)
    def _():
        y_ref[...] = xmid[...] + _rms(acc[...], gpost_ref[...])


def _dec_ffn(a, do, x2, wo_bf, gmix, subg, gpre, win_bf, cw, cb, conv_state_t, wout_bf, gpost, lam_i):
    n, d = x2.shape
    d_ff = wout_bf.shape[0]
    ck = d_ff // DEC_FFN_STEPS
    assert ck * DEC_FFN_STEPS == d_ff and ck % HEAD_W == 0
    nj = DEC_FFN_STEPS
    full = lambda shape: pl.BlockSpec(shape, lambda j: (0,) * len(shape))
    return pl.pallas_call(
        functools.partial(_dec_ffn_body, lam_i=lam_i),
        grid=(nj,),
        in_specs=[full(a.shape), full(do.shape), full(x2.shape), full(wo_bf.shape), full((1, d)),
                  full((1, HEAD_W)), full((1, d)),
                  pl.BlockSpec((d, ck), lambda j: (0, j)), pl.BlockSpec((d, ck), lambda j: (0, nj + j)),
                  pl.BlockSpec((3, ck), lambda j: (0, j)), pl.BlockSpec((1, ck), lambda j: (0, j)),
                  pl.BlockSpec((2, n, ck), lambda j: (0, 0, j)),
                  pl.BlockSpec((ck, d), lambda j: (j, 0)), full((1, d))],
        out_specs=[full((n, d)), pl.BlockSpec((n, ck), lambda j: (0, j))],
        out_shape=[jax.ShapeDtypeStruct((n, d), F32), jax.ShapeDtypeStruct((n, d_ff), F32)],
        scratch_shapes=[pltpu.VMEM((n, d), F32), pltpu.VMEM((n, d), BF16), pltpu.VMEM((n, d), F32)],
        compiler_params=_params("arbitrary"),
        name="dec_ffn",
    )(a, do, x2, wo_bf, gmix, subg, gpre, win_bf, win_bf, cw, cb, conv_state_t, wout_bf, gpost)


def kernel(x_prompt, x_sample, state_ret, cache_k, cache_v, state_conv, page_table,
           norm_mix_pre, norm_mix_post, w_in, w_o, lambda_q1, lambda_k1, lambda_q2, lambda_k2,
           subln_g, norm_ffn_pre, norm_ffn_post, w_ffn_in, conv_w, conv_b, w_ffn_out):
    batch, seq, d = x_prompt.shape
    n_dec = x_sample.shape[0]
    assert x_sample.shape[1] == 1, "the sample group is one token per row"
    depth = w_in.shape[0]
    n_heads = state_ret.shape[2]
    dk_ret = state_ret.shape[3]
    dh_diff = cache_k.shape[-1]
    d_ff = w_ffn_out.shape[1]
    w = n_heads * HEAD_W
    assert w_in.shape[2] == 7 * w and cache_v.shape[-1] == HEAD_W and 2 * dh_diff == HEAD_W
    assert seq % MIX_TILE == 0 and seq % FFN_TILE == 0 and d_ff % FFN_CHUNK == 0
    assert n_dec % DEC_GROUP == 0 and n_dec % V7X_LANES == 0 and seq % IN_TILE == 0 and IN_TILE % MIX_TILE == 0
    assert conv_w.shape[1] == 3 and state_conv.shape[2] == 2, "the FFN kernels carry exactly two past gate rows"
    assert dk_ret == HEAD_W and state_ret.shape[4] == HEAD_W

    xp = x_prompt.reshape(batch * seq, d)
    xs = x_sample.reshape(n_dec, d)
    outs = [[] for _ in range(8)]
    for l in range(depth):
        lam_i = 0.8 - 0.6 * math.exp(-0.3 * l)
        lamv = jnp.stack([lambda_q1[l], lambda_k1[l], lambda_q2[l], lambda_k2[l]]).astype(F32)
        row = lambda v: v.reshape(1, -1)
        w_in_bf = w_in[l].astype(BF16)
        g_pre, g_post = row(norm_mix_pre[l]), row(norm_mix_post[l])
        f_pre, f_post = row(norm_ffn_pre[l]), row(norm_ffn_post[l])
        subg = row(subln_g[l])
        cw, cb = conv_w[l], row(conv_b[l])

        cache_kt = jnp.transpose(cache_k[l], (0, 2, 3, 4, 1)).reshape(cache_k.shape[1], w, cache_k.shape[2])
        cache_vr = cache_v[l].reshape(cache_v.shape[1], cache_v.shape[2] * n_heads, HEAD_W)
        proj = functools.partial(_inproj, n_heads=n_heads, dk_ret=dk_ret, dh_diff=dh_diff)

        srq, srk, srv, srg, dq, skt, sv4, dk, dv = proj(xs, g_pre, w_in_bf, 1, n_dec, n_dec, key_block=None)
        (rq, rk, rv, rg, pdq, kt, v4, ktb, dvb, w_o_bf, w_ffn_in_bf, w_ffn_out_bf) = proj(
            xp, g_pre, w_in_bf, batch, seq, IN_TILE, key_block=MIX_TILE,
            riders=(w_o[l], w_ffn_in[l], w_ffn_out[l]))
        xp, s_fin, a, s_new = _mixer(lamv, rq, rk, rv, rg, pdq, ktb, dvb, xp, w_o_bf, g_post, subg,
                                     batch, seq, n_heads, lam_i, srq, srk, srv, srg, state_ret[l])
        outs[0].append(s_fin)
        outs[2].append(jnp.transpose(kt.reshape(batch, n_heads, 2, dh_diff, seq), (0, 4, 1, 2, 3)))
        outs[3].append(v4.reshape(batch, seq, n_heads, HEAD_W))

        xp, cs, do = _ffn_paged(xp, f_pre, w_ffn_in_bf, cw, cb, w_ffn_out_bf, f_post, batch, seq,
                                page_table, lamv, dq, dk, dv, cache_kt, cache_vr, n_heads, lam_i)
        outs[6].append(cs[:, 6:8, :])

        conv_t = jnp.swapaxes(state_conv[l], 0, 1)
        xs, g_new = _dec_ffn(a, do, xs, w_o_bf, g_post, subg, f_pre, w_ffn_in_bf, cw, cb, conv_t,
                             w_ffn_out_bf, f_post, lam_i)
        outs[1].append(s_new)
        outs[4].append(jnp.transpose(skt.reshape(n_heads, 2, dh_diff, n_dec), (3, 0, 1, 2))[:, None])
        outs[5].append(sv4.reshape(n_dec, 1, n_heads, HEAD_W))
        outs[7].append(jnp.stack([state_conv[l][:, 1, :], g_new], axis=1))

    st = [jnp.stack(o) for o in outs]
    return (xp.reshape(batch, seq, d), xs.reshape(n_dec, 1, d), st[0], st[1], st[2], st[3], st[4], st[5],
            st[6], st[7])
```

```python
import functools
import math

import jax
import jax.numpy as jnp
from jax import lax
from jax.experimental import pallas as pl
from jax.experimental.pallas import tpu as pltpu

F32 = jnp.float32
BF16 = jnp.bfloat16
EPS = 1e-6
NEG_INF = -1e30
FINITE_MIN = -3e38
LOG2_E = math.log2(math.e)

V7X_VMEM_BYTES = 64 * 1024 * 1024
V7X_MXU_DIM = 256
V7X_LANES = 128
VMEM_LIMIT_BYTES = V7X_VMEM_BYTES * 7 // 8

HEAD_W = V7X_LANES
IN_TILE = 1024
MIX_TILE = V7X_MXU_DIM
RET_CHUNK = MIX_TILE
FFN_TILE = 256
FFN_CHUNK = V7X_MXU_DIM
DEC_GROUP = 8
DEC_FFN_STEPS = 2
NT_DIMS = (((1,), (1,)), ((), ()))
TN_DIMS = (((0,), (0,)), ((), ()))


def _page_group(page):
    return max(1, V7X_MXU_DIM // page)


def _rms(x, g):
    return x * lax.rsqrt(jnp.mean(x * x, axis=-1, keepdims=True) + EPS) * g


def _params(*sem):
    return pltpu.CompilerParams(dimension_semantics=sem, vmem_limit_bytes=VMEM_LIMIT_BYTES)


def _const_spec(shape):
    n = len(shape)
    return pl.BlockSpec(shape, lambda *_: (0,) * n, pipeline_mode=pl.Buffered(1))


def _ret_log_decay(h):
    return math.log(1.0 - 2.0 ** (-5.0 - h))


def _lambda(lam_ref, lam_i):
    a = jnp.sum(lam_ref[0:1, :] * lam_ref[1:2, :], axis=-1, keepdims=True)
    b = jnp.sum(lam_ref[2:3, :] * lam_ref[3:4, :], axis=-1, keepdims=True)
    return jnp.exp(a) - jnp.exp(b) + lam_i


def _inproj_body(x_ref, g_ref, w_ref, *rest, n_riders, width, n_heads, k_scale, q_scale, key_block):
    rider_in, rest = rest[:n_riders], rest[n_riders:]
    rq_ref, rk_ref, rv_ref, rg_ref, dq_ref, kt_ref, v4_ref, kx_ref, vx_ref = rest[:9]
    for src, dst in zip(rider_in, rest[9:]):
        dst[...] = src[...].astype(BF16)
    h = _rms(x_ref[...], g_ref[...]).astype(BF16)

    def col(j):
        return jnp.dot(h, w_ref[:, j * width:(j + 1) * width], preferred_element_type=F32)

    dk = col(5)
    kt = dk.T
    kt_ref[0] = kt
    if key_block is None:
        kx_ref[...] = dk
    else:
        for c in range(kx_ref.shape[0]):
            kx_ref[c] = kt[:, c * key_block:(c + 1) * key_block].astype(BF16)
    dv = col(6)
    for hh in range(n_heads):
        v4_ref[:, hh, :] = dv[:, hh * HEAD_W:(hh + 1) * HEAD_W]
    vx_ref[...] = dv if key_block is None else dv.astype(BF16)
    rq_ref[...] = col(0)
    rk_ref[...] = col(1) * k_scale
    rv_ref[...] = col(2).astype(BF16)
    rg_ref[...] = col(3)
    dq_ref[...] = (col(4) * q_scale).astype(BF16)


def _inproj(x2, g, w_bf, batch, seq, tile, n_heads, dk_ret, dh_diff, key_block, riders=()):
    m, d = x2.shape
    width = n_heads * HEAD_W
    nt = seq // tile
    steps = m // tile
    row = lambda i: (i, 0)
    assert all(r.shape[0] % (16 * steps) == 0 for r in riders), "rider row blocks must be bf16-tile aligned"
    rider_specs = [pl.BlockSpec((r.shape[0] // steps, r.shape[1]), row) for r in riders]
    f32_out = jax.ShapeDtypeStruct((m, width), F32)
    bf_out = jax.ShapeDtypeStruct((m, width), BF16)
    spec = pl.BlockSpec((tile, width), row)
    if key_block is None:
        kx_shape, kx_spec, vx_shape = f32_out, spec, f32_out
    else:
        kx_shape = jax.ShapeDtypeStruct((m // key_block, width, key_block), BF16)
        kx_spec = pl.BlockSpec((tile // key_block, width, key_block), lambda i: (i, 0, 0))
        vx_shape = bf_out
    body = functools.partial(_inproj_body, n_riders=len(riders), width=width, n_heads=n_heads,
                             k_scale=dk_ret ** -0.5, q_scale=dh_diff ** -0.5, key_block=key_block)
    return pl.pallas_call(
        body,
        grid=(steps,),
        in_specs=[pl.BlockSpec((tile, d), row), _const_spec((1, d)), _const_spec(w_bf.shape)] + rider_specs,
        out_specs=[spec] * 5 + [pl.BlockSpec((1, width, tile), lambda i: (i // nt, 0, i % nt)),
                                pl.BlockSpec((tile, n_heads, HEAD_W), lambda i: (i, 0, 0)), kx_spec, spec]
        + rider_specs,
        out_shape=[f32_out, f32_out, bf_out, f32_out, bf_out,
                   jax.ShapeDtypeStruct((batch, width, seq), F32),
                   jax.ShapeDtypeStruct((m, n_heads, HEAD_W), F32), kx_shape, vx_shape]
        + [jax.ShapeDtypeStruct(r.shape, BF16) for r in riders],
        compiler_params=_params("arbitrary"),
        name="inproj",
    )(x2, g, w_bf, *riders)


def _retention_decay(length, h):
    i = lax.broadcasted_iota(jnp.int32, (length, length), 0)
    j = lax.broadcasted_iota(jnp.int32, (length, length), 1)
    diff = (i - j).astype(F32)
    causal = diff >= 0
    return jnp.where(causal, jnp.exp(jnp.where(causal, diff, 0.0) * _ret_log_decay(h)), 0.0)


def _retention_row_decays(length, h):
    lg = _ret_log_decay(h)
    ri = lax.broadcasted_iota(jnp.int32, (length, 1), 0).astype(F32)
    return jnp.exp((ri + 1.0) * lg), jnp.exp((length - 1.0 - ri) * lg), math.exp(length * lg)


def _pair_loop(n, body):
    done = 0
    for width in (4, 2, 1):
        trips = lax.shift_right_logical(n - done, width.bit_length() - 1)

        def several(i, carry, width=width, done=done):
            for j in range(width):
                body(done + width * i + j)
            return carry

        lax.fori_loop(0, trips, several, 0)
        done = done + width * trips


def _gated_group_norm(o, g):
    on = o * lax.rsqrt(jnp.mean(o * o, axis=-1, keepdims=True) + EPS)
    return on * (g * jax.nn.sigmoid(g))


def _mixer_body(lam_ref, rq_ref, rk_ref, rv_ref, rg_ref, dq_ref, kt_ref, vb_ref, x_ref, wo_ref,
                gpost_ref, subg_ref, srq_ref, srk_ref, srv_ref, srg_ref, sst_ref,
                y_ref, s_ref, sa_ref, ssn_ref,
                mix_scr, qq_scr, sc_scr, mx_scr, acc_scr, decay_scr,
                *, tq, ret_chunk, n_heads, lam_i, dec_period):
    t = pl.program_id(1)
    ret_w = n_heads * HEAD_W

    @pl.when((pl.program_id(0) * pl.num_programs(1) + t) % dec_period == 0)
    def _():
        _ret_decode_rows(srq_ref, srk_ref, srv_ref, srg_ref, sst_ref, sa_ref, ssn_ref,
                         grp=sa_ref.shape[0], n_heads=n_heads)

    @pl.when(t == 0)
    def _():
        s_ref[...] = jnp.zeros_like(s_ref)

    @pl.when((pl.program_id(0) == 0) & (t == 0))
    def _():
        for h in range(n_heads):
            decay_scr[h] = _retention_decay(ret_chunk, h)

    for c in range(tq // ret_chunk):
        rows = slice(c * ret_chunk, (c + 1) * ret_chunk)
        for h in range(n_heads):
            cols = slice(h * HEAD_W, (h + 1) * HEAD_W)
            decay = decay_scr[h]
            qdec, kdec, gl = _retention_row_decays(ret_chunk, h)
            q = rq_ref[rows, cols]
            k = rk_ref[rows, cols]
            v = rv_ref[rows, cols]
            s = s_ref[0, h]
            att = lax.dot_general(q.astype(BF16), k.astype(BF16), NT_DIMS,
                                  preferred_element_type=F32) * decay
            o = (jnp.dot(att.astype(BF16), v, preferred_element_type=F32)
                 + jnp.dot((q * qdec).astype(BF16), s.astype(BF16), preferred_element_type=F32))
            s_ref[0, h] = gl * s + lax.dot_general((k * kdec).astype(BF16), v, TN_DIMS,
                                                   preferred_element_type=F32)
            mix_scr[rows, cols] = _gated_group_norm(o, rg_ref[rows, cols]).astype(BF16)

    lane = lax.broadcasted_iota(jnp.int32, (tq, HEAD_W), 1)
    for h in range(n_heads):
        qh = dq_ref[:, h * HEAD_W:(h + 1) * HEAD_W]
        zero = jnp.zeros_like(qh)
        qq_scr[h, 0:tq] = jnp.where(lane < HEAD_W // 2, qh, zero)
        qq_scr[h, tq:2 * tq] = jnp.where(lane >= HEAD_W // 2, qh, zero)
    mx_scr[...] = jnp.full(mx_scr.shape, FINITE_MIN, F32)
    acc_scr[...] = jnp.zeros_like(acc_scr)
    key_j = lax.broadcasted_iota(jnp.int32, (1, tq), 1).astype(F32)
    lane_tiles = [slice(c * HEAD_W, (c + 1) * HEAD_W) for c in range(tq // HEAD_W)]

    def fold(x, op):
        return functools.reduce(op, [x[:, c] for c in lane_tiles])

    def scores(kb, causal):
        key_pos = key_j + ((kb - t) * tq).astype(F32)
        for h in range(n_heads):
            slope = 2.0 ** (-8.0 / n_heads * (h + 1))
            s = jnp.dot(qq_scr[h], kt_ref[kb, h * HEAD_W:(h + 1) * HEAD_W, :], preferred_element_type=F32)
            s = (s + slope * key_pos) * LOG2_E
            if causal is not None:
                s = jnp.where(causal, s, NEG_INF)
            sc_scr[kb, h] = s
            mx_scr[h] = jnp.maximum(mx_scr[h], fold(s, jnp.maximum))

    _pair_loop(t, lambda kb: scores(kb, None))
    ii = lax.broadcasted_iota(jnp.int32, (2 * tq, tq), 0)
    jj = lax.broadcasted_iota(jnp.int32, (2 * tq, tq), 1)
    scores(t, jnp.where(ii >= tq, ii - tq, ii) >= jj)

    for h in range(n_heads):
        mx_scr[h] = jnp.broadcast_to(jnp.max(mx_scr[h], axis=-1, keepdims=True), (2 * tq, HEAD_W))

    ones = jnp.ones((tq, HEAD_W), BF16)

    def weighted(kb):
        for h in range(n_heads):
            m = mx_scr[h]
            e = [jnp.exp2(sc_scr[kb, h, :, c] - m).astype(BF16) for c in lane_tiles]
            vr = vb_ref[pl.ds(pl.multiple_of(kb * tq, tq), tq), h * HEAD_W:(h + 1) * HEAD_W]
            acc_scr[h] += jnp.dot(jnp.concatenate(e, axis=1), jnp.concatenate([vr, ones], axis=1),
                                  preferred_element_type=F32)

    _pair_loop(t + 1, weighted)
    lam = _lambda(lam_ref, lam_i)
    subg = subg_ref[...]
    for h in range(n_heads):
        on = acc_scr[h, :, 0:HEAD_W] / acc_scr[h, :, HEAD_W:2 * HEAD_W]
        o = _rms(on[:tq] - lam * on[tq:], subg) * (1.0 - lam_i)
        mix_scr[:, ret_w + h * HEAD_W:ret_w + (h + 1) * HEAD_W] = o.astype(BF16)

    mo = jnp.dot(mix_scr[...], wo_ref[...], preferred_element_type=F32)
    y_ref[...] = x_ref[...] + _rms(mo, gpost_ref[...])


def _mixer(lamv, rq, rk, rv, rg, dq, ktb, dvb, x2, wo_bf, gpost, subg, batch, seq, n_heads, lam_i,
           srq, srk, srv, srg, sstate):
    m, d = x2.shape
    w = rq.shape[1]
    tq = MIX_TILE
    nt = seq // tq
    n_dec = srq.shape[0]
    grp = DEC_GROUP
    dec_period = batch * nt * grp // n_dec
    assert dec_period * n_dec == batch * nt * grp, "sample row groups must spread evenly over the mixer steps"
    tile = lambda b, t: (b * nt + t, 0)
    whole = lambda b, t: (b, 0)
    dec_rows = pl.BlockSpec((grp, w), lambda b, t: ((b * nt + t) // dec_period, 0))
    dec_state = pl.BlockSpec((grp, n_heads, HEAD_W, HEAD_W), lambda b, t: ((b * nt + t) // dec_period, 0, 0, 0))
    body = functools.partial(_mixer_body, tq=tq, ret_chunk=RET_CHUNK, n_heads=n_heads, lam_i=lam_i,
                             dec_period=dec_period)
    return pl.pallas_call(
        body,
        grid=(batch, nt),
        in_specs=[
            _const_spec(lamv.shape),
            pl.BlockSpec((tq, w), tile), pl.BlockSpec((tq, w), tile), pl.BlockSpec((tq, w), tile),
            pl.BlockSpec((tq, w), tile), pl.BlockSpec((tq, w), tile),
            pl.BlockSpec((nt, w, tq), lambda b, t: (b, 0, 0)), pl.BlockSpec((seq, w), whole),
            pl.BlockSpec((tq, d), tile),
            _const_spec(wo_bf.shape), _const_spec((1, d)), _const_spec((1, HEAD_W)),
            dec_rows, dec_rows, dec_rows, dec_rows, dec_state,
        ],
        out_specs=[pl.BlockSpec((tq, d), tile),
                   pl.BlockSpec((1, n_heads, HEAD_W, HEAD_W), lambda b, t: (b, 0, 0, 0)),
                   dec_rows, dec_state],
        out_shape=[jax.ShapeDtypeStruct((m, d), F32),
                   jax.ShapeDtypeStruct((batch, n_heads, HEAD_W, HEAD_W), F32),
                   jax.ShapeDtypeStruct((n_dec, w), BF16), jax.ShapeDtypeStruct(sstate.shape, F32)],
        scratch_shapes=[
            pltpu.VMEM((tq, 2 * w), BF16),
            pltpu.VMEM((n_heads, 2 * tq, HEAD_W), BF16),
            pltpu.VMEM((nt, n_heads, 2 * tq, tq), F32),
            pltpu.VMEM((n_heads, 2 * tq, HEAD_W), F32),
            pltpu.VMEM((n_heads, 2 * tq, 2 * HEAD_W), F32),
            pltpu.VMEM((n_heads, RET_CHUNK, RET_CHUNK), F32),
        ],
        compiler_params=_params("arbitrary", "arbitrary"),
        name="mixer",
    )(lamv, rq, rk, rv, rg, dq, ktb, dvb, x2, wo_bf, gpost, subg, srq, srk, srv, srg, sstate)


def _conv_gate(g, u, g1, g2, cw_ref, cb_ref, cols):
    c = cw_ref[0:1, cols] * g2 + cw_ref[1:2, cols] * g1 + cw_ref[2:3, cols] * g + cb_ref[:, cols]
    return (jax.nn.gelu(c) * u).astype(BF16)


def _ffn_body(pt_ref, x_ref, gpre_ref, win_ref, cw_ref, cb_ref, wout_ref, gpost_ref,
              lam_ref, q_ref, kn_ref, vn_ref, ck_hbm, cv_hbm, y_ref, cs_ref, o_ref,
              gbuf, carry, act, kbuf, vbuf, sem, sc_scr, *, tf, d_ff, ck, rows, n_pages, page, n_heads, lam_i):
    t = pl.program_id(1)
    step = pl.program_id(0) * pl.num_programs(1) + t
    n_rows = rows * pl.num_programs(0) * pl.num_programs(1)
    copies = functools.partial(_page_copies, pt_ref, ck_hbm, cv_hbm, kbuf, vbuf, sem, n_rows=q_ref.shape[0],
                               n_pages=n_pages)

    @pl.when(step == 0)
    def _():
        for slot in range(2):
            for c in copies(b=slot, slot=slot):
                c.start()

    @pl.when(t == 0)
    def _():
        carry[...] = jnp.zeros_like(carry)

    lam = _lambda(lam_ref, lam_i)
    for i in range(rows):
        g = step * rows + i
        slot = i % 2
        for c in copies(b=g, slot=slot):
            c.wait()
        this = pl.ds(g, 1)
        o_ref[this, :] = _paged_row(slot, q_ref[this, :], kn_ref[this, :], vn_ref[this, :], lam, kbuf, vbuf,
                                    sc_scr, n_pages=n_pages, page=page, n_heads=n_heads)

        @pl.when(g + 2 < n_rows)
        def _():
            for c in copies(b=g + 2, slot=slot):
                c.start()

    x = x_ref[...]
    h = _rms(x, gpre_ref[...]).astype(BF16)
    for j in range(d_ff // ck):
        cols = slice(j * ck, (j + 1) * ck)
        g = jnp.dot(h, win_ref[:, cols], preferred_element_type=F32)
        u = jnp.dot(h, win_ref[:, d_ff + j * ck:d_ff + (j + 1) * ck], preferred_element_type=F32)
        gbuf[0:8, :] = carry[:, cols]
        gbuf[8:8 + tf, :] = g
        act[:, cols] = _conv_gate(g, u, gbuf[7:7 + tf, :], gbuf[6:6 + tf, :], cw_ref, cb_ref, cols)
        carry[:, cols] = gbuf[tf:tf + 8, :]
    f = jnp.dot(act[...], wout_ref[...], preferred_element_type=F32)
    y_ref[...] = x + _rms(f, gpost_ref[...])
    cs_ref[0] = carry[...]


def _ffn_paged(x2, gpre, win_bf, cw, cb, wout_bf, gpost, batch, seq,
               page_table, lamv, dq, dk, dv, cache_kt, cache_v, n_heads, lam_i):
    m, d = x2.shape
    d_ff = wout_bf.shape[0]
    tf = FFN_TILE
    nt = seq // tf
    n, w = dq.shape
    rows = n // (batch * nt)
    assert rows * batch * nt == n and rows % 2 == 0, "sample rows must split evenly, in pairs, over the FFN steps"
    n_pages = page_table.shape[1]
    page = cache_kt.shape[2]
    grp = _page_group(page)
    assert n_pages % grp == 0
    tile = lambda b, t, pt: (b * nt + t, 0)
    const = _const_spec
    rspec = const((n, w))
    body = functools.partial(_ffn_body, tf=tf, d_ff=d_ff, ck=FFN_CHUNK, rows=rows, n_pages=n_pages, page=page,
                             n_heads=n_heads, lam_i=lam_i)
    grid_spec = pltpu.PrefetchScalarGridSpec(
        num_scalar_prefetch=1,
        grid=(batch, nt),
        in_specs=[pl.BlockSpec((tf, d), tile), const((1, d)), const(win_bf.shape), const(cw.shape),
                  const(cb.shape), const(wout_bf.shape), const((1, d)),
                  const(lamv.shape), rspec, rspec, rspec,
                  pl.BlockSpec(memory_space=pl.ANY), pl.BlockSpec(memory_space=pl.ANY)],
        out_specs=[pl.BlockSpec((tf, d), tile), pl.BlockSpec((1, 8, d_ff), lambda b, t, pt: (b, 0, 0)),
                   pl.BlockSpec((n, w), lambda b, t, pt: (0, 0))],
        scratch_shapes=[pltpu.VMEM((tf + 8, FFN_CHUNK), F32), pltpu.VMEM((8, d_ff), F32),
                        pltpu.VMEM((tf, d_ff), BF16),
                        pltpu.VMEM((2, n_pages, w, page), F32),
                        pltpu.VMEM((2, n_pages, page * n_heads, HEAD_W), F32),
                        pltpu.SemaphoreType.DMA((2, 2)),
                        pltpu.VMEM((n_pages // grp, 2 * n_heads, grp * page), F32)],
    )
    y, cs, o = pl.pallas_call(
        body,
        grid_spec=grid_spec,
        out_shape=[jax.ShapeDtypeStruct((m, d), F32), jax.ShapeDtypeStruct((batch, 8, d_ff), F32),
                   jax.ShapeDtypeStruct((n, w), F32)],
        compiler_params=_params("arbitrary", "arbitrary"),
        name="conv_ffn_paged_attn",
    )(page_table.T.reshape(-1), x2, gpre, win_bf, cw, cb, wout_bf, gpost,
      lamv, dq.astype(F32), dk, dv, cache_kt, cache_v)
    return y, cs, o


def _ret_decode_rows(rq_ref, rk_ref, rv_ref, rg_ref, s_ref, a_ref, sn_ref, *, grp, n_heads):
    row = lax.broadcasted_iota(jnp.int32, (grp, grp * HEAD_W), 0)
    lane = lax.broadcasted_iota(jnp.int32, (grp, grp * HEAD_W), 1)
    own = (lane // HEAD_W) == row

    def block_diag(x):
        return jnp.where(own, jnp.tile(x, (1, grp)), 0.0).astype(BF16)

    for h in range(n_heads):
        cols = slice(h * HEAD_W, (h + 1) * HEAD_W)
        gamma = math.exp(_ret_log_decay(h))
        q = rq_ref[:, cols]
        k = rk_ref[:, cols]
        v = rv_ref[:, cols]
        qb = q.astype(BF16).astype(F32)
        kb = k.astype(BF16).astype(F32)
        att = jnp.sum(qb * kb, axis=-1, keepdims=True).astype(BF16).astype(F32)
        s = s_ref[:, h].reshape(grp * HEAD_W, HEAD_W)
        o = att * v.astype(F32) + jnp.dot(block_diag(q * gamma), s.astype(BF16),
                                          preferred_element_type=F32)
        upd = lax.dot_general(block_diag(k), v, TN_DIMS, preferred_element_type=F32)
        sn_ref[:, h] = (gamma * s + upd).reshape(grp, HEAD_W, HEAD_W)
        a_ref[:, cols] = _gated_group_norm(o, rg_ref[:, cols]).astype(BF16)


def _page_copies(pt_ref, ck_hbm, cv_hbm, kbuf, vbuf, sem, b, slot, n_rows, n_pages):
    out = []
    for p in range(n_pages):
        page = pt_ref[p * n_rows + b]
        out.append(pltpu.make_async_copy(ck_hbm.at[page], kbuf.at[slot, p], sem.at[0, slot]))
        out.append(pltpu.make_async_copy(cv_hbm.at[page], vbuf.at[slot, p], sem.at[1, slot]))
    return out


def _paged_row(slot, q, kn_row, vn_row, lam, kbuf, vbuf, sc_scr, *, n_pages, page, n_heads):
    w = n_heads * HEAD_W
    nc = 2 * n_heads
    past = n_pages * page
    ci = lax.broadcasted_iota(jnp.int32, (nc, w), 0)
    ri = lax.broadcasted_iota(jnp.int32, (nc, w), 1)
    col_of = ri // HEAD_W + n_heads * ((ri % HEAD_W) // (HEAD_W // 2))
    q_sel = jnp.where(ci == col_of, jnp.broadcast_to(q, (nc, w)), 0.0)
    q_sel_bf = q_sel.astype(BF16)
    grp = _page_group(page)
    span = grp * page
    rowc = lax.broadcasted_iota(jnp.int32, (nc, span), 0)
    key_i = lax.broadcasted_iota(jnp.int32, (nc, span), 1)
    slope = jnp.zeros((nc, span), F32)
    for h in range(n_heads):
        slope = jnp.where(rowc % n_heads == h, 2.0 ** (-8.0 / n_heads * (h + 1)), slope)

    mx = jnp.full((nc, span), FINITE_MIN, F32)
    for pg in range(n_pages // grp):
        kt = jnp.concatenate([kbuf[slot, grp * pg + i] for i in range(grp)], axis=1).astype(BF16)
        dist = (past - (pg * span + key_i)).astype(F32)
        s = jnp.dot(q_sel_bf, kt, preferred_element_type=F32) - slope * dist
        sc_scr[pg] = s
        mx = jnp.maximum(mx, s)
    kn = kn_row.astype(BF16).astype(F32)
    s_self = jnp.broadcast_to(jnp.sum(q_sel * kn, axis=-1, keepdims=True), (nc, span))
    m = jnp.maximum(jnp.broadcast_to(jnp.max(mx, axis=-1, keepdims=True), (nc, span)), s_self)

    lp = jnp.zeros((nc, span), F32)
    for pg in range(n_pages // grp):
        e = jnp.exp(sc_scr[pg] - m)
        sc_scr[pg] = e
        lp = lp + e
    e_self = jnp.exp(s_self - m)
    l = jnp.broadcast_to(jnp.sum(lp, axis=-1, keepdims=True), (nc, span)) + e_self
    r = jnp.where(rowc < n_heads, 1.0, lam) / l

    def combine(pn):
        return (pn - pltpu.roll(pn, n_heads, axis=0)).astype(BF16)

    def v_rows(p):
        return jnp.concatenate([vbuf[slot, p, pl.ds(h, page, stride=n_heads), :] for h in range(n_heads)],
                               axis=1)

    acc = jnp.zeros((nc, w), F32)
    for pg in range(n_pages // grp):
        v = jnp.concatenate([v_rows(grp * pg + i) for i in range(grp)], axis=0).astype(BF16)
        acc = acc + jnp.dot(combine(sc_scr[pg] * r), v, preferred_element_type=F32)
    a_self = combine(e_self * r).astype(F32)
    vn = vn_row.astype(BF16).astype(F32)
    return jnp.concatenate(
        [acc[h:h + 1, h * HEAD_W:(h + 1) * HEAD_W]
         + a_self[h:h + 1, 0:HEAD_W] * vn[:, h * HEAD_W:(h + 1) * HEAD_W] for h in range(n_heads)], axis=-1)


def _dec_ffn_body(a_ref, do_ref, x_ref, wo_ref, gmix_ref, subg_ref, gpre_ref, wg_ref, wu_ref, cw_ref,
                  cb_ref, sc_ref, wout_ref, gpost_ref, y_ref, g_ref, xmid, hbuf, acc, *, lam_i):
    j = pl.program_id(0)
    ret_w = a_ref.shape[1]

    @pl.when(j == 0)
    def _():
        mo = jnp.dot(a_ref[...], wo_ref[0:ret_w, :], preferred_element_type=F32)
        for h in range(do_ref.shape[1] // HEAD_W):
            cols = slice(h * HEAD_W, (h + 1) * HEAD_W)
            bh = (_rms(do_ref[:, cols], subg_ref[...]) * (1.0 - lam_i)).astype(BF16)
            mo += jnp.dot(bh, wo_ref[ret_w + h * HEAD_W:ret_w + (h + 1) * HEAD_W, :],
                          preferred_element_type=F32)
        xm = x_ref[...] + _rms(mo, gmix_ref[...])
        xmid[...] = xm
        hbuf[...] = _rms(xm, gpre_ref[...]).astype(BF16)
        acc[...] = jnp.zeros_like(acc)

    h = hbuf[...]
    g = jnp.dot(h, wg_ref[...], preferred_element_type=F32)
    u = jnp.dot(h, wu_ref[...], preferred_element_type=F32)
    g_ref[...] = g
    c = cw_ref[0:1, :] * sc_ref[0] + cw_ref[1:2, :] * sc_ref[1] + cw_ref[2:3, :] * g + cb_ref[...]
    acc[...] += jnp.dot((jax.nn.gelu(c) * u).astype(BF16), wout_ref[...], preferred_element_type=F32)

    @pl.when(j == pl.num_programs(0) - 1)
    def _():
        y_ref[...] = xmid[...] + _rms(acc[...], gpost_ref[...])


def _dec_ffn(a, do, x2, wo_bf, gmix, subg, gpre, win_bf, cw, cb, conv_state_t, wout_bf, gpost, lam_i):
    n, d = x2.shape
    d_ff = wout_bf.shape[0]
    ck = d_ff // DEC_FFN_STEPS
    assert ck * DEC_FFN_STEPS == d_ff and ck % HEAD_W == 0
    nj = DEC_FFN_STEPS
    full = lambda shape: pl.BlockSpec(shape, lambda j: (0,) * len(shape))
    return pl.pallas_call(
        functools.partial(_dec_ffn_body, lam_i=lam_i),
        grid=(nj,),
        in_specs=[full(a.shape), full(do.shape), full(x2.shape), full(wo_bf.shape), full((1, d)),
                  full((1, HEAD_W)), full((1, d)),
                  pl.BlockSpec((d, ck), lambda j: (0, j)), pl.BlockSpec((d, ck), lambda j: (0, nj + j)),
                  pl.BlockSpec((3, ck), lambda j: (0, j)), pl.BlockSpec((1, ck), lambda j: (0, j)),
                  pl.BlockSpec((2, n, ck), lambda j: (0, 0, j)),
                  pl.BlockSpec((ck, d), lambda j: (j, 0)), full((1, d))],
        out_specs=[full((n, d)), pl.BlockSpec((n, ck), lambda j: (0, j))],
        out_shape=[jax.ShapeDtypeStruct((n, d), F32), jax.ShapeDtypeStruct((n, d_ff), F32)],
        scratch_shapes=[pltpu.VMEM((n, d), F32), pltpu.VMEM((n, d), BF16), pltpu.VMEM((n, d), F32)],
        compiler_params=_params("arbitrary"),
        name="dec_ffn",
    )(a, do, x2, wo_bf, gmix, subg, gpre, win_bf, win_bf, cw, cb, conv_state_t, wout_bf, gpost)


def kernel(x_prompt, x_sample, state_ret, cache_k, cache_v, state_conv, page_table,
           norm_mix_pre, norm_mix_post, w_in, w_o, lambda_q1, lambda_k1, lambda_q2, lambda_k2,
           subln_g, norm_ffn_pre, norm_ffn_post, w_ffn_in, conv_w, conv_b, w_ffn_out):
    batch, seq, d = x_prompt.shape
    n_dec = x_sample.shape[0]
    assert x_sample.shape[1] == 1, "the sample group is one token per row"
    depth = w_in.shape[0]
    n_heads = state_ret.shape[2]
    dk_ret = state_ret.shape[3]
    dh_diff = cache_k.shape[-1]
    d_ff = w_ffn_out.shape[1]
    w = n_heads * HEAD_W
    assert w_in.shape[2] == 7 * w and cache_v.shape[-1] == HEAD_W and 2 * dh_diff == HEAD_W
    assert seq % MIX_TILE == 0 and seq % FFN_TILE == 0 and d_ff % FFN_CHUNK == 0
    assert n_dec % DEC_GROUP == 0 and n_dec % V7X_LANES == 0 and seq % IN_TILE == 0 and IN_TILE % MIX_TILE == 0
    assert conv_w.shape[1] == 3 and state_conv.shape[2] == 2, "the FFN kernels carry exactly two past gate rows"
    assert dk_ret == HEAD_W and state_ret.shape[4] == HEAD_W

    xp = x_prompt.reshape(batch * seq, d)
    xs = x_sample.reshape(n_dec, d)
    outs = [[] for _ in range(8)]
    for l in range(depth):
        lam_i = 0.8 - 0.6 * math.exp(-0.3 * l)
        lamv = jnp.stack([lambda_q1[l], lambda_k1[l], lambda_q2[l], lambda_k2[l]]).astype(F32)
        row = lambda v: v.reshape(1, -1)
        w_in_bf = w_in[l].astype(BF16)
        g_pre, g_post = row(norm_mix_pre[l]), row(norm_mix_post[l])
        f_pre, f_post = row(norm_ffn_pre[l]), row(norm_ffn_post[l])
        subg = row(subln_g[l])
        cw, cb = conv_w[l], row(conv_b[l])

        cache_kt = jnp.transpose(cache_k[l], (0, 2, 3, 4, 1)).reshape(cache_k.shape[1], w, cache_k.shape[2])
        cache_vr = cache_v[l].reshape(cache_v.shape[1], cache_v.shape[2] * n_heads, HEAD_W)
        proj = functools.partial(_inproj, n_heads=n_heads, dk_ret=dk_ret, dh_diff=dh_diff)

        srq, srk, srv, srg, dq, skt, sv4, dk, dv = proj(xs, g_pre, w_in_bf, 1, n_dec, n_dec, key_block=None)
        (rq, rk, rv, rg, pdq, kt, v4, ktb, dvb, w_o_bf, w_ffn_in_bf, w_ffn_out_bf) = proj(
            xp, g_pre, w_in_bf, batch, seq, IN_TILE, key_block=MIX_TILE,
            riders=(w_o[l], w_ffn_in[l], w_ffn_out[l]))
        xp, s_fin, a, s_new = _mixer(lamv, rq, rk, rv, rg, pdq, ktb, dvb, xp, w_o_bf, g_post, subg,
                                     batch, seq, n_heads, lam_i, srq, srk, srv, srg, state_ret[l])
        outs[0].append(s_fin)
        outs[2].append(jnp.transpose(kt.reshape(batch, n_heads, 2, dh_diff, seq), (0, 4, 1, 2, 3)))
        outs[3].append(v4.reshape(batch, seq, n_heads, HEAD_W))

        xp, cs, do = _ffn_paged(xp, f_pre, w_ffn_in_bf, cw, cb, w_ffn_out_bf, f_post, batch, seq,
                                page_table, lamv, dq, dk, dv, cache_kt, cache_vr, n_heads, lam_i)
        outs[6].append(cs[:, 6:8, :])

        conv_t = jnp.swapaxes(state_conv[l], 0, 1)
        xs, g_new = _dec_ffn(a, do, xs, w_o_bf, g_post, subg, f_pre, w_ffn_in_bf, cw, cb, conv_t,
                             w_ffn_out_bf, f_post, lam_i)
        outs[1].append(s_new)
        outs[4].append(jnp.transpose(skt.reshape(n_heads, 2, dh_diff, n_dec), (3, 0, 1, 2))[:, None])
        outs[5].append(sv4.reshape(n_dec, 1, n_heads, HEAD_W))
        outs[7].append(jnp.stack([state_conv[l][:, 1, :], g_new], axis=1))

    st = [jnp.stack(o) for o in outs]
    return (xp.reshape(batch, seq, d), xs.reshape(n_dec, 1, d), st[0], st[1], st[2], st[3], st[4], st[5],
            st[6], st[7])
```

```python
import functools
import math

import jax
import jax.numpy as jnp
from jax import lax
from jax.experimental import pallas as pl
from jax.experimental.pallas import tpu as pltpu

F32 = jnp.float32
BF16 = jnp.bfloat16
EPS = 1e-6
NEG_INF = -1e30
FINITE_MIN = -3e38
LOG2_E = math.log2(math.e)

V7X_VMEM_BYTES = 64 * 1024 * 1024
V7X_MXU_DIM = 256
V7X_LANES = 128
VMEM_LIMIT_BYTES = V7X_VMEM_BYTES * 7 // 8

HEAD_W = V7X_LANES
IN_TILE = 1024
MIX_TILE = V7X_MXU_DIM
RET_CHUNK = MIX_TILE
FFN_TILE = 256
FFN_CHUNK = V7X_MXU_DIM
DEC_GROUP = 8
DEC_FFN_STEPS = 2
PAGE_SLOTS = 3
NT_DIMS = (((1,), (1,)), ((), ()))
TN_DIMS = (((0,), (0,)), ((), ()))


def _page_group(page):
    return max(1, V7X_MXU_DIM // page)


def _rms(x, g):
    return x * lax.rsqrt(jnp.mean(x * x, axis=-1, keepdims=True) + EPS) * g


def _params(*sem):
    return pltpu.CompilerParams(dimension_semantics=sem, vmem_limit_bytes=VMEM_LIMIT_BYTES)


def _const_spec(shape):
    n = len(shape)
    return pl.BlockSpec(shape, lambda *_: (0,) * n, pipeline_mode=pl.Buffered(1))


def _ret_log_decay(h):
    return math.log(1.0 - 2.0 ** (-5.0 - h))


def _lambda(lam_ref, lam_i):
    a = jnp.sum(lam_ref[0:1, :] * lam_ref[1:2, :], axis=-1, keepdims=True)
    b = jnp.sum(lam_ref[2:3, :] * lam_ref[3:4, :], axis=-1, keepdims=True)
    return jnp.exp(a) - jnp.exp(b) + lam_i


def _inproj_body(x_ref, g_ref, w_ref, *rest, n_riders, width, n_heads, k_scale, q_scale, key_block):
    rider_in, rest = rest[:n_riders], rest[n_riders:]
    rq_ref, rk_ref, rv_ref, rg_ref, dq_ref, kt_ref, v4_ref, kx_ref, vx_ref = rest[:9]
    for src, dst in zip(rider_in, rest[9:]):
        dst[...] = src[...].astype(BF16)
    h = _rms(x_ref[...], g_ref[...]).astype(BF16)

    def col(j):
        return jnp.dot(h, w_ref[:, j * width:(j + 1) * width], preferred_element_type=F32)

    dk = col(5)
    kt = dk.T
    kt_ref[0] = kt
    if key_block is None:
        kx_ref[...] = dk
    else:
        for c in range(kx_ref.shape[0]):
            kx_ref[c] = kt[:, c * key_block:(c + 1) * key_block].astype(BF16)
    dv = col(6)
    for hh in range(n_heads):
        v4_ref[:, hh, :] = dv[:, hh * HEAD_W:(hh + 1) * HEAD_W]
    vx_ref[...] = dv if key_block is None else dv.astype(BF16)
    rq_ref[...] = col(0)
    rk_ref[...] = col(1) * k_scale
    rv_ref[...] = col(2).astype(BF16)
    rg_ref[...] = col(3)
    dq_ref[...] = (col(4) * q_scale).astype(BF16)


def _inproj(x2, g, w_bf, batch, seq, tile, n_heads, dk_ret, dh_diff, key_block, riders=()):
    m, d = x2.shape
    width = n_heads * HEAD_W
    nt = seq // tile
    steps = m // tile
    row = lambda i: (i, 0)
    assert all(r.shape[0] % (16 * steps) == 0 for r in riders), "rider row blocks must be bf16-tile aligned"
    rider_specs = [pl.BlockSpec((r.shape[0] // steps, r.shape[1]), row) for r in riders]
    f32_out = jax.ShapeDtypeStruct((m, width), F32)
    bf_out = jax.ShapeDtypeStruct((m, width), BF16)
    spec = pl.BlockSpec((tile, width), row)
    if key_block is None:
        kx_shape, kx_spec, vx_shape = f32_out, spec, f32_out
    else:
        kx_shape = jax.ShapeDtypeStruct((m // key_block, width, key_block), BF16)
        kx_spec = pl.BlockSpec((tile // key_block, width, key_block), lambda i: (i, 0, 0))
        vx_shape = bf_out
    body = functools.partial(_inproj_body, n_riders=len(riders), width=width, n_heads=n_heads,
                             k_scale=dk_ret ** -0.5, q_scale=dh_diff ** -0.5, key_block=key_block)
    return pl.pallas_call(
        body,
        grid=(steps,),
        in_specs=[pl.BlockSpec((tile, d), row), _const_spec((1, d)), _const_spec(w_bf.shape)] + rider_specs,
        out_specs=[spec] * 5 + [pl.BlockSpec((1, width, tile), lambda i: (i // nt, 0, i % nt)),
                                pl.BlockSpec((tile, n_heads, HEAD_W), lambda i: (i, 0, 0)), kx_spec, spec]
        + rider_specs,
        out_shape=[f32_out, f32_out, bf_out, f32_out, bf_out,
                   jax.ShapeDtypeStruct((batch, width, seq), F32),
                   jax.ShapeDtypeStruct((m, n_heads, HEAD_W), F32), kx_shape, vx_shape]
        + [jax.ShapeDtypeStruct(r.shape, BF16) for r in riders],
        compiler_params=_params("arbitrary"),
        name="inproj",
    )(x2, g, w_bf, *riders)


def _retention_decay(length, h):
    i = lax.broadcasted_iota(jnp.int32, (length, length), 0)
    j = lax.broadcasted_iota(jnp.int32, (length, length), 1)
    diff = (i - j).astype(F32)
    causal = diff >= 0
    return jnp.where(causal, jnp.exp(jnp.where(causal, diff, 0.0) * _ret_log_decay(h)), 0.0)


def _retention_row_decays(length, h):
    lg = _ret_log_decay(h)
    ri = lax.broadcasted_iota(jnp.int32, (length, 1), 0).astype(F32)
    return jnp.exp((ri + 1.0) * lg), jnp.exp((length - 1.0 - ri) * lg), math.exp(length * lg)


def _pair_loop(n, body):
    done = 0
    for width in (4, 2, 1):
        trips = lax.shift_right_logical(n - done, width.bit_length() - 1)

        def several(i, carry, width=width, done=done):
            for j in range(width):
                body(done + width * i + j)
            return carry

        lax.fori_loop(0, trips, several, 0)
        done = done + width * trips


def _gated_group_norm(o, g):
    on = o * lax.rsqrt(jnp.mean(o * o, axis=-1, keepdims=True) + EPS)
    return on * (g * jax.nn.sigmoid(g))


def _mixer_body(lam_ref, rq_ref, rk_ref, rv_ref, rg_ref, dq_ref, kt_ref, vb_ref, x_ref, wo_ref,
                gpost_ref, subg_ref, srq_ref, srk_ref, srv_ref, srg_ref, sst_ref,
                y_ref, s_ref, sa_ref, ssn_ref,
                mix_scr, qq_scr, sc_scr, mx_scr, acc_scr, decay_scr,
                *, tq, ret_chunk, n_heads, lam_i, dec_period):
    t = pl.program_id(1)
    ret_w = n_heads * HEAD_W

    @pl.when((pl.program_id(0) * pl.num_programs(1) + t) % dec_period == 0)
    def _():
        _ret_decode_rows(srq_ref, srk_ref, srv_ref, srg_ref, sst_ref, sa_ref, ssn_ref,
                         grp=sa_ref.shape[0], n_heads=n_heads)

    @pl.when(t == 0)
    def _():
        s_ref[...] = jnp.zeros_like(s_ref)

    @pl.when((pl.program_id(0) == 0) & (t == 0))
    def _():
        for h in range(n_heads):
            decay_scr[h] = _retention_decay(ret_chunk, h)

    for c in range(tq // ret_chunk):
        rows = slice(c * ret_chunk, (c + 1) * ret_chunk)
        for h in range(n_heads):
            cols = slice(h * HEAD_W, (h + 1) * HEAD_W)
            decay = decay_scr[h]
            qdec, kdec, gl = _retention_row_decays(ret_chunk, h)
            q = rq_ref[rows, cols]
            k = rk_ref[rows, cols]
            v = rv_ref[rows, cols]
            s = s_ref[0, h]
            att = lax.dot_general(q.astype(BF16), k.astype(BF16), NT_DIMS,
                                  preferred_element_type=F32) * decay
            o = (jnp.dot(att.astype(BF16), v, preferred_element_type=F32)
                 + jnp.dot((q * qdec).astype(BF16), s.astype(BF16), preferred_element_type=F32))
            s_ref[0, h] = gl * s + lax.dot_general((k * kdec).astype(BF16), v, TN_DIMS,
                                                   preferred_element_type=F32)
            mix_scr[rows, cols] = _gated_group_norm(o, rg_ref[rows, cols]).astype(BF16)

    lane = lax.broadcasted_iota(jnp.int32, (tq, HEAD_W), 1)
    for h in range(n_heads):
        qh = dq_ref[:, h * HEAD_W:(h + 1) * HEAD_W]
        zero = jnp.zeros_like(qh)
        qq_scr[h, 0:tq] = jnp.where(lane < HEAD_W // 2, qh, zero)
        qq_scr[h, tq:2 * tq] = jnp.where(lane >= HEAD_W // 2, qh, zero)
    mx_scr[...] = jnp.full(mx_scr.shape, FINITE_MIN, F32)
    acc_scr[...] = jnp.zeros_like(acc_scr)
    key_j = lax.broadcasted_iota(jnp.int32, (1, tq), 1).astype(F32)
    lane_tiles = [slice(c * HEAD_W, (c + 1) * HEAD_W) for c in range(tq // HEAD_W)]

    def fold(x, op):
        return functools.reduce(op, [x[:, c] for c in lane_tiles])

    def scores(kb, causal):
        key_pos = key_j + ((kb - t) * tq).astype(F32)
        for h in range(n_heads):
            slope = 2.0 ** (-8.0 / n_heads * (h + 1))
            s = jnp.dot(qq_scr[h], kt_ref[kb, h * HEAD_W:(h + 1) * HEAD_W, :], preferred_element_type=F32)
            s = (s + slope * key_pos) * LOG2_E
            if causal is not None:
                s = jnp.where(causal, s, NEG_INF)
            sc_scr[kb, h] = s
            mx_scr[h] = jnp.maximum(mx_scr[h], fold(s, jnp.maximum))

    _pair_loop(t, lambda kb: scores(kb, None))
    ii = lax.broadcasted_iota(jnp.int32, (2 * tq, tq), 0)
    jj = lax.broadcasted_iota(jnp.int32, (2 * tq, tq), 1)
    scores(t, jnp.where(ii >= tq, ii - tq, ii) >= jj)

    for h in range(n_heads):
        mx_scr[h] = jnp.broadcast_to(jnp.max(mx_scr[h], axis=-1, keepdims=True), (2 * tq, HEAD_W))

    ones = jnp.ones((tq, HEAD_W), BF16)

    def weighted(kb):
        for h in range(n_heads):
            m = mx_scr[h]
            e = [jnp.exp2(sc_scr[kb, h, :, c] - m).astype(BF16) for c in lane_tiles]
            vr = vb_ref[pl.ds(pl.multiple_of(kb * tq, tq), tq), h * HEAD_W:(h + 1) * HEAD_W]
            acc_scr[h] += jnp.dot(jnp.concatenate(e, axis=1), jnp.concatenate([vr, ones], axis=1),
                                  preferred_element_type=F32)

    _pair_loop(t + 1, weighted)
    lam = _lambda(lam_ref, lam_i)
    subg = subg_ref[...]
    for h in range(n_heads):
        on = acc_scr[h, :, 0:HEAD_W] / acc_scr[h, :, HEAD_W:2 * HEAD_W]
        o = _rms(on[:tq] - lam * on[tq:], subg) * (1.0 - lam_i)
        mix_scr[:, ret_w + h * HEAD_W:ret_w + (h + 1) * HEAD_W] = o.astype(BF16)

    mo = jnp.dot(mix_scr[...], wo_ref[...], preferred_element_type=F32)
    y_ref[...] = x_ref[...] + _rms(mo, gpost_ref[...])


def _mixer(lamv, rq, rk, rv, rg, dq, ktb, dvb, x2, wo_bf, gpost, subg, batch, seq, n_heads, lam_i,
           srq, srk, srv, srg, sstate):
    m, d = x2.shape
    w = rq.shape[1]
    tq = MIX_TILE
    nt = seq // tq
    n_dec = srq.shape[0]
    grp = DEC_GROUP
    dec_period = batch * nt * grp // n_dec
    assert dec_period * n_dec == batch * nt * grp, "sample row groups must spread evenly over the mixer steps"
    tile = lambda b, t: (b * nt + t, 0)
    whole = lambda b, t: (b, 0)
    dec_rows = pl.BlockSpec((grp, w), lambda b, t: ((b * nt + t) // dec_period, 0))
    dec_state = pl.BlockSpec((grp, n_heads, HEAD_W, HEAD_W), lambda b, t: ((b * nt + t) // dec_period, 0, 0, 0))
    body = functools.partial(_mixer_body, tq=tq, ret_chunk=RET_CHUNK, n_heads=n_heads, lam_i=lam_i,
                             dec_period=dec_period)
    return pl.pallas_call(
        body,
        grid=(batch, nt),
        in_specs=[
            _const_spec(lamv.shape),
            pl.BlockSpec((tq, w), tile), pl.BlockSpec((tq, w), tile), pl.BlockSpec((tq, w), tile),
            pl.BlockSpec((tq, w), tile), pl.BlockSpec((tq, w), tile),
            pl.BlockSpec((nt, w, tq), lambda b, t: (b, 0, 0)), pl.BlockSpec((seq, w), whole),
            pl.BlockSpec((tq, d), tile),
            _const_spec(wo_bf.shape), _const_spec((1, d)), _const_spec((1, HEAD_W)),
            dec_rows, dec_rows, dec_rows, dec_rows, dec_state,
        ],
        out_specs=[pl.BlockSpec((tq, d), tile),
                   pl.BlockSpec((1, n_heads, HEAD_W, HEAD_W), lambda b, t: (b, 0, 0, 0)),
                   dec_rows, dec_state],
        out_shape=[jax.ShapeDtypeStruct((m, d), F32),
                   jax.ShapeDtypeStruct((batch, n_heads, HEAD_W, HEAD_W), F32),
                   jax.ShapeDtypeStruct((n_dec, w), BF16), jax.ShapeDtypeStruct(sstate.shape, F32)],
        scratch_shapes=[
            pltpu.VMEM((tq, 2 * w), BF16),
            pltpu.VMEM((n_heads, 2 * tq, HEAD_W), BF16),
            pltpu.VMEM((nt, n_heads, 2 * tq, tq), F32),
            pltpu.VMEM((n_heads, 2 * tq, HEAD_W), F32),
            pltpu.VMEM((n_heads, 2 * tq, 2 * HEAD_W), F32),
            pltpu.VMEM((n_heads, RET_CHUNK, RET_CHUNK), F32),
        ],
        compiler_params=_params("arbitrary", "arbitrary"),
        name="mixer",
    )(lamv, rq, rk, rv, rg, dq, ktb, dvb, x2, wo_bf, gpost, subg, srq, srk, srv, srg, sstate)


def _conv_gate(g, u, g1, g2, cw_ref, cb_ref, cols):
    c = cw_ref[0:1, cols] * g2 + cw_ref[1:2, cols] * g1 + cw_ref[2:3, cols] * g + cb_ref[:, cols]
    return (jax.nn.gelu(c) * u).astype(BF16)


def _ffn_body(pt_ref, x_ref, gpre_ref, win_ref, cw_ref, cb_ref, wout_ref, gpost_ref,
              lam_ref, q_ref, kn_ref, vn_ref, ck_hbm, cv_hbm, y_ref, cs_ref, o_ref,
              gbuf, carry, act, kbuf, vbuf, sem, sc_scr, *, tf, d_ff, ck, rows, n_pages, page, n_heads, lam_i):
    t = pl.program_id(1)
    step = pl.program_id(0) * pl.num_programs(1) + t
    n_rows = rows * pl.num_programs(0) * pl.num_programs(1)
    copies = functools.partial(_page_copies, pt_ref, ck_hbm, cv_hbm, kbuf, vbuf, sem, n_rows=q_ref.shape[0],
                               n_pages=n_pages)

    @pl.when(step == 0)
    def _():
        for slot in range(PAGE_SLOTS):
            for c in copies(b=slot, slot=slot):
                c.start()

    @pl.when(t == 0)
    def _():
        carry[...] = jnp.zeros_like(carry)

    lam = _lambda(lam_ref, lam_i)
    for i in range(rows):
        g = step * rows + i
        slot = lax.rem(g, PAGE_SLOTS)
        for c in copies(b=g, slot=slot):
            c.wait()
        this = pl.ds(g, 1)
        o_ref[this, :] = _paged_row(slot, q_ref[this, :], kn_ref[this, :], vn_ref[this, :], lam, kbuf, vbuf,
                                    sc_scr, n_pages=n_pages, page=page, n_heads=n_heads)

        @pl.when(g + PAGE_SLOTS < n_rows)
        def _():
            for c in copies(b=g + PAGE_SLOTS, slot=slot):
                c.start()

    x = x_ref[...]
    h = _rms(x, gpre_ref[...]).astype(BF16)
    for j in range(d_ff // ck):
        cols = slice(j * ck, (j + 1) * ck)
        g = jnp.dot(h, win_ref[:, cols], preferred_element_type=F32)
        u = jnp.dot(h, win_ref[:, d_ff + j * ck:d_ff + (j + 1) * ck], preferred_element_type=F32)
        gbuf[0:8, :] = carry[:, cols]
        gbuf[8:8 + tf, :] = g
        act[:, cols] = _conv_gate(g, u, gbuf[7:7 + tf, :], gbuf[6:6 + tf, :], cw_ref, cb_ref, cols)
        carry[:, cols] = gbuf[tf:tf + 8, :]
    f = jnp.dot(act[...], wout_ref[...], preferred_element_type=F32)
    y_ref[...] = x + _rms(f, gpost_ref[...])
    cs_ref[0] = carry[...]


def _ffn_paged(x2, gpre, win_bf, cw, cb, wout_bf, gpost, batch, seq,
               page_table, lamv, dq, dk, dv, cache_kt, cache_v, n_heads, lam_i):
    m, d = x2.shape
    d_ff = wout_bf.shape[0]
    tf = FFN_TILE
    nt = seq // tf
    n, w = dq.shape
    rows = n // (batch * nt)
    assert rows * batch * nt == n and n >= PAGE_SLOTS, "sample rows must split evenly over the FFN steps"
    n_pages = page_table.shape[1]
    page = cache_kt.shape[2]
    grp = _page_group(page)
    assert n_pages % grp == 0
    tile = lambda b, t, pt: (b * nt + t, 0)
    const = _const_spec
    rspec = const((n, w))
    body = functools.partial(_ffn_body, tf=tf, d_ff=d_ff, ck=FFN_CHUNK, rows=rows, n_pages=n_pages, page=page,
                             n_heads=n_heads, lam_i=lam_i)
    grid_spec = pltpu.PrefetchScalarGridSpec(
        num_scalar_prefetch=1,
        grid=(batch, nt),
        in_specs=[pl.BlockSpec((tf, d), tile), const((1, d)), const(win_bf.shape), const(cw.shape),
                  const(cb.shape), const(wout_bf.shape), const((1, d)),
                  const(lamv.shape), rspec, rspec, rspec,
                  pl.BlockSpec(memory_space=pl.ANY), pl.BlockSpec(memory_space=pl.ANY)],
        out_specs=[pl.BlockSpec((tf, d), tile), pl.BlockSpec((1, 8, d_ff), lambda b, t, pt: (b, 0, 0)),
                   pl.BlockSpec((n, w), lambda b, t, pt: (0, 0))],
        scratch_shapes=[pltpu.VMEM((tf + 8, FFN_CHUNK), F32), pltpu.VMEM((8, d_ff), F32),
                        pltpu.VMEM((tf, d_ff), BF16),
                        pltpu.VMEM((PAGE_SLOTS, n_pages, w, page), F32),
                        pltpu.VMEM((PAGE_SLOTS, n_pages, page * n_heads, HEAD_W), F32),
                        pltpu.SemaphoreType.DMA((2, PAGE_SLOTS)),
                        pltpu.VMEM((n_pages // grp, 2 * n_heads, grp * page), F32)],
    )
    y, cs, o = pl.pallas_call(
        body,
        grid_spec=grid_spec,
        out_shape=[jax.ShapeDtypeStruct((m, d), F32), jax.ShapeDtypeStruct((batch, 8, d_ff), F32),
                   jax.ShapeDtypeStruct((n, w), F32)],
        compiler_params=_params("arbitrary", "arbitrary"),
        name="conv_ffn_paged_attn",
    )(page_table.T.reshape(-1), x2, gpre, win_bf, cw, cb, wout_bf, gpost,
      lamv, dq.astype(F32), dk, dv, cache_kt, cache_v)
    return y, cs, o


def _ret_decode_rows(rq_ref, rk_ref, rv_ref, rg_ref, s_ref, a_ref, sn_ref, *, grp, n_heads):
    row = lax.broadcasted_iota(jnp.int32, (grp, grp * HEAD_W), 0)
    lane = lax.broadcasted_iota(jnp.int32, (grp, grp * HEAD_W), 1)
    own = (lane // HEAD_W) == row

    def block_diag(x):
        return jnp.where(own, jnp.tile(x, (1, grp)), 0.0).astype(BF16)

    for h in range(n_heads):
        cols = slice(h * HEAD_W, (h + 1) * HEAD_W)
        gamma = math.exp(_ret_log_decay(h))
        q = rq_ref[:, cols]
        k = rk_ref[:, cols]
        v = rv_ref[:, cols]
        qb = q.astype(BF16).astype(F32)
        kb = k.astype(BF16).astype(F32)
        att = jnp.sum(qb * kb, axis=-1, keepdims=True).astype(BF16).astype(F32)
        s = s_ref[:, h].reshape(grp * HEAD_W, HEAD_W)
        o = att * v.astype(F32) + jnp.dot(block_diag(q * gamma), s.astype(BF16),
                                          preferred_element_type=F32)
        upd = lax.dot_general(block_diag(k), v, TN_DIMS, preferred_element_type=F32)
        sn_ref[:, h] = (gamma * s + upd).reshape(grp, HEAD_W, HEAD_W)
        a_ref[:, cols] = _gated_group_norm(o, rg_ref[:, cols]).astype(BF16)


def _page_copies(pt_ref, ck_hbm, cv_hbm, kbuf, vbuf, sem, b, slot, n_rows, n_pages):
    out = []
    for p in range(n_pages):
        page = pt_ref[p * n_rows + b]
        out.append(pltpu.make_async_copy(ck_hbm.at[page], kbuf.at[slot, p], sem.at[0, slot]))
        out.append(pltpu.make_async_copy(cv_hbm.at[page], vbuf.at[slot, p], sem.at[1, slot]))
    return out


def _paged_row(slot, q, kn_row, vn_row, lam, kbuf, vbuf, sc_scr, *, n_pages, page, n_heads):
    w = n_heads * HEAD_W
    nc = 2 * n_heads
    past = n_pages * page
    ci = lax.broadcasted_iota(jnp.int32, (nc, w), 0)
    ri = lax.broadcasted_iota(jnp.int32, (nc, w), 1)
    col_of = ri // HEAD_W + n_heads * ((ri % HEAD_W) // (HEAD_W // 2))
    q_sel = jnp.where(ci == col_of, jnp.broadcast_to(q, (nc, w)), 0.0)
    q_sel_bf = q_sel.astype(BF16)
    grp = _page_group(page)
    span = grp * page
    rowc = lax.broadcasted_iota(jnp.int32, (nc, span), 0)
    key_i = lax.broadcasted_iota(jnp.int32, (nc, span), 1)
    slope = jnp.zeros((nc, span), F32)
    for h in range(n_heads):
        slope = jnp.where(rowc % n_heads == h, 2.0 ** (-8.0 / n_heads * (h + 1)), slope)

    mx = jnp.full((nc, span), FINITE_MIN, F32)
    for pg in range(n_pages // grp):
        kt = jnp.concatenate([kbuf[slot, grp * pg + i] for i in range(grp)], axis=1).astype(BF16)
        dist = (past - (pg * span + key_i)).astype(F32)
        s = jnp.dot(q_sel_bf, kt, preferred_element_type=F32) - slope * dist
        sc_scr[pg] = s
        mx = jnp.maximum(mx, s)
    kn = kn_row.astype(BF16).astype(F32)
    s_self = jnp.broadcast_to(jnp.sum(q_sel * kn, axis=-1, keepdims=True), (nc, span))
    m = jnp.maximum(jnp.broadcast_to(jnp.max(mx, axis=-1, keepdims=True), (nc, span)), s_self)

    lp = jnp.zeros((nc, span), F32)
    for pg in range(n_pages // grp):
        e = jnp.exp(sc_scr[pg] - m)
        sc_scr[pg] = e
        lp = lp + e
    e_self = jnp.exp(s_self - m)
    l = jnp.broadcast_to(jnp.sum(lp, axis=-1, keepdims=True), (nc, span)) + e_self
    r = jnp.where(rowc < n_heads, 1.0, lam) / l

    def combine(pn):
        return (pn - pltpu.roll(pn, n_heads, axis=0)).astype(BF16)

    def v_rows(p):
        return jnp.concatenate([vbuf[slot, p, pl.ds(h, page, stride=n_heads), :] for h in range(n_heads)],
                               axis=1)

    acc = jnp.zeros((nc, w), F32)
    for pg in range(n_pages // grp):
        v = jnp.concatenate([v_rows(grp * pg + i) for i in range(grp)], axis=0).astype(BF16)
        acc = acc + jnp.dot(combine(sc_scr[pg] * r), v, preferred_element_type=F32)
    a_self = combine(e_self * r).astype(F32)
    vn = vn_row.astype(BF16).astype(F32)
    return jnp.concatenate(
        [acc[h:h + 1, h * HEAD_W:(h + 1) * HEAD_W]
         + a_self[h:h + 1, 0:HEAD_W] * vn[:, h * HEAD_W:(h + 1) * HEAD_W] for h in range(n_heads)], axis=-1)


def _dec_ffn_body(a_ref, do_ref, x_ref, wo_ref, gmix_ref, subg_ref, gpre_ref, wg_ref, wu_ref, cw_ref,
                  cb_ref, sc_ref, wout_ref, gpost_ref, y_ref, g_ref, xmid, hbuf, acc, *, lam_i):
    j = pl.program_id(0)
    ret_w = a_ref.shape[1]

    @pl.when(j == 0)
    def _():
        mo = jnp.dot(a_ref[...], wo_ref[0:ret_w, :], preferred_element_type=F32)
        for h in range(do_ref.shape[1] // HEAD_W):
            cols = slice(h * HEAD_W, (h + 1) * HEAD_W)
            bh = (_rms(do_ref[:, cols], subg_ref[...]) * (1.0 - lam_i)).astype(BF16)
            mo += jnp.dot(bh, wo_ref[ret_w + h * HEAD_W:ret_w + (h + 1) * HEAD_W, :],
                          preferred_element_type=F32)
        xm = x_ref[...] + _rms(mo, gmix_ref[...])
        xmid[...] = xm
        hbuf[...] = _rms(xm, gpre_ref[...]).astype(BF16)
        acc[...] = jnp.zeros_like(acc)

    h = hbuf[...]
    g = jnp.dot(h, wg_ref[...], preferred_element_type=F32)
    u = jnp.dot(h, wu_ref[...], preferred_element_type=F32)
    g_ref[...] = g
    c = cw_ref[0:1, :] * sc_ref[0] + cw_ref[1:2, :] * sc_ref[1] + cw_ref[2:3, :] * g + cb_ref[...]
    acc[...] += jnp.dot((jax.nn.gelu(c) * u).astype(BF16), wout_ref[...], preferred_element_type=F32)

    @pl.when(j == pl.num_programs(0) - 1)
    def _():
        y_ref[...] = xmid[...] + _rms(acc[...], gpost_ref[...])


def _dec_ffn(a, do, x2, wo_bf, gmix, subg, gpre, win_bf, cw, cb, conv_state_t, wout_bf, gpost, lam_i):
    n, d = x2.shape
    d_ff = wout_bf.shape[0]
    ck = d_ff // DEC_FFN_STEPS
    assert ck * DEC_FFN_STEPS == d_ff and ck % HEAD_W == 0
    nj = DEC_FFN_STEPS
    full = lambda shape: pl.BlockSpec(shape, lambda j: (0,) * len(shape))
    return pl.pallas_call(
        functools.partial(_dec_ffn_body, lam_i=lam_i),
        grid=(nj,),
        in_specs=[full(a.shape), full(do.shape), full(x2.shape), full(wo_bf.shape), full((1, d)),
                  full((1, HEAD_W)), full((1, d)),
                  pl.BlockSpec((d, ck), lambda j: (0, j)), pl.BlockSpec((d, ck), lambda j: (0, nj + j)),
                  pl.BlockSpec((3, ck), lambda j: (0, j)), pl.BlockSpec((1, ck), lambda j: (0, j)),
                  pl.BlockSpec((2, n, ck), lambda j: (0, 0, j)),
                  pl.BlockSpec((ck, d), lambda j: (j, 0)), full((1, d))],
        out_specs=[full((n, d)), pl.BlockSpec((n, ck), lambda j: (0, j))],
        out_shape=[jax.ShapeDtypeStruct((n, d), F32), jax.ShapeDtypeStruct((n, d_ff), F32)],
        scratch_shapes=[pltpu.VMEM((n, d), F32), pltpu.VMEM((n, d), BF16), pltpu.VMEM((n, d), F32)],
        compiler_params=_params("arbitrary"),
        name="dec_ffn",
    )(a, do, x2, wo_bf, gmix, subg, gpre, win_bf, win_bf, cw, cb, conv_state_t, wout_bf, gpost)


def kernel(x_prompt, x_sample, state_ret, cache_k, cache_v, state_conv, page_table,
           norm_mix_pre, norm_mix_post, w_in, w_o, lambda_q1, lambda_k1, lambda_q2, lambda_k2,
           subln_g, norm_ffn_pre, norm_ffn_post, w_ffn_in, conv_w, conv_b, w_ffn_out):
    batch, seq, d = x_prompt.shape
    n_dec = x_sample.shape[0]
    assert x_sample.shape[1] == 1, "the sample group is one token per row"
    depth = w_in.shape[0]
    n_heads = state_ret.shape[2]
    dk_ret = state_ret.shape[3]
    dh_diff = cache_k.shape[-1]
    d_ff = w_ffn_out.shape[1]
    w = n_heads * HEAD_W
    assert w_in.shape[2] == 7 * w and cache_v.shape[-1] == HEAD_W and 2 * dh_diff == HEAD_W
    assert seq % MIX_TILE == 0 and seq % FFN_TILE == 0 and d_ff % FFN_CHUNK == 0
    assert n_dec % DEC_GROUP == 0 and n_dec % V7X_LANES == 0 and seq % IN_TILE == 0 and IN_TILE % MIX_TILE == 0
    assert conv_w.shape[1] == 3 and state_conv.shape[2] == 2, "the FFN kernels carry exactly two past gate rows"
    assert dk_ret == HEAD_W and state_ret.shape[4] == HEAD_W

    xp = x_prompt.reshape(batch * seq, d)
    xs = x_sample.reshape(n_dec, d)
    outs = [[] for _ in range(8)]
    for l in range(depth):
        lam_i = 0.8 - 0.6 * math.exp(-0.3 * l)
        lamv = jnp.stack([lambda_q1[l], lambda_k1[l], lambda_q2[l], lambda_k2[l]]).astype(F32)
        row = lambda v: v.reshape(1, -1)
        w_in_bf = w_in[l].astype(BF16)
        g_pre, g_post = row(norm_mix_pre[l]), row(norm_mix_post[l])
        f_pre, f_post = row(norm_ffn_pre[l]), row(norm_ffn_post[l])
        subg = row(subln_g[l])
        cw, cb = conv_w[l], row(conv_b[l])

        cache_kt = jnp.transpose(cache_k[l], (0, 2, 3, 4, 1)).reshape(cache_k.shape[1], w, cache_k.shape[2])
        cache_vr = cache_v[l].reshape(cache_v.shape[1], cache_v.shape[2] * n_heads, HEAD_W)
        proj = functools.partial(_inproj, n_heads=n_heads, dk_ret=dk_ret, dh_diff=dh_diff)

        srq, srk, srv, srg, dq, skt, sv4, dk, dv = proj(xs, g_pre, w_in_bf, 1, n_dec, n_dec, key_block=None)
        (rq, rk, rv, rg, pdq, kt, v4, ktb, dvb, w_o_bf, w_ffn_in_bf, w_ffn_out_bf) = proj(
            xp, g_pre, w_in_bf, batch, seq, IN_TILE, key_block=MIX_TILE,
            riders=(w_o[l], w_ffn_in[l], w_ffn_out[l]))
        xp, s_fin, a, s_new = _mixer(lamv, rq, rk, rv, rg, pdq, ktb, dvb, xp, w_o_bf, g_post, subg,
                                     batch, seq, n_heads, lam_i, srq, srk, srv, srg, state_ret[l])
        outs[0].append(s_fin)
        outs[2].append(jnp.transpose(kt.reshape(batch, n_heads, 2, dh_diff, seq), (0, 4, 1, 2, 3)))
        outs[3].append(v4.reshape(batch, seq, n_heads, HEAD_W))

        xp, cs, do = _ffn_paged(xp, f_pre, w_ffn_in_bf, cw, cb, w_ffn_out_bf, f_post, batch, seq,
                                page_table, lamv, dq, dk, dv, cache_kt, cache_vr, n_heads, lam_i)
        outs[6].append(cs[:, 6:8, :])

        conv_t = jnp.swapaxes(state_conv[l], 0, 1)
        xs, g_new = _dec_ffn(a, do, xs, w_o_bf, g_post, subg, f_pre, w_ffn_in_bf, cw, cb, conv_t,
                             w_ffn_out_bf, f_post, lam_i)
        outs[1].append(s_new)
        outs[4].append(jnp.transpose(skt.reshape(n_heads, 2, dh_diff, n_dec), (3, 0, 1, 2))[:, None])
        outs[5].append(sv4.reshape(n_dec, 1, n_heads, HEAD_W))
        outs[7].append(jnp.stack([state_conv[l][:, 1, :], g_new], axis=1))

    st = [jnp.stack(o) for o in outs]
    return (xp.reshape(batch, seq, d), xs.reshape(n_dec, 1, d), st[0], st[1], st[2], st[3], st[4], st[5],
            st[6], st[7])
```

```python
import functools
import math

import jax
import jax.numpy as jnp
from jax import lax
from jax.experimental import pallas as pl
from jax.experimental.pallas import tpu as pltpu

F32 = jnp.float32
BF16 = jnp.bfloat16
EPS = 1e-6
NEG_INF = -1e30
FINITE_MIN = -3e38
LOG2_E = math.log2(math.e)

V7X_VMEM_BYTES = 64 * 1024 * 1024
V7X_MXU_DIM = 256
V7X_LANES = 128
VMEM_LIMIT_BYTES = V7X_VMEM_BYTES * 7 // 8

HEAD_W = V7X_LANES
IN_TILE = 1024
MIX_TILE = V7X_MXU_DIM
RET_CHUNK = MIX_TILE
FFN_TILE = 256
FFN_CHUNK = V7X_MXU_DIM
DEC_GROUP = 8
DEC_FFN_STEPS = 2
NT_DIMS = (((1,), (1,)), ((), ()))
TN_DIMS = (((0,), (0,)), ((), ()))


def _page_group(page):
    return max(1, V7X_MXU_DIM // page)


def _rms(x, g):
    return x * lax.rsqrt(jnp.mean(x * x, axis=-1, keepdims=True) + EPS) * g


def _params(*sem):
    return pltpu.CompilerParams(dimension_semantics=sem, vmem_limit_bytes=VMEM_LIMIT_BYTES)


def _const_spec(shape):
    n = len(shape)
    return pl.BlockSpec(shape, lambda *_: (0,) * n, pipeline_mode=pl.Buffered(1))


def _ret_log_decay(h):
    return math.log(1.0 - 2.0 ** (-5.0 - h))


def _lambda(lam_ref, lam_i):
    a = jnp.sum(lam_ref[0:1, :] * lam_ref[1:2, :], axis=-1, keepdims=True)
    b = jnp.sum(lam_ref[2:3, :] * lam_ref[3:4, :], axis=-1, keepdims=True)
    return jnp.exp(a) - jnp.exp(b) + lam_i


def _inproj_body(x_ref, g_ref, w_ref, *rest, n_riders, width, n_heads, k_scale, q_scale, key_block):
    rider_in, rest = rest[:n_riders], rest[n_riders:]
    rq_ref, rk_ref, rv_ref, rg_ref, dq_ref, kt_ref, v4_ref, kx_ref, vx_ref = rest[:9]
    for src, dst in zip(rider_in, rest[9:]):
        dst[...] = src[...].astype(BF16)
    h = _rms(x_ref[...], g_ref[...]).astype(BF16)

    def col(j):
        return jnp.dot(h, w_ref[:, j * width:(j + 1) * width], preferred_element_type=F32)

    dk = col(5)
    kt = dk.T
    kt_ref[0] = kt
    if key_block is None:
        kx_ref[...] = dk
    else:
        for c in range(kx_ref.shape[0]):
            kx_ref[c] = kt[:, c * key_block:(c + 1) * key_block].astype(BF16)
    dv = col(6)
    for hh in range(n_heads):
        v4_ref[:, hh, :] = dv[:, hh * HEAD_W:(hh + 1) * HEAD_W]
    vx_ref[...] = dv if key_block is None else dv.astype(BF16)
    rq_ref[...] = col(0)
    rk_ref[...] = col(1) * k_scale
    rv_ref[...] = col(2).astype(BF16)
    rg_ref[...] = col(3)
    dq_ref[...] = (col(4) * q_scale).astype(BF16)


def _inproj(x2, g, w_bf, batch, seq, tile, n_heads, dk_ret, dh_diff, key_block, riders=()):
    m, d = x2.shape
    width = n_heads * HEAD_W
    nt = seq // tile
    steps = m // tile
    row = lambda i: (i, 0)
    assert all(r.shape[0] % (16 * steps) == 0 for r in riders), "rider row blocks must be bf16-tile aligned"
    rider_specs = [pl.BlockSpec((r.shape[0] // steps, r.shape[1]), row) for r in riders]
    f32_out = jax.ShapeDtypeStruct((m, width), F32)
    bf_out = jax.ShapeDtypeStruct((m, width), BF16)
    spec = pl.BlockSpec((tile, width), row)
    if key_block is None:
        kx_shape, kx_spec, vx_shape = f32_out, spec, f32_out
    else:
        kx_shape = jax.ShapeDtypeStruct((m // key_block, width, key_block), BF16)
        kx_spec = pl.BlockSpec((tile // key_block, width, key_block), lambda i: (i, 0, 0))
        vx_shape = bf_out
    body = functools.partial(_inproj_body, n_riders=len(riders), width=width, n_heads=n_heads,
                             k_scale=dk_ret ** -0.5, q_scale=dh_diff ** -0.5, key_block=key_block)
    return pl.pallas_call(
        body,
        grid=(steps,),
        in_specs=[pl.BlockSpec((tile, d), row), _const_spec((1, d)), _const_spec(w_bf.shape)] + rider_specs,
        out_specs=[spec] * 5 + [pl.BlockSpec((1, width, tile), lambda i: (i // nt, 0, i % nt)),
                                pl.BlockSpec((tile, n_heads, HEAD_W), lambda i: (i, 0, 0)), kx_spec, spec]
        + rider_specs,
        out_shape=[f32_out, f32_out, bf_out, f32_out, bf_out,
                   jax.ShapeDtypeStruct((batch, width, seq), F32),
                   jax.ShapeDtypeStruct((m, n_heads, HEAD_W), F32), kx_shape, vx_shape]
        + [jax.ShapeDtypeStruct(r.shape, BF16) for r in riders],
        compiler_params=_params("arbitrary"),
        name="inproj",
    )(x2, g, w_bf, *riders)


def _retention_decay(length, h):
    i = lax.broadcasted_iota(jnp.int32, (length, length), 0)
    j = lax.broadcasted_iota(jnp.int32, (length, length), 1)
    diff = (i - j).astype(F32)
    causal = diff >= 0
    return jnp.where(causal, jnp.exp(jnp.where(causal, diff, 0.0) * _ret_log_decay(h)), 0.0)


def _retention_row_decays(length, h):
    lg = _ret_log_decay(h)
    ri = lax.broadcasted_iota(jnp.int32, (length, 1), 0).astype(F32)
    return jnp.exp((ri + 1.0) * lg), jnp.exp((length - 1.0 - ri) * lg), math.exp(length * lg)


def _pair_loop(n, body):
    done = 0
    for width in (4, 2, 1):
        trips = lax.shift_right_logical(n - done, width.bit_length() - 1)

        def several(i, carry, width=width, done=done):
            for j in range(width):
                body(done + width * i + j)
            return carry

        lax.fori_loop(0, trips, several, 0)
        done = done + width * trips


def _gated_group_norm(o, g):
    on = o * lax.rsqrt(jnp.mean(o * o, axis=-1, keepdims=True) + EPS)
    return on * (g * jax.nn.sigmoid(g))


def _mixer_body(lam_ref, rq_ref, rk_ref, rv_ref, rg_ref, dq_ref, kt_ref, vb_ref, x_ref, wo_ref,
                gpost_ref, subg_ref, srq_ref, srk_ref, srv_ref, srg_ref, sst_ref,
                y_ref, s_ref, sa_ref, ssn_ref,
                mix_scr, qq_scr, sc_scr, mx_scr, acc_scr, decay_scr,
                *, tq, ret_chunk, n_heads, lam_i, dec_period):
    t = pl.program_id(1)
    ret_w = n_heads * HEAD_W

    @pl.when((pl.program_id(0) * pl.num_programs(1) + t) % dec_period == 0)
    def _():
        _ret_decode_rows(srq_ref, srk_ref, srv_ref, srg_ref, sst_ref, sa_ref, ssn_ref,
                         grp=sa_ref.shape[0], n_heads=n_heads)

    @pl.when(t == 0)
    def _():
        s_ref[...] = jnp.zeros_like(s_ref)

    @pl.when((pl.program_id(0) == 0) & (t == 0))
    def _():
        for h in range(n_heads):
            decay_scr[h] = _retention_decay(ret_chunk, h)

    for c in range(tq // ret_chunk):
        rows = slice(c * ret_chunk, (c + 1) * ret_chunk)
        for h in range(n_heads):
            cols = slice(h * HEAD_W, (h + 1) * HEAD_W)
            decay = decay_scr[h]
            qdec, kdec, gl = _retention_row_decays(ret_chunk, h)
            q = rq_ref[rows, cols]
            k = rk_ref[rows, cols]
            v = rv_ref[rows, cols]
            s = s_ref[0, h]
            att = lax.dot_general(q.astype(BF16), k.astype(BF16), NT_DIMS,
                                  preferred_element_type=F32) * decay
            o = (jnp.dot(att.astype(BF16), v, preferred_element_type=F32)
                 + jnp.dot((q * qdec).astype(BF16), s.astype(BF16), preferred_element_type=F32))
            s_ref[0, h] = gl * s + lax.dot_general((k * kdec).astype(BF16), v, TN_DIMS,
                                                   preferred_element_type=F32)
            mix_scr[rows, cols] = _gated_group_norm(o, rg_ref[rows, cols]).astype(BF16)

    lane = lax.broadcasted_iota(jnp.int32, (tq, HEAD_W), 1)
    for h in range(n_heads):
        qh = dq_ref[:, h * HEAD_W:(h + 1) * HEAD_W]
        zero = jnp.zeros_like(qh)
        qq_scr[h, 0:tq] = jnp.where(lane < HEAD_W // 2, qh, zero)
        qq_scr[h, tq:2 * tq] = jnp.where(lane >= HEAD_W // 2, qh, zero)
    mx_scr[...] = jnp.full(mx_scr.shape, FINITE_MIN, F32)
    acc_scr[...] = jnp.zeros_like(acc_scr)
    key_j = lax.broadcasted_iota(jnp.int32, (1, tq), 1).astype(F32)
    lane_tiles = [slice(c * HEAD_W, (c + 1) * HEAD_W) for c in range(tq // HEAD_W)]

    def fold(x, op):
        return functools.reduce(op, [x[:, c] for c in lane_tiles])

    def scores(kb, causal):
        key_pos = key_j + ((kb - t) * tq).astype(F32)
        for h in range(n_heads):
            slope = 2.0 ** (-8.0 / n_heads * (h + 1))
            s = jnp.dot(qq_scr[h], kt_ref[kb, h * HEAD_W:(h + 1) * HEAD_W, :], preferred_element_type=F32)
            s = (s + slope * key_pos) * LOG2_E
            if causal is not None:
                s = jnp.where(causal, s, NEG_INF)
            sc_scr[kb, h] = s
            mx_scr[h] = jnp.maximum(mx_scr[h], fold(s, jnp.maximum))

    _pair_loop(t, lambda kb: scores(kb, None))
    ii = lax.broadcasted_iota(jnp.int32, (2 * tq, tq), 0)
    jj = lax.broadcasted_iota(jnp.int32, (2 * tq, tq), 1)
    scores(t, jnp.where(ii >= tq, ii - tq, ii) >= jj)

    for h in range(n_heads):
        mx_scr[h] = jnp.broadcast_to(jnp.max(mx_scr[h], axis=-1, keepdims=True), (2 * tq, HEAD_W))

    ones = jnp.ones((tq, HEAD_W), BF16)

    def weighted(kb):
        for h in range(n_heads):
            m = mx_scr[h]
            e = [jnp.exp2(sc_scr[kb, h, :, c] - m).astype(BF16) for c in lane_tiles]
            vr = vb_ref[pl.ds(pl.multiple_of(kb * tq, tq), tq), h * HEAD_W:(h + 1) * HEAD_W]
            acc_scr[h] += jnp.dot(jnp.concatenate(e, axis=1), jnp.concatenate([vr, ones], axis=1),
                                  preferred_element_type=F32)

    _pair_loop(t + 1, weighted)
    lam = _lambda(lam_ref, lam_i)
    subg = subg_ref[...]
    for h in range(n_heads):
        on = acc_scr[h, :, 0:HEAD_W] / acc_scr[h, :, HEAD_W:2 * HEAD_W]
        o = _rms(on[:tq] - lam * on[tq:], subg) * (1.0 - lam_i)
        mix_scr[:, ret_w + h * HEAD_W:ret_w + (h + 1) * HEAD_W] = o.astype(BF16)

    mo = jnp.dot(mix_scr[...], wo_ref[...], preferred_element_type=F32)
    y_ref[...] = x_ref[...] + _rms(mo, gpost_ref[...])


def _mixer(lamv, rq, rk, rv, rg, dq, ktb, dvb, x2, wo_bf, gpost, subg, batch, seq, n_heads, lam_i,
           srq, srk, srv, srg, sstate):
    m, d = x2.shape
    w = rq.shape[1]
    tq = MIX_TILE
    nt = seq // tq
    n_dec = srq.shape[0]
    grp = DEC_GROUP
    dec_period = batch * nt * grp // n_dec
    assert dec_period * n_dec == batch * nt * grp, "sample row groups must spread evenly over the mixer steps"
    tile = lambda b, t: (b * nt + t, 0)
    whole = lambda b, t: (b, 0)
    dec_rows = pl.BlockSpec((grp, w), lambda b, t: ((b * nt + t) // dec_period, 0))
    dec_state = pl.BlockSpec((grp, n_heads, HEAD_W, HEAD_W), lambda b, t: ((b * nt + t) // dec_period, 0, 0, 0))
    body = functools.partial(_mixer_body, tq=tq, ret_chunk=RET_CHUNK, n_heads=n_heads, lam_i=lam_i,
                             dec_period=dec_period)
    return pl.pallas_call(
        body,
        grid=(batch, nt),
        in_specs=[
            _const_spec(lamv.shape),
            pl.BlockSpec((tq, w), tile), pl.BlockSpec((tq, w), tile), pl.BlockSpec((tq, w), tile),
            pl.BlockSpec((tq, w), tile), pl.BlockSpec((tq, w), tile),
            pl.BlockSpec((nt, w, tq), lambda b, t: (b, 0, 0)), pl.BlockSpec((seq, w), whole),
            pl.BlockSpec((tq, d), tile),
            _const_spec(wo_bf.shape), _const_spec((1, d)), _const_spec((1, HEAD_W)),
            dec_rows, dec_rows, dec_rows, dec_rows, dec_state,
        ],
        out_specs=[pl.BlockSpec((tq, d), tile),
                   pl.BlockSpec((1, n_heads, HEAD_W, HEAD_W), lambda b, t: (b, 0, 0, 0)),
                   dec_rows, dec_state],
        out_shape=[jax.ShapeDtypeStruct((m, d), F32),
                   jax.ShapeDtypeStruct((batch, n_heads, HEAD_W, HEAD_W), F32),
                   jax.ShapeDtypeStruct((n_dec, w), BF16), jax.ShapeDtypeStruct(sstate.shape, F32)],
        scratch_shapes=[
            pltpu.VMEM((tq, 2 * w), BF16),
            pltpu.VMEM((n_heads, 2 * tq, HEAD_W), BF16),
            pltpu.VMEM((nt, n_heads, 2 * tq, tq), F32),
            pltpu.VMEM((n_heads, 2 * tq, HEAD_W), F32),
            pltpu.VMEM((n_heads, 2 * tq, 2 * HEAD_W), F32),
            pltpu.VMEM((n_heads, RET_CHUNK, RET_CHUNK), F32),
        ],
        compiler_params=_params("arbitrary", "arbitrary"),
        name="mixer",
    )(lamv, rq, rk, rv, rg, dq, ktb, dvb, x2, wo_bf, gpost, subg, srq, srk, srv, srg, sstate)


def _conv_gate(g, u, g1, g2, cw_ref, cb_ref, cols):
    c = cw_ref[0:1, cols] * g2 + cw_ref[1:2, cols] * g1 + cw_ref[2:3, cols] * g + cb_ref[:, cols]
    return (jax.nn.gelu(c) * u).astype(BF16)


def _ffn_body(pt_ref, x_ref, gpre_ref, win_ref, cw_ref, cb_ref, wout_ref, gpost_ref,
              lam_ref, q_ref, kn_ref, vn_ref, ck_hbm, cv_hbm, y_ref, cs_ref, o_ref,
              act, kbuf, vbuf, sc_scr, sem, gbuf, carry, *, tf, d_ff, ck, rows, n_pages, page, n_heads, lam_i):
    t = pl.program_id(1)
    step = pl.program_id(0) * pl.num_programs(1) + t
    n_rows = rows * pl.num_programs(0) * pl.num_programs(1)
    copies = functools.partial(_page_copies, pt_ref, ck_hbm, cv_hbm, kbuf, vbuf, sem, n_rows=q_ref.shape[0],
                               n_pages=n_pages)

    @pl.when(step == 0)
    def _():
        for slot in range(2):
            for c in copies(b=slot, slot=slot):
                c.start()

    @pl.when(t == 0)
    def _():
        carry[...] = jnp.zeros_like(carry)

    lam = _lambda(lam_ref, lam_i)
    for i in range(rows):
        g = step * rows + i
        slot = i % 2
        for c in copies(b=g, slot=slot):
            c.wait()
        this = pl.ds(g, 1)
        o_ref[this, :] = _paged_row(slot, q_ref[this, :], kn_ref[this, :], vn_ref[this, :], lam, kbuf, vbuf,
                                    sc_scr, n_pages=n_pages, page=page, n_heads=n_heads)

        @pl.when(g + 2 < n_rows)
        def _():
            for c in copies(b=g + 2, slot=slot):
                c.start()

    x = x_ref[...]
    h = _rms(x, gpre_ref[...]).astype(BF16)
    for j in range(d_ff // ck):
        cols = slice(j * ck, (j + 1) * ck)
        g = jnp.dot(h, win_ref[:, cols], preferred_element_type=F32)
        u = jnp.dot(h, win_ref[:, d_ff + j * ck:d_ff + (j + 1) * ck], preferred_element_type=F32)
        gbuf[0:8, :] = carry[:, cols]
        gbuf[8:8 + tf, :] = g
        act[:, cols] = _conv_gate(g, u, gbuf[7:7 + tf, :], gbuf[6:6 + tf, :], cw_ref, cb_ref, cols)
        carry[:, cols] = gbuf[tf:tf + 8, :]
    f = jnp.dot(act[...], wout_ref[...], preferred_element_type=F32)
    y_ref[...] = x + _rms(f, gpost_ref[...])
    cs_ref[0] = carry[...]


def _ffn_paged(x2, gpre, win_bf, cw, cb, wout_bf, gpost, batch, seq,
               page_table, lamv, dq, dk, dv, cache_kt, cache_v, n_heads, lam_i):
    m, d = x2.shape
    d_ff = wout_bf.shape[0]
    tf = FFN_TILE
    nt = seq // tf
    n, w = dq.shape
    rows = n // (batch * nt)
    assert rows * batch * nt == n and rows % 2 == 0, "sample rows must split evenly, in pairs, over the FFN steps"
    n_pages = page_table.shape[1]
    page = cache_kt.shape[2]
    grp = _page_group(page)
    assert n_pages % grp == 0
    tile = lambda b, t, pt: (b * nt + t, 0)
    const = _const_spec
    rspec = const((n, w))
    body = functools.partial(_ffn_body, tf=tf, d_ff=d_ff, ck=FFN_CHUNK, rows=rows, n_pages=n_pages, page=page,
                             n_heads=n_heads, lam_i=lam_i)
    grid_spec = pltpu.PrefetchScalarGridSpec(
        num_scalar_prefetch=1,
        grid=(batch, nt),
        in_specs=[pl.BlockSpec((tf, d), tile), const((1, d)), const(win_bf.shape), const(cw.shape),
                  const(cb.shape), const(wout_bf.shape), const((1, d)),
                  const(lamv.shape), rspec, rspec, rspec,
                  pl.BlockSpec(memory_space=pl.ANY), pl.BlockSpec(memory_space=pl.ANY)],
        out_specs=[pl.BlockSpec((tf, d), tile), pl.BlockSpec((1, 8, d_ff), lambda b, t, pt: (b, 0, 0)),
                   pl.BlockSpec((n, w), lambda b, t, pt: (0, 0))],
        scratch_shapes=[pltpu.VMEM((tf, d_ff), BF16),
                        pltpu.VMEM((2, n_pages, w, page), F32),
                        pltpu.VMEM((2, n_pages, page * n_heads, HEAD_W), F32),
                        pltpu.VMEM((n_pages // grp, 2 * n_heads, grp * page), F32),
                        pltpu.SemaphoreType.DMA((2, 2)),
                        pltpu.VMEM((tf + 8, FFN_CHUNK), F32), pltpu.VMEM((8, d_ff), F32)],
    )
    y, cs, o = pl.pallas_call(
        body,
        grid_spec=grid_spec,
        out_shape=[jax.ShapeDtypeStruct((m, d), F32), jax.ShapeDtypeStruct((batch, 8, d_ff), F32),
                   jax.ShapeDtypeStruct((n, w), F32)],
        compiler_params=_params("arbitrary", "arbitrary"),
        name="conv_ffn_paged_attn",
    )(page_table.T.reshape(-1), x2, gpre, win_bf, cw, cb, wout_bf, gpost,
      lamv, dq.astype(F32), dk, dv, cache_kt, cache_v)
    return y, cs, o


def _ret_decode_rows(rq_ref, rk_ref, rv_ref, rg_ref, s_ref, a_ref, sn_ref, *, grp, n_heads):
    row = lax.broadcasted_iota(jnp.int32, (grp, grp * HEAD_W), 0)
    lane = lax.broadcasted_iota(jnp.int32, (grp, grp * HEAD_W), 1)
    own = (lane // HEAD_W) == row

    def block_diag(x):
        return jnp.where(own, jnp.tile(x, (1, grp)), 0.0).astype(BF16)

    for h in range(n_heads):
        cols = slice(h * HEAD_W, (h + 1) * HEAD_W)
        gamma = math.exp(_ret_log_decay(h))
        q = rq_ref[:, cols]
        k = rk_ref[:, cols]
        v = rv_ref[:, cols]
        qb = q.astype(BF16).astype(F32)
        kb = k.astype(BF16).astype(F32)
        att = jnp.sum(qb * kb, axis=-1, keepdims=True).astype(BF16).astype(F32)
        s = s_ref[:, h].reshape(grp * HEAD_W, HEAD_W)
        o = att * v.astype(F32) + jnp.dot(block_diag(q * gamma), s.astype(BF16),
                                          preferred_element_type=F32)
        upd = lax.dot_general(block_diag(k), v, TN_DIMS, preferred_element_type=F32)
        sn_ref[:, h] = (gamma * s + upd).reshape(grp, HEAD_W, HEAD_W)
        a_ref[:, cols] = _gated_group_norm(o, rg_ref[:, cols]).astype(BF16)


def _page_copies(pt_ref, ck_hbm, cv_hbm, kbuf, vbuf, sem, b, slot, n_rows, n_pages):
    out = []
    for p in range(n_pages):
        page = pt_ref[p * n_rows + b]
        out.append(pltpu.make_async_copy(ck_hbm.at[page], kbuf.at[slot, p], sem.at[0, slot]))
        out.append(pltpu.make_async_copy(cv_hbm.at[page], vbuf.at[slot, p], sem.at[1, slot]))
    return out


def _paged_row(slot, q, kn_row, vn_row, lam, kbuf, vbuf, sc_scr, *, n_pages, page, n_heads):
    w = n_heads * HEAD_W
    nc = 2 * n_heads
    past = n_pages * page
    ci = lax.broadcasted_iota(jnp.int32, (nc, w), 0)
    ri = lax.broadcasted_iota(jnp.int32, (nc, w), 1)
    col_of = ri // HEAD_W + n_heads * ((ri % HEAD_W) // (HEAD_W // 2))
    q_sel = jnp.where(ci == col_of, jnp.broadcast_to(q, (nc, w)), 0.0)
    q_sel_bf = q_sel.astype(BF16)
    grp = _page_group(page)
    span = grp * page
    rowc = lax.broadcasted_iota(jnp.int32, (nc, span), 0)
    key_i = lax.broadcasted_iota(jnp.int32, (nc, span), 1)
    slope = jnp.zeros((nc, span), F32)
    for h in range(n_heads):
        slope = jnp.where(rowc % n_heads == h, 2.0 ** (-8.0 / n_heads * (h + 1)), slope)

    mx = jnp.full((nc, span), FINITE_MIN, F32)
    for pg in range(n_pages // grp):
        kt = jnp.concatenate([kbuf[slot, grp * pg + i] for i in range(grp)], axis=1).astype(BF16)
        dist = (past - (pg * span + key_i)).astype(F32)
        s = jnp.dot(q_sel_bf, kt, preferred_element_type=F32) - slope * dist
        sc_scr[pg] = s
        mx = jnp.maximum(mx, s)
    kn = kn_row.astype(BF16).astype(F32)
    s_self = jnp.broadcast_to(jnp.sum(q_sel * kn, axis=-1, keepdims=True), (nc, span))
    m = jnp.maximum(jnp.broadcast_to(jnp.max(mx, axis=-1, keepdims=True), (nc, span)), s_self)

    lp = jnp.zeros((nc, span), F32)
    for pg in range(n_pages // grp):
        e = jnp.exp(sc_scr[pg] - m)
        sc_scr[pg] = e
        lp = lp + e
    e_self = jnp.exp(s_self - m)
    l = jnp.broadcast_to(jnp.sum(lp, axis=-1, keepdims=True), (nc, span)) + e_self
    r = jnp.where(rowc < n_heads, 1.0, lam) / l

    def combine(pn):
        return (pn - pltpu.roll(pn, n_heads, axis=0)).astype(BF16)

    def v_rows(p):
        return jnp.concatenate([vbuf[slot, p, pl.ds(h, page, stride=n_heads), :] for h in range(n_heads)],
                               axis=1)

    acc = jnp.zeros((nc, w), F32)
    for pg in range(n_pages // grp):
        v = jnp.concatenate([v_rows(grp * pg + i) for i in range(grp)], axis=0).astype(BF16)
        acc = acc + jnp.dot(combine(sc_scr[pg] * r), v, preferred_element_type=F32)
    a_self = combine(e_self * r).astype(F32)
    vn = vn_row.astype(BF16).astype(F32)
    return jnp.concatenate(
        [acc[h:h + 1, h * HEAD_W:(h + 1) * HEAD_W]
         + a_self[h:h + 1, 0:HEAD_W] * vn[:, h * HEAD_W:(h + 1) * HEAD_W] for h in range(n_heads)], axis=-1)


def _dec_ffn_body(a_ref, do_ref, x_ref, wo_ref, gmix_ref, subg_ref, gpre_ref, wg_ref, wu_ref, cw_ref,
                  cb_ref, sc_ref, wout_ref, gpost_ref, y_ref, g_ref, xmid, hbuf, acc, *, lam_i):
    j = pl.program_id(0)
    ret_w = a_ref.shape[1]

    @pl.when(j == 0)
    def _():
        mo = jnp.dot(a_ref[...], wo_ref[0:ret_w, :], preferred_element_type=F32)
        for h in range(do_ref.shape[1] // HEAD_W):
            cols = slice(h * HEAD_W, (h + 1) * HEAD_W)
            bh = (_rms(do_ref[:, cols], subg_ref[...]) * (1.0 - lam_i)).astype(BF16)
            mo += jnp.dot(bh, wo_ref[ret_w + h * HEAD_W:ret_w + (h + 1) * HEAD_W, :],
                          preferred_element_type=F32)
        xm = x_ref[...] + _rms(mo, gmix_ref[...])
        xmid[...] = xm
        hbuf[...] = _rms(xm, gpre_ref[...]).astype(BF16)
        acc[...] = jnp.zeros_like(acc)

    h = hbuf[...]
    g = jnp.dot(h, wg_ref[...], preferred_element_type=F32)
    u = jnp.dot(h, wu_ref[...], preferred_element_type=F32)
    g_ref[...] = g
    c = cw_ref[0:1, :] * sc_ref[0] + cw_ref[1:2, :] * sc_ref[1] + cw_ref[2:3, :] * g + cb_ref[...]
    acc[...] += jnp.dot((jax.nn.gelu(c) * u).astype(BF16), wout_ref[...], preferred_element_type=F32)

    @pl.when(j == pl.num_programs(0) - 1)
    def _():
        y_ref[...] = xmid[...] + _rms(acc[...], gpost_ref[...])


def _dec_ffn(a, do, x2, wo_bf, gmix, subg, gpre, win_bf, cw, cb, conv_state_t, wout_bf, gpost, lam_i):
    n, d = x2.shape
    d_ff = wout_bf.shape[0]
    ck = d_ff // DEC_FFN_STEPS
    assert ck * DEC_FFN_STEPS == d_ff and ck % HEAD_W == 0
    nj = DEC_FFN_STEPS
    full = lambda shape: pl.BlockSpec(shape, lambda j: (0,) * len(shape))
    return pl.pallas_call(
        functools.partial(_dec_ffn_body, lam_i=lam_i),
        grid=(nj,),
        in_specs=[full(a.shape), full(do.shape), full(x2.shape), full(wo_bf.shape), full((1, d)),
                  full((1, HEAD_W)), full((1, d)),
                  pl.BlockSpec((d, ck), lambda j: (0, j)), pl.BlockSpec((d, ck), lambda j: (0, nj + j)),
                  pl.BlockSpec((3, ck), lambda j: (0, j)), pl.BlockSpec((1, ck), lambda j: (0, j)),
                  pl.BlockSpec((2, n, ck), lambda j: (0, 0, j)),
                  pl.BlockSpec((ck, d), lambda j: (j, 0)), full((1, d))],
        out_specs=[full((n, d)), pl.BlockSpec((n, ck), lambda j: (0, j))],
        out_shape=[jax.ShapeDtypeStruct((n, d), F32), jax.ShapeDtypeStruct((n, d_ff), F32)],
        scratch_shapes=[pltpu.VMEM((n, d), F32), pltpu.VMEM((n, d), BF16), pltpu.VMEM((n, d), F32)],
        compiler_params=_params("arbitrary"),
        name="dec_ffn",
    )(a, do, x2, wo_bf, gmix, subg, gpre, win_bf, win_bf, cw, cb, conv_state_t, wout_bf, gpost)


def kernel(x_prompt, x_sample, state_ret, cache_k, cache_v, state_conv, page_table,
           norm_mix_pre, norm_mix_post, w_in, w_o, lambda_q1, lambda_k1, lambda_q2, lambda_k2,
           subln_g, norm_ffn_pre, norm_ffn_post, w_ffn_in, conv_w, conv_b, w_ffn_out):
    batch, seq, d = x_prompt.shape
    n_dec = x_sample.shape[0]
    assert x_sample.shape[1] == 1, "the sample group is one token per row"
    depth = w_in.shape[0]
    n_heads = state_ret.shape[2]
    dk_ret = state_ret.shape[3]
    dh_diff = cache_k.shape[-1]
    d_ff = w_ffn_out.shape[1]
    w = n_heads * HEAD_W
    assert w_in.shape[2] == 7 * w and cache_v.shape[-1] == HEAD_W and 2 * dh_diff == HEAD_W
    assert seq % MIX_TILE == 0 and seq % FFN_TILE == 0 and d_ff % FFN_CHUNK == 0
    assert n_dec % DEC_GROUP == 0 and n_dec % V7X_LANES == 0 and seq % IN_TILE == 0 and IN_TILE % MIX_TILE == 0
    assert conv_w.shape[1] == 3 and state_conv.shape[2] == 2, "the FFN kernels carry exactly two past gate rows"
    assert dk_ret == HEAD_W and state_ret.shape[4] == HEAD_W

    xp = x_prompt.reshape(batch * seq, d)
    xs = x_sample.reshape(n_dec, d)
    outs = [[] for _ in range(8)]
    for l in range(depth):
        lam_i = 0.8 - 0.6 * math.exp(-0.3 * l)
        lamv = jnp.stack([lambda_q1[l], lambda_k1[l], lambda_q2[l], lambda_k2[l]]).astype(F32)
        row = lambda v: v.reshape(1, -1)
        w_in_bf = w_in[l].astype(BF16)
        g_pre, g_post = row(norm_mix_pre[l]), row(norm_mix_post[l])
        f_pre, f_post = row(norm_ffn_pre[l]), row(norm_ffn_post[l])
        subg = row(subln_g[l])
        cw, cb = conv_w[l], row(conv_b[l])

        cache_kt = jnp.transpose(cache_k[l], (0, 2, 3, 4, 1)).reshape(cache_k.shape[1], w, cache_k.shape[2])
        cache_vr = cache_v[l].reshape(cache_v.shape[1], cache_v.shape[2] * n_heads, HEAD_W)
        proj = functools.partial(_inproj, n_heads=n_heads, dk_ret=dk_ret, dh_diff=dh_diff)

        srq, srk, srv, srg, dq, skt, sv4, dk, dv = proj(xs, g_pre, w_in_bf, 1, n_dec, n_dec, key_block=None)
        (rq, rk, rv, rg, pdq, kt, v4, ktb, dvb, w_o_bf, w_ffn_in_bf, w_ffn_out_bf) = proj(
            xp, g_pre, w_in_bf, batch, seq, IN_TILE, key_block=MIX_TILE,
            riders=(w_o[l], w_ffn_in[l], w_ffn_out[l]))
        xp, s_fin, a, s_new = _mixer(lamv, rq, rk, rv, rg, pdq, ktb, dvb, xp, w_o_bf, g_post, subg,
                                     batch, seq, n_heads, lam_i, srq, srk, srv, srg, state_ret[l])
        outs[0].append(s_fin)
        outs[2].append(jnp.transpose(kt.reshape(batch, n_heads, 2, dh_diff, seq), (0, 4, 1, 2, 3)))
        outs[3].append(v4.reshape(batch, seq, n_heads, HEAD_W))

        xp, cs, do = _ffn_paged(xp, f_pre, w_ffn_in_bf, cw, cb, w_ffn_out_bf, f_post, batch, seq,
                                page_table, lamv, dq, dk, dv, cache_kt, cache_vr, n_heads, lam_i)
        outs[6].append(cs[:, 6:8, :])

        conv_t = jnp.swapaxes(state_conv[l], 0, 1)
        xs, g_new = _dec_ffn(a, do, xs, w_o_bf, g_post, subg, f_pre, w_ffn_in_bf, cw, cb, conv_t,
                             w_ffn_out_bf, f_post, lam_i)
        outs[1].append(s_new)
        outs[4].append(jnp.transpose(skt.reshape(n_heads, 2, dh_diff, n_dec), (3, 0, 1, 2))[:, None])
        outs[5].append(sv4.reshape(n_dec, 1, n_heads, HEAD_W))
        outs[7].append(jnp.stack([state_conv[l][:, 1, :], g_new], axis=1))

    st = [jnp.stack(o) for o in outs]
    return (xp.reshape(batch, seq, d), xs.reshape(n_dec, 1, d), st[0], st[1], st[2], st[3], st[4], st[5],
            st[6], st[7])
```

```python
import functools
import math

import jax
import jax.numpy as jnp
from jax import lax
from jax.experimental import pallas as pl
from jax.experimental.pallas import tpu as pltpu

F32 = jnp.float32
BF16 = jnp.bfloat16
EPS = 1e-6
NEG_INF = -1e30
FINITE_MIN = -3e38
LOG2_E = math.log2(math.e)

V7X_VMEM_BYTES = 64 * 1024 * 1024
V7X_MXU_DIM = 256
V7X_LANES = 128
VMEM_LIMIT_BYTES = V7X_VMEM_BYTES * 7 // 8

HEAD_W = V7X_LANES
IN_TILE = 1024
MIX_TILE = V7X_MXU_DIM
RET_CHUNK = MIX_TILE
FFN_TILE = 256
FFN_CHUNK = V7X_MXU_DIM
DEC_GROUP = 8
DEC_FFN_STEPS = 2
NT_DIMS = (((1,), (1,)), ((), ()))
TN_DIMS = (((0,), (0,)), ((), ()))


def _page_group(page):
    return max(1, V7X_MXU_DIM // page)


def _rms(x, g):
    return x * lax.rsqrt(jnp.mean(x * x, axis=-1, keepdims=True) + EPS) * g


def _params(*sem):
    return pltpu.CompilerParams(dimension_semantics=sem, vmem_limit_bytes=VMEM_LIMIT_BYTES)


def _const_spec(shape):
    n = len(shape)
    return pl.BlockSpec(shape, lambda *_: (0,) * n, pipeline_mode=pl.Buffered(1))


def _ret_log_decay(h):
    return math.log(1.0 - 2.0 ** (-5.0 - h))


def _lambda(lam_ref, lam_i):
    a = jnp.sum(lam_ref[0:1, :] * lam_ref[1:2, :], axis=-1, keepdims=True)
    b = jnp.sum(lam_ref[2:3, :] * lam_ref[3:4, :], axis=-1, keepdims=True)
    return jnp.exp(a) - jnp.exp(b) + lam_i


def _inproj_body(x_ref, g_ref, w_ref, *rest, n_riders, width, n_heads, k_scale, q_scale, key_block):
    rider_in, rest = rest[:n_riders], rest[n_riders:]
    rq_ref, rk_ref, rv_ref, rg_ref, dq_ref, kt_ref, v4_ref, kx_ref, vx_ref = rest[:9]
    for src, dst in zip(rider_in, rest[9:]):
        dst[...] = src[...].astype(BF16)
    h = _rms(x_ref[...], g_ref[...]).astype(BF16)

    def col(j):
        return jnp.dot(h, w_ref[:, j * width:(j + 1) * width], preferred_element_type=F32)

    dk = col(5)
    kt = dk.T
    kt_ref[0] = kt
    if key_block is None:
        kx_ref[...] = dk
    else:
        for c in range(kx_ref.shape[0]):
            kx_ref[c] = kt[:, c * key_block:(c + 1) * key_block].astype(BF16)
    dv = col(6)
    for hh in range(n_heads):
        v4_ref[:, hh, :] = dv[:, hh * HEAD_W:(hh + 1) * HEAD_W]
    vx_ref[...] = dv if key_block is None else dv.astype(BF16)
    rq_ref[...] = col(0)
    rk_ref[...] = col(1) * k_scale
    rv_ref[...] = col(2).astype(BF16)
    rg_ref[...] = col(3)
    dq_ref[...] = (col(4) * q_scale).astype(BF16)


def _inproj(x2, g, w_bf, batch, seq, tile, n_heads, dk_ret, dh_diff, key_block, riders=()):
    m, d = x2.shape
    width = n_heads * HEAD_W
    nt = seq // tile
    steps = m // tile
    row = lambda i: (i, 0)
    assert all(r.shape[0] % (16 * steps) == 0 for r in riders), "rider row blocks must be bf16-tile aligned"
    rider_specs = [pl.BlockSpec((r.shape[0] // steps, r.shape[1]), row) for r in riders]
    f32_out = jax.ShapeDtypeStruct((m, width), F32)
    bf_out = jax.ShapeDtypeStruct((m, width), BF16)
    spec = pl.BlockSpec((tile, width), row)
    if key_block is None:
        kx_shape, kx_spec, vx_shape = f32_out, spec, f32_out
    else:
        kx_shape = jax.ShapeDtypeStruct((m // key_block, width, key_block), BF16)
        kx_spec = pl.BlockSpec((tile // key_block, width, key_block), lambda i: (i, 0, 0))
        vx_shape = bf_out
    body = functools.partial(_inproj_body, n_riders=len(riders), width=width, n_heads=n_heads,
                             k_scale=dk_ret ** -0.5, q_scale=dh_diff ** -0.5, key_block=key_block)
    return pl.pallas_call(
        body,
        grid=(steps,),
        in_specs=[pl.BlockSpec((tile, d), row), _const_spec((1, d)), _const_spec(w_bf.shape)] + rider_specs,
        out_specs=[spec] * 5 + [pl.BlockSpec((1, width, tile), lambda i: (i // nt, 0, i % nt)),
                                pl.BlockSpec((tile, n_heads, HEAD_W), lambda i: (i, 0, 0)), kx_spec, spec]
        + rider_specs,
        out_shape=[f32_out, f32_out, bf_out, f32_out, bf_out,
                   jax.ShapeDtypeStruct((batch, width, seq), F32),
                   jax.ShapeDtypeStruct((m, n_heads, HEAD_W), F32), kx_shape, vx_shape]
        + [jax.ShapeDtypeStruct(r.shape, BF16) for r in riders],
        compiler_params=_params("arbitrary"),
        name="inproj",
    )(x2, g, w_bf, *riders)


def _retention_decay(length, h):
    i = lax.broadcasted_iota(jnp.int32, (length, length), 0)
    j = lax.broadcasted_iota(jnp.int32, (length, length), 1)
    diff = (i - j).astype(F32)
    causal = diff >= 0
    return jnp.where(causal, jnp.exp(jnp.where(causal, diff, 0.0) * _ret_log_decay(h)), 0.0)


def _retention_row_decays(length, h):
    lg = _ret_log_decay(h)
    ri = lax.broadcasted_iota(jnp.int32, (length, 1), 0).astype(F32)
    return jnp.exp((ri + 1.0) * lg), jnp.exp((length - 1.0 - ri) * lg), math.exp(length * lg)


def _pair_loop(n, body):
    done = 0
    for width in (4, 2, 1):
        trips = lax.shift_right_logical(n - done, width.bit_length() - 1)

        def several(i, carry, width=width, done=done):
            for j in range(width):
                body(done + width * i + j)
            return carry

        lax.fori_loop(0, trips, several, 0)
        done = done + width * trips


def _gated_group_norm(o, g):
    on = o * lax.rsqrt(jnp.mean(o * o, axis=-1, keepdims=True) + EPS)
    return on * (g * jax.nn.sigmoid(g))


def _mixer_body(lam_ref, rq_ref, rk_ref, rv_ref, rg_ref, dq_ref, kt_ref, vb_ref, x_ref, wo_ref,
                gpost_ref, subg_ref, srq_ref, srk_ref, srv_ref, srg_ref, sst_ref,
                y_ref, s_ref, sa_ref, ssn_ref,
                mix_scr, qq_scr, sc_scr, mx_scr, acc_scr, decay_scr,
                *, tq, ret_chunk, n_heads, lam_i, dec_period):
    t = pl.program_id(1)
    ret_w = n_heads * HEAD_W

    @pl.when((pl.program_id(0) * pl.num_programs(1) + t) % dec_period == 0)
    def _():
        _ret_decode_rows(srq_ref, srk_ref, srv_ref, srg_ref, sst_ref, sa_ref, ssn_ref,
                         grp=sa_ref.shape[0], n_heads=n_heads)

    @pl.when(t == 0)
    def _():
        s_ref[...] = jnp.zeros_like(s_ref)

    @pl.when((pl.program_id(0) == 0) & (t == 0))
    def _():
        for h in range(n_heads):
            decay_scr[h] = _retention_decay(ret_chunk, h)

    for c in range(tq // ret_chunk):
        rows = slice(c * ret_chunk, (c + 1) * ret_chunk)
        for h in range(n_heads):
            cols = slice(h * HEAD_W, (h + 1) * HEAD_W)
            decay = decay_scr[h]
            qdec, kdec, gl = _retention_row_decays(ret_chunk, h)
            q = rq_ref[rows, cols]
            k = rk_ref[rows, cols]
            v = rv_ref[rows, cols]
            s = s_ref[0, h]
            att = lax.dot_general(q.astype(BF16), k.astype(BF16), NT_DIMS,
                                  preferred_element_type=F32) * decay
            o = (jnp.dot(att.astype(BF16), v, preferred_element_type=F32)
                 + jnp.dot((q * qdec).astype(BF16), s.astype(BF16), preferred_element_type=F32))
            s_ref[0, h] = gl * s + lax.dot_general((k * kdec).astype(BF16), v, TN_DIMS,
                                                   preferred_element_type=F32)
            mix_scr[rows, cols] = _gated_group_norm(o, rg_ref[rows, cols]).astype(BF16)

    lane = lax.broadcasted_iota(jnp.int32, (tq, HEAD_W), 1)
    for h in range(n_heads):
        qh = dq_ref[:, h * HEAD_W:(h + 1) * HEAD_W]
        zero = jnp.zeros_like(qh)
        qq_scr[h, 0:tq] = jnp.where(lane < HEAD_W // 2, qh, zero)
        qq_scr[h, tq:2 * tq] = jnp.where(lane >= HEAD_W // 2, qh, zero)
    mx_scr[...] = jnp.full(mx_scr.shape, FINITE_MIN, F32)
    acc_scr[...] = jnp.zeros_like(acc_scr)
    key_j = lax.broadcasted_iota(jnp.int32, (1, tq), 1).astype(F32)
    lane_tiles = [slice(c * HEAD_W, (c + 1) * HEAD_W) for c in range(tq // HEAD_W)]

    def fold(x, op):
        return functools.reduce(op, [x[:, c] for c in lane_tiles])

    def scores(kb, causal):
        key_pos = key_j + ((kb - t) * tq).astype(F32)
        for h in range(n_heads):
            slope = 2.0 ** (-8.0 / n_heads * (h + 1))
            s = jnp.dot(qq_scr[h], kt_ref[kb, h * HEAD_W:(h + 1) * HEAD_W, :], preferred_element_type=F32)
            s = (s + slope * key_pos) * LOG2_E
            if causal is not None:
                s = jnp.where(causal, s, NEG_INF)
            sc_scr[kb, h] = s
            mx_scr[h] = jnp.maximum(mx_scr[h], fold(s, jnp.maximum))

    _pair_loop(t, lambda kb: scores(kb, None))
    ii = lax.broadcasted_iota(jnp.int32, (2 * tq, tq), 0)
    jj = lax.broadcasted_iota(jnp.int32, (2 * tq, tq), 1)
    scores(t, jnp.where(ii >= tq, ii - tq, ii) >= jj)

    for h in range(n_heads):
        mx_scr[h] = jnp.broadcast_to(jnp.max(mx_scr[h], axis=-1, keepdims=True), (2 * tq, HEAD_W))

    ones = jnp.ones((tq, HEAD_W), BF16)

    def weighted(kb):
        for h in range(n_heads):
            m = mx_scr[h]
            e = [jnp.exp2(sc_scr[kb, h, :, c] - m).astype(BF16) for c in lane_tiles]
            vr = vb_ref[pl.ds(pl.multiple_of(kb * tq, tq), tq), h * HEAD_W:(h + 1) * HEAD_W]
            acc_scr[h] += jnp.dot(jnp.concatenate(e, axis=1), jnp.concatenate([vr, ones], axis=1),
                                  preferred_element_type=F32)

    _pair_loop(t + 1, weighted)
    lam = _lambda(lam_ref, lam_i)
    subg = subg_ref[...]
    for h in range(n_heads):
        on = acc_scr[h, :, 0:HEAD_W] / acc_scr[h, :, HEAD_W:2 * HEAD_W]
        o = _rms(on[:tq] - lam * on[tq:], subg) * (1.0 - lam_i)
        mix_scr[:, ret_w + h * HEAD_W:ret_w + (h + 1) * HEAD_W] = o.astype(BF16)

    mo = jnp.dot(mix_scr[...], wo_ref[...], preferred_element_type=F32)
    y_ref[...] = x_ref[...] + _rms(mo, gpost_ref[...])


def _mixer(lamv, rq, rk, rv, rg, dq, ktb, dvb, x2, wo_bf, gpost, subg, batch, seq, n_heads, lam_i,
           srq, srk, srv, srg, sstate):
    m, d = x2.shape
    w = rq.shape[1]
    tq = MIX_TILE
    nt = seq // tq
    n_dec = srq.shape[0]
    grp = DEC_GROUP
    dec_period = batch * nt * grp // n_dec
    assert dec_period * n_dec == batch * nt * grp, "sample row groups must spread evenly over the mixer steps"
    tile = lambda b, t: (b * nt + t, 0)
    whole = lambda b, t: (b, 0)
    dec_rows = pl.BlockSpec((grp, w), lambda b, t: ((b * nt + t) // dec_period, 0))
    dec_state = pl.BlockSpec((grp, n_heads, HEAD_W, HEAD_W), lambda b, t: ((b * nt + t) // dec_period, 0, 0, 0))
    body = functools.partial(_mixer_body, tq=tq, ret_chunk=RET_CHUNK, n_heads=n_heads, lam_i=lam_i,
                             dec_period=dec_period)
    return pl.pallas_call(
        body,
        grid=(batch, nt),
        in_specs=[
            _const_spec(lamv.shape),
            pl.BlockSpec((tq, w), tile), pl.BlockSpec((tq, w), tile), pl.BlockSpec((tq, w), tile),
            pl.BlockSpec((tq, w), tile), pl.BlockSpec((tq, w), tile),
            pl.BlockSpec((nt, w, tq), lambda b, t: (b, 0, 0)), pl.BlockSpec((seq, w), whole),
            pl.BlockSpec((tq, d), tile),
            _const_spec(wo_bf.shape), _const_spec((1, d)), _const_spec((1, HEAD_W)),
            dec_rows, dec_rows, dec_rows, dec_rows, dec_state,
        ],
        out_specs=[pl.BlockSpec((tq, d), tile),
                   pl.BlockSpec((1, n_heads, HEAD_W, HEAD_W), lambda b, t: (b, 0, 0, 0)),
                   dec_rows, dec_state],
        out_shape=[jax.ShapeDtypeStruct((m, d), F32),
                   jax.ShapeDtypeStruct((batch, n_heads, HEAD_W, HEAD_W), F32),
                   jax.ShapeDtypeStruct((n_dec, w), BF16), jax.ShapeDtypeStruct(sstate.shape, F32)],
        scratch_shapes=[
            pltpu.VMEM((tq, 2 * w), BF16),
            pltpu.VMEM((n_heads, 2 * tq, HEAD_W), BF16),
            pltpu.VMEM((nt, n_heads, 2 * tq, tq), F32),
            pltpu.VMEM((n_heads, 2 * tq, HEAD_W), F32),
            pltpu.VMEM((n_heads, 2 * tq, 2 * HEAD_W), F32),
            pltpu.VMEM((n_heads, RET_CHUNK, RET_CHUNK), F32),
        ],
        compiler_params=_params("arbitrary", "arbitrary"),
        name="mixer",
    )(lamv, rq, rk, rv, rg, dq, ktb, dvb, x2, wo_bf, gpost, subg, srq, srk, srv, srg, sstate)


def _conv_gate(g, u, g1, g2, cw_ref, cb_ref, cols):
    c = cw_ref[0:1, cols] * g2 + cw_ref[1:2, cols] * g1 + cw_ref[2:3, cols] * g + cb_ref[:, cols]
    return (jax.nn.gelu(c) * u).astype(BF16)


def _ffn_body(pt_ref, x_ref, gpre_ref, win_ref, cw_ref, cb_ref, wout_ref, gpost_ref,
              lam_ref, q_ref, kn_ref, vn_ref, ck_hbm, cv_hbm, y_ref, cs_ref, o_ref,
              act, kbuf, vbuf, sc_scr, sem, gbuf, carry, *, tf, d_ff, ck, rows, n_pages, page, n_heads, lam_i):
    t = pl.program_id(1)
    step = pl.program_id(0) * pl.num_programs(1) + t
    n_rows = rows * pl.num_programs(0) * pl.num_programs(1)
    copies = functools.partial(_page_copies, pt_ref, ck_hbm, cv_hbm, kbuf, vbuf, sem, n_rows=q_ref.shape[0],
                               n_pages=n_pages)

    @pl.when(step == 0)
    def _():
        for slot in range(2):
            for c in copies(b=slot, slot=slot):
                c.start()

    @pl.when(t == 0)
    def _():
        carry[...] = jnp.zeros_like(carry)

    lam = _lambda(lam_ref, lam_i)
    for i in range(rows):
        g = step * rows + i
        slot = i % 2
        for c in copies(b=g, slot=slot):
            c.wait()
        this = pl.ds(g, 1)
        o_ref[this, :] = _paged_row(slot, q_ref[this, :], kn_ref[this, :], vn_ref[this, :], lam, kbuf, vbuf,
                                    sc_scr, n_pages=n_pages, page=page, n_heads=n_heads)

        @pl.when(g + 2 < n_rows)
        def _():
            for n, c in enumerate(copies(b=g + 2, slot=slot)):
                c.start(priority=n % 2)

    x = x_ref[...]
    h = _rms(x, gpre_ref[...]).astype(BF16)
    for j in range(d_ff // ck):
        cols = slice(j * ck, (j + 1) * ck)
        g = jnp.dot(h, win_ref[:, cols], preferred_element_type=F32)
        u = jnp.dot(h, win_ref[:, d_ff + j * ck:d_ff + (j + 1) * ck], preferred_element_type=F32)
        gbuf[0:8, :] = carry[:, cols]
        gbuf[8:8 + tf, :] = g
        act[:, cols] = _conv_gate(g, u, gbuf[7:7 + tf, :], gbuf[6:6 + tf, :], cw_ref, cb_ref, cols)
        carry[:, cols] = gbuf[tf:tf + 8, :]
    f = jnp.dot(act[...], wout_ref[...], preferred_element_type=F32)
    y_ref[...] = x + _rms(f, gpost_ref[...])
    cs_ref[0] = carry[...]


def _ffn_paged(x2, gpre, win_bf, cw, cb, wout_bf, gpost, batch, seq,
               page_table, lamv, dq, dk, dv, cache_kt, cache_v, n_heads, lam_i):
    m, d = x2.shape
    d_ff = wout_bf.shape[0]
    tf = FFN_TILE
    nt = seq // tf
    n, w = dq.shape
    rows = n // (batch * nt)
    assert rows * batch * nt == n and rows % 2 == 0, "sample rows must split evenly, in pairs, over the FFN steps"
    n_pages = page_table.shape[1]
    page = cache_kt.shape[2]
    grp = _page_group(page)
    assert n_pages % grp == 0
    tile = lambda b, t, pt: (b * nt + t, 0)
    const = _const_spec
    rspec = const((n, w))
    body = functools.partial(_ffn_body, tf=tf, d_ff=d_ff, ck=FFN_CHUNK, rows=rows, n_pages=n_pages, page=page,
                             n_heads=n_heads, lam_i=lam_i)
    grid_spec = pltpu.PrefetchScalarGridSpec(
        num_scalar_prefetch=1,
        grid=(batch, nt),
        in_specs=[pl.BlockSpec((tf, d), tile), const((1, d)), const(win_bf.shape), const(cw.shape),
                  const(cb.shape), const(wout_bf.shape), const((1, d)),
                  const(lamv.shape), rspec, rspec, rspec,
                  pl.BlockSpec(memory_space=pl.ANY), pl.BlockSpec(memory_space=pl.ANY)],
        out_specs=[pl.BlockSpec((tf, d), tile), pl.BlockSpec((1, 8, d_ff), lambda b, t, pt: (b, 0, 0)),
                   pl.BlockSpec((n, w), lambda b, t, pt: (0, 0))],
        scratch_shapes=[pltpu.VMEM((tf, d_ff), BF16),
                        pltpu.VMEM((2, n_pages, w, page), F32),
                        pltpu.VMEM((2, n_pages, page * n_heads, HEAD_W), F32),
                        pltpu.VMEM((n_pages // grp, 2 * n_heads, grp * page), F32),
                        pltpu.SemaphoreType.DMA((2, 2)),
                        pltpu.VMEM((tf + 8, FFN_CHUNK), F32), pltpu.VMEM((8, d_ff), F32)],
    )
    y, cs, o = pl.pallas_call(
        body,
        grid_spec=grid_spec,
        out_shape=[jax.ShapeDtypeStruct((m, d), F32), jax.ShapeDtypeStruct((batch, 8, d_ff), F32),
                   jax.ShapeDtypeStruct((n, w), F32)],
        compiler_params=_params("arbitrary", "arbitrary"),
        name="conv_ffn_paged_attn",
    )(page_table.T.reshape(-1), x2, gpre, win_bf, cw, cb, wout_bf, gpost,
      lamv, dq.astype(F32), dk, dv, cache_kt, cache_v)
    return y, cs, o


def _ret_decode_rows(rq_ref, rk_ref, rv_ref, rg_ref, s_ref, a_ref, sn_ref, *, grp, n_heads):
    row = lax.broadcasted_iota(jnp.int32, (grp, grp * HEAD_W), 0)
    lane = lax.broadcasted_iota(jnp.int32, (grp, grp * HEAD_W), 1)
    own = (lane // HEAD_W) == row

    def block_diag(x):
        return jnp.where(own, jnp.tile(x, (1, grp)), 0.0).astype(BF16)

    for h in range(n_heads):
        cols = slice(h * HEAD_W, (h + 1) * HEAD_W)
        gamma = math.exp(_ret_log_decay(h))
        q = rq_ref[:, cols]
        k = rk_ref[:, cols]
        v = rv_ref[:, cols]
        qb = q.astype(BF16).astype(F32)
        kb = k.astype(BF16).astype(F32)
        att = jnp.sum(qb * kb, axis=-1, keepdims=True).astype(BF16).astype(F32)
        s = s_ref[:, h].reshape(grp * HEAD_W, HEAD_W)
        o = att * v.astype(F32) + jnp.dot(block_diag(q * gamma), s.astype(BF16),
                                          preferred_element_type=F32)
        upd = lax.dot_general(block_diag(k), v, TN_DIMS, preferred_element_type=F32)
        sn_ref[:, h] = (gamma * s + upd).reshape(grp, HEAD_W, HEAD_W)
        a_ref[:, cols] = _gated_group_norm(o, rg_ref[:, cols]).astype(BF16)


def _page_copies(pt_ref, ck_hbm, cv_hbm, kbuf, vbuf, sem, b, slot, n_rows, n_pages):
    out = []
    for p in range(n_pages):
        page = pt_ref[p * n_rows + b]
        out.append(pltpu.make_async_copy(ck_hbm.at[page], kbuf.at[slot, p], sem.at[0, slot]))
        out.append(pltpu.make_async_copy(cv_hbm.at[page], vbuf.at[slot, p], sem.at[1, slot]))
    return out


def _paged_row(slot, q, kn_row, vn_row, lam, kbuf, vbuf, sc_scr, *, n_pages, page, n_heads):
    w = n_heads * HEAD_W
    nc = 2 * n_heads
    past = n_pages * page
    ci = lax.broadcasted_iota(jnp.int32, (nc, w), 0)
    ri = lax.broadcasted_iota(jnp.int32, (nc, w), 1)
    col_of = ri // HEAD_W + n_heads * ((ri % HEAD_W) // (HEAD_W // 2))
    q_sel = jnp.where(ci == col_of, jnp.broadcast_to(q, (nc, w)), 0.0)
    q_sel_bf = q_sel.astype(BF16)
    grp = _page_group(page)
    span = grp * page
    rowc = lax.broadcasted_iota(jnp.int32, (nc, span), 0)
    key_i = lax.broadcasted_iota(jnp.int32, (nc, span), 1)
    slope = jnp.zeros((nc, span), F32)
    for h in range(n_heads):
        slope = jnp.where(rowc % n_heads == h, 2.0 ** (-8.0 / n_heads * (h + 1)), slope)

    mx = jnp.full((nc, span), FINITE_MIN, F32)
    for pg in range(n_pages // grp):
        kt = jnp.concatenate([kbuf[slot, grp * pg + i] for i in range(grp)], axis=1).astype(BF16)
        dist = (past - (pg * span + key_i)).astype(F32)
        s = jnp.dot(q_sel_bf, kt, preferred_element_type=F32) - slope * dist
        sc_scr[pg] = s
        mx = jnp.maximum(mx, s)
    kn = kn_row.astype(BF16).astype(F32)
    s_self = jnp.broadcast_to(jnp.sum(q_sel * kn, axis=-1, keepdims=True), (nc, span))
    m = jnp.maximum(jnp.broadcast_to(jnp.max(mx, axis=-1, keepdims=True), (nc, span)), s_self)

    lp = jnp.zeros((nc, span), F32)
    for pg in range(n_pages // grp):
        e = jnp.exp(sc_scr[pg] - m)
        sc_scr[pg] = e
        lp = lp + e
    e_self = jnp.exp(s_self - m)
    l = jnp.broadcast_to(jnp.sum(lp, axis=-1, keepdims=True), (nc, span)) + e_self
    r = jnp.where(rowc < n_heads, 1.0, lam) / l

    def combine(pn):
        return (pn - pltpu.roll(pn, n_heads, axis=0)).astype(BF16)

    def v_rows(p):
        return jnp.concatenate([vbuf[slot, p, pl.ds(h, page, stride=n_heads), :] for h in range(n_heads)],
                               axis=1)

    acc = jnp.zeros((nc, w), F32)
    for pg in range(n_pages // grp):
        v = jnp.concatenate([v_rows(grp * pg + i) for i in range(grp)], axis=0).astype(BF16)
        acc = acc + jnp.dot(combine(sc_scr[pg] * r), v, preferred_element_type=F32)
    a_self = combine(e_self * r).astype(F32)
    vn = vn_row.astype(BF16).astype(F32)
    return jnp.concatenate(
        [acc[h:h + 1, h * HEAD_W:(h + 1) * HEAD_W]
         + a_self[h:h + 1, 0:HEAD_W] * vn[:, h * HEAD_W:(h + 1) * HEAD_W] for h in range(n_heads)], axis=-1)


def _dec_ffn_body(a_ref, do_ref, x_ref, wo_ref, gmix_ref, subg_ref, gpre_ref, wg_ref, wu_ref, cw_ref,
                  cb_ref, sc_ref, wout_ref, gpost_ref, y_ref, g_ref, xmid, hbuf, acc, *, lam_i):
    j = pl.program_id(0)
    ret_w = a_ref.shape[1]

    @pl.when(j == 0)
    def _():
        mo = jnp.dot(a_ref[...], wo_ref[0:ret_w, :], preferred_element_type=F32)
        for h in range(do_ref.shape[1] // HEAD_W):
            cols = slice(h * HEAD_W, (h + 1) * HEAD_W)
            bh = (_rms(do_ref[:, cols], subg_ref[...]) * (1.0 - lam_i)).astype(BF16)
            mo += jnp.dot(bh, wo_ref[ret_w + h * HEAD_W:ret_w + (h + 1) * HEAD_W, :],
                          preferred_element_type=F32)
        xm = x_ref[...] + _rms(mo, gmix_ref[...])
        xmid[...] = xm
        hbuf[...] = _rms(xm, gpre_ref[...]).astype(BF16)
        acc[...] = jnp.zeros_like(acc)

    h = hbuf[...]
    g = jnp.dot(h, wg_ref[...], preferred_element_type=F32)
    u = jnp.dot(h, wu_ref[...], preferred_element_type=F32)
    g_ref[...] = g
    c = cw_ref[0:1, :] * sc_ref[0] + cw_ref[1:2, :] * sc_ref[1] + cw_ref[2:3, :] * g + cb_ref[...]
    acc[...] += jnp.dot((jax.nn.gelu(c) * u).astype(BF16), wout_ref[...], preferred_element_type=F32)

    @pl.when(j == pl.num_programs(0) - 1)
    def _():
        y_ref[...] = xmid[...] + _rms(acc[...], gpost_ref[...])


def _dec_ffn(a, do, x2, wo_bf, gmix, subg, gpre, win_bf, cw, cb, conv_state_t, wout_bf, gpost, lam_i):
    n, d = x2.shape
    d_ff = wout_bf.shape[0]
    ck = d_ff // DEC_FFN_STEPS
    assert ck * DEC_FFN_STEPS == d_ff and ck % HEAD_W == 0
    nj = DEC_FFN_STEPS
    full = lambda shape: pl.BlockSpec(shape, lambda j: (0,) * len(shape))
    return pl.pallas_call(
        functools.partial(_dec_ffn_body, lam_i=lam_i),
        grid=(nj,),
        in_specs=[full(a.shape), full(do.shape), full(x2.shape), full(wo_bf.shape), full((1, d)),
                  full((1, HEAD_W)), full((1, d)),
                  pl.BlockSpec((d, ck), lambda j: (0, j)), pl.BlockSpec((d, ck), lambda j: (0, nj + j)),
                  pl.BlockSpec((3, ck), lambda j: (0, j)), pl.BlockSpec((1, ck), lambda j: (0, j)),
                  pl.BlockSpec((2, n, ck), lambda j: (0, 0, j)),
                  pl.BlockSpec((ck, d), lambda j: (j, 0)), full((1, d))],
        out_specs=[full((n, d)), pl.BlockSpec((n, ck), lambda j: (0, j))],
        out_shape=[jax.ShapeDtypeStruct((n, d), F32), jax.ShapeDtypeStruct((n, d_ff), F32)],
        scratch_shapes=[pltpu.VMEM((n, d), F32), pltpu.VMEM((n, d), BF16), pltpu.VMEM((n, d), F32)],
        compiler_params=_params("arbitrary"),
        name="dec_ffn",
    )(a, do, x2, wo_bf, gmix, subg, gpre, win_bf, win_bf, cw, cb, conv_state_t, wout_bf, gpost)


def kernel(x_prompt, x_sample, state_ret, cache_k, cache_v, state_conv, page_table,
           norm_mix_pre, norm_mix_post, w_in, w_o, lambda_q1, lambda_k1, lambda_q2, lambda_k2,
           subln_g, norm_ffn_pre, norm_ffn_post, w_ffn_in, conv_w, conv_b, w_ffn_out):
    batch, seq, d = x_prompt.shape
    n_dec = x_sample.shape[0]
    assert x_sample.shape[1] == 1, "the sample group is one token per row"
    depth = w_in.shape[0]
    n_heads = state_ret.shape[2]
    dk_ret = state_ret.shape[3]
    dh_diff = cache_k.shape[-1]
    d_ff = w_ffn_out.shape[1]
    w = n_heads * HEAD_W
    assert w_in.shape[2] == 7 * w and cache_v.shape[-1] == HEAD_W and 2 * dh_diff == HEAD_W
    assert seq % MIX_TILE == 0 and seq % FFN_TILE == 0 and d_ff % FFN_CHUNK == 0
    assert n_dec % DEC_GROUP == 0 and n_dec % V7X_LANES == 0 and seq % IN_TILE == 0 and IN_TILE % MIX_TILE == 0
    assert conv_w.shape[1] == 3 and state_conv.shape[2] == 2, "the FFN kernels carry exactly two past gate rows"
    assert dk_ret == HEAD_W and state_ret.shape[4] == HEAD_W

    xp = x_prompt.reshape(batch * seq, d)
    xs = x_sample.reshape(n_dec, d)
    outs = [[] for _ in range(8)]
    for l in range(depth):
        lam_i = 0.8 - 0.6 * math.exp(-0.3 * l)
        lamv = jnp.stack([lambda_q1[l], lambda_k1[l], lambda_q2[l], lambda_k2[l]]).astype(F32)
        row = lambda v: v.reshape(1, -1)
        w_in_bf = w_in[l].astype(BF16)
        g_pre, g_post = row(norm_mix_pre[l]), row(norm_mix_post[l])
        f_pre, f_post = row(norm_ffn_pre[l]), row(norm_ffn_post[l])
        subg = row(subln_g[l])
        cw, cb = conv_w[l], row(conv_b[l])

        cache_kt = jnp.transpose(cache_k[l], (0, 2, 3, 4, 1)).reshape(cache_k.shape[1], w, cache_k.shape[2])
        cache_vr = cache_v[l].reshape(cache_v.shape[1], cache_v.shape[2] * n_heads, HEAD_W)
        proj = functools.partial(_inproj, n_heads=n_heads, dk_ret=dk_ret, dh_diff=dh_diff)

        srq, srk, srv, srg, dq, skt, sv4, dk, dv = proj(xs, g_pre, w_in_bf, 1, n_dec, n_dec, key_block=None)
        (rq, rk, rv, rg, pdq, kt, v4, ktb, dvb, w_o_bf, w_ffn_in_bf, w_ffn_out_bf) = proj(
            xp, g_pre, w_in_bf, batch, seq, IN_TILE, key_block=MIX_TILE,
            riders=(w_o[l], w_ffn_in[l], w_ffn_out[l]))
        xp, s_fin, a, s_new = _mixer(lamv, rq, rk, rv, rg, pdq, ktb, dvb, xp, w_o_bf, g_post, subg,
                                     batch, seq, n_heads, lam_i, srq, srk, srv, srg, state_ret[l])
        outs[0].append(s_fin)
        outs[2].append(jnp.transpose(kt.reshape(batch, n_heads, 2, dh_diff, seq), (0, 4, 1, 2, 3)))
        outs[3].append(v4.reshape(batch, seq, n_heads, HEAD_W))

        xp, cs, do = _ffn_paged(xp, f_pre, w_ffn_in_bf, cw, cb, w_ffn_out_bf, f_post, batch, seq,
                                page_table, lamv, dq, dk, dv, cache_kt, cache_vr, n_heads, lam_i)
        outs[6].append(cs[:, 6:8, :])

        conv_t = jnp.swapaxes(state_conv[l], 0, 1)
        xs, g_new = _dec_ffn(a, do, xs, w_o_bf, g_post, subg, f_pre, w_ffn_in_bf, cw, cb, conv_t,
                             w_ffn_out_bf, f_post, lam_i)
        outs[1].append(s_new)
        outs[4].append(jnp.transpose(skt.reshape(n_heads, 2, dh_diff, n_dec), (3, 0, 1, 2))[:, None])
        outs[5].append(sv4.reshape(n_dec, 1, n_heads, HEAD_W))
        outs[7].append(jnp.stack([state_conv[l][:, 1, :], g_new], axis=1))

    st = [jnp.stack(o) for o in outs]
    return (xp.reshape(batch, seq, d), xs.reshape(n_dec, 1, d), st[0], st[1], st[2], st[3], st[4], st[5],
            st[6], st[7])
```
